```python
import math
import jax, jax.numpy as jnp
from jax import lax
import numpy as np

D_MODEL = 2048
BATCH = 8
SEQ = 2048
DEPTH = 1

ATT_WIDTH = D_MODEL // 2
ATT_HEAD_DIM = 128
ATT_HEADS = ATT_WIDTH // ATT_HEAD_DIM
MOBA_BLOCK = 256
MOBA_TOPK = 3
MOBA_Q_CHUNK = 16

RWKV_WIDTH = D_MODEL - ATT_WIDTH
RWKV_HEAD_DIM = 64
RWKV_HEADS = RWKV_WIDTH // RWKV_HEAD_DIM
DECAY_LORA = max(32, int(round(1.8 * RWKV_WIDTH ** 0.5 / 32)) * 32)
AAA_LORA = max(32, int(round(1.8 * RWKV_WIDTH ** 0.5 / 32)) * 32)
GATE_LORA = max(32, int(round(0.6 * RWKV_WIDTH ** 0.8 / 32)) * 32)
SHIFT_WIDTH = 3 * RWKV_WIDTH + DECAY_LORA + AAA_LORA + GATE_LORA
PROJ_WIDTH = 3 * ATT_WIDTH + SHIFT_WIDTH

N_GROUPS = 4
EXPERTS_PER_GROUP = 8
N_EXPERTS = N_GROUPS * EXPERTS_PER_GROUP
TOP_K = 2
D_EXPERT = D_MODEL // 2
MOE_ROWS = 128

RMS_EPS = 1e-6
GN_EPS = 64e-5

kernel_name = 'hymba_moba_rwkv7_hier_moe_block'


def rms_norm(x, g):
    xf = x.astype(jnp.float32)
    y = xf * lax.rsqrt(jnp.mean(xf * xf, axis=-1, keepdims=True) + RMS_EPS)
    return (y * g).astype(x.dtype)


def moba_attention(q, k, v):
    b, s, h, d = q.shape
    n_blk = -(-s // MOBA_BLOCK)
    s_pad = n_blk * MOBA_BLOCK
    pad = ((0, 0), (0, s_pad - s), (0, 0), (0, 0))
    q, k, v = jnp.pad(q, pad), jnp.pad(k, pad), jnp.pad(v, pad)
    kb = k.reshape(b, n_blk, MOBA_BLOCK, h, d).transpose(0, 3, 1, 2, 4)
    vb = v.reshape(b, n_blk, MOBA_BLOCK, h, d).transpose(0, 3, 1, 2, 4)
    k_mean = jnp.mean(kb.astype(jnp.float32), axis=3)
    n_sel = min(MOBA_TOPK, n_blk)
    n_chunk = s_pad // MOBA_Q_CHUNK
    q_chunks = q.reshape(b, n_chunk, MOBA_Q_CHUNK, h, d).transpose(1, 0, 3, 2, 4)
    scale = 1.0 / math.sqrt(d)
    b_idx = jnp.arange(b)[:, None, None, None]
    h_idx = jnp.arange(h)[None, :, None, None]
    blk_ids = jnp.arange(n_blk)
    n_sel_keys = n_sel * MOBA_BLOCK

    def one_chunk(args):
        c, qc = args
        q_pos = c * MOBA_Q_CHUNK + jnp.arange(MOBA_Q_CHUNK)
        own = (c * MOBA_Q_CHUNK) // MOBA_BLOCK
        gate = jnp.einsum('bhqd,bhnd->bhqn', qc.astype(jnp.float32), k_mean)
        gate = jnp.where(blk_ids < own, gate, -jnp.inf)
        g_val, sel = lax.top_k(gate, n_sel)
        k_sel = kb[b_idx, h_idx, sel]
        v_sel = vb[b_idx, h_idx, sel]
        s_sel = jnp.einsum('bhqd,bhqnkd->bhqnk', qc, k_sel).astype(jnp.float32) * scale
        s_sel = jnp.where(jnp.isfinite(g_val)[..., None], s_sel, -jnp.inf)
        k_own = lax.dynamic_index_in_dim(kb, own, axis=2, keepdims=False)
        v_own = lax.dynamic_index_in_dim(vb, own, axis=2, keepdims=False)
        k_pos = own * MOBA_BLOCK + jnp.arange(MOBA_BLOCK)
        s_own = jnp.einsum('bhqd,bhkd->bhqk', qc, k_own).astype(jnp.float32) * scale
        s_own = jnp.where(k_pos[None, :] <= q_pos[:, None], s_own, -jnp.inf)
        scores = jnp.concatenate([s_sel.reshape(b, h, MOBA_Q_CHUNK, n_sel_keys), s_own], axis=-1)
        p = jax.nn.softmax(scores, axis=-1).astype(v.dtype)
        p_sel = p[..., :n_sel_keys].reshape(b, h, MOBA_Q_CHUNK, n_sel, MOBA_BLOCK)
        p_own = p[..., n_sel_keys:]
        return (jnp.einsum('bhqnk,bhqnkd->bhqd', p_sel, v_sel)
                + jnp.einsum('bhqk,bhkd->bhqd', p_own, v_own))

    out = lax.map(one_chunk, (jnp.arange(n_chunk), q_chunks))
    return out.transpose(1, 0, 3, 2, 4).reshape(b, s_pad, h, d)[:, :s]


def rwkv7_step(state, inp):
    r_t, w_t, k_t, v_t, a_t, b_t = inp
    sa = jnp.einsum('bhij,bhj->bhi', state, a_t)
    state = (state * w_t[:, :, None, :] + sa[..., None] * b_t[:, :, None, :]
             + v_t[..., None] * k_t[:, :, None, :])
    return state, jnp.einsum('bhij,bhj->bhi', state, r_t)


def rwkv7_time_mix(p, mu, w0, w2, a0, a2, g2, k_k, k_a, r_k, lnx_w, lnx_b):
    b, s, _ = p.shape
    c = RWKV_WIDTH
    prev = jnp.pad(p, ((0, 0), (1, 0), (0, 0)))[:, :s]
    p = p + (prev - p) * mu
    r, k, v = p[..., :c], p[..., c:2 * c], p[..., 2 * c:3 * c]
    o = 3 * c
    xw = p[..., o:o + DECAY_LORA]
    o += DECAY_LORA
    xa = p[..., o:o + AAA_LORA]
    o += AAA_LORA
    xg = p[..., o:o + GATE_LORA]
    w = -jax.nn.softplus(-(w0 + jnp.tanh(xw) @ w2)) - 0.5
    decay = jnp.exp(-jnp.exp(w.astype(jnp.float32)))
    a = jax.nn.sigmoid(a0 + xa @ a2)
    g = jax.nn.sigmoid(xg) @ g2
    hs = (b, s, RWKV_HEADS, RWKV_HEAD_DIM)
    kk = (k * k_k).reshape(hs).astype(jnp.float32)
    kk = kk / jnp.maximum(jnp.sqrt(jnp.sum(kk * kk, axis=-1, keepdims=True)), 1e-12)
    k = k * (1.0 + (a - 1.0) * k_a)
    r_h = r.reshape(hs).astype(jnp.float32)
    k_h = k.reshape(hs).astype(jnp.float32)
    v_h = v.reshape(hs).astype(jnp.float32)
    a_h = a.reshape(hs).astype(jnp.float32)
    xs = tuple(t.transpose(1, 0, 2, 3) for t in
               (r_h, decay.reshape(hs), k_h, v_h, -kk, kk * a_h))
    state0 = jnp.zeros((b, RWKV_HEADS, RWKV_HEAD_DIM, RWKV_HEAD_DIM), jnp.float32)
    _, y = lax.scan(rwkv7_step, state0, xs)
    y = y.transpose(1, 0, 2, 3)
    mean = jnp.mean(y, axis=-1, keepdims=True)
    var = jnp.mean(jnp.square(y - mean), axis=-1, keepdims=True)
    y = ((y - mean) * lax.rsqrt(var + GN_EPS)).reshape(b, s, c) * lnx_w + lnx_b
    bonus = jnp.sum(r_h * k_h * r_k, axis=-1, keepdims=True) * v_h
    y = y + bonus.reshape(b, s, c)
    return (y * g).astype(p.dtype)


def hier_moe(h, wg, bg, we, be, w1, w3, w2):
    b, s, d = h.shape
    t = b * s
    hf = h.reshape(t, d)
    pg = jax.nn.softmax((hf @ wg).astype(jnp.float32) + bg, axis=-1)
    g_gate, g_idx = lax.top_k(pg, 1)
    le = ((hf @ we).astype(jnp.float32) + be).reshape(t, N_GROUPS, EXPERTS_PER_GROUP)
    le = le[jnp.arange(t), g_idx[:, 0]]
    e_val, e_loc = lax.top_k(le, TOP_K)
    gate = g_gate * jax.nn.softmax(e_val, axis=-1)
    expert = g_idx * EXPERTS_PER_GROUP + e_loc
    n_assign = t * TOP_K
    e_flat = expert.reshape(n_assign)
    tok = jnp.repeat(jnp.arange(t), TOP_K)
    order = jnp.argsort(e_flat)
    e_sorted = e_flat[order]
    tok_sorted = tok[order]
    w_sorted = gate.reshape(n_assign)[order]
    counts = jnp.bincount(e_flat, length=N_EXPERTS)
    starts = jnp.cumsum(counts) - counts
    padded = (counts + MOE_ROWS - 1) // MOE_ROWS * MOE_ROWS
    pad_ends = jnp.cumsum(padded)
    pad_starts = pad_ends - padded
    dest = pad_starts[e_sorted] + jnp.arange(n_assign) - starts[e_sorted]
    n_rb = -(-n_assign // MOE_ROWS) + N_EXPERTS
    x_buf = jnp.zeros((n_rb * MOE_ROWS, d), h.dtype).at[dest].set(hf[tok_sorted])
    blk_expert = jnp.minimum(
        jnp.searchsorted(pad_ends, jnp.arange(n_rb) * MOE_ROWS, side='right'), N_EXPERTS - 1)

    def expert_block(args):
        xb, e = args
        hid = jax.nn.silu(xb @ w1[e]) * (xb @ w3[e])
        return hid @ w2[e]

    y_buf = lax.map(expert_block, (x_buf.reshape(n_rb, MOE_ROWS, d), blk_expert))
    y_buf = y_buf.reshape(n_rb * MOE_ROWS, d)
    y = jnp.zeros((t, d), h.dtype).at[tok_sorted].add(
        y_buf[dest] * w_sorted[:, None].astype(h.dtype))
    return y.reshape(b, s, d)


def setup_inputs(seed: int = 0) -> dict:
    key = jax.random.key(seed)
    ks = jax.random.split(key, 26)
    L = DEPTH

    def nrm(k, shape, scale):
        return jax.random.normal(k, shape, jnp.float32) * scale

    return {
        'x': nrm(ks[0], (BATCH, SEQ, D_MODEL), 1.0),
        'norm1_g': 1.0 + nrm(ks[1], (L, D_MODEL), 0.1),
        'w_in': nrm(ks[2], (L, D_MODEL, PROJ_WIDTH), D_MODEL ** -0.5),
        'attn_out_g': 1.0 + nrm(ks[3], (L, ATT_WIDTH), 0.1),
        'rwkv_mu': jax.random.uniform(ks[4], (L, SHIFT_WIDTH), jnp.float32),
        'rwkv_w0': jax.random.uniform(ks[5], (L, RWKV_WIDTH), jnp.float32, -6.0, 0.5),
        'rwkv_w2': nrm(ks[6], (L, DECAY_LORA, RWKV_WIDTH), 0.5 * DECAY_LORA ** -0.5),
        'rwkv_a0': nrm(ks[7], (L, RWKV_WIDTH), 0.5),
        'rwkv_a2': nrm(ks[8], (L, AAA_LORA, RWKV_WIDTH), AAA_LORA ** -0.5),
        'rwkv_g2': nrm(ks[9], (L, GATE_LORA, RWKV_WIDTH), GATE_LORA ** -0.5),
        'rwkv_k_k': 0.85 + nrm(ks[10], (L, RWKV_WIDTH), 0.1),
        'rwkv_k_a': 1.0 + nrm(ks[11], (L, RWKV_WIDTH), 0.1),
        'rwkv_r_k': nrm(ks[12], (L, RWKV_HEADS, RWKV_HEAD_DIM), 0.1),
        'rwkv_lnx_w': 1.0 + nrm(ks[13], (L, RWKV_WIDTH), 0.1),
        'rwkv_lnx_b': nrm(ks[14], (L, RWKV_WIDTH), 0.02),
        'w_out': nrm(ks[15], (L, D_MODEL, D_MODEL), D_MODEL ** -0.5),
        'norm2_g': 1.0 + nrm(ks[16], (L, D_MODEL), 0.1),
        'router_group_w': nrm(ks[17], (L, D_MODEL, N_GROUPS), D_MODEL ** -0.5),
        'router_group_b': nrm(ks[18], (L, N_GROUPS), 0.01),
        'router_expert_w': nrm(ks[19], (L, D_MODEL, N_EXPERTS), D_MODEL ** -0.5),
        'router_expert_b': nrm(ks[20], (L, N_EXPERTS), 0.01),
        'expert_w1': nrm(ks[21], (L, N_EXPERTS, D_MODEL, D_EXPERT), D_MODEL ** -0.5),
        'expert_w3': nrm(ks[22], (L, N_EXPERTS, D_MODEL, D_EXPERT), D_MODEL ** -0.5),
        'expert_w2': nrm(ks[23], (L, N_EXPERTS, D_EXPERT, D_MODEL), D_EXPERT ** -0.5),
        'final_g': 1.0 + nrm(ks[24], (D_MODEL,), 0.1),
    }


def reference(x, norm1_g, w_in, attn_out_g, rwkv_mu, rwkv_w0, rwkv_w2, rwkv_a0, rwkv_a2,
              rwkv_g2, rwkv_k_k, rwkv_k_a, rwkv_r_k, rwkv_lnx_w, rwkv_lnx_b, w_out, norm2_g,
              router_group_w, router_group_b, router_expert_w, router_expert_b,
              expert_w1, expert_w3, expert_w2, final_g):
    b, s, _ = x.shape
    for l in range(DEPTH):
        hn = rms_norm(x, norm1_g[l])
        proj = hn @ w_in[l]
        head_shape = (b, s, ATT_HEADS, ATT_HEAD_DIM)
        q = proj[..., :ATT_WIDTH].reshape(head_shape)
        k = proj[..., ATT_WIDTH:2 * ATT_WIDTH].reshape(head_shape)
        v = proj[..., 2 * ATT_WIDTH:3 * ATT_WIDTH].reshape(head_shape)
        att = moba_attention(q, k, v).reshape(b, s, ATT_WIDTH)
        att = rms_norm(att, attn_out_g[l])
        rw = rwkv7_time_mix(proj[..., 3 * ATT_WIDTH:], rwkv_mu[l], rwkv_w0[l], rwkv_w2[l],
                            rwkv_a0[l], rwkv_a2[l], rwkv_g2[l], rwkv_k_k[l], rwkv_k_a[l],
                            rwkv_r_k[l], rwkv_lnx_w[l], rwkv_lnx_b[l])
        x = x + jnp.concatenate([att, rw], axis=-1) @ w_out[l]
        x = x + hier_moe(rms_norm(x, norm2_g[l]), router_group_w[l], router_group_b[l],
                         router_expert_w[l], router_expert_b[l],
                         expert_w1[l], expert_w3[l], expert_w2[l])
    return rms_norm(x, final_g)
```

```python
import functools
import math

import jax
import jax.numpy as jnp
from jax import lax
from jax.experimental import pallas as pl
from jax.experimental.pallas import tpu as pltpu

F32 = jnp.float32
BF16 = jnp.bfloat16

LANES = 128
ATT_HEAD_DIM = 128
RWKV_HEAD_DIM = 64
MOBA_BLOCK = 256
MOBA_TOPK = 3
N_GROUPS = 4
EXPERTS_PER_GROUP = 8
N_EXPERTS = N_GROUPS * EXPERTS_PER_GROUP
TOP_K = 2
RMS_EPS = 1e-6
GN_EPS = 64e-5
NEG_BIG = -1e30
RWKV_CHUNK = 64
EXPERT_ROWS = 256
COMBINE_ROWS = 256
VMEM_LIMIT = 56 * 1024 * 1024


def _round_up(n, m):
    return (n + m - 1) // m * m


def _dot(a, b):
    return jnp.dot(a.astype(BF16), b.astype(BF16), preferred_element_type=F32)


def _dot_nt(a, b):
    return lax.dot_general(a.astype(BF16), b.astype(BF16), (((1,), (1,)), ((), ())),
                           preferred_element_type=F32)


def _split3(a):
    hi = a.astype(BF16)
    r1 = a - hi.astype(F32)
    mid = r1.astype(BF16)
    lo = (r1 - mid.astype(F32)).astype(BF16)
    return hi, mid, lo


def _dot_hi(a, b):
    ah, am, _ = _split3(a)
    bh, bm, _ = _split3(b)
    d = functools.partial(jnp.dot, preferred_element_type=F32)
    return d(ah, bh) + (d(am, bh) + d(ah, bm))


def _sigmoid(x):
    return 1.0 / (1.0 + jnp.exp(-x))


def _rms(x, g):
    ms = jnp.mean(x * x, axis=-1, keepdims=True)
    return x * lax.rsqrt(ms + RMS_EPS) * g


def _in_proj_kernel(x_ref, g_ref, w_ref, o_ref, xn_ref):
    @pl.when(pl.program_id(1) == 0)
    def _():
        xn_ref[...] = _rms(x_ref[...], g_ref[...]).astype(BF16)

    o_ref[...] = jnp.dot(xn_ref[...], w_ref[...], preferred_element_type=F32)


def _in_proj(x2, g, w, tm, tn):
    t, d = x2.shape
    n = w.shape[1]
    return pl.pallas_call(
        _in_proj_kernel,
        out_shape=jax.ShapeDtypeStruct((t, n), F32),
        grid=(t // tm, n // tn),
        in_specs=[pl.BlockSpec((tm, d), lambda i, j: (i, 0)),
                  pl.BlockSpec((1, d), lambda i, j: (0, 0)),
                  pl.BlockSpec((d, tn), lambda i, j: (0, j))],
        out_specs=pl.BlockSpec((tm, tn), lambda i, j: (i, j)),
        scratch_shapes=[pltpu.VMEM((tm, d), BF16)],
        compiler_params=pltpu.CompilerParams(
            dimension_semantics=("arbitrary", "arbitrary"), vmem_limit_bytes=VMEM_LIMIT),
        name="in_proj",
    )(x2, g, w)


def _moba_kernel(q_ref, k_ref, v_ref, o_ref, kb_ref, vt_ref, km_ref, sel_ref, *, nblk, scale):
    qi = pl.program_id(2)
    bs = MOBA_BLOCK

    @pl.when(qi == 0)
    def _():
        for n in range(nblk):
            kblk = k_ref[n * bs:(n + 1) * bs, :]
            kb_ref[n] = kblk.astype(BF16)
            km_ref[n:n + 1, :] = jnp.sum(kblk, axis=0, keepdims=True) * (1.0 / bs)
            vt_ref[n] = v_ref[n * bs:(n + 1) * bs, :].T.astype(BF16)

    qt = q_ref[...].T
    qtb = qt.astype(BF16)

    gate = _dot_hi(km_ref[...], qt)
    blk = lax.broadcasted_iota(jnp.int32, (nblk, bs), 0)
    eligible = blk < qi
    for n in range(nblk):
        gn = gate[n:n + 1, :]
        beats = eligible & ((gate > gn) | ((gate == gn) & (blk < n)))
        rank = jnp.sum(jnp.where(beats, 1.0, 0.0), axis=0, keepdims=True)
        limit = jnp.where(n < qi, float(MOBA_TOPK), 0.0)
        sel_ref[n:n + 1, :] = jnp.where(rank < limit, 1.0, 0.0)

    s = jnp.dot(kb_ref[qi], qtb, preferred_element_type=F32) * scale
    kpos = lax.broadcasted_iota(jnp.int32, (bs, bs), 0)
    qpos = lax.broadcasted_iota(jnp.int32, (bs, bs), 1)
    s = jnp.where(kpos <= qpos, s, NEG_BIG)
    m = jnp.max(s, axis=0, keepdims=True)
    p = jnp.exp(s - m)
    l = jnp.sum(p, axis=0, keepdims=True)
    acc = jnp.dot(vt_ref[qi], p.astype(BF16), preferred_element_type=F32)

    def body(n, carry):
        m, l, acc = carry
        s = jnp.dot(kb_ref[n], qtb, preferred_element_type=F32) * scale
        s = jnp.where(sel_ref[pl.ds(n, 1), :] > 0.0, s, NEG_BIG)
        m_new = jnp.maximum(m, jnp.max(s, axis=0, keepdims=True))
        alpha = jnp.exp(m - m_new)
        p = jnp.exp(s - m_new)
        l = alpha * l + jnp.sum(p, axis=0, keepdims=True)
        acc = alpha * acc + jnp.dot(vt_ref[n], p.astype(BF16), preferred_element_type=F32)
        return m_new, l, acc

    m, l, acc = lax.fori_loop(0, qi, body, (m, l, acc))
    o_ref[...] = (acc / l).T


def _moba(proj, b, s, heads, col0):
    d = ATT_HEAD_DIM
    nblk = s // MOBA_BLOCK
    kern = functools.partial(_moba_kernel, nblk=nblk, scale=1.0 / math.sqrt(d))
    return pl.pallas_call(
        kern,
        out_shape=jax.ShapeDtypeStruct((b * s, heads * d), F32),
        grid=(b, heads, nblk),
        in_specs=[pl.BlockSpec((MOBA_BLOCK, d), lambda bi, h, qi: (bi * nblk + qi, col0 + h)),
                  pl.BlockSpec((s, d), lambda bi, h, qi: (bi, col0 + heads + h)),
                  pl.BlockSpec((s, d), lambda bi, h, qi: (bi, col0 + 2 * heads + h))],
        out_specs=pl.BlockSpec((MOBA_BLOCK, d), lambda bi, h, qi: (bi * nblk + qi, h)),
        scratch_shapes=[pltpu.VMEM((nblk, MOBA_BLOCK, d), BF16),
                        pltpu.VMEM((nblk, d, MOBA_BLOCK), BF16),
                        pltpu.VMEM((nblk, d), F32),
                        pltpu.VMEM((nblk, MOBA_BLOCK), F32)],
        compiler_params=pltpu.CompilerParams(
            dimension_semantics=("arbitrary", "arbitrary", "arbitrary"), vmem_limit_bytes=VMEM_LIMIT),
        name="moba",
    )(proj, proj, proj)


def _head_sum(x, lo):
    s0 = jnp.sum(jnp.where(lo, x, 0.0), axis=-1, keepdims=True)
    s1 = jnp.sum(jnp.where(lo, 0.0, x), axis=-1, keepdims=True)
    return jnp.where(lo, s0, s1)


def _stack_heads(x, lo):
    return jnp.concatenate([jnp.where(lo, x, 0.0), jnp.where(lo, 0.0, x)], axis=0)


def _rwkv_kernel(p_ref, mu_ref, w0_ref, w2_ref, a0_ref, a2_ref, g2_ref, kk_ref, ka_ref, rk_ref,
                 lnw_ref, lnb_ref, o_ref, carry_ref, state_ref, *, cw, pw, pa, pg):
    c = pl.program_id(1)
    C = RWKV_CHUNK
    C2 = 2 * C

    @pl.when(c == 0)
    def _():
        carry_ref[...] = jnp.zeros_like(carry_ref)
        state_ref[...] = jnp.zeros_like(state_ref)

    pr = p_ref[...]
    row = lax.broadcasted_iota(jnp.int32, pr.shape, 0)
    prev = jnp.where(row == 0, carry_ref[...], pltpu.roll(pr, 1, 0))
    carry_ref[...] = pr[C - 1:C, :]
    ps = pr + (prev - pr) * mu_ref[...]

    r = ps[:, 0:cw]
    k = ps[:, cw:2 * cw]
    v = ps[:, 2 * cw:3 * cw]
    o = 3 * cw
    xw = ps[:, o:o + pw]
    xa = ps[:, o + pw:o + pw + pa]
    xg = ps[:, o + pw + pa:o + pw + pa + pg]

    z = -(w0_ref[...] + _dot(jnp.tanh(xw), w2_ref[...]))
    softplus = jnp.maximum(z, 0.0) + jnp.log(1.0 + jnp.exp(-jnp.abs(z)))
    logw = -jnp.exp(-softplus - 0.5)
    asig = _sigmoid(a0_ref[...] + _dot(xa, a2_ref[...]))
    gate = _dot(_sigmoid(xg), g2_ref[...])
    kkr = k * kk_ref[...]
    kmod = k * (1.0 + (asig - 1.0) * ka_ref[...])

    ti = lax.broadcasted_iota(jnp.int32, (C, C), 0)
    si = lax.broadcasted_iota(jnp.int32, (C, C), 1)
    tril = jnp.where(si <= ti, 1.0, 0.0).astype(BF16)
    lh, lm, ll = _split3(logw)
    d32 = functools.partial(jnp.dot, preferred_element_type=F32)
    cum = d32(tril, lh) + (d32(tril, lm) + d32(tril, ll))
    cum_end = cum[C - 1:C, :]
    e_prev = jnp.exp(cum - logw)
    e_incl = jnp.exp(cum)
    e_inv = jnp.exp(-cum)
    e_end = jnp.exp(cum_end - cum)
    g_end = jnp.exp(cum_end)

    lane = lax.broadcasted_iota(jnp.int32, (C, LANES), 1)
    lo = lane < RWKV_HEAD_DIM
    rho = lax.broadcasted_iota(jnp.int32, (C2, C2), 0)
    sig = lax.broadcasted_iota(jnp.int32, (C2, C2), 1)
    same_head = (rho >= C) == (sig >= C)
    strict = same_head & ((sig % C) < (rho % C))
    incl = same_head & ((sig % C) <= (rho % C))
    eye = jnp.where(rho == sig, 1.0, 0.0)

    for p in range(cw // LANES):
        sl = slice(p * LANES, (p + 1) * LANES)
        kk2 = kkr[:, sl]
        nrm = jnp.sqrt(_head_sum(kk2 * kk2, lo))
        kkn = kk2 / jnp.maximum(nrm, 1e-12)
        a2 = -kkn
        b2 = kkn * asig[:, sl]
        r2 = r[:, sl]
        k2 = kmod[:, sl]
        v2 = v[:, sl]
        at = a2 * e_prev[:, sl]
        rt = r2 * e_incl[:, sl]
        bt = b2 * e_inv[:, sl]
        kt = k2 * e_inv[:, sl]

        lhs = jnp.concatenate([_stack_heads(at, lo), _stack_heads(rt, lo)], axis=0)
        rhs = jnp.concatenate([bt, bt, kt, kt], axis=0)
        gm = _dot_nt(lhs, rhs)
        lab = jnp.where(strict, gm[0:C2, 0:C2], 0.0)
        lak = jnp.where(strict, gm[0:C2, C2:2 * C2], 0.0)
        mrb = jnp.where(incl, gm[C2:2 * C2, 0:C2], 0.0)
        mrk = jnp.where(incl, gm[C2:2 * C2, C2:2 * C2], 0.0)

        tinv = eye + lab
        pw2 = lab
        for _ in range(int(math.log2(C)) - 1):
            pw2 = _dot(pw2, pw2)
            tinv = tinv + _dot(tinv, pw2)

        s0 = state_ref[p]
        ars = _dot_nt(jnp.concatenate([at, rt], axis=0), s0)
        v2s = _stack_heads(v2, lo)
        u2s = _dot(tinv, _stack_heads(ars[0:C], lo) + _dot(lak, v2s))
        y2s = _stack_heads(ars[C:C2], lo) + _dot(mrb, u2s) + _dot(mrk, v2s)
        u2 = u2s[0:C] + u2s[C:C2]
        y2 = y2s[0:C] + y2s[C:C2]

        uv = jnp.concatenate([u2, v2], axis=0)
        bk = jnp.concatenate([b2 * e_end[:, sl], k2 * e_end[:, sl]], axis=0)
        upd = _dot(uv.T, bk)
        state_ref[p] = s0 * g_end[:, sl] + jnp.where(same_head, upd, 0.0)

        mean = _head_sum(y2, lo) * (1.0 / RWKV_HEAD_DIM)
        yc = y2 - mean
        var = _head_sum(yc * yc, lo) * (1.0 / RWKV_HEAD_DIM)
        yn = yc * lax.rsqrt(var + GN_EPS) * lnw_ref[:, sl] + lnb_ref[:, sl]
        bonus = _head_sum(r2 * k2 * rk_ref[:, sl], lo) * v2
        o_ref[:, sl] = (yn + bonus) * gate[:, sl]


def _rwkv(proj, b, s, cw, wpad, params):
    (mu, w0, w2, a0, a2, g2, kk, ka, rk, lnw, lnb) = params
    C = RWKV_CHUNK
    nch = s // C
    pw, pa, pg = w2.shape[0], a2.shape[0], g2.shape[0]
    kern = functools.partial(_rwkv_kernel, cw=cw, pw=pw, pa=pa, pg=pg)
    row = lambda n: pl.BlockSpec((1, n), lambda bi, c: (0, 0))
    full = lambda a: pl.BlockSpec(a.shape, lambda bi, c: (0, 0))
    return pl.pallas_call(
        kern,
        out_shape=jax.ShapeDtypeStruct((b * s, cw), F32),
        grid=(b, nch),
        in_specs=[pl.BlockSpec((C, wpad), lambda bi, c: (bi * nch + c, 0)),
                  row(wpad), row(cw), full(w2), row(cw), full(a2), full(g2),
                  row(cw), row(cw), row(cw), row(cw), row(cw)],
        out_specs=pl.BlockSpec((C, cw), lambda bi, c: (bi * nch + c, 0)),
        scratch_shapes=[pltpu.VMEM((1, wpad), F32),
                        pltpu.VMEM((cw // LANES, LANES, LANES), F32)],
        compiler_params=pltpu.CompilerParams(
            dimension_semantics=("arbitrary", "arbitrary"), vmem_limit_bytes=VMEM_LIMIT),
        name="rwkv",
    )(proj, mu, w0, w2, a0, a2, g2, kk, ka, rk, lnw, lnb)


def _out_proj_kernel(att_ref, rw_ref, x_ref, ga_ref, woa_ref, wor_ref, g2_ref, wrh_ref, wrm_ref, br_ref,
                     x1_ref, hn_ref, route_ref):
    att = _rms(att_ref[...], ga_ref[...])
    y = (jnp.dot(att.astype(BF16), woa_ref[...], preferred_element_type=F32)
         + jnp.dot(rw_ref[...].astype(BF16), wor_ref[...], preferred_element_type=F32))
    x1 = x_ref[...] + y
    x1_ref[...] = x1
    hn = _rms(x1, g2_ref[...])
    hn_ref[...] = hn

    hh, hm, _ = _split3(hn)
    d32 = functools.partial(jnp.dot, preferred_element_type=F32)
    logits = d32(hh, wrh_ref[...]) + (d32(hm, wrh_ref[...]) + d32(hh, wrm_ref[...])) + br_ref[...]
    lane = lax.broadcasted_iota(jnp.int32, logits.shape, 1)
    lanef = lane.astype(F32)
    big = float(LANES)

    is_g = lane < N_GROUPS
    gmax = jnp.max(jnp.where(is_g, logits, -jnp.inf), axis=-1, keepdims=True)
    gidx = jnp.min(jnp.where(is_g & (logits == gmax), lanef, big), axis=-1, keepdims=True)
    gsum = jnp.sum(jnp.where(is_g, jnp.exp(logits - gmax), 0.0), axis=-1, keepdims=True)
    g_gate = 1.0 / gsum

    e_lo = N_GROUPS + EXPERTS_PER_GROUP * gidx
    in_grp = (lanef >= e_lo) & (lanef < e_lo + EXPERTS_PER_GROUP)
    e1 = jnp.max(jnp.where(in_grp, logits, -jnp.inf), axis=-1, keepdims=True)
    i1 = jnp.min(jnp.where(in_grp & (logits == e1), lanef, big), axis=-1, keepdims=True)
    rest = in_grp & (lanef != i1)
    e2 = jnp.max(jnp.where(rest, logits, -jnp.inf), axis=-1, keepdims=True)
    i2 = jnp.min(jnp.where(rest & (logits == e2), lanef, big), axis=-1, keepdims=True)
    t = jnp.exp(e2 - e1)
    w1 = g_gate * (1.0 / (1.0 + t))
    w2 = g_gate * (t / (1.0 + t))
    route_ref[...] = jnp.where(lane == 0, i1 - N_GROUPS,
                     jnp.where(lane == 1, i2 - N_GROUPS,
                     jnp.where(lane == 2, w1,
                     jnp.where(lane == 3, w2, 0.0))))


def _out_proj(att, rw, x2, ga, woa, wor, g2, wrh, wrm, br, tm):
    t, d = x2.shape
    wa = att.shape[1]
    wr = rw.shape[1]
    c = lambda shape: pl.BlockSpec(shape, lambda i: (0, 0))
    return pl.pallas_call(
        _out_proj_kernel,
        out_shape=(jax.ShapeDtypeStruct((t, d), F32), jax.ShapeDtypeStruct((t, d), F32),
                   jax.ShapeDtypeStruct((t, LANES), F32)),
        grid=(t // tm,),
        in_specs=[pl.BlockSpec((tm, wa), lambda i: (i, 0)),
                  pl.BlockSpec((tm, wr), lambda i: (i, 0)),
                  pl.BlockSpec((tm, d), lambda i: (i, 0)),
                  c((1, wa)), c((wa, d)), c((wr, d)), c((1, d)), c((d, LANES)), c((d, LANES)), c((1, LANES))],
        out_specs=(pl.BlockSpec((tm, d), lambda i: (i, 0)), pl.BlockSpec((tm, d), lambda i: (i, 0)),
                   pl.BlockSpec((tm, LANES), lambda i: (i, 0))),
        compiler_params=pltpu.CompilerParams(
            dimension_semantics=("arbitrary",), vmem_limit_bytes=VMEM_LIMIT),
        name="out_proj",
    )(att, rw, x2, ga, woa, wor, g2, wrh, wrm, br)


def _row_copy(src_hbm, dst_vmem, sem, src_row, dst_row):
    return pltpu.make_async_copy(src_hbm.at[pl.ds(src_row, 1)], dst_vmem.at[pl.ds(dst_row, 1)], sem)


def _expert_kernel(te_ref, nu_ref, rows_hbm, x_hbm, w1_ref, w3_ref, w2_ref, o_ref,
                   idx_ref, xbuf_ref, isem, gsem):
    i = pl.program_id(0)
    tm = EXPERT_ROWS

    @pl.when(i < nu_ref[0])
    def _():
        idx_cp = pltpu.make_async_copy(rows_hbm.at[pl.ds(i, 1)], idx_ref, isem)
        idx_cp.start()
        idx_cp.wait()

        def issue(r, carry):
            _row_copy(x_hbm, xbuf_ref, gsem, idx_ref[0, r], r).start()
            return carry

        lax.fori_loop(0, tm, issue, 0)

        def drain(r, carry):
            _row_copy(x_hbm, xbuf_ref, gsem, 0, r).wait()
            return carry

        lax.fori_loop(0, tm, drain, 0)

        xb = xbuf_ref[...].astype(BF16)
        h1 = jnp.dot(xb, w1_ref[0], preferred_element_type=F32)
        h3 = jnp.dot(xb, w3_ref[0], preferred_element_type=F32)
        hid = (h1 * _sigmoid(h1)) * h3
        o_ref[...] = jnp.dot(hid.astype(BF16), w2_ref[0], preferred_element_type=F32)

    @pl.when(i >= nu_ref[0])
    def _():
        o_ref[...] = jnp.zeros_like(o_ref)


def _experts(tile_e, n_used, row_tok, hn, w1, w3, w2):
    nt = tile_e.shape[0]
    t, d = hn.shape
    de = w1.shape[2]
    tm = EXPERT_ROWS
    wspec = lambda shape: pl.BlockSpec(shape, lambda i, te, nu: (te[i], 0, 0))
    grid_spec = pltpu.PrefetchScalarGridSpec(
        num_scalar_prefetch=2,
        grid=(nt,),
        in_specs=[pl.BlockSpec(memory_space=pl.ANY),
                  pl.BlockSpec(memory_space=pl.ANY),
                  wspec((1, d, de)), wspec((1, d, de)), wspec((1, de, d))],
        out_specs=pl.BlockSpec((tm, d), lambda i, te, nu: (i, 0)),
        scratch_shapes=[pltpu.SMEM((1, tm), jnp.int32),
                        pltpu.VMEM((tm, d), F32),
                        pltpu.SemaphoreType.DMA, pltpu.SemaphoreType.DMA],
    )
    return pl.pallas_call(
        _expert_kernel,
        out_shape=jax.ShapeDtypeStruct((nt * tm, d), F32),
        grid_spec=grid_spec,
        compiler_params=pltpu.CompilerParams(
            dimension_semantics=("arbitrary",), vmem_limit_bytes=VMEM_LIMIT),
        name="experts",
    )(tile_e, n_used, row_tok, hn, w1, w3, w2)


def _combine_kernel(dest_hbm, y_hbm, x1_ref, route_ref, fg_ref, o_ref, idx_ref, buf_ref, isem, gsem):
    i = pl.program_id(0)
    tg = COMBINE_ROWS
    idx_cp = pltpu.make_async_copy(dest_hbm.at[pl.ds(i, 1)], idx_ref, isem)
    idx_cp.start()
    idx_cp.wait()

    def issue(r, carry):
        _row_copy(y_hbm, buf_ref, gsem, idx_ref[0, r], r).start()
        return carry

    lax.fori_loop(0, TOP_K * tg, issue, 0)

    def drain(r, carry):
        _row_copy(y_hbm, buf_ref, gsem, 0, r).wait()
        return carry

    lax.fori_loop(0, TOP_K * tg, drain, 0)

    route = route_ref[...]
    y = buf_ref[0:tg, :] * route[:, 2:3] + buf_ref[tg:2 * tg, :] * route[:, 3:4]
    o_ref[...] = _rms(x1_ref[...] + y, fg_ref[...])


def _combine(dest, ybuf, x1, route, fg):
    t, d = x1.shape
    tg = COMBINE_ROWS
    return pl.pallas_call(
        _combine_kernel,
        out_shape=jax.ShapeDtypeStruct((t, d), F32),
        grid=(t // tg,),
        in_specs=[pl.BlockSpec(memory_space=pl.ANY),
                  pl.BlockSpec(memory_space=pl.ANY),
                  pl.BlockSpec((tg, d), lambda i: (i, 0)),
                  pl.BlockSpec((tg, LANES), lambda i: (i, 0)),
                  pl.BlockSpec((1, d), lambda i: (0, 0))],
        out_specs=pl.BlockSpec((tg, d), lambda i: (i, 0)),
        scratch_shapes=[pltpu.SMEM((1, TOP_K * tg), jnp.int32),
                        pltpu.VMEM((TOP_K * tg, d), F32),
                        pltpu.SemaphoreType.DMA, pltpu.SemaphoreType.DMA],
        compiler_params=pltpu.CompilerParams(
            dimension_semantics=("arbitrary",), vmem_limit_bytes=VMEM_LIMIT),
        name="combine",
    )(dest, ybuf, x1, route, fg)


def _pad_cols(m, n):
    return jnp.pad(m, ((0, 0), (0, n - m.shape[1])))


def _pad_rows(m, n):
    return jnp.pad(m, ((0, n - m.shape[0]), (0, 0)))


def _pick_tile(n, cap):
    best = LANES
    for c in range(LANES, cap + 1, LANES):
        if n % c == 0:
            best = c
    return best


def _dispatch_plan(experts, n_tok):
    tm = EXPERT_ROWS
    n_assign = n_tok * TOP_K
    nt = n_assign // tm + N_EXPERTS
    e_flat = experts.reshape(n_assign)
    onehot = (e_flat[:, None] == jnp.arange(N_EXPERTS, dtype=jnp.int32)[None, :]).astype(jnp.int32)
    csum = jnp.cumsum(onehot, axis=0)
    rank = jnp.take_along_axis(csum, e_flat[:, None], axis=1)[:, 0] - 1
    counts = csum[-1]
    tiles_e = (counts + tm - 1) // tm
    tile_end = jnp.cumsum(tiles_e)
    tile_start = tile_end - tiles_e
    dest = tile_start[e_flat] * tm + rank
    n_used = tile_end[-1:].astype(jnp.int32)
    tile_e = jnp.minimum(jnp.searchsorted(tile_end, jnp.arange(nt, dtype=jnp.int32), side="right"),
                         N_EXPERTS - 1).astype(jnp.int32)
    tok = jnp.arange(n_assign, dtype=jnp.int32) // TOP_K
    row_tok = jnp.zeros((nt * tm,), jnp.int32).at[dest].set(tok)
    return tile_e, n_used, row_tok.reshape(nt, tm), dest.astype(jnp.int32)


def _layer(x2, b, s, norm1_g, w_in, attn_out_g, mu, w0, w2, a0, a2, g2, k_k, k_a, r_k, lnx_w, lnx_b,
           w_out, norm2_g, wg, bg, we, be, ew1, ew3, ew2, final_g):
    t, d = x2.shape
    aw = attn_out_g.shape[0]
    cw = w0.shape[0]
    heads = aw // ATT_HEAD_DIM
    dl, al, gl = w2.shape[0], a2.shape[0], g2.shape[0]
    pw, pa, pg = _round_up(dl, LANES), _round_up(al, LANES), _round_up(gl, LANES)
    wpad = 3 * cw + pw + pa + pg

    sh = w_in[:, 3 * aw:]
    o = 3 * cw
    w_arr = jnp.concatenate(
        [sh[:, :o], _pad_cols(sh[:, o:o + dl], pw), _pad_cols(sh[:, o + dl:o + dl + al], pa),
         _pad_cols(sh[:, o + dl + al:o + dl + al + gl], pg), w_in[:, :3 * aw]], axis=1).astype(BF16)
    mu_arr = jnp.concatenate(
        [mu[:o], jnp.pad(mu[o:o + dl], (0, pw - dl)), jnp.pad(mu[o + dl:o + dl + al], (0, pa - al)),
         jnp.pad(mu[o + dl + al:], (0, pg - gl))])[None, :]

    n_all = w_arr.shape[1]
    proj = _in_proj(x2, norm1_g[None, :], w_arr, tm=min(512, t), tn=_pick_tile(n_all, 2048))

    att = _moba(proj, b, s, heads, wpad // LANES)
    r1 = lambda a: a.reshape(1, -1)
    rw = _rwkv(proj, b, s, cw, wpad,
               (mu_arr, r1(w0), _pad_rows(w2, pw).astype(BF16), r1(a0), _pad_rows(a2, pa).astype(BF16),
                _pad_rows(g2, pg).astype(BF16), r1(k_k), r1(k_a), r1(r_k), r1(lnx_w), r1(lnx_b)))

    wr = _pad_cols(jnp.concatenate([wg, we], axis=1), LANES)
    wrh = wr.astype(BF16)
    wrm = (wr - wrh.astype(F32)).astype(BF16)
    br = jnp.pad(jnp.concatenate([bg, be]), (0, LANES - N_GROUPS - N_EXPERTS))[None, :]
    wo = w_out.astype(BF16)
    x1, hn, route = _out_proj(att, rw, x2, r1(attn_out_g), wo[:aw], wo[aw:], r1(norm2_g), wrh, wrm, br,
                              tm=min(256, t))

    experts = route[:, :TOP_K].astype(jnp.int32)
    tile_e, n_used, row_tok, dest = _dispatch_plan(experts, t)
    ybuf = _experts(tile_e, n_used, row_tok, hn, ew1.astype(BF16), ew3.astype(BF16), ew2.astype(BF16))

    tg = COMBINE_ROWS
    dest_tiles = dest.reshape(t // tg, tg, TOP_K).transpose(0, 2, 1).reshape(t // tg, TOP_K * tg)
    return _combine(dest_tiles, ybuf, x1, route, r1(final_g))


def kernel(x, norm1_g, w_in, attn_out_g, rwkv_mu, rwkv_w0, rwkv_w2, rwkv_a0, rwkv_a2, rwkv_g2, rwkv_k_k,
           rwkv_k_a, rwkv_r_k, rwkv_lnx_w, rwkv_lnx_b, w_out, norm2_g, router_group_w, router_group_b,
           router_expert_w, router_expert_b, expert_w1, expert_w3, expert_w2, final_g):
    b, s, d = x.shape
    assert norm1_g.shape[0] == 1, "single-layer block"
    assert s % MOBA_BLOCK == 0 and s % RWKV_CHUNK == 0
    assert (b * s) % max(EXPERT_ROWS, COMBINE_ROWS) == 0
    out = _layer(x.reshape(b * s, d), b, s, norm1_g[0], w_in[0], attn_out_g[0], rwkv_mu[0], rwkv_w0[0],
                 rwkv_w2[0], rwkv_a0[0], rwkv_a2[0], rwkv_g2[0], rwkv_k_k[0], rwkv_k_a[0], rwkv_r_k[0],
                 rwkv_lnx_w[0], rwkv_lnx_b[0], w_out[0], norm2_g[0], router_group_w[0], router_group_b[0],
                 router_expert_w[0], router_expert_b[0], expert_w1[0], expert_w3[0], expert_w2[0], final_g)
    return out.reshape(b, s, d)
```

```python
import functools
import math

import jax
import jax.numpy as jnp
from jax import lax
from jax.experimental import pallas as pl
from jax.experimental.pallas import tpu as pltpu

F32 = jnp.float32
BF16 = jnp.bfloat16

LANES = 128
SLAB_LANES = 128
ATT_HEAD_DIM = 128
RWKV_HEAD_DIM = 64
MOBA_BLOCK = 256
MOBA_TOPK = 3
N_GROUPS = 4
EXPERTS_PER_GROUP = 8
N_EXPERTS = N_GROUPS * EXPERTS_PER_GROUP
TOP_K = 2
RMS_EPS = 1e-6
GN_EPS = 64e-5
NEG_BIG = -1e30
RWKV_CHUNK = 64
EXPERT_ROWS = 256
COMBINE_ROWS = 256
VMEM_LIMIT = 56 * 1024 * 1024


def _round_up(n, m):
    return (n + m - 1) // m * m


def _dot(a, b):
    return jnp.dot(a.astype(BF16), b.astype(BF16), preferred_element_type=F32)


def _dot_nt(a, b):
    return lax.dot_general(a.astype(BF16), b.astype(BF16), (((1,), (1,)), ((), ())),
                           preferred_element_type=F32)


def _split3(a):
    hi = a.astype(BF16)
    r1 = a - hi.astype(F32)
    mid = r1.astype(BF16)
    lo = (r1 - mid.astype(F32)).astype(BF16)
    return hi, mid, lo


def _dot_hi(a, b):
    ah, am, _ = _split3(a)
    bh, bm, _ = _split3(b)
    d = functools.partial(jnp.dot, preferred_element_type=F32)
    return d(ah, bh) + (d(am, bh) + d(ah, bm))


def _sigmoid(x):
    return 1.0 / (1.0 + jnp.exp(-x))


def _rms(x, g):
    ms = jnp.mean(x * x, axis=-1, keepdims=True)
    return x * lax.rsqrt(ms + RMS_EPS) * g


def _in_proj_kernel(x_ref, g_ref, w_ref, o_ref, xn_ref):
    @pl.when(pl.program_id(1) == 0)
    def _():
        xn_ref[...] = _rms(x_ref[...], g_ref[...]).astype(BF16)

    o_ref[...] = jnp.dot(xn_ref[...], w_ref[...], preferred_element_type=F32)


def _in_proj(x2, g, w, tm, tn):
    t, d = x2.shape
    n = w.shape[1]
    return pl.pallas_call(
        _in_proj_kernel,
        out_shape=jax.ShapeDtypeStruct((t, n), F32),
        grid=(t // tm, n // tn),
        in_specs=[pl.BlockSpec((tm, d), lambda i, j: (i, 0)),
                  pl.BlockSpec((1, d), lambda i, j: (0, 0)),
                  pl.BlockSpec((d, tn), lambda i, j: (0, j))],
        out_specs=pl.BlockSpec((tm, tn), lambda i, j: (i, j)),
        scratch_shapes=[pltpu.VMEM((tm, d), BF16)],
        compiler_params=pltpu.CompilerParams(
            dimension_semantics=("arbitrary", "arbitrary"), vmem_limit_bytes=VMEM_LIMIT),
        name="in_proj",
    )(x2, g, w)


def _moba_kernel(q_ref, k_ref, v_ref, o_ref, *, nblk, scale):
    bs = MOBA_BLOCK
    k = k_ref[...]
    kb = k.astype(BF16)
    vtb = v_ref[...].T.astype(BF16)
    qt = q_ref[...].T
    qtb = qt.astype(BF16)
    kmean = jnp.concatenate(
        [jnp.sum(k[n * bs:(n + 1) * bs, :], axis=0, keepdims=True) for n in range(nblk)], axis=0) * (1.0 / bs)
    gate_all = _dot_hi(kmean, qt)
    kpos = lax.broadcasted_iota(jnp.int32, (bs, bs), 0)
    qpos = lax.broadcasted_iota(jnp.int32, (bs, bs), 1)
    causal = kpos <= qpos

    for qi in range(nblk):
        qs = slice(qi * bs, (qi + 1) * bs)
        nk = (qi + 1) * bs
        s_all = jnp.dot(kb[0:nk, :], qtb[:, qs], preferred_element_type=F32) * scale
        blocks = []
        if qi > MOBA_TOPK:
            gate = gate_all[0:qi, qs]
            blk = lax.broadcasted_iota(jnp.int32, (qi, bs), 0)
        for n in range(qi):
            sn = s_all[n * bs:(n + 1) * bs, :]
            if qi > MOBA_TOPK:
                gn = gate[n:n + 1, :]
                beats = (gate > gn) | ((gate == gn) & (blk < n))
                rank = jnp.sum(jnp.where(beats, 1.0, 0.0), axis=0, keepdims=True)
                sn = jnp.where(rank < float(MOBA_TOPK), sn, NEG_BIG)
            blocks.append(sn)
        blocks.append(jnp.where(causal, s_all[qi * bs:nk, :], NEG_BIG))
        m = functools.reduce(jnp.maximum, [jnp.max(x, axis=0, keepdims=True) for x in blocks])
        ps = [jnp.exp(x - m) for x in blocks]
        l = functools.reduce(jnp.add, [jnp.sum(x, axis=0, keepdims=True) for x in ps])
        pb = jnp.concatenate([x.astype(BF16) for x in ps], axis=0)
        acc = jnp.dot(vtb[:, 0:nk], pb, preferred_element_type=F32)
        o_ref[qs, :] = (acc / l).T


def _moba(proj, b, s, heads, col0):
    d = ATT_HEAD_DIM
    nblk = s // MOBA_BLOCK
    kern = functools.partial(_moba_kernel, nblk=nblk, scale=1.0 / math.sqrt(d))
    return pl.pallas_call(
        kern,
        out_shape=jax.ShapeDtypeStruct((b * s, heads * d), F32),
        grid=(b, heads),
        in_specs=[pl.BlockSpec((s, d), lambda bi, h: (bi, col0 + h)),
                  pl.BlockSpec((s, d), lambda bi, h: (bi, col0 + heads + h)),
                  pl.BlockSpec((s, d), lambda bi, h: (bi, col0 + 2 * heads + h))],
        out_specs=pl.BlockSpec((s, d), lambda bi, h: (bi, h)),
        compiler_params=pltpu.CompilerParams(
            dimension_semantics=("arbitrary", "arbitrary"), vmem_limit_bytes=VMEM_LIMIT),
        name="moba",
    )(proj, proj, proj)


def _head_sum(x, lo):
    s0 = jnp.sum(jnp.where(lo, x, 0.0), axis=-1, keepdims=True)
    s1 = jnp.sum(jnp.where(lo, 0.0, x), axis=-1, keepdims=True)
    return jnp.where(lo, s0, s1)


def _stack_heads(x, lo):
    return jnp.concatenate([jnp.where(lo, x, 0.0), jnp.where(lo, 0.0, x)], axis=0)


def _rwkv_kernel(p_ref, mu_ref, w0_ref, w2_ref, a0_ref, a2_ref, g2_ref, kk_ref, ka_ref, rk_ref,
                 lnw_ref, lnb_ref, o_ref, carry_ref, state_ref, *, cw, pw, pa, pg):
    c = pl.program_id(1)
    C = RWKV_CHUNK
    C2 = 2 * C

    @pl.when(c == 0)
    def _():
        carry_ref[...] = jnp.zeros_like(carry_ref)
        state_ref[...] = jnp.zeros_like(state_ref)

    pr = p_ref[...]
    row = lax.broadcasted_iota(jnp.int32, pr.shape, 0)
    prev = jnp.where(row == 0, carry_ref[...], pltpu.roll(pr, 1, 0))
    carry_ref[...] = pr[C - 1:C, :]
    ps = pr + (prev - pr) * mu_ref[...]

    r = ps[:, 0:cw]
    k = ps[:, cw:2 * cw]
    v = ps[:, 2 * cw:3 * cw]
    o = 3 * cw
    xw = ps[:, o:o + pw]
    xa = ps[:, o + pw:o + pw + pa]
    xg = ps[:, o + pw + pa:o + pw + pa + pg]

    z = -(w0_ref[...] + _dot(jnp.tanh(xw), w2_ref[...]))
    softplus = jnp.maximum(z, 0.0) + jnp.log(1.0 + jnp.exp(-jnp.abs(z)))
    logw = -jnp.exp(-softplus - 0.5)
    asig = _sigmoid(a0_ref[...] + _dot(xa, a2_ref[...]))
    gate = _dot(_sigmoid(xg), g2_ref[...])
    kkr = k * kk_ref[...]
    kmod = k * (1.0 + (asig - 1.0) * ka_ref[...])

    ti = lax.broadcasted_iota(jnp.int32, (C, C), 0)
    si = lax.broadcasted_iota(jnp.int32, (C, C), 1)
    tril = jnp.where(si <= ti, 1.0, 0.0).astype(BF16)
    lh, lm, ll = _split3(logw)
    d32 = functools.partial(jnp.dot, preferred_element_type=F32)
    cum = d32(tril, lh) + (d32(tril, lm) + d32(tril, ll))
    cum_end = cum[C - 1:C, :]
    e_prev = jnp.exp(cum - logw)
    e_incl = jnp.exp(cum)
    e_inv = jnp.exp(-cum)
    e_end = jnp.exp(cum_end - cum)
    g_end = jnp.exp(cum_end)

    lane = lax.broadcasted_iota(jnp.int32, (C, LANES), 1)
    lo = lane < RWKV_HEAD_DIM
    rho = lax.broadcasted_iota(jnp.int32, (C2, C2), 0)
    sig = lax.broadcasted_iota(jnp.int32, (C2, C2), 1)
    same_head = (rho >= C) == (sig >= C)
    strict = same_head & ((sig % C) < (rho % C))
    incl = same_head & ((sig % C) <= (rho % C))
    eye = jnp.where(rho == sig, 1.0, 0.0)

    pairs = range(cw // LANES)
    sls = [slice(p * LANES, (p + 1) * LANES) for p in pairs]
    incl2 = jnp.concatenate([incl, incl], axis=1)
    strict2 = jnp.concatenate([strict, strict], axis=1)

    b2, r2, k2, v2, at, rt, v2s, gm = [], [], [], [], [], [], [], []
    for sl in sls:
        kk2 = kkr[:, sl]
        nrm = jnp.sqrt(_head_sum(kk2 * kk2, lo))
        kkn = kk2 / jnp.maximum(nrm, 1e-12)
        b2.append(kkn * asig[:, sl])
        r2.append(r[:, sl])
        k2.append(kmod[:, sl])
        v2.append(v[:, sl])
        at.append(-kkn * e_prev[:, sl])
        rt.append(r2[-1] * e_incl[:, sl])
        bt = b2[-1] * e_inv[:, sl]
        kt = k2[-1] * e_inv[:, sl]
        v2s.append(_stack_heads(v2[-1], lo))
        lhs = jnp.concatenate([_stack_heads(at[-1], lo), _stack_heads(rt[-1], lo)], axis=0)
        rhs = jnp.concatenate([bt, bt, kt, kt], axis=0)
        gm.append(_dot_nt(lhs, rhs))

    s0 = [state_ref[p] for p in pairs]
    ars = [_dot_nt(jnp.concatenate([at[p], rt[p]], axis=0), s0[p]) for p in pairs]
    labk = [jnp.where(strict2, g[0:C2, :], 0.0) for g in gm]
    mrbk = [jnp.where(incl2, g[C2:2 * C2, :], 0.0) for g in gm]
    rhs_u = [_stack_heads(ars[p][0:C], lo) + _dot(labk[p][:, C2:2 * C2], v2s[p]) for p in pairs]

    pw2 = [m[:, 0:C2] for m in labk]
    tinv = [eye + m for m in pw2]
    for _ in range(int(math.log2(C)) - 1):
        pw2 = [_dot(m, m) for m in pw2]
        tinv = [t + _dot(t, m) for t, m in zip(tinv, pw2)]

    u2s = [_dot(t, x) for t, x in zip(tinv, rhs_u)]
    y2s = [_stack_heads(ars[p][C:C2], lo) + _dot(mrbk[p], jnp.concatenate([u2s[p], v2s[p]], axis=0))
           for p in pairs]
    u2 = [m[0:C] + m[C:C2] for m in u2s]
    y2 = [m[0:C] + m[C:C2] for m in y2s]

    for p, sl in zip(pairs, sls):
        uv = jnp.concatenate([u2[p], v2[p]], axis=0)
        bk = jnp.concatenate([b2[p] * e_end[:, sl], k2[p] * e_end[:, sl]], axis=0)
        upd = _dot(uv.T, bk)
        state_ref[p] = s0[p] * g_end[:, sl] + jnp.where(same_head, upd, 0.0)

    for p, sl in zip(pairs, sls):
        mean = _head_sum(y2[p], lo) * (1.0 / RWKV_HEAD_DIM)
        yc = y2[p] - mean
        var = _head_sum(yc * yc, lo) * (1.0 / RWKV_HEAD_DIM)
        yn = yc * lax.rsqrt(var + GN_EPS) * lnw_ref[:, sl] + lnb_ref[:, sl]
        bonus = _head_sum(r2[p] * k2[p] * rk_ref[:, sl], lo) * v2[p]
        o_ref[:, sl] = (yn + bonus) * gate[:, sl]


def _rwkv(proj, b, s, cw, wpad, params):
    (mu, w0, w2, a0, a2, g2, kk, ka, rk, lnw, lnb) = params
    C = RWKV_CHUNK
    nch = s // C
    pw, pa, pg = w2.shape[0], a2.shape[0], g2.shape[0]
    kern = functools.partial(_rwkv_kernel, cw=cw, pw=pw, pa=pa, pg=pg)
    row = lambda n: pl.BlockSpec((1, n), lambda bi, c: (0, 0))
    full = lambda a: pl.BlockSpec(a.shape, lambda bi, c: (0, 0))
    return pl.pallas_call(
        kern,
        out_shape=jax.ShapeDtypeStruct((b * s, cw), F32),
        grid=(b, nch),
        in_specs=[pl.BlockSpec((C, wpad), lambda bi, c: (bi * nch + c, 0)),
                  row(wpad), row(cw), full(w2), row(cw), full(a2), full(g2),
                  row(cw), row(cw), row(cw), row(cw), row(cw)],
        out_specs=pl.BlockSpec((C, cw), lambda bi, c: (bi * nch + c, 0)),
        scratch_shapes=[pltpu.VMEM((1, wpad), F32),
                        pltpu.VMEM((cw // LANES, LANES, LANES), F32)],
        compiler_params=pltpu.CompilerParams(
            dimension_semantics=("arbitrary", "arbitrary"), vmem_limit_bytes=VMEM_LIMIT),
        name="rwkv",
    )(proj, mu, w0, w2, a0, a2, g2, kk, ka, rk, lnw, lnb)


def _out_proj_kernel(att_ref, rw_ref, x_ref, ga_ref, woa_ref, wor_ref, g2_ref, wrh_ref, wrm_ref, br_ref,
                     x1_ref, hn_ref, route_ref):
    att = _rms(att_ref[...], ga_ref[...])
    y = (jnp.dot(att.astype(BF16), woa_ref[...], preferred_element_type=F32)
         + jnp.dot(rw_ref[...].astype(BF16), wor_ref[...], preferred_element_type=F32))
    x1 = x_ref[...] + y
    x1_ref[...] = x1
    hn = _rms(x1, g2_ref[...])
    _store_slabs(hn_ref, hn)

    hh, hm, _ = _split3(hn)
    d32 = functools.partial(jnp.dot, preferred_element_type=F32)
    logits = d32(hh, wrh_ref[...]) + (d32(hm, wrh_ref[...]) + d32(hh, wrm_ref[...])) + br_ref[...]
    lane = lax.broadcasted_iota(jnp.int32, logits.shape, 1)
    lanef = lane.astype(F32)
    big = float(LANES)

    is_g = lane < N_GROUPS
    gmax = jnp.max(jnp.where(is_g, logits, -jnp.inf), axis=-1, keepdims=True)
    gidx = jnp.min(jnp.where(is_g & (logits == gmax), lanef, big), axis=-1, keepdims=True)
    gsum = jnp.sum(jnp.where(is_g, jnp.exp(logits - gmax), 0.0), axis=-1, keepdims=True)
    g_gate = 1.0 / gsum

    e_lo = N_GROUPS + EXPERTS_PER_GROUP * gidx
    in_grp = (lanef >= e_lo) & (lanef < e_lo + EXPERTS_PER_GROUP)
    e1 = jnp.max(jnp.where(in_grp, logits, -jnp.inf), axis=-1, keepdims=True)
    i1 = jnp.min(jnp.where(in_grp & (logits == e1), lanef, big), axis=-1, keepdims=True)
    rest = in_grp & (lanef != i1)
    e2 = jnp.max(jnp.where(rest, logits, -jnp.inf), axis=-1, keepdims=True)
    i2 = jnp.min(jnp.where(rest & (logits == e2), lanef, big), axis=-1, keepdims=True)
    t = jnp.exp(e2 - e1)
    w1 = g_gate * (1.0 / (1.0 + t))
    w2 = g_gate * (t / (1.0 + t))
    route_ref[...] = jnp.where(lane == 0, i1 - N_GROUPS,
                     jnp.where(lane == 1, i2 - N_GROUPS,
                     jnp.where(lane == 2, w1,
                     jnp.where(lane == 3, w2, 0.0))))


def _out_proj(att, rw, x2, ga, woa, wor, g2, wrh, wrm, br, tm):
    t, d = x2.shape
    wa = att.shape[1]
    wr = rw.shape[1]
    sr = d // SLAB_LANES
    c = lambda shape: pl.BlockSpec(shape, lambda i: (0, 0))
    return pl.pallas_call(
        _out_proj_kernel,
        out_shape=(jax.ShapeDtypeStruct((t, d), F32), jax.ShapeDtypeStruct((t * sr, SLAB_LANES), F32),
                   jax.ShapeDtypeStruct((t, LANES), F32)),
        grid=(t // tm,),
        in_specs=[pl.BlockSpec((tm, wa), lambda i: (i, 0)),
                  pl.BlockSpec((tm, wr), lambda i: (i, 0)),
                  pl.BlockSpec((tm, d), lambda i: (i, 0)),
                  c((1, wa)), c((wa, d)), c((wr, d)), c((1, d)), c((d, LANES)), c((d, LANES)), c((1, LANES))],
        out_specs=(pl.BlockSpec((tm, d), lambda i: (i, 0)), pl.BlockSpec((tm * sr, SLAB_LANES), lambda i: (i, 0)),
                   pl.BlockSpec((tm, LANES), lambda i: (i, 0))),
        compiler_params=pltpu.CompilerParams(
            dimension_semantics=("arbitrary",), vmem_limit_bytes=VMEM_LIMIT),
        name="out_proj",
    )(att, rw, x2, ga, woa, wor, g2, wrh, wrm, br)


def _store_slabs(ref, x):
    rows, d = x.shape
    sr = d // SLAB_LANES
    for s in range(sr):
        ref[pl.ds(s, rows, stride=sr), :] = x[:, s * SLAB_LANES:(s + 1) * SLAB_LANES]


def _load_slabs(ref, slot, first, rows, sr):
    return jnp.concatenate([ref[slot, pl.ds(first * sr + s, rows, stride=sr), :] for s in range(sr)], axis=1)


def _gather_rows(i, nt, idx_hbm, src_hbm, idx_ref, buf_ref, isem, gsem, rows, sr):
    def idx_copy(tile, slot):
        return pltpu.make_async_copy(idx_hbm.at[pl.ds(tile, 1)], idx_ref.at[pl.ds(slot, 1)], isem.at[slot])

    def start_rows(slot):
        for r in range(rows):
            src = pl.multiple_of(idx_ref[slot, r], sr)
            pltpu.make_async_copy(src_hbm.at[pl.ds(src, sr)], buf_ref.at[slot, pl.ds(r * sr, sr)],
                                  gsem.at[slot]).start()

    slot = jnp.bitwise_and(i, 1)

    @pl.when(i == 0)
    def _():
        idx_copy(0, 0).start()
        idx_copy(0, 0).wait()
        start_rows(0)
        if nt > 1:
            idx_copy(1, 1).start()

    @pl.when(i + 1 < nt)
    def _():
        idx_copy(i + 1, 1 - slot).wait()
        start_rows(1 - slot)

    @pl.when(i + 2 < nt)
    def _():
        idx_copy(i + 2, slot).start()

    pltpu.make_async_copy(src_hbm.at[pl.ds(0, rows * sr)], buf_ref.at[slot], gsem.at[slot]).wait()
    return slot


def _expert_kernel(te_ref, nu_ref, rows_hbm, x_hbm, w1_ref, w3_ref, w2_ref, o_ref,
                   idx_ref, xbuf_ref, isem, gsem, *, nt, sr):
    i = pl.program_id(0)
    tm = EXPERT_ROWS
    slot = _gather_rows(i, nt, rows_hbm, x_hbm, idx_ref, xbuf_ref, isem, gsem, tm, sr)

    @pl.when(i < nu_ref[0])
    def _():
        xb = _load_slabs(xbuf_ref, slot, 0, tm, sr).astype(BF16)
        h1 = jnp.dot(xb, w1_ref[0], preferred_element_type=F32)
        h3 = jnp.dot(xb, w3_ref[0], preferred_element_type=F32)
        hid = (h1 * _sigmoid(h1)) * h3
        _store_slabs(o_ref, jnp.dot(hid.astype(BF16), w2_ref[0], preferred_element_type=F32))

    @pl.when(i >= nu_ref[0])
    def _():
        o_ref[...] = jnp.zeros_like(o_ref)


def _experts(tile_e, n_used, row_src, hn_slabs, w1, w3, w2):
    nt = tile_e.shape[0]
    d, de = w1.shape[1], w1.shape[2]
    sr = d // SLAB_LANES
    tm = EXPERT_ROWS
    wspec = lambda shape: pl.BlockSpec(shape, lambda i, te, nu: (te[i], 0, 0))
    grid_spec = pltpu.PrefetchScalarGridSpec(
        num_scalar_prefetch=2,
        grid=(nt,),
        in_specs=[pl.BlockSpec(memory_space=pl.ANY),
                  pl.BlockSpec(memory_space=pl.ANY),
                  wspec((1, d, de)), wspec((1, d, de)), wspec((1, de, d))],
        out_specs=pl.BlockSpec((tm * sr, SLAB_LANES), lambda i, te, nu: (i, 0)),
        scratch_shapes=[pltpu.SMEM((2, tm), jnp.int32),
                        pltpu.VMEM((2, tm * sr, SLAB_LANES), F32),
                        pltpu.SemaphoreType.DMA((2,)), pltpu.SemaphoreType.DMA((2,))],
    )
    return pl.pallas_call(
        functools.partial(_expert_kernel, nt=nt, sr=sr),
        out_shape=jax.ShapeDtypeStruct((nt * tm * sr, SLAB_LANES), F32),
        grid_spec=grid_spec,
        compiler_params=pltpu.CompilerParams(
            dimension_semantics=("arbitrary",), vmem_limit_bytes=VMEM_LIMIT),
        name="experts",
    )(tile_e, n_used, row_src, hn_slabs, w1, w3, w2)


def _combine_kernel(dest_hbm, y_hbm, x1_ref, route_ref, fg_ref, o_ref, idx_ref, buf_ref, isem, gsem,
                    *, nt, sr):
    i = pl.program_id(0)
    tg = COMBINE_ROWS
    slot = _gather_rows(i, nt, dest_hbm, y_hbm, idx_ref, buf_ref, isem, gsem, TOP_K * tg, sr)
    route = route_ref[...]
    y = (_load_slabs(buf_ref, slot, 0, tg, sr) * route[:, 2:3]
         + _load_slabs(buf_ref, slot, tg, tg, sr) * route[:, 3:4])
    o_ref[...] = _rms(x1_ref[...] + y, fg_ref[...])


def _combine(dest, ybuf, x1, route, fg):
    t, d = x1.shape
    tg = COMBINE_ROWS
    sr = d // SLAB_LANES
    return pl.pallas_call(
        functools.partial(_combine_kernel, nt=t // tg, sr=sr),
        out_shape=jax.ShapeDtypeStruct((t, d), F32),
        grid=(t // tg,),
        in_specs=[pl.BlockSpec(memory_space=pl.ANY),
                  pl.BlockSpec(memory_space=pl.ANY),
                  pl.BlockSpec((tg, d), lambda i: (i, 0)),
                  pl.BlockSpec((tg, LANES), lambda i: (i, 0)),
                  pl.BlockSpec((1, d), lambda i: (0, 0))],
        out_specs=pl.BlockSpec((tg, d), lambda i: (i, 0)),
        scratch_shapes=[pltpu.SMEM((2, TOP_K * tg), jnp.int32),
                        pltpu.VMEM((2, TOP_K * tg * sr, SLAB_LANES), F32),
                        pltpu.SemaphoreType.DMA((2,)), pltpu.SemaphoreType.DMA((2,))],
        compiler_params=pltpu.CompilerParams(
            dimension_semantics=("arbitrary",), vmem_limit_bytes=VMEM_LIMIT),
        name="combine",
    )(dest, ybuf, x1, route, fg)


def _pad_cols(m, n):
    return jnp.pad(m, ((0, 0), (0, n - m.shape[1])))


def _pad_rows(m, n):
    return jnp.pad(m, ((0, n - m.shape[0]), (0, 0)))


def _pick_tile(n, cap):
    best = LANES
    for c in range(LANES, cap + 1, LANES):
        if n % c == 0:
            best = c
    return best


def _dispatch_plan(experts, n_tok):
    tm = EXPERT_ROWS
    n_assign = n_tok * TOP_K
    nt = n_assign // tm + N_EXPERTS
    e_flat = experts.reshape(n_assign)
    onehot = (e_flat[:, None] == jnp.arange(N_EXPERTS, dtype=jnp.int32)[None, :]).astype(jnp.int32)
    csum = jnp.cumsum(onehot, axis=0)
    rank = jnp.take_along_axis(csum, e_flat[:, None], axis=1)[:, 0] - 1
    counts = csum[-1]
    tiles_e = (counts + tm - 1) // tm
    tile_end = jnp.cumsum(tiles_e)
    tile_start = tile_end - tiles_e
    dest = tile_start[e_flat] * tm + rank
    n_used = tile_end[-1:].astype(jnp.int32)
    tile_e = jnp.minimum(jnp.searchsorted(tile_end, jnp.arange(nt, dtype=jnp.int32), side="right"),
                         N_EXPERTS - 1).astype(jnp.int32)
    tok = jnp.arange(n_assign, dtype=jnp.int32) // TOP_K
    row_tok = jnp.zeros((nt * tm,), jnp.int32).at[dest].set(tok)
    return tile_e, n_used, row_tok.reshape(nt, tm), dest.astype(jnp.int32)


def _layer(x2, b, s, norm1_g, w_in, attn_out_g, mu, w0, w2, a0, a2, g2, k_k, k_a, r_k, lnx_w, lnx_b,
           w_out, norm2_g, wg, bg, we, be, ew1, ew3, ew2, final_g):
    t, d = x2.shape
    aw = attn_out_g.shape[0]
    cw = w0.shape[0]
    heads = aw // ATT_HEAD_DIM
    dl, al, gl = w2.shape[0], a2.shape[0], g2.shape[0]
    pw, pa, pg = _round_up(dl, LANES), _round_up(al, LANES), _round_up(gl, LANES)
    wpad = 3 * cw + pw + pa + pg

    sh = w_in[:, 3 * aw:]
    o = 3 * cw
    w_arr = jnp.concatenate(
        [sh[:, :o], _pad_cols(sh[:, o:o + dl], pw), _pad_cols(sh[:, o + dl:o + dl + al], pa),
         _pad_cols(sh[:, o + dl + al:o + dl + al + gl], pg), w_in[:, :3 * aw]], axis=1).astype(BF16)
    mu_arr = jnp.concatenate(
        [mu[:o], jnp.pad(mu[o:o + dl], (0, pw - dl)), jnp.pad(mu[o + dl:o + dl + al], (0, pa - al)),
         jnp.pad(mu[o + dl + al:], (0, pg - gl))])[None, :]

    n_all = w_arr.shape[1]
    proj = _in_proj(x2, norm1_g[None, :], w_arr, tm=min(512, t), tn=_pick_tile(n_all, 2048))

    att = _moba(proj, b, s, heads, wpad // LANES)
    r1 = lambda a: a.reshape(1, -1)
    rw = _rwkv(proj, b, s, cw, wpad,
               (mu_arr, r1(w0), _pad_rows(w2, pw).astype(BF16), r1(a0), _pad_rows(a2, pa).astype(BF16),
                _pad_rows(g2, pg).astype(BF16), r1(k_k), r1(k_a), r1(r_k), r1(lnx_w), r1(lnx_b)))

    wr = _pad_cols(jnp.concatenate([wg, we], axis=1), LANES)
    wrh = wr.astype(BF16)
    wrm = (wr - wrh.astype(F32)).astype(BF16)
    br = jnp.pad(jnp.concatenate([bg, be]), (0, LANES - N_GROUPS - N_EXPERTS))[None, :]
    wo = w_out.astype(BF16)
    x1, hn, route = _out_proj(att, rw, x2, r1(attn_out_g), wo[:aw], wo[aw:], r1(norm2_g), wrh, wrm, br,
                              tm=min(256, t))

    experts = route[:, :TOP_K].astype(jnp.int32)
    tile_e, n_used, row_tok, dest = _dispatch_plan(experts, t)
    sr = d // SLAB_LANES
    ybuf = _experts(tile_e, n_used, row_tok * sr, hn, ew1.astype(BF16), ew3.astype(BF16), ew2.astype(BF16))

    tg = COMBINE_ROWS
    dest_tiles = dest.reshape(t // tg, tg, TOP_K).transpose(0, 2, 1).reshape(t // tg, TOP_K * tg)
    return _combine(dest_tiles * sr, ybuf, x1, route, r1(final_g))


def kernel(x, norm1_g, w_in, attn_out_g, rwkv_mu, rwkv_w0, rwkv_w2, rwkv_a0, rwkv_a2, rwkv_g2, rwkv_k_k,
           rwkv_k_a, rwkv_r_k, rwkv_lnx_w, rwkv_lnx_b, w_out, norm2_g, router_group_w, router_group_b,
           router_expert_w, router_expert_b, expert_w1, expert_w3, expert_w2, final_g):
    b, s, d = x.shape
    assert norm1_g.shape[0] == 1, "single-layer block"
    assert s % MOBA_BLOCK == 0 and s % RWKV_CHUNK == 0
    assert (b * s) % max(EXPERT_ROWS, COMBINE_ROWS) == 0
    out = _layer(x.reshape(b * s, d), b, s, norm1_g[0], w_in[0], attn_out_g[0], rwkv_mu[0], rwkv_w0[0],
                 rwkv_w2[0], rwkv_a0[0], rwkv_a2[0], rwkv_g2[0], rwkv_k_k[0], rwkv_k_a[0], rwkv_r_k[0],
                 rwkv_lnx_w[0], rwkv_lnx_b[0], w_out[0], norm2_g[0], router_group_w[0], router_group_b[0],
                 router_expert_w[0], router_expert_b[0], expert_w1[0], expert_w3[0], expert_w2[0], final_g)
    return out.reshape(b, s, d)
```

```python
import functools
import math

import jax
import jax.numpy as jnp
from jax import lax
from jax.experimental import pallas as pl
from jax.experimental.pallas import tpu as pltpu

F32 = jnp.float32
BF16 = jnp.bfloat16

LANES = 128
MXU_WIDTH = 256
SLAB_LANES = 128
ATT_HEAD_DIM = 128
RWKV_HEAD_DIM = 64
MOBA_BLOCK = 256
MOBA_TOPK = 3
N_GROUPS = 4
EXPERTS_PER_GROUP = 8
N_EXPERTS = N_GROUPS * EXPERTS_PER_GROUP
TOP_K = 2
RMS_EPS = 1e-6
GN_EPS = 64e-5
NEG_BIG = -1e30
RWKV_CHUNK = 64
EXPERT_ROWS = 256
COMBINE_ROWS = 256
VMEM_LIMIT = 56 * 1024 * 1024
EXPERT_VMEM_LIMIT = 60 * 1024 * 1024
WEIGHT_CAST_CHUNKS = 16


def _round_up(n, m):
    return (n + m - 1) // m * m


def _dot(a, b):
    return jnp.dot(a.astype(BF16), b.astype(BF16), preferred_element_type=F32)


def _dot_nt(a, b):
    return lax.dot_general(a.astype(BF16), b.astype(BF16), (((1,), (1,)), ((), ())),
                           preferred_element_type=F32)


def _split3(a):
    hi = a.astype(BF16)
    r1 = a - hi.astype(F32)
    mid = r1.astype(BF16)
    lo = (r1 - mid.astype(F32)).astype(BF16)
    return hi, mid, lo


def _dot_hi(a, b):
    ah, am, _ = _split3(a)
    bh, bm, _ = _split3(b)
    d = functools.partial(jnp.dot, preferred_element_type=F32)
    return d(ah, bh) + (d(am, bh) + d(ah, bm))


def _sigmoid(x):
    return 1.0 / (1.0 + jnp.exp(-x))


def _rms(x, g):
    ms = jnp.mean(x * x, axis=-1, keepdims=True)
    return x * lax.rsqrt(ms + RMS_EPS) * g


def _in_proj_kernel(x_ref, g_ref, w_ref, o_ref):
    xn = _rms(x_ref[...], g_ref[...]).astype(BF16)
    o_ref[...] = jnp.dot(xn, w_ref[...], preferred_element_type=F32)


def _in_proj(x2, g, w, tm, tn):
    t, d = x2.shape
    n = w.shape[1]
    return pl.pallas_call(
        _in_proj_kernel,
        out_shape=jax.ShapeDtypeStruct((t, n), F32),
        grid=(n // tn, t // tm),
        in_specs=[pl.BlockSpec((tm, d), lambda j, i: (i, 0)),
                  pl.BlockSpec((1, d), lambda j, i: (0, 0)),
                  pl.BlockSpec((d, tn), lambda j, i: (0, j))],
        out_specs=pl.BlockSpec((tm, tn), lambda j, i: (i, j)),
        compiler_params=pltpu.CompilerParams(
            dimension_semantics=("arbitrary", "arbitrary"), vmem_limit_bytes=VMEM_LIMIT),
        name="in_proj",
    )(x2, g, w)


def _moba_kernel(q_ref, k_ref, v_ref, o_ref, *, nblk, scale):
    bs = MOBA_BLOCK
    k = k_ref[...]
    kb = k.astype(BF16)
    vtb = v_ref[...].T.astype(BF16)
    qt = q_ref[...].T
    qtb = qt.astype(BF16)
    kmean = jnp.concatenate(
        [jnp.sum(k[n * bs:(n + 1) * bs, :], axis=0, keepdims=True) for n in range(nblk)], axis=0) * (1.0 / bs)
    gate_all = _dot_hi(kmean, qt)
    kpos = lax.broadcasted_iota(jnp.int32, (bs, bs), 0)
    qpos = lax.broadcasted_iota(jnp.int32, (bs, bs), 1)
    causal = kpos <= qpos

    for qi in range(nblk):
        qs = slice(qi * bs, (qi + 1) * bs)
        nk = (qi + 1) * bs
        s_all = jnp.dot(kb[0:nk, :], qtb[:, qs], preferred_element_type=F32) * scale
        blocks = []
        if qi > MOBA_TOPK:
            gate = gate_all[0:qi, qs]
            blk = lax.broadcasted_iota(jnp.int32, (qi, bs), 0)
        for n in range(qi):
            sn = s_all[n * bs:(n + 1) * bs, :]
            if qi > MOBA_TOPK:
                gn = gate[n:n + 1, :]
                beats = (gate > gn) | ((gate == gn) & (blk < n))
                rank = jnp.sum(jnp.where(beats, 1.0, 0.0), axis=0, keepdims=True)
                sn = jnp.where(rank < float(MOBA_TOPK), sn, NEG_BIG)
            blocks.append(sn)
        blocks.append(jnp.where(causal, s_all[qi * bs:nk, :], NEG_BIG))
        m = functools.reduce(jnp.maximum, [jnp.max(x, axis=0, keepdims=True) for x in blocks])
        ps = [jnp.exp(x - m) for x in blocks]
        l = functools.reduce(jnp.add, [jnp.sum(x, axis=0, keepdims=True) for x in ps])
        pb = jnp.concatenate([x.astype(BF16) for x in ps], axis=0)
        acc = jnp.dot(vtb[:, 0:nk], pb, preferred_element_type=F32)
        o_ref[qs, :] = (acc / l).T


def _moba(proj, b, s, heads, col0):
    d = ATT_HEAD_DIM
    nblk = s // MOBA_BLOCK
    kern = functools.partial(_moba_kernel, nblk=nblk, scale=1.0 / math.sqrt(d))
    return pl.pallas_call(
        kern,
        out_shape=jax.ShapeDtypeStruct((b * s, heads * d), F32),
        grid=(b, heads),
        in_specs=[pl.BlockSpec((s, d), lambda bi, h: (bi, col0 + h)),
                  pl.BlockSpec((s, d), lambda bi, h: (bi, col0 + heads + h)),
                  pl.BlockSpec((s, d), lambda bi, h: (bi, col0 + 2 * heads + h))],
        out_specs=pl.BlockSpec((s, d), lambda bi, h: (bi, h)),
        compiler_params=pltpu.CompilerParams(
            dimension_semantics=("arbitrary", "arbitrary"), vmem_limit_bytes=VMEM_LIMIT),
        name="moba",
    )(proj, proj, proj)


def _head_sum(x, lo):
    s0 = jnp.sum(jnp.where(lo, x, 0.0), axis=-1, keepdims=True)
    s1 = jnp.sum(jnp.where(lo, 0.0, x), axis=-1, keepdims=True)
    return jnp.where(lo, s0, s1)


def _stack_heads(x, lo):
    return jnp.concatenate([jnp.where(lo, x, 0.0), jnp.where(lo, 0.0, x)], axis=0)


def _rwkv_kernel(p_ref, mu_ref, w0_ref, w2_ref, a0_ref, a2_ref, g2_ref, kk_ref, ka_ref, rk_ref,
                 lnw_ref, lnb_ref, o_ref, carry_ref, state_ref, *, cw, pw, pa, pg):
    c = pl.program_id(1)
    C = RWKV_CHUNK
    C2 = 2 * C

    @pl.when(c == 0)
    def _():
        carry_ref[...] = jnp.zeros_like(carry_ref)
        state_ref[...] = jnp.zeros_like(state_ref)

    pr = p_ref[...]
    row = lax.broadcasted_iota(jnp.int32, pr.shape, 0)
    prev = jnp.where(row == 0, carry_ref[...], pltpu.roll(pr, 1, 0))
    carry_ref[...] = pr[C - 1:C, :]
    ps = pr + (prev - pr) * mu_ref[...]

    r = ps[:, 0:cw]
    k = ps[:, cw:2 * cw]
    v = ps[:, 2 * cw:3 * cw]
    o = 3 * cw
    xw = ps[:, o:o + pw]
    xa = ps[:, o + pw:o + pw + pa]
    xg = ps[:, o + pw + pa:o + pw + pa + pg]

    z = -(w0_ref[...] + _dot(jnp.tanh(xw), w2_ref[...]))
    softplus = jnp.maximum(z, 0.0) + jnp.log(1.0 + jnp.exp(-jnp.abs(z)))
    logw = -jnp.exp(-softplus - 0.5)
    asig = _sigmoid(a0_ref[...] + _dot(xa, a2_ref[...]))
    gate = _dot(_sigmoid(xg), g2_ref[...])
    kkr = k * kk_ref[...]
    kmod = k * (1.0 + (asig - 1.0) * ka_ref[...])

    ti = lax.broadcasted_iota(jnp.int32, (C, C), 0)
    si = lax.broadcasted_iota(jnp.int32, (C, C), 1)
    tril = jnp.where(si <= ti, 1.0, 0.0).astype(BF16)
    lh, lm, ll = _split3(logw)
    d32 = functools.partial(jnp.dot, preferred_element_type=F32)
    cum = d32(tril, lh) + (d32(tril, lm) + d32(tril, ll))
    cum_end = cum[C - 1:C, :]
    e_prev = jnp.exp(cum - logw)
    e_incl = jnp.exp(cum)
    e_inv = jnp.exp(-cum)
    e_end = jnp.exp(cum_end - cum)
    g_end = jnp.exp(cum_end)

    lane = lax.broadcasted_iota(jnp.int32, (C, LANES), 1)
    lo = lane < RWKV_HEAD_DIM
    rho = lax.broadcasted_iota(jnp.int32, (C2, C2), 0)
    sig = lax.broadcasted_iota(jnp.int32, (C2, C2), 1)
    same_head = (rho >= C) == (sig >= C)
    strict = same_head & ((sig % C) < (rho % C))
    incl = same_head & ((sig % C) <= (rho % C))
    eye = jnp.where(rho == sig, 1.0, 0.0)

    pairs = range(cw // LANES)
    sls = [slice(p * LANES, (p + 1) * LANES) for p in pairs]
    incl2 = jnp.concatenate([incl, incl], axis=1)
    strict2 = jnp.concatenate([strict, strict], axis=1)

    b2, r2, k2, v2, at, rt, v2s, gm = [], [], [], [], [], [], [], []
    for sl in sls:
        kk2 = kkr[:, sl]
        nrm = jnp.sqrt(_head_sum(kk2 * kk2, lo))
        kkn = kk2 / jnp.maximum(nrm, 1e-12)
        b2.append(kkn * asig[:, sl])
        r2.append(r[:, sl])
        k2.append(kmod[:, sl])
        v2.append(v[:, sl])
        at.append(-kkn * e_prev[:, sl])
        rt.append(r2[-1] * e_incl[:, sl])
        bt = b2[-1] * e_inv[:, sl]
        kt = k2[-1] * e_inv[:, sl]
        v2s.append(_stack_heads(v2[-1], lo))
        lhs = jnp.concatenate([_stack_heads(at[-1], lo), _stack_heads(rt[-1], lo)], axis=0)
        rhs = jnp.concatenate([bt, bt, kt, kt], axis=0)
        gm.append(_dot_nt(lhs, rhs))

    s0 = [state_ref[p] for p in pairs]
    ars = [_dot_nt(jnp.concatenate([at[p], rt[p]], axis=0), s0[p]) for p in pairs]
    labk = [jnp.where(strict2, g[0:C2, :], 0.0) for g in gm]
    mrbk = [jnp.where(incl2, g[C2:2 * C2, :], 0.0) for g in gm]
    rhs_u = [_stack_heads(ars[p][0:C], lo) + _dot(labk[p][:, C2:2 * C2], v2s[p]) for p in pairs]

    pw2 = [m[:, 0:C2] for m in labk]
    tinv = [eye + m for m in pw2]
    for _ in range(int(math.log2(C)) - 1):
        pw2 = [_dot(m, m) for m in pw2]
        tinv = [t + _dot(t, m) for t, m in zip(tinv, pw2)]

    u2s = [_dot(t, x) for t, x in zip(tinv, rhs_u)]
    y2s = [_stack_heads(ars[p][C:C2], lo) + _dot(mrbk[p], jnp.concatenate([u2s[p], v2s[p]], axis=0))
           for p in pairs]
    u2 = [m[0:C] + m[C:C2] for m in u2s]
    y2 = [m[0:C] + m[C:C2] for m in y2s]

    for p, sl in zip(pairs, sls):
        uv = jnp.concatenate([u2[p], v2[p]], axis=0)
        bk = jnp.concatenate([b2[p] * e_end[:, sl], k2[p] * e_end[:, sl]], axis=0)
        upd = _dot(uv.T, bk)
        state_ref[p] = s0[p] * g_end[:, sl] + jnp.where(same_head, upd, 0.0)

    for p, sl in zip(pairs, sls):
        mean = _head_sum(y2[p], lo) * (1.0 / RWKV_HEAD_DIM)
        yc = y2[p] - mean
        var = _head_sum(yc * yc, lo) * (1.0 / RWKV_HEAD_DIM)
        yn = yc * lax.rsqrt(var + GN_EPS) * lnw_ref[:, sl] + lnb_ref[:, sl]
        bonus = _head_sum(r2[p] * k2[p] * rk_ref[:, sl], lo) * v2[p]
        o_ref[:, sl] = (yn + bonus) * gate[:, sl]


def _rwkv(proj, b, s, cw, wpad, params):
    (mu, w0, w2, a0, a2, g2, kk, ka, rk, lnw, lnb) = params
    C = RWKV_CHUNK
    nch = s // C
    pw, pa, pg = w2.shape[0], a2.shape[0], g2.shape[0]
    kern = functools.partial(_rwkv_kernel, cw=cw, pw=pw, pa=pa, pg=pg)
    row = lambda n: pl.BlockSpec((1, n), lambda bi, c: (0, 0))
    full = lambda a: pl.BlockSpec(a.shape, lambda bi, c: (0, 0))
    return pl.pallas_call(
        kern,
        out_shape=jax.ShapeDtypeStruct((b * s, cw), F32),
        grid=(b, nch),
        in_specs=[pl.BlockSpec((C, wpad), lambda bi, c: (bi * nch + c, 0)),
                  row(wpad), row(cw), full(w2), row(cw), full(a2), full(g2),
                  row(cw), row(cw), row(cw), row(cw), row(cw)],
        out_specs=pl.BlockSpec((C, cw), lambda bi, c: (bi * nch + c, 0)),
        scratch_shapes=[pltpu.VMEM((1, wpad), F32),
                        pltpu.VMEM((cw // LANES, LANES, LANES), F32)],
        compiler_params=pltpu.CompilerParams(
            dimension_semantics=("arbitrary", "arbitrary"), vmem_limit_bytes=VMEM_LIMIT),
        name="rwkv",
    )(proj, mu, w0, w2, a0, a2, g2, kk, ka, rk, lnw, lnb)


def _out_proj_kernel(att_ref, rw_ref, x_ref, ga_ref, woa_ref, wor_ref, g2_ref, wrh_ref, wrm_ref, br_ref,
                     x1_ref, hn_ref, route_ref):
    att = _rms(att_ref[...], ga_ref[...])
    y = (jnp.dot(att.astype(BF16), woa_ref[...], preferred_element_type=F32)
         + jnp.dot(rw_ref[...].astype(BF16), wor_ref[...], preferred_element_type=F32))
    x1 = x_ref[...] + y
    x1_ref[...] = x1
    hn = _rms(x1, g2_ref[...])
    _store_slabs(hn_ref, hn)

    hh, hm, _ = _split3(hn)
    d32 = functools.partial(jnp.dot, preferred_element_type=F32)
    logits = d32(hh, wrh_ref[...]) + (d32(hm, wrh_ref[...]) + d32(hh, wrm_ref[...])) + br_ref[...]
    lane = lax.broadcasted_iota(jnp.int32, logits.shape, 1)
    lanef = lane.astype(F32)
    big = float(LANES)

    is_g = lane < N_GROUPS
    gmax = jnp.max(jnp.where(is_g, logits, -jnp.inf), axis=-1, keepdims=True)
    gidx = jnp.min(jnp.where(is_g & (logits == gmax), lanef, big), axis=-1, keepdims=True)
    gsum = jnp.sum(jnp.where(is_g, jnp.exp(logits - gmax), 0.0), axis=-1, keepdims=True)
    g_gate = 1.0 / gsum

    e_lo = N_GROUPS + EXPERTS_PER_GROUP * gidx
    in_grp = (lanef >= e_lo) & (lanef < e_lo + EXPERTS_PER_GROUP)
    e1 = jnp.max(jnp.where(in_grp, logits, -jnp.inf), axis=-1, keepdims=True)
    i1 = jnp.min(jnp.where(in_grp & (logits == e1), lanef, big), axis=-1, keepdims=True)
    rest = in_grp & (lanef != i1)
    e2 = jnp.max(jnp.where(rest, logits, -jnp.inf), axis=-1, keepdims=True)
    i2 = jnp.min(jnp.where(rest & (logits == e2), lanef, big), axis=-1, keepdims=True)
    t = jnp.exp(e2 - e1)
    w1 = g_gate * (1.0 / (1.0 + t))
    w2 = g_gate * (t / (1.0 + t))
    route_ref[...] = jnp.where(lane == 0, i1 - N_GROUPS,
                     jnp.where(lane == 1, i2 - N_GROUPS,
                     jnp.where(lane == 2, w1,
                     jnp.where(lane == 3, w2, 0.0))))


def _out_proj(att, rw, x2, ga, woa, wor, g2, wrh, wrm, br, tm):
    t, d = x2.shape
    wa = att.shape[1]
    wr = rw.shape[1]
    sr = d // SLAB_LANES
    c = lambda shape: pl.BlockSpec(shape, lambda i: (0, 0))
    return pl.pallas_call(
        _out_proj_kernel,
        out_shape=(jax.ShapeDtypeStruct((t, d), F32), jax.ShapeDtypeStruct((t * sr, SLAB_LANES), F32),
                   jax.ShapeDtypeStruct((t, LANES), F32)),
        grid=(t // tm,),
        in_specs=[pl.BlockSpec((tm, wa), lambda i: (i, 0)),
                  pl.BlockSpec((tm, wr), lambda i: (i, 0)),
                  pl.BlockSpec((tm, d), lambda i: (i, 0)),
                  c((1, wa)), c((wa, d)), c((wr, d)), c((1, d)), c((d, LANES)), c((d, LANES)), c((1, LANES))],
        out_specs=(pl.BlockSpec((tm, d), lambda i: (i, 0)), pl.BlockSpec((tm * sr, SLAB_LANES), lambda i: (i, 0)),
                   pl.BlockSpec((tm, LANES), lambda i: (i, 0))),
        compiler_params=pltpu.CompilerParams(
            dimension_semantics=("arbitrary",), vmem_limit_bytes=VMEM_LIMIT),
        name="out_proj",
    )(att, rw, x2, ga, woa, wor, g2, wrh, wrm, br)


def _store_slabs(ref, x):
    rows, d = x.shape
    sr = d // SLAB_LANES
    for s in range(sr):
        ref[pl.ds(s, rows, stride=sr), :] = x[:, s * SLAB_LANES:(s + 1) * SLAB_LANES]


def _load_slabs(ref, slot, first, rows, sr):
    return jnp.concatenate([ref[slot, pl.ds(first * sr + s, rows, stride=sr), :] for s in range(sr)], axis=1)


def _gather_rows(i, nt, idx_hbm, src_hbm, idx_ref, buf_ref, isem, gsem, rows, sr):
    def idx_copy(tile, slot):
        return pltpu.make_async_copy(idx_hbm.at[pl.ds(tile, 1)], idx_ref.at[pl.ds(slot, 1)], isem.at[slot])

    def start_rows(slot):
        for r in range(rows):
            src = pl.multiple_of(idx_ref[slot, r], sr)
            pltpu.make_async_copy(src_hbm.at[pl.ds(src, sr)], buf_ref.at[slot, pl.ds(r * sr, sr)],
                                  gsem.at[slot]).start()

    slot = jnp.bitwise_and(i, 1)

    @pl.when(i == 0)
    def _():
        idx_copy(0, 0).start()
        idx_copy(0, 0).wait()
        start_rows(0)
        if nt > 1:
            idx_copy(1, 1).start()

    @pl.when(i + 1 < nt)
    def _():
        idx_copy(i + 1, 1 - slot).wait()
        start_rows(1 - slot)

    @pl.when(i + 2 < nt)
    def _():
        idx_copy(i + 2, slot).start()

    pltpu.make_async_copy(src_hbm.at[pl.ds(0, rows * sr)], buf_ref.at[slot], gsem.at[slot]).wait()
    return slot


def _expert_kernel(te_ref, first_ref, next_ref, nu_ref, rows_hbm, x_hbm, w1_hbm, w3_hbm, w2_hbm, o_ref,
                   idx_ref, xbuf_ref, isem, gsem, stage_refs, wb_refs, wsem, *, nt, sr):
    i = pl.program_id(0)
    tm = EXPERT_ROWS
    slot = _gather_rows(i, nt, rows_hbm, x_hbm, idx_ref, xbuf_ref, isem, gsem, tm, sr)
    w_hbm = (w1_hbm, w3_hbm, w2_hbm)
    used = i < nu_ref[0]

    def weight_copies(e):
        return [pltpu.make_async_copy(w.at[e], st, wsem.at[j]) for j, (w, st) in enumerate(zip(w_hbm, stage_refs))]

    @pl.when(i == 0)
    def _():
        for cp in weight_copies(te_ref[0]):
            cp.start()

    @pl.when(used & (first_ref[i] == 1))
    def _():
        for cp in weight_copies(te_ref[i]):
            cp.wait()
        for st, wb in zip(stage_refs, wb_refs):
            rows = st.shape[0] // WEIGHT_CAST_CHUNKS

            def cast_chunk(c, carry, st=st, wb=wb, rows=rows):
                r0 = pl.multiple_of(c * rows, rows)
                wb[pl.ds(r0, rows), :] = st[pl.ds(r0, rows), :].astype(BF16)
                return carry

            lax.fori_loop(0, WEIGHT_CAST_CHUNKS, cast_chunk, 0)

        @pl.when(next_ref[i] >= 0)
        def _():
            for cp in weight_copies(next_ref[i]):
                cp.start()

    @pl.when(used)
    def _():
        xb = _load_slabs(xbuf_ref, slot, 0, tm, sr).astype(BF16)
        h1 = jnp.dot(xb, wb_refs[0][...], preferred_element_type=F32)
        h3 = jnp.dot(xb, wb_refs[1][...], preferred_element_type=F32)
        hid = (h1 * _sigmoid(h1)) * h3
        _store_slabs(o_ref, jnp.dot(hid.astype(BF16), wb_refs[2][...], preferred_element_type=F32))

    @pl.when(jnp.logical_not(used))
    def _():
        o_ref[...] = jnp.zeros_like(o_ref)


def _experts(tile_e, tile_first, tile_next, n_used, row_src, hn_slabs, w1, w3, w2):
    nt = tile_e.shape[0]
    d, de = w1.shape[1], w1.shape[2]
    sr = d // SLAB_LANES
    tm = EXPERT_ROWS
    hbm = pl.BlockSpec(memory_space=pl.ANY)
    grid_spec = pltpu.PrefetchScalarGridSpec(
        num_scalar_prefetch=4,
        grid=(nt,),
        in_specs=[hbm, hbm, hbm, hbm, hbm],
        out_specs=pl.BlockSpec((tm * sr, SLAB_LANES), lambda i, *_: (i, 0)),
        scratch_shapes=[pltpu.SMEM((2, tm), jnp.int32),
                        pltpu.VMEM((2, tm * sr, SLAB_LANES), F32),
                        pltpu.SemaphoreType.DMA((2,)), pltpu.SemaphoreType.DMA((2,)),
                        [pltpu.VMEM((d, de), F32), pltpu.VMEM((d, de), F32), pltpu.VMEM((de, d), F32)],
                        [pltpu.VMEM((d, de), BF16), pltpu.VMEM((d, de), BF16), pltpu.VMEM((de, d), BF16)],
                        pltpu.SemaphoreType.DMA((3,))],
    )
    return pl.pallas_call(
        functools.partial(_expert_kernel, nt=nt, sr=sr),
        out_shape=jax.ShapeDtypeStruct((nt * tm * sr, SLAB_LANES), F32),
        grid_spec=grid_spec,
        compiler_params=pltpu.CompilerParams(
            dimension_semantics=("arbitrary",), vmem_limit_bytes=EXPERT_VMEM_LIMIT),
        name="experts",
    )(tile_e, tile_first, tile_next, n_used, row_src, hn_slabs, w1, w3, w2)


def _combine_kernel(dest_hbm, y_hbm, x1_ref, route_ref, fg_ref, o_ref, idx_ref, buf_ref, isem, gsem,
                    *, nt, sr):
    i = pl.program_id(0)
    tg = COMBINE_ROWS
    slot = _gather_rows(i, nt, dest_hbm, y_hbm, idx_ref, buf_ref, isem, gsem, TOP_K * tg, sr)
    route = route_ref[...]
    y = (_load_slabs(buf_ref, slot, 0, tg, sr) * route[:, 2:3]
         + _load_slabs(buf_ref, slot, tg, tg, sr) * route[:, 3:4])
    o_ref[...] = _rms(x1_ref[...] + y, fg_ref[...])


def _combine(dest, ybuf, x1, route, fg):
    t, d = x1.shape
    tg = COMBINE_ROWS
    sr = d // SLAB_LANES
    return pl.pallas_call(
        functools.partial(_combine_kernel, nt=t // tg, sr=sr),
        out_shape=jax.ShapeDtypeStruct((t, d), F32),
        grid=(t // tg,),
        in_specs=[pl.BlockSpec(memory_space=pl.ANY),
                  pl.BlockSpec(memory_space=pl.ANY),
                  pl.BlockSpec((tg, d), lambda i: (i, 0)),
                  pl.BlockSpec((tg, LANES), lambda i: (i, 0)),
                  pl.BlockSpec((1, d), lambda i: (0, 0))],
        out_specs=pl.BlockSpec((tg, d), lambda i: (i, 0)),
        scratch_shapes=[pltpu.SMEM((2, TOP_K * tg), jnp.int32),
                        pltpu.VMEM((2, TOP_K * tg * sr, SLAB_LANES), F32),
                        pltpu.SemaphoreType.DMA((2,)), pltpu.SemaphoreType.DMA((2,))],
        compiler_params=pltpu.CompilerParams(
            dimension_semantics=("arbitrary",), vmem_limit_bytes=VMEM_LIMIT),
        name="combine",
    )(dest, ybuf, x1, route, fg)


def _pad_cols(m, n):
    return jnp.pad(m, ((0, 0), (0, n - m.shape[1])))


def _pad_rows(m, n):
    return jnp.pad(m, ((0, n - m.shape[0]), (0, 0)))


def _pick_tile(n, cap):
    for unit in (MXU_WIDTH, LANES):
        fits = [c for c in range(unit, cap + 1, unit) if n % c == 0]
        if fits:
            return fits[-1]
    raise ValueError(f"no lane-aligned tile divides {n}")


def _dispatch_plan(experts, n_tok):
    tm = EXPERT_ROWS
    n_assign = n_tok * TOP_K
    nt = n_assign // tm + N_EXPERTS
    e_flat = experts.reshape(n_assign)
    onehot = (e_flat[:, None] == jnp.arange(N_EXPERTS, dtype=jnp.int32)[None, :]).astype(jnp.int32)
    csum = jnp.cumsum(onehot, axis=0)
    rank = jnp.take_along_axis(csum, e_flat[:, None], axis=1)[:, 0] - 1
    counts = csum[-1]
    tiles_e = (counts + tm - 1) // tm
    tile_end = jnp.cumsum(tiles_e)
    tile_start = tile_end - tiles_e
    dest = tile_start[e_flat] * tm + rank
    n_used = tile_end[-1:].astype(jnp.int32)
    tile_e = jnp.minimum(jnp.searchsorted(tile_end, jnp.arange(nt, dtype=jnp.int32), side="right"),
                         N_EXPERTS - 1).astype(jnp.int32)
    tok = jnp.arange(n_assign, dtype=jnp.int32) // TOP_K
    row_tok = jnp.zeros((nt * tm,), jnp.int32).at[dest].set(tok)
    tiles = jnp.arange(nt, dtype=jnp.int32)
    tile_first = ((tiles == 0) | (tile_e != jnp.roll(tile_e, 1))).astype(jnp.int32)
    next_start = tile_end[tile_e]
    tile_next = jnp.where(next_start < n_used[0], tile_e[jnp.minimum(next_start, nt - 1)], -1).astype(jnp.int32)
    return tile_e, tile_first, tile_next, n_used, row_tok.reshape(nt, tm), dest.astype(jnp.int32)


def _layer(x2, b, s, norm1_g, w_in, attn_out_g, mu, w0, w2, a0, a2, g2, k_k, k_a, r_k, lnx_w, lnx_b,
           w_out, norm2_g, wg, bg, we, be, ew1, ew3, ew2, final_g):
    t, d = x2.shape
    aw = attn_out_g.shape[0]
    cw = w0.shape[0]
    heads = aw // ATT_HEAD_DIM
    dl, al, gl = w2.shape[0], a2.shape[0], g2.shape[0]
    pw, pa, pg = _round_up(dl, LANES), _round_up(al, LANES), _round_up(gl, LANES)
    wpad = 3 * cw + pw + pa + pg

    sh = w_in[:, 3 * aw:]
    o = 3 * cw
    w_arr = jnp.concatenate(
        [sh[:, :o], _pad_cols(sh[:, o:o + dl], pw), _pad_cols(sh[:, o + dl:o + dl + al], pa),
         _pad_cols(sh[:, o + dl + al:o + dl + al + gl], pg), w_in[:, :3 * aw]], axis=1).astype(BF16)
    mu_arr = jnp.concatenate(
        [mu[:o], jnp.pad(mu[o:o + dl], (0, pw - dl)), jnp.pad(mu[o + dl:o + dl + al], (0, pa - al)),
         jnp.pad(mu[o + dl + al:], (0, pg - gl))])[None, :]

    n_all = w_arr.shape[1]
    proj = _in_proj(x2, norm1_g[None, :], w_arr, tm=min(512, t), tn=_pick_tile(n_all, 4096))

    att = _moba(proj, b, s, heads, wpad // LANES)
    r1 = lambda a: a.reshape(1, -1)
    rw = _rwkv(proj, b, s, cw, wpad,
               (mu_arr, r1(w0), _pad_rows(w2, pw).astype(BF16), r1(a0), _pad_rows(a2, pa).astype(BF16),
                _pad_rows(g2, pg).astype(BF16), r1(k_k), r1(k_a), r1(r_k), r1(lnx_w), r1(lnx_b)))

    wr = _pad_cols(jnp.concatenate([wg, we], axis=1), LANES)
    wrh = wr.astype(BF16)
    wrm = (wr - wrh.astype(F32)).astype(BF16)
    br = jnp.pad(jnp.concatenate([bg, be]), (0, LANES - N_GROUPS - N_EXPERTS))[None, :]
    wo = w_out.astype(BF16)
    x1, hn, route = _out_proj(att, rw, x2, r1(attn_out_g), wo[:aw], wo[aw:], r1(norm2_g), wrh, wrm, br,
                              tm=min(256, t))

    experts = route[:, :TOP_K].astype(jnp.int32)
    tile_e, tile_first, tile_next, n_used, row_tok, dest = _dispatch_plan(experts, t)
    sr = d // SLAB_LANES
    ybuf = _experts(tile_e, tile_first, tile_next, n_used, row_tok * sr, hn, ew1, ew3, ew2)

    tg = COMBINE_ROWS
    dest_tiles = dest.reshape(t // tg, tg, TOP_K).transpose(0, 2, 1).reshape(t // tg, TOP_K * tg)
    return _combine(dest_tiles * sr, ybuf, x1, route, r1(final_g))


def kernel(x, norm1_g, w_in, attn_out_g, rwkv_mu, rwkv_w0, rwkv_w2, rwkv_a0, rwkv_a2, rwkv_g2, rwkv_k_k,
           rwkv_k_a, rwkv_r_k, rwkv_lnx_w, rwkv_lnx_b, w_out, norm2_g, router_group_w, router_group_b,
           router_expert_w, router_expert_b, expert_w1, expert_w3, expert_w2, final_g):
    b, s, d = x.shape
    assert norm1_g.shape[0] == 1, "single-layer block"
    assert s % MOBA_BLOCK == 0 and s % RWKV_CHUNK == 0
    assert (b * s) % max(EXPERT_ROWS, COMBINE_ROWS) == 0
    out = _layer(x.reshape(b * s, d), b, s, norm1_g[0], w_in[0], attn_out_g[0], rwkv_mu[0], rwkv_w0[0],
                 rwkv_w2[0], rwkv_a0[0], rwkv_a2[0], rwkv_g2[0], rwkv_k_k[0], rwkv_k_a[0], rwkv_r_k[0],
                 rwkv_lnx_w[0], rwkv_lnx_b[0], w_out[0], norm2_g[0], router_group_w[0], router_group_b[0],
                 router_expert_w[0], router_expert_b[0], expert_w1[0], expert_w3[0], expert_w2[0], final_g)
    return out.reshape(b, s, d)
```

```python
import functools
import math

import jax
import jax.numpy as jnp
from jax import lax
from jax.experimental import pallas as pl
from jax.experimental.pallas import tpu as pltpu

F32 = jnp.float32
BF16 = jnp.bfloat16

LANES = 128
MXU_WIDTH = 256
SLAB_LANES = 128
ATT_HEAD_DIM = 128
RWKV_HEAD_DIM = 64
MOBA_BLOCK = 256
MOBA_TOPK = 3
N_GROUPS = 4
EXPERTS_PER_GROUP = 8
N_EXPERTS = N_GROUPS * EXPERTS_PER_GROUP
TOP_K = 2
RMS_EPS = 1e-6
GN_EPS = 64e-5
NEG_BIG = -1e30
RWKV_CHUNK = 64
EXPERT_ROWS = 256
COMBINE_ROWS = 256
VMEM_LIMIT = 56 * 1024 * 1024
EXPERT_VMEM_LIMIT = 60 * 1024 * 1024
WEIGHT_CAST_CHUNKS = 16


def _round_up(n, m):
    return (n + m - 1) // m * m


def _dot(a, b):
    return jnp.dot(a.astype(BF16), b.astype(BF16), preferred_element_type=F32)


def _dot_nt(a, b):
    return lax.dot_general(a.astype(BF16), b.astype(BF16), (((1,), (1,)), ((), ())),
                           preferred_element_type=F32)


def _split3(a):
    hi = a.astype(BF16)
    r1 = a - hi.astype(F32)
    mid = r1.astype(BF16)
    lo = (r1 - mid.astype(F32)).astype(BF16)
    return hi, mid, lo


def _dot_hi(a, b):
    ah, am, _ = _split3(a)
    bh, bm, _ = _split3(b)
    d = functools.partial(jnp.dot, preferred_element_type=F32)
    return d(ah, bh) + (d(am, bh) + d(ah, bm))


def _sigmoid(x):
    return 1.0 / (1.0 + jnp.exp(-x))


def _rms(x, g):
    ms = jnp.mean(x * x, axis=-1, keepdims=True)
    return x * lax.rsqrt(ms + RMS_EPS) * g


def _in_proj_kernel(x_ref, g_ref, w_ref, o_ref):
    xn = _rms(x_ref[...], g_ref[...]).astype(BF16)
    o_ref[...] = jnp.dot(xn, w_ref[...], preferred_element_type=F32)


def _in_proj(x2, g, w, tm, tn):
    t, d = x2.shape
    n = w.shape[1]
    return pl.pallas_call(
        _in_proj_kernel,
        out_shape=jax.ShapeDtypeStruct((t, n), F32),
        grid=(n // tn, t // tm),
        in_specs=[pl.BlockSpec((tm, d), lambda j, i: (i, 0)),
                  pl.BlockSpec((1, d), lambda j, i: (0, 0)),
                  pl.BlockSpec((d, tn), lambda j, i: (0, j))],
        out_specs=pl.BlockSpec((tm, tn), lambda j, i: (i, j)),
        compiler_params=pltpu.CompilerParams(
            dimension_semantics=("arbitrary", "arbitrary"), vmem_limit_bytes=VMEM_LIMIT),
        name="in_proj",
    )(x2, g, w)


def _moba_kernel(q_ref, k_ref, v_ref, o_ref, *, nblk, scale):
    bs = MOBA_BLOCK
    k = k_ref[...]
    kb = k.astype(BF16)
    vtb = v_ref[...].T.astype(BF16)
    qt = q_ref[...].T
    qtb = qt.astype(BF16)
    kmean = jnp.concatenate(
        [jnp.sum(k[n * bs:(n + 1) * bs, :], axis=0, keepdims=True) for n in range(nblk)], axis=0) * (1.0 / bs)
    gate_all = _dot_hi(kmean, qt)
    kpos = lax.broadcasted_iota(jnp.int32, (bs, bs), 0)
    qpos = lax.broadcasted_iota(jnp.int32, (bs, bs), 1)
    causal = kpos <= qpos

    for qi in range(nblk):
        qs = slice(qi * bs, (qi + 1) * bs)
        nk = (qi + 1) * bs
        s_all = jnp.dot(kb[0:nk, :], qtb[:, qs], preferred_element_type=F32) * scale
        blocks = []
        if qi > MOBA_TOPK:
            gate = gate_all[0:qi, qs]
            blk = lax.broadcasted_iota(jnp.int32, (qi, bs), 0)
        for n in range(qi):
            sn = s_all[n * bs:(n + 1) * bs, :]
            if qi > MOBA_TOPK:
                gn = gate[n:n + 1, :]
                beats = (gate > gn) | ((gate == gn) & (blk < n))
                rank = jnp.sum(jnp.where(beats, 1.0, 0.0), axis=0, keepdims=True)
                sn = jnp.where(rank < float(MOBA_TOPK), sn, NEG_BIG)
            blocks.append(sn)
        blocks.append(jnp.where(causal, s_all[qi * bs:nk, :], NEG_BIG))
        m = functools.reduce(jnp.maximum, [jnp.max(x, axis=0, keepdims=True) for x in blocks])
        ps = [jnp.exp(x - m) for x in blocks]
        l = functools.reduce(jnp.add, [jnp.sum(x, axis=0, keepdims=True) for x in ps])
        pb = jnp.concatenate([x.astype(BF16) for x in ps], axis=0)
        acc = jnp.dot(vtb[:, 0:nk], pb, preferred_element_type=F32)
        o_ref[qs, :] = (acc / l).T


def _moba(proj, b, s, heads, col0):
    d = ATT_HEAD_DIM
    nblk = s // MOBA_BLOCK
    kern = functools.partial(_moba_kernel, nblk=nblk, scale=1.0 / math.sqrt(d))
    return pl.pallas_call(
        kern,
        out_shape=jax.ShapeDtypeStruct((b * s, heads * d), F32),
        grid=(b, heads),
        in_specs=[pl.BlockSpec((s, d), lambda bi, h: (bi, col0 + h)),
                  pl.BlockSpec((s, d), lambda bi, h: (bi, col0 + heads + h)),
                  pl.BlockSpec((s, d), lambda bi, h: (bi, col0 + 2 * heads + h))],
        out_specs=pl.BlockSpec((s, d), lambda bi, h: (bi, h)),
        compiler_params=pltpu.CompilerParams(
            dimension_semantics=("arbitrary", "arbitrary"), vmem_limit_bytes=VMEM_LIMIT),
        name="moba",
    )(proj, proj, proj)


def _head_sum(x, lo):
    s0 = jnp.sum(jnp.where(lo, x, 0.0), axis=-1, keepdims=True)
    s1 = jnp.sum(jnp.where(lo, 0.0, x), axis=-1, keepdims=True)
    return jnp.where(lo, s0, s1)


def _stack_heads(x, lo):
    return jnp.concatenate([jnp.where(lo, x, 0.0), jnp.where(lo, 0.0, x)], axis=0)


def _rwkv_kernel(p_ref, mu_ref, w0_ref, w2_ref, a0_ref, a2_ref, g2_ref, kk_ref, ka_ref, rk_ref,
                 lnw_ref, lnb_ref, o_ref, carry_ref, state_ref, *, cw, pw, pa, pg):
    c = pl.program_id(1)
    C = RWKV_CHUNK
    C2 = 2 * C

    @pl.when(c == 0)
    def _():
        carry_ref[...] = jnp.zeros_like(carry_ref)
        state_ref[...] = jnp.zeros_like(state_ref)

    pr = p_ref[...]
    row = lax.broadcasted_iota(jnp.int32, pr.shape, 0)
    prev = jnp.where(row == 0, carry_ref[...], pltpu.roll(pr, 1, 0))
    carry_ref[...] = pr[C - 1:C, :]
    ps = pr + (prev - pr) * mu_ref[...]

    r = ps[:, 0:cw]
    k = ps[:, cw:2 * cw]
    v = ps[:, 2 * cw:3 * cw]
    o = 3 * cw
    xw = ps[:, o:o + pw]
    xa = ps[:, o + pw:o + pw + pa]
    xg = ps[:, o + pw + pa:o + pw + pa + pg]

    z = -(w0_ref[...] + _dot(jnp.tanh(xw), w2_ref[...]))
    softplus = jnp.maximum(z, 0.0) + jnp.log(1.0 + jnp.exp(-jnp.abs(z)))
    logw = -jnp.exp(-softplus - 0.5)
    asig = _sigmoid(a0_ref[...] + _dot(xa, a2_ref[...]))
    gate = _dot(_sigmoid(xg), g2_ref[...])
    kkr = k * kk_ref[...]
    kmod = k * (1.0 + (asig - 1.0) * ka_ref[...])

    ti = lax.broadcasted_iota(jnp.int32, (C, C), 0)
    si = lax.broadcasted_iota(jnp.int32, (C, C), 1)
    tril = jnp.where(si <= ti, 1.0, 0.0).astype(BF16)
    lh, lm, ll = _split3(logw)
    d32 = functools.partial(jnp.dot, preferred_element_type=F32)
    cum = d32(tril, lh) + (d32(tril, lm) + d32(tril, ll))
    cum_end = cum[C - 1:C, :]
    e_prev = jnp.exp(cum - logw)
    e_incl = jnp.exp(cum)
    e_inv = jnp.exp(-cum)
    e_end = jnp.exp(cum_end - cum)
    g_end = jnp.exp(cum_end)

    lane = lax.broadcasted_iota(jnp.int32, (C, LANES), 1)
    lo = lane < RWKV_HEAD_DIM
    rho = lax.broadcasted_iota(jnp.int32, (C2, C2), 0)
    sig = lax.broadcasted_iota(jnp.int32, (C2, C2), 1)
    same_head = (rho >= C) == (sig >= C)
    strict = same_head & ((sig % C) < (rho % C))
    incl = same_head & ((sig % C) <= (rho % C))
    eye = jnp.where(rho == sig, 1.0, 0.0)

    pairs = range(cw // LANES)
    sls = [slice(p * LANES, (p + 1) * LANES) for p in pairs]
    incl2 = jnp.concatenate([incl, incl], axis=1)
    strict2 = jnp.concatenate([strict, strict], axis=1)

    b2, r2, k2, v2, at, rt, v2s, gm = [], [], [], [], [], [], [], []
    for sl in sls:
        kk2 = kkr[:, sl]
        nrm = jnp.sqrt(_head_sum(kk2 * kk2, lo))
        kkn = kk2 / jnp.maximum(nrm, 1e-12)
        b2.append(kkn * asig[:, sl])
        r2.append(r[:, sl])
        k2.append(kmod[:, sl])
        v2.append(v[:, sl])
        at.append(-kkn * e_prev[:, sl])
        rt.append(r2[-1] * e_incl[:, sl])
        bt = b2[-1] * e_inv[:, sl]
        kt = k2[-1] * e_inv[:, sl]
        v2s.append(_stack_heads(v2[-1], lo))
        lhs = jnp.concatenate([_stack_heads(at[-1], lo), _stack_heads(rt[-1], lo)], axis=0)
        rhs = jnp.concatenate([bt, bt, kt, kt], axis=0)
        gm.append(_dot_nt(lhs, rhs))

    s0 = [state_ref[p] for p in pairs]
    ars = [_dot_nt(jnp.concatenate([at[p], rt[p]], axis=0), s0[p]) for p in pairs]
    labk = [jnp.where(strict2, g[0:C2, :], 0.0) for g in gm]
    mrbk = [jnp.where(incl2, g[C2:2 * C2, :], 0.0) for g in gm]
    rhs_u = [_stack_heads(ars[p][0:C], lo) + _dot(labk[p][:, C2:2 * C2], v2s[p]) for p in pairs]

    lab = [m[:, 0:C2] for m in labk]
    tinv = [eye + m for m in lab]
    pw2 = [_dot(m, m) for m in lab]
    n_pow = int(math.log2(C))
    for k in range(1, n_pow):
        if k < n_pow - 1:
            prod = [_dot(jnp.concatenate([t, m], axis=0), m) for t, m in zip(tinv, pw2)]
            tinv = [t + pr[0:C2] for t, pr in zip(tinv, prod)]
            pw2 = [pr[C2:2 * C2] for pr in prod]
        else:
            tinv = [t + _dot(t, m) for t, m in zip(tinv, pw2)]

    u2s = [_dot(t, x) for t, x in zip(tinv, rhs_u)]
    y2s = [_stack_heads(ars[p][C:C2], lo) + _dot(mrbk[p], jnp.concatenate([u2s[p], v2s[p]], axis=0))
           for p in pairs]
    u2 = [m[0:C] + m[C:C2] for m in u2s]
    y2 = [m[0:C] + m[C:C2] for m in y2s]

    for p, sl in zip(pairs, sls):
        uv = jnp.concatenate([u2[p], v2[p]], axis=0)
        bk = jnp.concatenate([b2[p] * e_end[:, sl], k2[p] * e_end[:, sl]], axis=0)
        upd = _dot(uv.T, bk)
        state_ref[p] = s0[p] * g_end[:, sl] + jnp.where(same_head, upd, 0.0)

    for p, sl in zip(pairs, sls):
        mean = _head_sum(y2[p], lo) * (1.0 / RWKV_HEAD_DIM)
        yc = y2[p] - mean
        var = _head_sum(yc * yc, lo) * (1.0 / RWKV_HEAD_DIM)
        yn = yc * lax.rsqrt(var + GN_EPS) * lnw_ref[:, sl] + lnb_ref[:, sl]
        bonus = _head_sum(r2[p] * k2[p] * rk_ref[:, sl], lo) * v2[p]
        o_ref[:, sl] = (yn + bonus) * gate[:, sl]


def _rwkv(proj, b, s, cw, wpad, params):
    (mu, w0, w2, a0, a2, g2, kk, ka, rk, lnw, lnb) = params
    C = RWKV_CHUNK
    nch = s // C
    pw, pa, pg = w2.shape[0], a2.shape[0], g2.shape[0]
    kern = functools.partial(_rwkv_kernel, cw=cw, pw=pw, pa=pa, pg=pg)
    row = lambda n: pl.BlockSpec((1, n), lambda bi, c: (0, 0))
    full = lambda a: pl.BlockSpec(a.shape, lambda bi, c: (0, 0))
    return pl.pallas_call(
        kern,
        out_shape=jax.ShapeDtypeStruct((b * s, cw), F32),
        grid=(b, nch),
        in_specs=[pl.BlockSpec((C, wpad), lambda bi, c: (bi * nch + c, 0)),
                  row(wpad), row(cw), full(w2), row(cw), full(a2), full(g2),
                  row(cw), row(cw), row(cw), row(cw), row(cw)],
        out_specs=pl.BlockSpec((C, cw), lambda bi, c: (bi * nch + c, 0)),
        scratch_shapes=[pltpu.VMEM((1, wpad), F32),
                        pltpu.VMEM((cw // LANES, LANES, LANES), F32)],
        compiler_params=pltpu.CompilerParams(
            dimension_semantics=("arbitrary", "arbitrary"), vmem_limit_bytes=VMEM_LIMIT),
        name="rwkv",
    )(proj, mu, w0, w2, a0, a2, g2, kk, ka, rk, lnw, lnb)


def _out_proj_kernel(att_ref, rw_ref, x_ref, ga_ref, woa_ref, wor_ref, g2_ref, wrh_ref, wrm_ref, br_ref,
                     x1_ref, hn_ref, route_ref, counts_ref):
    att = _rms(att_ref[...], ga_ref[...])
    y = (jnp.dot(att.astype(BF16), woa_ref[...], preferred_element_type=F32)
         + jnp.dot(rw_ref[...].astype(BF16), wor_ref[...], preferred_element_type=F32))
    x1 = x_ref[...] + y
    x1_ref[...] = x1
    hn = _rms(x1, g2_ref[...])
    _store_slabs(hn_ref, hn)

    hh, hm, _ = _split3(hn)
    d32 = functools.partial(jnp.dot, preferred_element_type=F32)
    logits = d32(hh, wrh_ref[...]) + (d32(hm, wrh_ref[...]) + d32(hh, wrm_ref[...])) + br_ref[...]
    lane = lax.broadcasted_iota(jnp.int32, logits.shape, 1)
    lanef = lane.astype(F32)
    big = float(LANES)

    is_g = lane < N_GROUPS
    gmax = jnp.max(jnp.where(is_g, logits, -jnp.inf), axis=-1, keepdims=True)
    gidx = jnp.min(jnp.where(is_g & (logits == gmax), lanef, big), axis=-1, keepdims=True)
    gsum = jnp.sum(jnp.where(is_g, jnp.exp(logits - gmax), 0.0), axis=-1, keepdims=True)
    g_gate = 1.0 / gsum

    e_lo = N_GROUPS + EXPERTS_PER_GROUP * gidx
    in_grp = (lanef >= e_lo) & (lanef < e_lo + EXPERTS_PER_GROUP)
    e1 = jnp.max(jnp.where(in_grp, logits, -jnp.inf), axis=-1, keepdims=True)
    i1 = jnp.min(jnp.where(in_grp & (logits == e1), lanef, big), axis=-1, keepdims=True)
    rest = in_grp & (lanef != i1)
    e2 = jnp.max(jnp.where(rest, logits, -jnp.inf), axis=-1, keepdims=True)
    i2 = jnp.min(jnp.where(rest & (logits == e2), lanef, big), axis=-1, keepdims=True)
    t = jnp.exp(e2 - e1)
    w1 = g_gate * (1.0 / (1.0 + t))
    w2 = g_gate * (t / (1.0 + t))

    @pl.when(pl.program_id(0) == 0)
    def _():
        counts_ref[...] = jnp.zeros_like(counts_ref)

    tm = logits.shape[0]
    pick1 = lanef == i1
    pick2 = lanef == i2
    onehot = jnp.where(pick1 | pick2, 1.0, 0.0)
    ti = lax.broadcasted_iota(jnp.int32, (tm, tm), 0)
    si = lax.broadcasted_iota(jnp.int32, (tm, tm), 1)
    before = _dot(jnp.where(si < ti, 1.0, 0.0), onehot) + counts_ref[...]
    r1 = jnp.sum(jnp.where(pick1, before, 0.0), axis=-1, keepdims=True)
    r2 = jnp.sum(jnp.where(pick2, before, 0.0), axis=-1, keepdims=True)
    counts_ref[...] = counts_ref[...] + jnp.sum(onehot, axis=0, keepdims=True)

    route_ref[...] = jnp.where(lane == 0, i1 - N_GROUPS,
                     jnp.where(lane == 1, i2 - N_GROUPS,
                     jnp.where(lane == 2, w1,
                     jnp.where(lane == 3, w2,
                     jnp.where(lane == 4, r1,
                     jnp.where(lane == 5, r2, 0.0))))))


def _out_proj(att, rw, x2, ga, woa, wor, g2, wrh, wrm, br, tm):
    t, d = x2.shape
    wa = att.shape[1]
    wr = rw.shape[1]
    sr = d // SLAB_LANES
    c = lambda shape: pl.BlockSpec(shape, lambda i: (0, 0))
    return pl.pallas_call(
        _out_proj_kernel,
        out_shape=(jax.ShapeDtypeStruct((t, d), F32), jax.ShapeDtypeStruct((t * sr, SLAB_LANES), F32),
                   jax.ShapeDtypeStruct((t, LANES), F32), jax.ShapeDtypeStruct((1, LANES), F32)),
        grid=(t // tm,),
        in_specs=[pl.BlockSpec((tm, wa), lambda i: (i, 0)),
                  pl.BlockSpec((tm, wr), lambda i: (i, 0)),
                  pl.BlockSpec((tm, d), lambda i: (i, 0)),
                  c((1, wa)), c((wa, d)), c((wr, d)), c((1, d)), c((d, LANES)), c((d, LANES)), c((1, LANES))],
        out_specs=(pl.BlockSpec((tm, d), lambda i: (i, 0)), pl.BlockSpec((tm * sr, SLAB_LANES), lambda i: (i, 0)),
                   pl.BlockSpec((tm, LANES), lambda i: (i, 0)), c((1, LANES))),
        compiler_params=pltpu.CompilerParams(
            dimension_semantics=("arbitrary",), vmem_limit_bytes=VMEM_LIMIT),
        name="out_proj",
    )(att, rw, x2, ga, woa, wor, g2, wrh, wrm, br)


def _store_slabs(ref, x):
    rows, d = x.shape
    sr = d // SLAB_LANES
    for s in range(sr):
        ref[pl.ds(s, rows, stride=sr), :] = x[:, s * SLAB_LANES:(s + 1) * SLAB_LANES]


def _load_slabs(ref, slot, first, rows, sr):
    return jnp.concatenate([ref[slot, pl.ds(first * sr + s, rows, stride=sr), :] for s in range(sr)], axis=1)


def _gather_rows(i, nt, idx_hbm, src_hbm, idx_ref, buf_ref, isem, gsem, rows, sr):
    def idx_copy(tile, slot):
        return pltpu.make_async_copy(idx_hbm.at[pl.ds(tile, 1)], idx_ref.at[pl.ds(slot, 1)], isem.at[slot])

    def start_rows(slot):
        for r in range(rows):
            src = pl.multiple_of(idx_ref[slot, r], sr)
            pltpu.make_async_copy(src_hbm.at[pl.ds(src, sr)], buf_ref.at[slot, pl.ds(r * sr, sr)],
                                  gsem.at[slot]).start()

    slot = jnp.bitwise_and(i, 1)

    @pl.when(i == 0)
    def _():
        idx_copy(0, 0).start()
        idx_copy(0, 0).wait()
        start_rows(0)
        if nt > 1:
            idx_copy(1, 1).start()

    @pl.when(i + 1 < nt)
    def _():
        idx_copy(i + 1, 1 - slot).wait()
        start_rows(1 - slot)

    @pl.when(i + 2 < nt)
    def _():
        idx_copy(i + 2, slot).start()

    pltpu.make_async_copy(src_hbm.at[pl.ds(0, rows * sr)], buf_ref.at[slot], gsem.at[slot]).wait()
    return slot


def _expert_kernel(te_ref, first_ref, next_ref, nu_ref, rows_hbm, x_hbm, w1_hbm, w3_hbm, w2_hbm, o_ref,
                   idx_ref, xbuf_ref, isem, gsem, stage_refs, wb_refs, wsem, *, nt, sr):
    i = pl.program_id(0)
    tm = EXPERT_ROWS
    slot = _gather_rows(i, nt, rows_hbm, x_hbm, idx_ref, xbuf_ref, isem, gsem, tm, sr)
    w_hbm = (w1_hbm, w3_hbm, w2_hbm)
    used = i < nu_ref[0]

    def weight_copies(e):
        return [pltpu.make_async_copy(w.at[e], st, wsem.at[j]) for j, (w, st) in enumerate(zip(w_hbm, stage_refs))]

    @pl.when(i == 0)
    def _():
        for cp in weight_copies(te_ref[0]):
            cp.start(priority=1)

    @pl.when(used & (first_ref[i] == 1))
    def _():
        for cp in weight_copies(te_ref[i]):
            cp.wait()
        for st, wb in zip(stage_refs, wb_refs):
            rows = st.shape[0] // WEIGHT_CAST_CHUNKS

            def cast_chunk(c, carry, st=st, wb=wb, rows=rows):
                r0 = pl.multiple_of(c * rows, rows)
                wb[pl.ds(r0, rows), :] = st[pl.ds(r0, rows), :].astype(BF16)
                return carry

            lax.fori_loop(0, WEIGHT_CAST_CHUNKS, cast_chunk, 0)

        @pl.when(next_ref[i] >= 0)
        def _():
            for cp in weight_copies(next_ref[i]):
                cp.start(priority=1)

    @pl.when(used)
    def _():
        xb = _load_slabs(xbuf_ref, slot, 0, tm, sr).astype(BF16)
        h1 = jnp.dot(xb, wb_refs[0][...], preferred_element_type=F32)
        h3 = jnp.dot(xb, wb_refs[1][...], preferred_element_type=F32)
        hid = (h1 * _sigmoid(h1)) * h3
        _store_slabs(o_ref, jnp.dot(hid.astype(BF16), wb_refs[2][...], preferred_element_type=F32))

    @pl.when(jnp.logical_not(used))
    def _():
        o_ref[...] = jnp.zeros_like(o_ref)


def _experts(tile_e, tile_first, tile_next, n_used, row_src, hn_slabs, w1, w3, w2):
    nt = tile_e.shape[0]
    d, de = w1.shape[1], w1.shape[2]
    sr = d // SLAB_LANES
    tm = EXPERT_ROWS
    hbm = pl.BlockSpec(memory_space=pl.ANY)
    grid_spec = pltpu.PrefetchScalarGridSpec(
        num_scalar_prefetch=4,
        grid=(nt,),
        in_specs=[hbm, hbm, hbm, hbm, hbm],
        out_specs=pl.BlockSpec((tm * sr, SLAB_LANES), lambda i, *_: (i, 0)),
        scratch_shapes=[pltpu.SMEM((2, tm), jnp.int32),
                        pltpu.VMEM((2, tm * sr, SLAB_LANES), F32),
                        pltpu.SemaphoreType.DMA((2,)), pltpu.SemaphoreType.DMA((2,)),
                        [pltpu.VMEM((d, de), F32), pltpu.VMEM((d, de), F32), pltpu.VMEM((de, d), F32)],
                        [pltpu.VMEM((d, de), BF16), pltpu.VMEM((d, de), BF16), pltpu.VMEM((de, d), BF16)],
                        pltpu.SemaphoreType.DMA((3,))],
    )
    return pl.pallas_call(
        functools.partial(_expert_kernel, nt=nt, sr=sr),
        out_shape=jax.ShapeDtypeStruct((nt * tm * sr, SLAB_LANES), F32),
        grid_spec=grid_spec,
        compiler_params=pltpu.CompilerParams(
            dimension_semantics=("arbitrary",), vmem_limit_bytes=EXPERT_VMEM_LIMIT),
        name="experts",
    )(tile_e, tile_first, tile_next, n_used, row_src, hn_slabs, w1, w3, w2)


def _combine_kernel(dest_hbm, y_hbm, x1_ref, route_ref, fg_ref, o_ref, idx_ref, buf_ref, isem, gsem,
                    *, nt, sr):
    i = pl.program_id(0)
    tg = COMBINE_ROWS
    slot = _gather_rows(i, nt, dest_hbm, y_hbm, idx_ref, buf_ref, isem, gsem, TOP_K * tg, sr)
    route = route_ref[...]
    y = (_load_slabs(buf_ref, slot, 0, tg, sr) * route[:, 2:3]
         + _load_slabs(buf_ref, slot, tg, tg, sr) * route[:, 3:4])
    o_ref[...] = _rms(x1_ref[...] + y, fg_ref[...])


def _combine(dest, ybuf, x1, route, fg):
    t, d = x1.shape
    tg = COMBINE_ROWS
    sr = d // SLAB_LANES
    return pl.pallas_call(
        functools.partial(_combine_kernel, nt=t // tg, sr=sr),
        out_shape=jax.ShapeDtypeStruct((t, d), F32),
        grid=(t // tg,),
        in_specs=[pl.BlockSpec(memory_space=pl.ANY),
                  pl.BlockSpec(memory_space=pl.ANY),
                  pl.BlockSpec((tg, d), lambda i: (i, 0)),
                  pl.BlockSpec((tg, LANES), lambda i: (i, 0)),
                  pl.BlockSpec((1, d), lambda i: (0, 0))],
        out_specs=pl.BlockSpec((tg, d), lambda i: (i, 0)),
        scratch_shapes=[pltpu.SMEM((2, TOP_K * tg), jnp.int32),
                        pltpu.VMEM((2, TOP_K * tg * sr, SLAB_LANES), F32),
                        pltpu.SemaphoreType.DMA((2,)), pltpu.SemaphoreType.DMA((2,))],
        compiler_params=pltpu.CompilerParams(
            dimension_semantics=("arbitrary",), vmem_limit_bytes=VMEM_LIMIT),
        name="combine",
    )(dest, ybuf, x1, route, fg)


def _pad_cols(m, n):
    return jnp.pad(m, ((0, 0), (0, n - m.shape[1])))


def _pad_rows(m, n):
    return jnp.pad(m, ((0, n - m.shape[0]), (0, 0)))


def _pick_tile(n, cap):
    for unit in (MXU_WIDTH, LANES):
        fits = [c for c in range(unit, cap + 1, unit) if n % c == 0]
        if fits:
            return fits[-1]
    raise ValueError(f"no lane-aligned tile divides {n}")


def _dispatch_plan(experts, rank, counts, n_tok):
    tm = EXPERT_ROWS
    n_assign = n_tok * TOP_K
    nt = n_assign // tm + N_EXPERTS
    e_flat = experts.reshape(n_assign)
    rank = rank.reshape(n_assign)
    tiles_e = (counts + tm - 1) // tm
    tile_end = jnp.cumsum(tiles_e)
    tile_start = tile_end - tiles_e
    dest = tile_start[e_flat] * tm + rank
    n_used = tile_end[-1:].astype(jnp.int32)
    tile_e = jnp.minimum(jnp.searchsorted(tile_end, jnp.arange(nt, dtype=jnp.int32), side="right"),
                         N_EXPERTS - 1).astype(jnp.int32)
    tok = jnp.arange(n_assign, dtype=jnp.int32) // TOP_K
    row_tok = jnp.zeros((nt * tm,), jnp.int32).at[dest].set(tok)
    tiles = jnp.arange(nt, dtype=jnp.int32)
    tile_first = ((tiles == 0) | (tile_e != jnp.roll(tile_e, 1))).astype(jnp.int32)
    next_start = tile_end[tile_e]
    tile_next = jnp.where(next_start < n_used[0], tile_e[jnp.minimum(next_start, nt - 1)], -1).astype(jnp.int32)
    return tile_e, tile_first, tile_next, n_used, row_tok.reshape(nt, tm), dest.astype(jnp.int32)


def _layer(x2, b, s, norm1_g, w_in, attn_out_g, mu, w0, w2, a0, a2, g2, k_k, k_a, r_k, lnx_w, lnx_b,
           w_out, norm2_g, wg, bg, we, be, ew1, ew3, ew2, final_g):
    t, d = x2.shape
    aw = attn_out_g.shape[0]
    cw = w0.shape[0]
    heads = aw // ATT_HEAD_DIM
    dl, al, gl = w2.shape[0], a2.shape[0], g2.shape[0]
    pw, pa, pg = _round_up(dl, LANES), _round_up(al, LANES), _round_up(gl, LANES)
    wpad = 3 * cw + pw + pa + pg

    sh = w_in[:, 3 * aw:]
    o = 3 * cw
    w_arr = jnp.concatenate(
        [sh[:, :o], _pad_cols(sh[:, o:o + dl], pw), _pad_cols(sh[:, o + dl:o + dl + al], pa),
         _pad_cols(sh[:, o + dl + al:o + dl + al + gl], pg), w_in[:, :3 * aw]], axis=1).astype(BF16)
    mu_arr = jnp.concatenate(
        [mu[:o], jnp.pad(mu[o:o + dl], (0, pw - dl)), jnp.pad(mu[o + dl:o + dl + al], (0, pa - al)),
         jnp.pad(mu[o + dl + al:], (0, pg - gl))])[None, :]

    n_all = w_arr.shape[1]
    proj = _in_proj(x2, norm1_g[None, :], w_arr, tm=min(512, t), tn=_pick_tile(n_all, 4096))

    att = _moba(proj, b, s, heads, wpad // LANES)
    r1 = lambda a: a.reshape(1, -1)
    rw = _rwkv(proj, b, s, cw, wpad,
               (mu_arr, r1(w0), _pad_rows(w2, pw).astype(BF16), r1(a0), _pad_rows(a2, pa).astype(BF16),
                _pad_rows(g2, pg).astype(BF16), r1(k_k), r1(k_a), r1(r_k), r1(lnx_w), r1(lnx_b)))

    wr = _pad_cols(jnp.concatenate([wg, we], axis=1), LANES)
    wrh = wr.astype(BF16)
    wrm = (wr - wrh.astype(F32)).astype(BF16)
    br = jnp.pad(jnp.concatenate([bg, be]), (0, LANES - N_GROUPS - N_EXPERTS))[None, :]
    wo = w_out.astype(BF16)
    x1, hn, route, counts = _out_proj(att, rw, x2, r1(attn_out_g), wo[:aw], wo[aw:], r1(norm2_g), wrh, wrm, br,
                                      tm=min(256, t))

    plan_in = route[:, :6].astype(jnp.int32)
    counts = counts[0, N_GROUPS:N_GROUPS + N_EXPERTS].astype(jnp.int32)
    tile_e, tile_first, tile_next, n_used, row_tok, dest = _dispatch_plan(
        plan_in[:, 0:TOP_K], plan_in[:, 4:4 + TOP_K], counts, t)
    sr = d // SLAB_LANES
    ybuf = _experts(tile_e, tile_first, tile_next, n_used, row_tok * sr, hn, ew1, ew3, ew2)

    tg = COMBINE_ROWS
    dest_tiles = dest.reshape(t // tg, tg, TOP_K).transpose(0, 2, 1).reshape(t // tg, TOP_K * tg)
    return _combine(dest_tiles * sr, ybuf, x1, route, r1(final_g))


def kernel(x, norm1_g, w_in, attn_out_g, rwkv_mu, rwkv_w0, rwkv_w2, rwkv_a0, rwkv_a2, rwkv_g2, rwkv_k_k,
           rwkv_k_a, rwkv_r_k, rwkv_lnx_w, rwkv_lnx_b, w_out, norm2_g, router_group_w, router_group_b,
           router_expert_w, router_expert_b, expert_w1, expert_w3, expert_w2, final_g):
    b, s, d = x.shape
    assert norm1_g.shape[0] == 1, "single-layer block"
    assert s % MOBA_BLOCK == 0 and s % RWKV_CHUNK == 0
    assert (b * s) % max(EXPERT_ROWS, COMBINE_ROWS) == 0
    out = _layer(x.reshape(b * s, d), b, s, norm1_g[0], w_in[0], attn_out_g[0], rwkv_mu[0], rwkv_w0[0],
                 rwkv_w2[0], rwkv_a0[0], rwkv_a2[0], rwkv_g2[0], rwkv_k_k[0], rwkv_k_a[0], rwkv_r_k[0],
                 rwkv_lnx_w[0], rwkv_lnx_b[0], w_out[0], norm2_g[0], router_group_w[0], router_group_b[0],
                 router_expert_w[0], router_expert_b[0], expert_w1[0], expert_w3[0], expert_w2[0], final_g)
    return out.reshape(b, s, d)
```

```python
import functools
import math

import jax
import jax.numpy as jnp
from jax import lax
from jax.experimental import pallas as pl
from jax.experimental.pallas import tpu as pltpu

F32 = jnp.float32
BF16 = jnp.bfloat16

LANES = 128
MXU_WIDTH = 256
SLAB_LANES = 128
ATT_HEAD_DIM = 128
RWKV_HEAD_DIM = 64
MOBA_BLOCK = 256
MOBA_TOPK = 3
N_GROUPS = 4
EXPERTS_PER_GROUP = 8
N_EXPERTS = N_GROUPS * EXPERTS_PER_GROUP
TOP_K = 2
RMS_EPS = 1e-6
GN_EPS = 64e-5
NEG_BIG = -1e30
RWKV_CHUNK = 64
EXPERT_ROWS = 256
COMBINE_ROWS = 256
VMEM_LIMIT = 56 * 1024 * 1024
EXPERT_VMEM_LIMIT = 60 * 1024 * 1024
WEIGHT_CAST_CHUNKS = 16
EXPERT_PIECES = 4


def _round_up(n, m):
    return (n + m - 1) // m * m


def _dot(a, b):
    return jnp.dot(a.astype(BF16), b.astype(BF16), preferred_element_type=F32)


def _dot_nt(a, b):
    return lax.dot_general(a.astype(BF16), b.astype(BF16), (((1,), (1,)), ((), ())),
                           preferred_element_type=F32)


def _split3(a):
    hi = a.astype(BF16)
    r1 = a - hi.astype(F32)
    mid = r1.astype(BF16)
    lo = (r1 - mid.astype(F32)).astype(BF16)
    return hi, mid, lo


def _dot_hi(a, b):
    ah, am, _ = _split3(a)
    bh, bm, _ = _split3(b)
    d = functools.partial(jnp.dot, preferred_element_type=F32)
    return d(ah, bh) + (d(am, bh) + d(ah, bm))


def _sigmoid(x):
    return 1.0 / (1.0 + jnp.exp(-x))


def _rms(x, g):
    ms = jnp.mean(x * x, axis=-1, keepdims=True)
    return x * lax.rsqrt(ms + RMS_EPS) * g


def _in_proj_kernel(x_ref, g_ref, w_ref, o_ref):
    xn = _rms(x_ref[...], g_ref[...]).astype(BF16)
    o_ref[...] = jnp.dot(xn, w_ref[...], preferred_element_type=F32)


def _in_proj(x2, g, w, tm, tn):
    t, d = x2.shape
    n = w.shape[1]
    return pl.pallas_call(
        _in_proj_kernel,
        out_shape=jax.ShapeDtypeStruct((t, n), F32),
        grid=(n // tn, t // tm),
        in_specs=[pl.BlockSpec((tm, d), lambda j, i: (i, 0)),
                  pl.BlockSpec((1, d), lambda j, i: (0, 0)),
                  pl.BlockSpec((d, tn), lambda j, i: (0, j))],
        out_specs=pl.BlockSpec((tm, tn), lambda j, i: (i, j)),
        compiler_params=pltpu.CompilerParams(
            dimension_semantics=("arbitrary", "arbitrary"), vmem_limit_bytes=VMEM_LIMIT),
        name="in_proj",
    )(x2, g, w)


def _moba_kernel(q_ref, k_ref, v_ref, o_ref, *, nblk, scale):
    bs = MOBA_BLOCK
    k = k_ref[...]
    kb = k.astype(BF16)
    vtb = v_ref[...].T.astype(BF16)
    qt = q_ref[...].T
    qtb = qt.astype(BF16)
    kmean = jnp.concatenate(
        [jnp.sum(k[n * bs:(n + 1) * bs, :], axis=0, keepdims=True) for n in range(nblk)], axis=0) * (1.0 / bs)
    gate_all = _dot_hi(kmean, qt)
    kpos = lax.broadcasted_iota(jnp.int32, (bs, bs), 0)
    qpos = lax.broadcasted_iota(jnp.int32, (bs, bs), 1)
    causal = kpos <= qpos

    for qi in range(nblk):
        qs = slice(qi * bs, (qi + 1) * bs)
        nk = (qi + 1) * bs
        s_all = jnp.dot(kb[0:nk, :], qtb[:, qs], preferred_element_type=F32) * scale
        blocks = []
        if qi > MOBA_TOPK:
            gate = gate_all[0:qi, qs]
            blk = lax.broadcasted_iota(jnp.int32, (qi, bs), 0)
        for n in range(qi):
            sn = s_all[n * bs:(n + 1) * bs, :]
            if qi > MOBA_TOPK:
                gn = gate[n:n + 1, :]
                beats = (gate > gn) | ((gate == gn) & (blk < n))
                rank = jnp.sum(jnp.where(beats, 1.0, 0.0), axis=0, keepdims=True)
                sn = jnp.where(rank < float(MOBA_TOPK), sn, NEG_BIG)
            blocks.append(sn)
        blocks.append(jnp.where(causal, s_all[qi * bs:nk, :], NEG_BIG))
        m = functools.reduce(jnp.maximum, [jnp.max(x, axis=0, keepdims=True) for x in blocks])
        ps = [jnp.exp(x - m) for x in blocks]
        l = functools.reduce(jnp.add, [jnp.sum(x, axis=0, keepdims=True) for x in ps])
        pb = jnp.concatenate([x.astype(BF16) for x in ps], axis=0)
        acc = jnp.dot(vtb[:, 0:nk], pb, preferred_element_type=F32)
        o_ref[qs, :] = (acc / l).T


def _moba(proj, b, s, heads, col0):
    d = ATT_HEAD_DIM
    nblk = s // MOBA_BLOCK
    kern = functools.partial(_moba_kernel, nblk=nblk, scale=1.0 / math.sqrt(d))
    return pl.pallas_call(
        kern,
        out_shape=jax.ShapeDtypeStruct((b * s, heads * d), F32),
        grid=(b, heads),
        in_specs=[pl.BlockSpec((s, d), lambda bi, h: (bi, col0 + h)),
                  pl.BlockSpec((s, d), lambda bi, h: (bi, col0 + heads + h)),
                  pl.BlockSpec((s, d), lambda bi, h: (bi, col0 + 2 * heads + h))],
        out_specs=pl.BlockSpec((s, d), lambda bi, h: (bi, h)),
        compiler_params=pltpu.CompilerParams(
            dimension_semantics=("arbitrary", "arbitrary"), vmem_limit_bytes=VMEM_LIMIT),
        name="moba",
    )(proj, proj, proj)


def _head_sum(x, lo):
    s0 = jnp.sum(jnp.where(lo, x, 0.0), axis=-1, keepdims=True)
    s1 = jnp.sum(jnp.where(lo, 0.0, x), axis=-1, keepdims=True)
    return jnp.where(lo, s0, s1)


def _stack_heads(x, lo):
    return jnp.concatenate([jnp.where(lo, x, 0.0), jnp.where(lo, 0.0, x)], axis=0)


def _rwkv_kernel(p_ref, mu_ref, w0_ref, w2_ref, a0_ref, a2_ref, g2_ref, kk_ref, ka_ref, rk_ref,
                 lnw_ref, lnb_ref, o_ref, carry_ref, state_ref, *, cw, pw, pa, pg):
    c = pl.program_id(1)
    C = RWKV_CHUNK
    C2 = 2 * C

    @pl.when(c == 0)
    def _():
        carry_ref[...] = jnp.zeros_like(carry_ref)
        state_ref[...] = jnp.zeros_like(state_ref)

    pr = p_ref[...]
    row = lax.broadcasted_iota(jnp.int32, pr.shape, 0)
    prev = jnp.where(row == 0, carry_ref[...], pltpu.roll(pr, 1, 0))
    carry_ref[...] = pr[C - 1:C, :]
    ps = pr + (prev - pr) * mu_ref[...]

    r = ps[:, 0:cw]
    k = ps[:, cw:2 * cw]
    v = ps[:, 2 * cw:3 * cw]
    o = 3 * cw
    xw = ps[:, o:o + pw]
    xa = ps[:, o + pw:o + pw + pa]
    xg = ps[:, o + pw + pa:o + pw + pa + pg]

    z = -(w0_ref[...] + _dot(jnp.tanh(xw), w2_ref[...]))
    softplus = jnp.maximum(z, 0.0) + jnp.log(1.0 + jnp.exp(-jnp.abs(z)))
    logw = -jnp.exp(-softplus - 0.5)
    asig = _sigmoid(a0_ref[...] + _dot(xa, a2_ref[...]))
    gate = _dot(_sigmoid(xg), g2_ref[...])
    kkr = k * kk_ref[...]
    kmod = k * (1.0 + (asig - 1.0) * ka_ref[...])

    ti = lax.broadcasted_iota(jnp.int32, (C, C), 0)
    si = lax.broadcasted_iota(jnp.int32, (C, C), 1)
    tril = jnp.where(si <= ti, 1.0, 0.0).astype(BF16)
    lh, lm, ll = _split3(logw)
    d32 = functools.partial(jnp.dot, preferred_element_type=F32)
    cum = d32(tril, lh) + (d32(tril, lm) + d32(tril, ll))
    cum_end = cum[C - 1:C, :]
    e_prev = jnp.exp(cum - logw)
    e_incl = jnp.exp(cum)
    e_inv = jnp.exp(-cum)
    e_end = jnp.exp(cum_end - cum)
    g_end = jnp.exp(cum_end)

    lane = lax.broadcasted_iota(jnp.int32, (C, LANES), 1)
    lo = lane < RWKV_HEAD_DIM
    rho = lax.broadcasted_iota(jnp.int32, (C2, C2), 0)
    sig = lax.broadcasted_iota(jnp.int32, (C2, C2), 1)
    same_head = (rho >= C) == (sig >= C)
    strict = same_head & ((sig % C) < (rho % C))
    incl = same_head & ((sig % C) <= (rho % C))
    eye = jnp.where(rho == sig, 1.0, 0.0)

    pairs = range(cw // LANES)
    sls = [slice(p * LANES, (p + 1) * LANES) for p in pairs]
    incl2 = jnp.concatenate([incl, incl], axis=1)
    strict2 = jnp.concatenate([strict, strict], axis=1)

    b2, r2, k2, v2, at, rt, v2s, gm = [], [], [], [], [], [], [], []
    for sl in sls:
        kk2 = kkr[:, sl]
        nrm = jnp.sqrt(_head_sum(kk2 * kk2, lo))
        kkn = kk2 / jnp.maximum(nrm, 1e-12)
        b2.append(kkn * asig[:, sl])
        r2.append(r[:, sl])
        k2.append(kmod[:, sl])
        v2.append(v[:, sl])
        at.append(-kkn * e_prev[:, sl])
        rt.append(r2[-1] * e_incl[:, sl])
        bt = b2[-1] * e_inv[:, sl]
        kt = k2[-1] * e_inv[:, sl]
        v2s.append(_stack_heads(v2[-1], lo))
        lhs = jnp.concatenate([_stack_heads(at[-1], lo), _stack_heads(rt[-1], lo)], axis=0)
        rhs = jnp.concatenate([bt, bt, kt, kt], axis=0)
        gm.append(_dot_nt(lhs, rhs))

    s0 = [state_ref[p] for p in pairs]
    ars = [_dot_nt(jnp.concatenate([at[p], rt[p]], axis=0), s0[p]) for p in pairs]
    labk = [jnp.where(strict2, g[0:C2, :], 0.0) for g in gm]
    mrbk = [jnp.where(incl2, g[C2:2 * C2, :], 0.0) for g in gm]
    rhs_u = [_stack_heads(ars[p][0:C], lo) + _dot(labk[p][:, C2:2 * C2], v2s[p]) for p in pairs]

    lab = [m[:, 0:C2] for m in labk]
    tinv = [eye + m for m in lab]
    pw2 = [_dot(m, m) for m in lab]
    n_pow = int(math.log2(C))
    for k in range(1, n_pow):
        if k < n_pow - 1:
            prod = [_dot(jnp.concatenate([t, m], axis=0), m) for t, m in zip(tinv, pw2)]
            tinv = [t + pr[0:C2] for t, pr in zip(tinv, prod)]
            pw2 = [pr[C2:2 * C2] for pr in prod]
        else:
            tinv = [t + _dot(t, m) for t, m in zip(tinv, pw2)]

    u2s = [_dot(t, x) for t, x in zip(tinv, rhs_u)]
    y2s = [_stack_heads(ars[p][C:C2], lo) + _dot(mrbk[p], jnp.concatenate([u2s[p], v2s[p]], axis=0))
           for p in pairs]
    u2 = [m[0:C] + m[C:C2] for m in u2s]
    y2 = [m[0:C] + m[C:C2] for m in y2s]

    for p, sl in zip(pairs, sls):
        uv = jnp.concatenate([u2[p], v2[p]], axis=0)
        bk = jnp.concatenate([b2[p] * e_end[:, sl], k2[p] * e_end[:, sl]], axis=0)
        upd = _dot(uv.T, bk)
        state_ref[p] = s0[p] * g_end[:, sl] + jnp.where(same_head, upd, 0.0)

    for p, sl in zip(pairs, sls):
        mean = _head_sum(y2[p], lo) * (1.0 / RWKV_HEAD_DIM)
        yc = y2[p] - mean
        var = _head_sum(yc * yc, lo) * (1.0 / RWKV_HEAD_DIM)
        yn = yc * lax.rsqrt(var + GN_EPS) * lnw_ref[:, sl] + lnb_ref[:, sl]
        bonus = _head_sum(r2[p] * k2[p] * rk_ref[:, sl], lo) * v2[p]
        o_ref[:, sl] = (yn + bonus) * gate[:, sl]


def _rwkv(proj, b, s, cw, wpad, params):
    (mu, w0, w2, a0, a2, g2, kk, ka, rk, lnw, lnb) = params
    C = RWKV_CHUNK
    nch = s // C
    pw, pa, pg = w2.shape[0], a2.shape[0], g2.shape[0]
    kern = functools.partial(_rwkv_kernel, cw=cw, pw=pw, pa=pa, pg=pg)
    row = lambda n: pl.BlockSpec((1, n), lambda bi, c: (0, 0))
    full = lambda a: pl.BlockSpec(a.shape, lambda bi, c: (0, 0))
    return pl.pallas_call(
        kern,
        out_shape=jax.ShapeDtypeStruct((b * s, cw), F32),
        grid=(b, nch),
        in_specs=[pl.BlockSpec((C, wpad), lambda bi, c: (bi * nch + c, 0)),
                  row(wpad), row(cw), full(w2), row(cw), full(a2), full(g2),
                  row(cw), row(cw), row(cw), row(cw), row(cw)],
        out_specs=pl.BlockSpec((C, cw), lambda bi, c: (bi * nch + c, 0)),
        scratch_shapes=[pltpu.VMEM((1, wpad), F32),
                        pltpu.VMEM((cw // LANES, LANES, LANES), F32)],
        compiler_params=pltpu.CompilerParams(
            dimension_semantics=("arbitrary", "arbitrary"), vmem_limit_bytes=VMEM_LIMIT),
        name="rwkv",
    )(proj, mu, w0, w2, a0, a2, g2, kk, ka, rk, lnw, lnb)


def _out_proj_kernel(att_ref, rw_ref, x_ref, ga_ref, woa_ref, wor_ref, g2_ref, wrh_ref, wrm_ref, br_ref,
                     x1_ref, hn_ref, route_ref, counts_ref):
    att = _rms(att_ref[...], ga_ref[...])
    y = (jnp.dot(att.astype(BF16), woa_ref[...], preferred_element_type=F32)
         + jnp.dot(rw_ref[...].astype(BF16), wor_ref[...], preferred_element_type=F32))
    x1 = x_ref[...] + y
    x1_ref[...] = x1
    hn = _rms(x1, g2_ref[...])
    _store_slabs(hn_ref, hn)

    hh, hm, _ = _split3(hn)
    d32 = functools.partial(jnp.dot, preferred_element_type=F32)
    logits = d32(hh, wrh_ref[...]) + (d32(hm, wrh_ref[...]) + d32(hh, wrm_ref[...])) + br_ref[...]
    lane = lax.broadcasted_iota(jnp.int32, logits.shape, 1)
    lanef = lane.astype(F32)
    big = float(LANES)

    is_g = lane < N_GROUPS
    gmax = jnp.max(jnp.where(is_g, logits, -jnp.inf), axis=-1, keepdims=True)
    gidx = jnp.min(jnp.where(is_g & (logits == gmax), lanef, big), axis=-1, keepdims=True)
    gsum = jnp.sum(jnp.where(is_g, jnp.exp(logits - gmax), 0.0), axis=-1, keepdims=True)
    g_gate = 1.0 / gsum

    e_lo = N_GROUPS + EXPERTS_PER_GROUP * gidx
    in_grp = (lanef >= e_lo) & (lanef < e_lo + EXPERTS_PER_GROUP)
    e1 = jnp.max(jnp.where(in_grp, logits, -jnp.inf), axis=-1, keepdims=True)
    i1 = jnp.min(jnp.where(in_grp & (logits == e1), lanef, big), axis=-1, keepdims=True)
    rest = in_grp & (lanef != i1)
    e2 = jnp.max(jnp.where(rest, logits, -jnp.inf), axis=-1, keepdims=True)
    i2 = jnp.min(jnp.where(rest & (logits == e2), lanef, big), axis=-1, keepdims=True)
    t = jnp.exp(e2 - e1)
    w1 = g_gate * (1.0 / (1.0 + t))
    w2 = g_gate * (t / (1.0 + t))

    @pl.when(pl.program_id(0) == 0)
    def _():
        counts_ref[...] = jnp.zeros_like(counts_ref)

    tm = logits.shape[0]
    pick1 = lanef == i1
    pick2 = lanef == i2
    onehot = jnp.where(pick1 | pick2, 1.0, 0.0)
    ti = lax.broadcasted_iota(jnp.int32, (tm, tm), 0)
    si = lax.broadcasted_iota(jnp.int32, (tm, tm), 1)
    before = _dot(jnp.where(si < ti, 1.0, 0.0), onehot) + counts_ref[...]
    r1 = jnp.sum(jnp.where(pick1, before, 0.0), axis=-1, keepdims=True)
    r2 = jnp.sum(jnp.where(pick2, before, 0.0), axis=-1, keepdims=True)
    counts_ref[...] = counts_ref[...] + jnp.sum(onehot, axis=0, keepdims=True)

    route_ref[...] = jnp.where(lane == 0, i1 - N_GROUPS,
                     jnp.where(lane == 1, i2 - N_GROUPS,
                     jnp.where(lane == 2, w1,
                     jnp.where(lane == 3, w2,
                     jnp.where(lane == 4, r1,
                     jnp.where(lane == 5, r2, 0.0))))))


def _out_proj(att, rw, x2, ga, woa, wor, g2, wrh, wrm, br, tm):
    t, d = x2.shape
    wa = att.shape[1]
    wr = rw.shape[1]
    sr = d // SLAB_LANES
    c = lambda shape: pl.BlockSpec(shape, lambda i: (0, 0))
    return pl.pallas_call(
        _out_proj_kernel,
        out_shape=(jax.ShapeDtypeStruct((t, d), F32), jax.ShapeDtypeStruct((t * sr, SLAB_LANES), F32),
                   jax.ShapeDtypeStruct((t, LANES), F32), jax.ShapeDtypeStruct((1, LANES), F32)),
        grid=(t // tm,),
        in_specs=[pl.BlockSpec((tm, wa), lambda i: (i, 0)),
                  pl.BlockSpec((tm, wr), lambda i: (i, 0)),
                  pl.BlockSpec((tm, d), lambda i: (i, 0)),
                  c((1, wa)), c((wa, d)), c((wr, d)), c((1, d)), c((d, LANES)), c((d, LANES)), c((1, LANES))],
        out_specs=(pl.BlockSpec((tm, d), lambda i: (i, 0)), pl.BlockSpec((tm * sr, SLAB_LANES), lambda i: (i, 0)),
                   pl.BlockSpec((tm, LANES), lambda i: (i, 0)), c((1, LANES))),
        compiler_params=pltpu.CompilerParams(
            dimension_semantics=("arbitrary",), vmem_limit_bytes=VMEM_LIMIT),
        name="out_proj",
    )(att, rw, x2, ga, woa, wor, g2, wrh, wrm, br)


def _store_slabs(ref, x):
    rows, d = x.shape
    sr = d // SLAB_LANES
    for s in range(sr):
        ref[pl.ds(s, rows, stride=sr), :] = x[:, s * SLAB_LANES:(s + 1) * SLAB_LANES]


def _load_slabs(ref, slot, first, rows, sr):
    return jnp.concatenate([ref[slot, pl.ds(first * sr + s, rows, stride=sr), :] for s in range(sr)], axis=1)


def _gather_rows(i, nt, idx_hbm, src_hbm, idx_ref, buf_ref, isem, gsem, rows, sr):
    def idx_copy(tile, slot):
        return pltpu.make_async_copy(idx_hbm.at[pl.ds(tile, 1)], idx_ref.at[pl.ds(slot, 1)], isem.at[slot])

    def start_rows(slot):
        for r in range(rows):
            src = pl.multiple_of(idx_ref[slot, r], sr)
            pltpu.make_async_copy(src_hbm.at[pl.ds(src, sr)], buf_ref.at[slot, pl.ds(r * sr, sr)],
                                  gsem.at[slot]).start()

    slot = jnp.bitwise_and(i, 1)

    @pl.when(i == 0)
    def _():
        idx_copy(0, 0).start()
        idx_copy(0, 0).wait()
        start_rows(0)
        if nt > 1:
            idx_copy(1, 1).start()

    @pl.when(i + 1 < nt)
    def _():
        idx_copy(i + 1, 1 - slot).wait()
        start_rows(1 - slot)

    @pl.when(i + 2 < nt)
    def _():
        idx_copy(i + 2, slot).start()

    pltpu.make_async_copy(src_hbm.at[pl.ds(0, rows * sr)], buf_ref.at[slot], gsem.at[slot]).wait()
    return slot


def _expert_kernel(te_ref, first_ref, next_ref, nu_ref, rows_hbm, x_hbm, w1_hbm, w3_hbm, w2_hbm, o_ref,
                   idx_ref, xbuf_ref, isem, gsem, stage_refs, wb_refs, wsem, *, nt, sr):
    i = pl.program_id(0)
    tm = EXPERT_ROWS
    w_hbm = (w1_hbm, w3_hbm, w2_hbm)
    n_used = nu_ref[0]
    used = i < n_used
    slot = jnp.bitwise_and(i, 1)
    other = 1 - slot
    last = nt - 1

    def weight_copies(e):
        return [pltpu.make_async_copy(w.at[e], st, wsem.at[j]) for j, (w, st) in enumerate(zip(w_hbm, stage_refs))]

    def idx_copy(tile, s):
        return pltpu.make_async_copy(rows_hbm.at[pl.ds(tile, 1)], idx_ref.at[pl.ds(s, 1)], isem.at[s])

    def token_copy(s, r):
        src = pl.multiple_of(idx_ref[s, r], sr)
        return pltpu.make_async_copy(x_hbm.at[pl.ds(src, sr)], xbuf_ref.at[s, pl.ds(r * sr, sr)], gsem.at[s])

    def tokens_landed(s):
        return pltpu.make_async_copy(x_hbm.at[pl.ds(0, tm * sr)], xbuf_ref.at[s], gsem.at[s])

    @pl.when(i == 0)
    def _():
        idx_copy(0, 0).start()
        idx_copy(0, 0).wait()
        for r in range(tm):
            token_copy(0, r).start()
        idx_copy(min(1, last), 1).start()
        for cp in weight_copies(te_ref[0]):
            cp.start(priority=1)

    idx_copy(jnp.minimum(i + 1, last), other).wait()

    @pl.when(i < last)
    def _():
        idx_copy(jnp.minimum(i + 2, last), slot).start()

    @pl.when(i <= n_used)
    def _():
        tokens_landed(slot).wait()

    @pl.when(used & (first_ref[i] == 1))
    def _():
        for cp in weight_copies(te_ref[i]):
            cp.wait()
        for st, wb in zip(stage_refs, wb_refs):
            rows = st.shape[0] // WEIGHT_CAST_CHUNKS

            def cast_chunk(c, carry, st=st, wb=wb, rows=rows):
                r0 = pl.multiple_of(c * rows, rows)
                wb[pl.ds(r0, rows), :] = st[pl.ds(r0, rows), :].astype(BF16)
                return carry

            lax.fori_loop(0, WEIGHT_CAST_CHUNKS, cast_chunk, 0)

        @pl.when(next_ref[i] >= 0)
        def _():
            for cp in weight_copies(next_ref[i]):
                cp.start(priority=1)

    @pl.when(used)
    def _():
        w1b, w3b, w2b = wb_refs
        d, de = w1b.shape
        pieces = 2 * EXPERT_PIECES
        per = tm // pieces
        dc, dn = de // EXPERT_PIECES, d // EXPERT_PIECES
        xb = _load_slabs(xbuf_ref, slot, 0, tm, sr).astype(BF16)
        hid = []
        for c in range(EXPERT_PIECES):
            h1 = jnp.dot(xb, w1b[:, c * dc:(c + 1) * dc], preferred_element_type=F32)
            h3 = jnp.dot(xb, w3b[:, c * dc:(c + 1) * dc], preferred_element_type=F32)
            hid.append(((h1 * _sigmoid(h1)) * h3).astype(BF16))
            for r in range(c * per, (c + 1) * per):
                token_copy(other, r).start()
        hidb = jnp.concatenate(hid, axis=1)
        for c in range(EXPERT_PIECES):
            y = jnp.dot(hidb, w2b[:, c * dn:(c + 1) * dn], preferred_element_type=F32)
            for s in range(dn // SLAB_LANES):
                o_ref[pl.ds(c * (dn // SLAB_LANES) + s, tm, stride=sr), :] = y[:, s * SLAB_LANES:(s + 1) * SLAB_LANES]
            for r in range((EXPERT_PIECES + c) * per, (EXPERT_PIECES + c + 1) * per):
                token_copy(other, r).start()

    @pl.when(used & (i == last))
    def _():
        tokens_landed(other).wait()

    @pl.when(jnp.logical_not(used))
    def _():
        o_ref[...] = jnp.zeros_like(o_ref)


def _experts(tile_e, tile_first, tile_next, n_used, row_src, hn_slabs, w1, w3, w2):
    nt = tile_e.shape[0]
    d, de = w1.shape[1], w1.shape[2]
    sr = d // SLAB_LANES
    tm = EXPERT_ROWS
    hbm = pl.BlockSpec(memory_space=pl.ANY)
    grid_spec = pltpu.PrefetchScalarGridSpec(
        num_scalar_prefetch=4,
        grid=(nt,),
        in_specs=[hbm, hbm, hbm, hbm, hbm],
        out_specs=pl.BlockSpec((tm * sr, SLAB_LANES), lambda i, *_: (i, 0)),
        scratch_shapes=[pltpu.SMEM((2, tm), jnp.int32),
                        pltpu.VMEM((2, tm * sr, SLAB_LANES), F32),
                        pltpu.SemaphoreType.DMA((2,)), pltpu.SemaphoreType.DMA((2,)),
                        [pltpu.VMEM((d, de), F32), pltpu.VMEM((d, de), F32), pltpu.VMEM((de, d), F32)],
                        [pltpu.VMEM((d, de), BF16), pltpu.VMEM((d, de), BF16), pltpu.VMEM((de, d), BF16)],
                        pltpu.SemaphoreType.DMA((3,))],
    )
    return pl.pallas_call(
        functools.partial(_expert_kernel, nt=nt, sr=sr),
        out_shape=jax.ShapeDtypeStruct((nt * tm * sr, SLAB_LANES), F32),
        grid_spec=grid_spec,
        compiler_params=pltpu.CompilerParams(
            dimension_semantics=("arbitrary",), vmem_limit_bytes=EXPERT_VMEM_LIMIT),
        name="experts",
    )(tile_e, tile_first, tile_next, n_used, row_src, hn_slabs, w1, w3, w2)


def _combine_kernel(dest_hbm, y_hbm, x1_ref, route_ref, fg_ref, o_ref, idx_ref, buf_ref, isem, gsem,
                    *, nt, sr):
    i = pl.program_id(0)
    tg = COMBINE_ROWS
    slot = _gather_rows(i, nt, dest_hbm, y_hbm, idx_ref, buf_ref, isem, gsem, TOP_K * tg, sr)
    route = route_ref[...]
    y = (_load_slabs(buf_ref, slot, 0, tg, sr) * route[:, 2:3]
         + _load_slabs(buf_ref, slot, tg, tg, sr) * route[:, 3:4])
    o_ref[...] = _rms(x1_ref[...] + y, fg_ref[...])


def _combine(dest, ybuf, x1, route, fg):
    t, d = x1.shape
    tg = COMBINE_ROWS
    sr = d // SLAB_LANES
    return pl.pallas_call(
        functools.partial(_combine_kernel, nt=t // tg, sr=sr),
        out_shape=jax.ShapeDtypeStruct((t, d), F32),
        grid=(t // tg,),
        in_specs=[pl.BlockSpec(memory_space=pl.ANY),
                  pl.BlockSpec(memory_space=pl.ANY),
                  pl.BlockSpec((tg, d), lambda i: (i, 0)),
                  pl.BlockSpec((tg, LANES), lambda i: (i, 0)),
                  pl.BlockSpec((1, d), lambda i: (0, 0))],
        out_specs=pl.BlockSpec((tg, d), lambda i: (i, 0)),
        scratch_shapes=[pltpu.SMEM((2, TOP_K * tg), jnp.int32),
                        pltpu.VMEM((2, TOP_K * tg * sr, SLAB_LANES), F32),
                        pltpu.SemaphoreType.DMA((2,)), pltpu.SemaphoreType.DMA((2,))],
        compiler_params=pltpu.CompilerParams(
            dimension_semantics=("arbitrary",), vmem_limit_bytes=VMEM_LIMIT),
        name="combine",
    )(dest, ybuf, x1, route, fg)


def _pad_cols(m, n):
    return jnp.pad(m, ((0, 0), (0, n - m.shape[1])))


def _pad_rows(m, n):
    return jnp.pad(m, ((0, n - m.shape[0]), (0, 0)))


def _pick_tile(n, cap):
    for unit in (MXU_WIDTH, LANES):
        fits = [c for c in range(unit, cap + 1, unit) if n % c == 0]
        if fits:
            return fits[-1]
    raise ValueError(f"no lane-aligned tile divides {n}")


def _dispatch_plan(experts, rank, counts, n_tok):
    tm = EXPERT_ROWS
    n_assign = n_tok * TOP_K
    nt = n_assign // tm + N_EXPERTS
    e_flat = experts.reshape(n_assign)
    rank = rank.reshape(n_assign)
    tiles_e = (counts + tm - 1) // tm
    tile_end = jnp.cumsum(tiles_e)
    tile_start = tile_end - tiles_e
    dest = tile_start[e_flat] * tm + rank
    n_used = tile_end[-1:].astype(jnp.int32)
    tiles = jnp.arange(nt, dtype=jnp.int32)
    tile_e = jnp.minimum(jnp.sum((tile_end[None, :] <= tiles[:, None]).astype(jnp.int32), axis=1),
                         N_EXPERTS - 1).astype(jnp.int32)
    tok = jnp.arange(n_assign, dtype=jnp.int32) // TOP_K
    row_tok = jnp.zeros((nt * tm,), jnp.int32).at[dest].set(tok)
    tile_first = ((tiles == 0) | (tile_e != jnp.roll(tile_e, 1))).astype(jnp.int32)
    next_start = tile_end[tile_e]
    tile_next = jnp.where(next_start < n_used[0], tile_e[jnp.minimum(next_start, nt - 1)], -1).astype(jnp.int32)
    return tile_e, tile_first, tile_next, n_used, row_tok.reshape(nt, tm), dest.astype(jnp.int32)


def _layer(x2, b, s, norm1_g, w_in, attn_out_g, mu, w0, w2, a0, a2, g2, k_k, k_a, r_k, lnx_w, lnx_b,
           w_out, norm2_g, wg, bg, we, be, ew1, ew3, ew2, final_g):
    t, d = x2.shape
    aw = attn_out_g.shape[0]
    cw = w0.shape[0]
    heads = aw // ATT_HEAD_DIM
    dl, al, gl = w2.shape[0], a2.shape[0], g2.shape[0]
    pw, pa, pg = _round_up(dl, LANES), _round_up(al, LANES), _round_up(gl, LANES)
    wpad = 3 * cw + pw + pa + pg

    sh = w_in[:, 3 * aw:]
    o = 3 * cw
    w_arr = jnp.concatenate(
        [sh[:, :o], _pad_cols(sh[:, o:o + dl], pw), _pad_cols(sh[:, o + dl:o + dl + al], pa),
         _pad_cols(sh[:, o + dl + al:o + dl + al + gl], pg), w_in[:, :3 * aw]], axis=1).astype(BF16)
    mu_arr = jnp.concatenate(
        [mu[:o], jnp.pad(mu[o:o + dl], (0, pw - dl)), jnp.pad(mu[o + dl:o + dl + al], (0, pa - al)),
         jnp.pad(mu[o + dl + al:], (0, pg - gl))])[None, :]

    n_all = w_arr.shape[1]
    proj = _in_proj(x2, norm1_g[None, :], w_arr, tm=min(512, t), tn=_pick_tile(n_all, 4096))

    att = _moba(proj, b, s, heads, wpad // LANES)
    r1 = lambda a: a.reshape(1, -1)
    rw = _rwkv(proj, b, s, cw, wpad,
               (mu_arr, r1(w0), _pad_rows(w2, pw).astype(BF16), r1(a0), _pad_rows(a2, pa).astype(BF16),
                _pad_rows(g2, pg).astype(BF16), r1(k_k), r1(k_a), r1(r_k), r1(lnx_w), r1(lnx_b)))

    wr = _pad_cols(jnp.concatenate([wg, we], axis=1), LANES)
    wrh = wr.astype(BF16)
    wrm = (wr - wrh.astype(F32)).astype(BF16)
    br = jnp.pad(jnp.concatenate([bg, be]), (0, LANES - N_GROUPS - N_EXPERTS))[None, :]
    wo = w_out.astype(BF16)
    x1, hn, route, counts = _out_proj(att, rw, x2, r1(attn_out_g), wo[:aw], wo[aw:], r1(norm2_g), wrh, wrm, br,
                                      tm=min(256, t))

    plan_in = route[:, :6].astype(jnp.int32)
    counts = counts[0, N_GROUPS:N_GROUPS + N_EXPERTS].astype(jnp.int32)
    tile_e, tile_first, tile_next, n_used, row_tok, dest = _dispatch_plan(
        plan_in[:, 0:TOP_K], plan_in[:, 4:4 + TOP_K], counts, t)
    sr = d // SLAB_LANES
    ybuf = _experts(tile_e, tile_first, tile_next, n_used, row_tok * sr, hn, ew1, ew3, ew2)

    tg = COMBINE_ROWS
    dest_tiles = dest.reshape(t // tg, tg, TOP_K).transpose(0, 2, 1).reshape(t // tg, TOP_K * tg)
    return _combine(dest_tiles * sr, ybuf, x1, route, r1(final_g))


def kernel(x, norm1_g, w_in, attn_out_g, rwkv_mu, rwkv_w0, rwkv_w2, rwkv_a0, rwkv_a2, rwkv_g2, rwkv_k_k,
           rwkv_k_a, rwkv_r_k, rwkv_lnx_w, rwkv_lnx_b, w_out, norm2_g, router_group_w, router_group_b,
           router_expert_w, router_expert_b, expert_w1, expert_w3, expert_w2, final_g):
    b, s, d = x.shape
    assert norm1_g.shape[0] == 1, "single-layer block"
    assert s % MOBA_BLOCK == 0 and s % RWKV_CHUNK == 0
    assert (b * s) % max(EXPERT_ROWS, COMBINE_ROWS) == 0
    out = _layer(x.reshape(b * s, d), b, s, norm1_g[0], w_in[0], attn_out_g[0], rwkv_mu[0], rwkv_w0[0],
                 rwkv_w2[0], rwkv_a0[0], rwkv_a2[0], rwkv_g2[0], rwkv_k_k[0], rwkv_k_a[0], rwkv_r_k[0],
                 rwkv_lnx_w[0], rwkv_lnx_b[0], w_out[0], norm2_g[0], router_group_w[0], router_group_b[0],
                 router_expert_w[0], router_expert_b[0], expert_w1[0], expert_w3[0], expert_w2[0], final_g)
    return out.reshape(b, s, d)
```

```python
import functools
import math

import jax
import jax.numpy as jnp
from jax import lax
from jax.experimental import pallas as pl
from jax.experimental.pallas import tpu as pltpu

F32 = jnp.float32
BF16 = jnp.bfloat16

LANES = 128
MXU_WIDTH = 256
SLAB_LANES = 128
ATT_HEAD_DIM = 128
RWKV_HEAD_DIM = 64
MOBA_BLOCK = 256
MOBA_TOPK = 3
N_GROUPS = 4
EXPERTS_PER_GROUP = 8
N_EXPERTS = N_GROUPS * EXPERTS_PER_GROUP
TOP_K = 2
RMS_EPS = 1e-6
GN_EPS = 64e-5
NEG_BIG = -1e30
RWKV_CHUNK = 64
EXPERT_ROWS = 256
COMBINE_ROWS = 256
VMEM_LIMIT = 56 * 1024 * 1024
EXPERT_VMEM_LIMIT = 60 * 1024 * 1024
WEIGHT_CAST_CHUNKS = 16


def _round_up(n, m):
    return (n + m - 1) // m * m


def _dot(a, b):
    return jnp.dot(a.astype(BF16), b.astype(BF16), preferred_element_type=F32)


def _dot_nt(a, b):
    return lax.dot_general(a.astype(BF16), b.astype(BF16), (((1,), (1,)), ((), ())),
                           preferred_element_type=F32)


def _split3(a):
    hi = a.astype(BF16)
    r1 = a - hi.astype(F32)
    mid = r1.astype(BF16)
    lo = (r1 - mid.astype(F32)).astype(BF16)
    return hi, mid, lo


def _dot_hi(a, b):
    ah, am, _ = _split3(a)
    bh, bm, _ = _split3(b)
    d = functools.partial(jnp.dot, preferred_element_type=F32)
    return d(ah, bh) + (d(am, bh) + d(ah, bm))


def _sigmoid(x):
    return 1.0 / (1.0 + jnp.exp(-x))


def _rms(x, g):
    ms = jnp.mean(x * x, axis=-1, keepdims=True)
    return x * lax.rsqrt(ms + RMS_EPS) * g


def _in_proj_kernel(x_ref, g_ref, w_ref, o_ref):
    xn = _rms(x_ref[...], g_ref[...]).astype(BF16)
    o_ref[...] = jnp.dot(xn, w_ref[...], preferred_element_type=F32)


def _in_proj(x2, g, w, tm, tn):
    t, d = x2.shape
    n = w.shape[1]
    return pl.pallas_call(
        _in_proj_kernel,
        out_shape=jax.ShapeDtypeStruct((t, n), F32),
        grid=(n // tn, t // tm),
        in_specs=[pl.BlockSpec((tm, d), lambda j, i: (i, 0)),
                  pl.BlockSpec((1, d), lambda j, i: (0, 0)),
                  pl.BlockSpec((d, tn), lambda j, i: (0, j))],
        out_specs=pl.BlockSpec((tm, tn), lambda j, i: (i, j)),
        compiler_params=pltpu.CompilerParams(
            dimension_semantics=("arbitrary", "arbitrary"), vmem_limit_bytes=VMEM_LIMIT),
        name="in_proj",
    )(x2, g, w)


def _moba_kernel(q_ref, k_ref, v_ref, o_ref, *, nblk, scale):
    bs = MOBA_BLOCK
    k = k_ref[...]
    kb = k.astype(BF16)
    vtb = v_ref[...].T.astype(BF16)
    qt = q_ref[...].T
    qtb = qt.astype(BF16)
    kmean = jnp.concatenate(
        [jnp.sum(k[n * bs:(n + 1) * bs, :], axis=0, keepdims=True) for n in range(nblk)], axis=0) * (1.0 / bs)
    gate_all = _dot_hi(kmean, qt)
    kpos = lax.broadcasted_iota(jnp.int32, (bs, bs), 0)
    qpos = lax.broadcasted_iota(jnp.int32, (bs, bs), 1)
    causal = kpos <= qpos

    for qi in range(nblk):
        qs = slice(qi * bs, (qi + 1) * bs)
        nk = (qi + 1) * bs
        s_all = jnp.dot(kb[0:nk, :], qtb[:, qs], preferred_element_type=F32) * scale
        blocks = []
        if qi > MOBA_TOPK:
            gate = gate_all[0:qi, qs]
            blk = lax.broadcasted_iota(jnp.int32, (qi, bs), 0)
        for n in range(qi):
            sn = s_all[n * bs:(n + 1) * bs, :]
            if qi > MOBA_TOPK:
                gn = gate[n:n + 1, :]
                beats = (gate > gn) | ((gate == gn) & (blk < n))
                rank = jnp.sum(jnp.where(beats, 1.0, 0.0), axis=0, keepdims=True)
                sn = jnp.where(rank < float(MOBA_TOPK), sn, NEG_BIG)
            blocks.append(sn)
        blocks.append(jnp.where(causal, s_all[qi * bs:nk, :], NEG_BIG))
        m = functools.reduce(jnp.maximum, [jnp.max(x, axis=0, keepdims=True) for x in blocks])
        ps = [jnp.exp(x - m) for x in blocks]
        l = functools.reduce(jnp.add, [jnp.sum(x, axis=0, keepdims=True) for x in ps])
        pb = jnp.concatenate([x.astype(BF16) for x in ps], axis=0)
        acc = jnp.dot(vtb[:, 0:nk], pb, preferred_element_type=F32)
        o_ref[qs, :] = (acc / l).T


def _moba(proj, b, s, heads, col0):
    d = ATT_HEAD_DIM
    nblk = s // MOBA_BLOCK
    kern = functools.partial(_moba_kernel, nblk=nblk, scale=1.0 / math.sqrt(d))
    return pl.pallas_call(
        kern,
        out_shape=jax.ShapeDtypeStruct((b * s, heads * d), F32),
        grid=(b, heads),
        in_specs=[pl.BlockSpec((s, d), lambda bi, h: (bi, col0 + h)),
                  pl.BlockSpec((s, d), lambda bi, h: (bi, col0 + heads + h)),
                  pl.BlockSpec((s, d), lambda bi, h: (bi, col0 + 2 * heads + h))],
        out_specs=pl.BlockSpec((s, d), lambda bi, h: (bi, h)),
        compiler_params=pltpu.CompilerParams(
            dimension_semantics=("arbitrary", "arbitrary"), vmem_limit_bytes=VMEM_LIMIT),
        name="moba",
    )(proj, proj, proj)


def _head_sum(x, lo):
    s0 = jnp.sum(jnp.where(lo, x, 0.0), axis=-1, keepdims=True)
    s1 = jnp.sum(jnp.where(lo, 0.0, x), axis=-1, keepdims=True)
    return jnp.where(lo, s0, s1)


def _stack_heads(x, lo):
    return jnp.concatenate([jnp.where(lo, x, 0.0), jnp.where(lo, 0.0, x)], axis=0)


def _rwkv_kernel(p_ref, mu_ref, w0_ref, w2_ref, a0_ref, a2_ref, g2_ref, kk_ref, ka_ref, rk_ref,
                 lnw_ref, lnb_ref, o_ref, carry_ref, state_ref, *, cw, pw, pa, pg):
    c = pl.program_id(1)
    C = RWKV_CHUNK
    C2 = 2 * C

    @pl.when(c == 0)
    def _():
        carry_ref[...] = jnp.zeros_like(carry_ref)
        state_ref[...] = jnp.zeros_like(state_ref)

    pr = p_ref[...]
    row = lax.broadcasted_iota(jnp.int32, pr.shape, 0)
    prev = jnp.where(row == 0, carry_ref[...], pltpu.roll(pr, 1, 0))
    carry_ref[...] = pr[C - 1:C, :]
    ps = pr + (prev - pr) * mu_ref[...]

    r = ps[:, 0:cw]
    k = ps[:, cw:2 * cw]
    v = ps[:, 2 * cw:3 * cw]
    o = 3 * cw
    xw = ps[:, o:o + pw]
    xa = ps[:, o + pw:o + pw + pa]
    xg = ps[:, o + pw + pa:o + pw + pa + pg]

    z = -(w0_ref[...] + _dot(jnp.tanh(xw), w2_ref[...]))
    softplus = jnp.maximum(z, 0.0) + jnp.log(1.0 + jnp.exp(-jnp.abs(z)))
    logw = -jnp.exp(-softplus - 0.5)
    asig = _sigmoid(a0_ref[...] + _dot(xa, a2_ref[...]))
    gate = _dot(_sigmoid(xg), g2_ref[...])
    kkr = k * kk_ref[...]
    kmod = k * (1.0 + (asig - 1.0) * ka_ref[...])

    ti = lax.broadcasted_iota(jnp.int32, (C, C), 0)
    si = lax.broadcasted_iota(jnp.int32, (C, C), 1)
    tril = jnp.where(si <= ti, 1.0, 0.0).astype(BF16)
    lh, lm, ll = _split3(logw)
    d32 = functools.partial(jnp.dot, preferred_element_type=F32)
    cum = d32(tril, lh) + (d32(tril, lm) + d32(tril, ll))
    cum_end = cum[C - 1:C, :]
    e_prev = jnp.exp(cum - logw)
    e_incl = jnp.exp(cum)
    e_inv = jnp.exp(-cum)
    e_end = jnp.exp(cum_end - cum)
    g_end = jnp.exp(cum_end)

    lane = lax.broadcasted_iota(jnp.int32, (C, LANES), 1)
    lo = lane < RWKV_HEAD_DIM
    rho = lax.broadcasted_iota(jnp.int32, (C2, C2), 0)
    sig = lax.broadcasted_iota(jnp.int32, (C2, C2), 1)
    same_head = (rho >= C) == (sig >= C)
    strict = same_head & ((sig % C) < (rho % C))
    incl = same_head & ((sig % C) <= (rho % C))
    eye = jnp.where(rho == sig, 1.0, 0.0)

    pairs = range(cw // LANES)
    sls = [slice(p * LANES, (p + 1) * LANES) for p in pairs]
    incl2 = jnp.concatenate([incl, incl], axis=1)
    strict2 = jnp.concatenate([strict, strict], axis=1)

    b2, r2, k2, v2, at, rt, v2s, gm = [], [], [], [], [], [], [], []
    for sl in sls:
        kk2 = kkr[:, sl]
        nrm = jnp.sqrt(_head_sum(kk2 * kk2, lo))
        kkn = kk2 / jnp.maximum(nrm, 1e-12)
        b2.append(kkn * asig[:, sl])
        r2.append(r[:, sl])
        k2.append(kmod[:, sl])
        v2.append(v[:, sl])
        at.append(-kkn * e_prev[:, sl])
        rt.append(r2[-1] * e_incl[:, sl])
        bt = b2[-1] * e_inv[:, sl]
        kt = k2[-1] * e_inv[:, sl]
        v2s.append(_stack_heads(v2[-1], lo))
        lhs = jnp.concatenate([_stack_heads(at[-1], lo), _stack_heads(rt[-1], lo)], axis=0)
        rhs = jnp.concatenate([bt, bt, kt, kt], axis=0)
        gm.append(_dot_nt(lhs, rhs))

    s0 = [state_ref[p] for p in pairs]
    ars = [_dot_nt(jnp.concatenate([at[p], rt[p]], axis=0), s0[p]) for p in pairs]
    labk = [jnp.where(strict2, g[0:C2, :], 0.0) for g in gm]
    mrbk = [jnp.where(incl2, g[C2:2 * C2, :], 0.0) for g in gm]
    rhs_u = [_stack_heads(ars[p][0:C], lo) + _dot(labk[p][:, C2:2 * C2], v2s[p]) for p in pairs]

    lab = [m[:, 0:C2] for m in labk]
    tinv = [eye + m for m in lab]
    pw2 = [_dot(m, m) for m in lab]
    n_pow = int(math.log2(C))
    for k in range(1, n_pow):
        if k < n_pow - 1:
            prod = [_dot(jnp.concatenate([t, m], axis=0), m) for t, m in zip(tinv, pw2)]
            tinv = [t + pr[0:C2] for t, pr in zip(tinv, prod)]
            pw2 = [pr[C2:2 * C2] for pr in prod]
        else:
            tinv = [t + _dot(t, m) for t, m in zip(tinv, pw2)]

    u2s = [_dot(t, x) for t, x in zip(tinv, rhs_u)]
    y2s = [_stack_heads(ars[p][C:C2], lo) + _dot(mrbk[p], jnp.concatenate([u2s[p], v2s[p]], axis=0))
           for p in pairs]
    u2 = [m[0:C] + m[C:C2] for m in u2s]
    y2 = [m[0:C] + m[C:C2] for m in y2s]

    for p, sl in zip(pairs, sls):
        uv = jnp.concatenate([u2[p], v2[p]], axis=0)
        bk = jnp.concatenate([b2[p] * e_end[:, sl], k2[p] * e_end[:, sl]], axis=0)
        upd = _dot(uv.T, bk)
        state_ref[p] = s0[p] * g_end[:, sl] + jnp.where(same_head, upd, 0.0)

    for p, sl in zip(pairs, sls):
        mean = _head_sum(y2[p], lo) * (1.0 / RWKV_HEAD_DIM)
        yc = y2[p] - mean
        var = _head_sum(yc * yc, lo) * (1.0 / RWKV_HEAD_DIM)
        yn = yc * lax.rsqrt(var + GN_EPS) * lnw_ref[:, sl] + lnb_ref[:, sl]
        bonus = _head_sum(r2[p] * k2[p] * rk_ref[:, sl], lo) * v2[p]
        o_ref[:, sl] = (yn + bonus) * gate[:, sl]


def _rwkv(proj, b, s, cw, wpad, params):
    (mu, w0, w2, a0, a2, g2, kk, ka, rk, lnw, lnb) = params
    C = RWKV_CHUNK
    nch = s // C
    pw, pa, pg = w2.shape[0], a2.shape[0], g2.shape[0]
    kern = functools.partial(_rwkv_kernel, cw=cw, pw=pw, pa=pa, pg=pg)
    row = lambda n: pl.BlockSpec((1, n), lambda bi, c: (0, 0))
    full = lambda a: pl.BlockSpec(a.shape, lambda bi, c: (0, 0))
    return pl.pallas_call(
        kern,
        out_shape=jax.ShapeDtypeStruct((b * s, cw), F32),
        grid=(b, nch),
        in_specs=[pl.BlockSpec((C, wpad), lambda bi, c: (bi * nch + c, 0)),
                  row(wpad), row(cw), full(w2), row(cw), full(a2), full(g2),
                  row(cw), row(cw), row(cw), row(cw), row(cw)],
        out_specs=pl.BlockSpec((C, cw), lambda bi, c: (bi * nch + c, 0)),
        scratch_shapes=[pltpu.VMEM((1, wpad), F32),
                        pltpu.VMEM((cw // LANES, LANES, LANES), F32)],
        compiler_params=pltpu.CompilerParams(
            dimension_semantics=("arbitrary", "arbitrary"), vmem_limit_bytes=VMEM_LIMIT),
        name="rwkv",
    )(proj, mu, w0, w2, a0, a2, g2, kk, ka, rk, lnw, lnb)


def _out_proj_kernel(att_ref, rw_ref, x_ref, ga_ref, woa_ref, wor_ref, g2_ref, wrh_ref, wrm_ref, br_ref,
                     x1_ref, hn_ref, route_ref, counts_ref):
    att = _rms(att_ref[...], ga_ref[...])
    y = (jnp.dot(att.astype(BF16), woa_ref[...], preferred_element_type=F32)
         + jnp.dot(rw_ref[...].astype(BF16), wor_ref[...], preferred_element_type=F32))
    x1 = x_ref[...] + y
    x1_ref[...] = x1
    hn = _rms(x1, g2_ref[...])
    _store_slabs(hn_ref, hn)

    hh, hm, _ = _split3(hn)
    d32 = functools.partial(jnp.dot, preferred_element_type=F32)
    logits = d32(hh, wrh_ref[...]) + (d32(hm, wrh_ref[...]) + d32(hh, wrm_ref[...])) + br_ref[...]
    lane = lax.broadcasted_iota(jnp.int32, logits.shape, 1)
    lanef = lane.astype(F32)
    big = float(LANES)

    is_g = lane < N_GROUPS
    gmax = jnp.max(jnp.where(is_g, logits, -jnp.inf), axis=-1, keepdims=True)
    gidx = jnp.min(jnp.where(is_g & (logits == gmax), lanef, big), axis=-1, keepdims=True)
    gsum = jnp.sum(jnp.where(is_g, jnp.exp(logits - gmax), 0.0), axis=-1, keepdims=True)
    g_gate = 1.0 / gsum

    e_lo = N_GROUPS + EXPERTS_PER_GROUP * gidx
    in_grp = (lanef >= e_lo) & (lanef < e_lo + EXPERTS_PER_GROUP)
    e1 = jnp.max(jnp.where(in_grp, logits, -jnp.inf), axis=-1, keepdims=True)
    i1 = jnp.min(jnp.where(in_grp & (logits == e1), lanef, big), axis=-1, keepdims=True)
    rest = in_grp & (lanef != i1)
    e2 = jnp.max(jnp.where(rest, logits, -jnp.inf), axis=-1, keepdims=True)
    i2 = jnp.min(jnp.where(rest & (logits == e2), lanef, big), axis=-1, keepdims=True)
    t = jnp.exp(e2 - e1)
    w1 = g_gate * (1.0 / (1.0 + t))
    w2 = g_gate * (t / (1.0 + t))

    @pl.when(pl.program_id(0) == 0)
    def _():
        counts_ref[...] = jnp.zeros_like(counts_ref)

    tm = logits.shape[0]
    pick1 = lanef == i1
    pick2 = lanef == i2
    onehot = jnp.where(pick1 | pick2, 1.0, 0.0)
    ti = lax.broadcasted_iota(jnp.int32, (tm, tm), 0)
    si = lax.broadcasted_iota(jnp.int32, (tm, tm), 1)
    before = _dot(jnp.where(si < ti, 1.0, 0.0), onehot) + counts_ref[...]
    r1 = jnp.sum(jnp.where(pick1, before, 0.0), axis=-1, keepdims=True)
    r2 = jnp.sum(jnp.where(pick2, before, 0.0), axis=-1, keepdims=True)
    counts_ref[...] = counts_ref[...] + jnp.sum(onehot, axis=0, keepdims=True)

    route_ref[...] = jnp.where(lane == 0, i1 - N_GROUPS,
                     jnp.where(lane == 1, i2 - N_GROUPS,
                     jnp.where(lane == 2, w1,
                     jnp.where(lane == 3, w2,
                     jnp.where(lane == 4, r1,
                     jnp.where(lane == 5, r2, 0.0))))))


def _out_proj(att, rw, x2, ga, woa, wor, g2, wrh, wrm, br, tm):
    t, d = x2.shape
    wa = att.shape[1]
    wr = rw.shape[1]
    sr = d // SLAB_LANES
    c = lambda shape: pl.BlockSpec(shape, lambda i: (0, 0))
    return pl.pallas_call(
        _out_proj_kernel,
        out_shape=(jax.ShapeDtypeStruct((t, d), F32), jax.ShapeDtypeStruct((t * sr, SLAB_LANES), F32),
                   jax.ShapeDtypeStruct((t, LANES), F32), jax.ShapeDtypeStruct((1, LANES), F32)),
        grid=(t // tm,),
        in_specs=[pl.BlockSpec((tm, wa), lambda i: (i, 0)),
                  pl.BlockSpec((tm, wr), lambda i: (i, 0)),
                  pl.BlockSpec((tm, d), lambda i: (i, 0)),
                  c((1, wa)), c((wa, d)), c((wr, d)), c((1, d)), c((d, LANES)), c((d, LANES)), c((1, LANES))],
        out_specs=(pl.BlockSpec((tm, d), lambda i: (i, 0)), pl.BlockSpec((tm * sr, SLAB_LANES), lambda i: (i, 0)),
                   pl.BlockSpec((tm, LANES), lambda i: (i, 0)), c((1, LANES))),
        compiler_params=pltpu.CompilerParams(
            dimension_semantics=("arbitrary",), vmem_limit_bytes=VMEM_LIMIT),
        name="out_proj",
    )(att, rw, x2, ga, woa, wor, g2, wrh, wrm, br)


def _store_slabs(ref, x):
    rows, d = x.shape
    sr = d // SLAB_LANES
    for s in range(sr):
        ref[pl.ds(s, rows, stride=sr), :] = x[:, s * SLAB_LANES:(s + 1) * SLAB_LANES]


def _load_slabs(ref, slot, first, rows, sr):
    return jnp.concatenate([ref[slot, pl.ds(first * sr + s, rows, stride=sr), :] for s in range(sr)], axis=1)


def _gather_rows(i, nt, idx_hbm, src_hbm, idx_ref, buf_ref, isem, gsem, rows, sr):
    def idx_copy(tile, slot):
        return pltpu.make_async_copy(idx_hbm.at[pl.ds(tile, 1)], idx_ref.at[pl.ds(slot, 1)], isem.at[slot])

    def start_rows(slot):
        for r in range(rows):
            src = pl.multiple_of(idx_ref[slot, r], sr)
            pltpu.make_async_copy(src_hbm.at[pl.ds(src, sr)], buf_ref.at[slot, pl.ds(r * sr, sr)],
                                  gsem.at[slot]).start()

    slot = jnp.bitwise_and(i, 1)

    @pl.when(i == 0)
    def _():
        idx_copy(0, 0).start()
        idx_copy(0, 0).wait()
        start_rows(0)
        if nt > 1:
            idx_copy(1, 1).start()

    @pl.when(i + 1 < nt)
    def _():
        idx_copy(i + 1, 1 - slot).wait()
        start_rows(1 - slot)

    @pl.when(i + 2 < nt)
    def _():
        idx_copy(i + 2, slot).start()

    pltpu.make_async_copy(src_hbm.at[pl.ds(0, rows * sr)], buf_ref.at[slot], gsem.at[slot]).wait()
    return slot


def _dispatch_kernel(idx_hbm, x_hbm, o_hbm, idx_ref, zero_ref, isem, csem, *, nt, n_tok, n_pad, sr):
    i = pl.program_id(0)
    slot = jnp.bitwise_and(i, 1)
    other = 1 - slot
    tg = n_tok // TOP_K
    n_all = n_tok + n_pad

    def idx_copy(tile, s):
        return pltpu.make_async_copy(idx_hbm.at[pl.ds(tile, 1)], idx_ref.at[pl.ds(s, 1)], isem.at[s])

    def copies_done(s):
        return pltpu.make_async_copy(x_hbm.at[pl.ds(0, n_all * sr)], o_hbm.at[pl.ds(0, n_all * sr)], csem.at[s])

    @pl.when(i == 0)
    def _():
        zero_ref[...] = jnp.zeros_like(zero_ref)
        idx_copy(0, 0).start()

    idx_copy(i, slot).wait()

    @pl.when(i + 1 < nt)
    def _():
        idx_copy(i + 1, other).start()

    for r in range(n_tok):
        src = pl.multiple_of((i * tg + (r % tg)) * sr, sr)
        dst = pl.multiple_of(idx_ref[slot, r], sr)
        pltpu.make_async_copy(x_hbm.at[pl.ds(src, sr)], o_hbm.at[pl.ds(dst, sr)], csem.at[slot]).start()
    for r in range(n_tok, n_all):
        dst = pl.multiple_of(idx_ref[slot, r], sr)
        pltpu.make_async_copy(zero_ref, o_hbm.at[pl.ds(dst, sr)], csem.at[slot]).start()

    @pl.when(i > 0)
    def _():
        copies_done(other).wait()

    @pl.when(i == nt - 1)
    def _():
        copies_done(slot).wait()


def _dispatch(idx, hn_slabs, n_rows, sr):
    nt, n_all = idx.shape
    n_tok = TOP_K * COMBINE_ROWS
    hbm = pl.BlockSpec(memory_space=pl.ANY)
    return pl.pallas_call(
        functools.partial(_dispatch_kernel, nt=nt, n_tok=n_tok, n_pad=n_all - n_tok, sr=sr),
        out_shape=jax.ShapeDtypeStruct((n_rows * sr, SLAB_LANES), F32),
        grid=(nt,),
        in_specs=[hbm, hbm],
        out_specs=hbm,
        scratch_shapes=[pltpu.SMEM((2, n_all), jnp.int32), pltpu.VMEM((sr, SLAB_LANES), F32),
                        pltpu.SemaphoreType.DMA((2,)), pltpu.SemaphoreType.DMA((2,))],
        compiler_params=pltpu.CompilerParams(dimension_semantics=("arbitrary",)),
        name="dispatch",
    )(idx, hn_slabs)


def _expert_kernel(te_ref, first_ref, next_ref, nu_ref, x_ref, w1_hbm, w3_hbm, w2_hbm, o_ref,
                   stage_refs, wb_refs, wsem, *, sr):
    i = pl.program_id(0)
    tm = EXPERT_ROWS
    w_hbm = (w1_hbm, w3_hbm, w2_hbm)
    used = i < nu_ref[0]

    def weight_copies(e):
        return [pltpu.make_async_copy(w.at[e], st, wsem.at[j]) for j, (w, st) in enumerate(zip(w_hbm, stage_refs))]

    @pl.when(i == 0)
    def _():
        for cp in weight_copies(te_ref[0]):
            cp.start()

    @pl.when(used & (first_ref[i] == 1))
    def _():
        for cp in weight_copies(te_ref[i]):
            cp.wait()
        for st, wb in zip(stage_refs, wb_refs):
            rows = st.shape[0] // WEIGHT_CAST_CHUNKS

            def cast_chunk(c, carry, st=st, wb=wb, rows=rows):
                r0 = pl.multiple_of(c * rows, rows)
                wb[pl.ds(r0, rows), :] = st[pl.ds(r0, rows), :].astype(BF16)
                return carry

            lax.fori_loop(0, WEIGHT_CAST_CHUNKS, cast_chunk, 0)

        @pl.when(next_ref[i] >= 0)
        def _():
            for cp in weight_copies(next_ref[i]):
                cp.start()

    @pl.when(used)
    def _():
        xb = jnp.concatenate([x_ref[pl.ds(s, tm, stride=sr), :] for s in range(sr)], axis=1).astype(BF16)
        h1 = jnp.dot(xb, wb_refs[0][...], preferred_element_type=F32)
        h3 = jnp.dot(xb, wb_refs[1][...], preferred_element_type=F32)
        hid = (h1 * _sigmoid(h1)) * h3
        _store_slabs(o_ref, jnp.dot(hid.astype(BF16), wb_refs[2][...], preferred_element_type=F32))

    @pl.when(jnp.logical_not(used))
    def _():
        o_ref[...] = jnp.zeros_like(o_ref)


def _experts(tile_e, tile_first, tile_next, n_used, x_slabs, w1, w3, w2):
    nt = tile_e.shape[0]
    d, de = w1.shape[1], w1.shape[2]
    sr = d // SLAB_LANES
    tm = EXPERT_ROWS
    hbm = pl.BlockSpec(memory_space=pl.ANY)
    grid_spec = pltpu.PrefetchScalarGridSpec(
        num_scalar_prefetch=4,
        grid=(nt,),
        in_specs=[pl.BlockSpec((tm * sr, SLAB_LANES), lambda i, te, fi, nx, nu: (jnp.minimum(i, nu[0] - 1), 0)),
                  hbm, hbm, hbm],
        out_specs=pl.BlockSpec((tm * sr, SLAB_LANES), lambda i, *_: (i, 0)),
        scratch_shapes=[[pltpu.VMEM((d, de), F32), pltpu.VMEM((d, de), F32), pltpu.VMEM((de, d), F32)],
                        [pltpu.VMEM((d, de), BF16), pltpu.VMEM((d, de), BF16), pltpu.VMEM((de, d), BF16)],
                        pltpu.SemaphoreType.DMA((3,))],
    )
    return pl.pallas_call(
        functools.partial(_expert_kernel, sr=sr),
        out_shape=jax.ShapeDtypeStruct((nt * tm * sr, SLAB_LANES), F32),
        grid_spec=grid_spec,
        compiler_params=pltpu.CompilerParams(
            dimension_semantics=("arbitrary",), vmem_limit_bytes=EXPERT_VMEM_LIMIT),
        name="experts",
    )(tile_e, tile_first, tile_next, n_used, x_slabs, w1, w3, w2)


def _combine_kernel(dest_hbm, y_hbm, x1_ref, route_ref, fg_ref, o_ref, idx_ref, buf_ref, isem, gsem,
                    *, nt, sr):
    i = pl.program_id(0)
    tg = COMBINE_ROWS
    slot = _gather_rows(i, nt, dest_hbm, y_hbm, idx_ref, buf_ref, isem, gsem, TOP_K * tg, sr)
    route = route_ref[...]
    y = (_load_slabs(buf_ref, slot, 0, tg, sr) * route[:, 2:3]
         + _load_slabs(buf_ref, slot, tg, tg, sr) * route[:, 3:4])
    o_ref[...] = _rms(x1_ref[...] + y, fg_ref[...])


def _combine(dest, ybuf, x1, route, fg):
    t, d = x1.shape
    tg = COMBINE_ROWS
    sr = d // SLAB_LANES
    return pl.pallas_call(
        functools.partial(_combine_kernel, nt=t // tg, sr=sr),
        out_shape=jax.ShapeDtypeStruct((t, d), F32),
        grid=(t // tg,),
        in_specs=[pl.BlockSpec(memory_space=pl.ANY),
                  pl.BlockSpec(memory_space=pl.ANY),
                  pl.BlockSpec((tg, d), lambda i: (i, 0)),
                  pl.BlockSpec((tg, LANES), lambda i: (i, 0)),
                  pl.BlockSpec((1, d), lambda i: (0, 0))],
        out_specs=pl.BlockSpec((tg, d), lambda i: (i, 0)),
        scratch_shapes=[pltpu.SMEM((2, TOP_K * tg), jnp.int32),
                        pltpu.VMEM((2, TOP_K * tg * sr, SLAB_LANES), F32),
                        pltpu.SemaphoreType.DMA((2,)), pltpu.SemaphoreType.DMA((2,))],
        compiler_params=pltpu.CompilerParams(
            dimension_semantics=("arbitrary",), vmem_limit_bytes=VMEM_LIMIT),
        name="combine",
    )(dest, ybuf, x1, route, fg)


def _pad_cols(m, n):
    return jnp.pad(m, ((0, 0), (0, n - m.shape[1])))


def _pad_rows(m, n):
    return jnp.pad(m, ((0, n - m.shape[0]), (0, 0)))


def _pick_tile(n, cap):
    for unit in (MXU_WIDTH, LANES):
        fits = [c for c in range(unit, cap + 1, unit) if n % c == 0]
        if fits:
            return fits[-1]
    raise ValueError(f"no lane-aligned tile divides {n}")


def _dispatch_plan(experts, rank, counts, n_tok):
    tm = EXPERT_ROWS
    nt = n_tok * TOP_K // tm + N_EXPERTS
    tiles_e = (counts + tm - 1) // tm
    tile_end = jnp.cumsum(tiles_e)
    tile_start = tile_end - tiles_e
    dest = (tile_start[experts] * tm + rank).astype(jnp.int32)
    n_used = tile_end[-1:].astype(jnp.int32)
    tiles = jnp.arange(nt, dtype=jnp.int32)
    tile_e = jnp.minimum(jnp.sum((tile_end[None, :] <= tiles[:, None]).astype(jnp.int32), axis=1),
                         N_EXPERTS - 1).astype(jnp.int32)
    tile_first = ((tiles == 0) | (tile_e != jnp.roll(tile_e, 1))).astype(jnp.int32)
    next_start = tile_end[tile_e]
    tile_next = jnp.where(next_start < n_used[0], tile_e[jnp.minimum(next_start, nt - 1)], -1).astype(jnp.int32)

    seg_len = jnp.concatenate([tiles_e * tm - counts, (nt - n_used) * tm])
    seg_first = jnp.concatenate([tile_start * tm + counts, n_used * tm])
    seg_stop = jnp.cumsum(seg_len)
    j = jnp.arange(N_EXPERTS * tm, dtype=jnp.int32)
    seg = jnp.sum((seg_stop[None, :] <= j[:, None]).astype(jnp.int32), axis=1)
    pad_rows = (seg_first[seg] + j - (seg_stop - seg_len)[seg]).astype(jnp.int32)
    return tile_e, tile_first, tile_next, n_used, dest, pad_rows


def _layer(x2, b, s, norm1_g, w_in, attn_out_g, mu, w0, w2, a0, a2, g2, k_k, k_a, r_k, lnx_w, lnx_b,
           w_out, norm2_g, wg, bg, we, be, ew1, ew3, ew2, final_g):
    t, d = x2.shape
    aw = attn_out_g.shape[0]
    cw = w0.shape[0]
    heads = aw // ATT_HEAD_DIM
    dl, al, gl = w2.shape[0], a2.shape[0], g2.shape[0]
    pw, pa, pg = _round_up(dl, LANES), _round_up(al, LANES), _round_up(gl, LANES)
    wpad = 3 * cw + pw + pa + pg

    sh = w_in[:, 3 * aw:]
    o = 3 * cw
    w_arr = jnp.concatenate(
        [sh[:, :o], _pad_cols(sh[:, o:o + dl], pw), _pad_cols(sh[:, o + dl:o + dl + al], pa),
         _pad_cols(sh[:, o + dl + al:o + dl + al + gl], pg), w_in[:, :3 * aw]], axis=1).astype(BF16)
    mu_arr = jnp.concatenate(
        [mu[:o], jnp.pad(mu[o:o + dl], (0, pw - dl)), jnp.pad(mu[o + dl:o + dl + al], (0, pa - al)),
         jnp.pad(mu[o + dl + al:], (0, pg - gl))])[None, :]

    n_all = w_arr.shape[1]
    proj = _in_proj(x2, norm1_g[None, :], w_arr, tm=min(512, t), tn=_pick_tile(n_all, 4096))

    att = _moba(proj, b, s, heads, wpad // LANES)
    r1 = lambda a: a.reshape(1, -1)
    rw = _rwkv(proj, b, s, cw, wpad,
               (mu_arr, r1(w0), _pad_rows(w2, pw).astype(BF16), r1(a0), _pad_rows(a2, pa).astype(BF16),
                _pad_rows(g2, pg).astype(BF16), r1(k_k), r1(k_a), r1(r_k), r1(lnx_w), r1(lnx_b)))

    wr = _pad_cols(jnp.concatenate([wg, we], axis=1), LANES)
    wrh = wr.astype(BF16)
    wrm = (wr - wrh.astype(F32)).astype(BF16)
    br = jnp.pad(jnp.concatenate([bg, be]), (0, LANES - N_GROUPS - N_EXPERTS))[None, :]
    wo = w_out.astype(BF16)
    x1, hn, route, counts = _out_proj(att, rw, x2, r1(attn_out_g), wo[:aw], wo[aw:], r1(norm2_g), wrh, wrm, br,
                                      tm=min(256, t))

    plan_in = route[:, :6].astype(jnp.int32)
    counts = counts[0, N_GROUPS:N_GROUPS + N_EXPERTS].astype(jnp.int32)
    tg = COMBINE_ROWS
    n_tiles = t // tg
    pad_per_tile = N_EXPERTS * EXPERT_ROWS // n_tiles
    tile_e, tile_first, tile_next, n_used, dest, pad_rows = _dispatch_plan(
        plan_in[:, 0:TOP_K], plan_in[:, 4:4 + TOP_K], counts, t)

    sr = d // SLAB_LANES
    dest_tiles = dest.reshape(n_tiles, tg, TOP_K).transpose(0, 2, 1).reshape(n_tiles, TOP_K * tg) * sr
    disp_idx = jnp.concatenate([dest_tiles, pad_rows.reshape(n_tiles, pad_per_tile) * sr], axis=1)
    xbuf = _dispatch(disp_idx, hn, tile_e.shape[0] * EXPERT_ROWS, sr)
    ybuf = _experts(tile_e, tile_first, tile_next, n_used, xbuf, ew1, ew3, ew2)
    return _combine(dest_tiles, ybuf, x1, route, r1(final_g))


def kernel(x, norm1_g, w_in, attn_out_g, rwkv_mu, rwkv_w0, rwkv_w2, rwkv_a0, rwkv_a2, rwkv_g2, rwkv_k_k,
           rwkv_k_a, rwkv_r_k, rwkv_lnx_w, rwkv_lnx_b, w_out, norm2_g, router_group_w, router_group_b,
           router_expert_w, router_expert_b, expert_w1, expert_w3, expert_w2, final_g):
    b, s, d = x.shape
    assert norm1_g.shape[0] == 1, "single-layer block"
    assert s % MOBA_BLOCK == 0 and s % RWKV_CHUNK == 0
    assert (b * s) % max(EXPERT_ROWS, COMBINE_ROWS) == 0
    assert (N_EXPERTS * EXPERT_ROWS) % ((b * s) // COMBINE_ROWS) == 0, "zero-fill rows split evenly over tiles"
    out = _layer(x.reshape(b * s, d), b, s, norm1_g[0], w_in[0], attn_out_g[0], rwkv_mu[0], rwkv_w0[0],
                 rwkv_w2[0], rwkv_a0[0], rwkv_a2[0], rwkv_g2[0], rwkv_k_k[0], rwkv_k_a[0], rwkv_r_k[0],
                 rwkv_lnx_w[0], rwkv_lnx_b[0], w_out[0], norm2_g[0], router_group_w[0], router_group_b[0],
                 router_expert_w[0], router_expert_b[0], expert_w1[0], expert_w3[0], expert_w2[0], final_g)
    return out.reshape(b, s, d)
```

```python
import functools
import math

import jax
import jax.numpy as jnp
from jax import lax
from jax.experimental import pallas as pl
from jax.experimental.pallas import tpu as pltpu

F32 = jnp.float32
BF16 = jnp.bfloat16

LANES = 128
MXU_WIDTH = 256
SLAB_LANES = 128
ATT_HEAD_DIM = 128
RWKV_HEAD_DIM = 64
MOBA_BLOCK = 256
MOBA_TOPK = 3
N_GROUPS = 4
EXPERTS_PER_GROUP = 8
N_EXPERTS = N_GROUPS * EXPERTS_PER_GROUP
TOP_K = 2
RMS_EPS = 1e-6
GN_EPS = 64e-5
NEG_BIG = -1e30
RWKV_CHUNK = 64
EXPERT_ROWS = 256
COMBINE_ROWS = 256
VMEM_LIMIT = 56 * 1024 * 1024
EXPERT_VMEM_LIMIT = 60 * 1024 * 1024
WEIGHT_CAST_CHUNKS = 16


def _round_up(n, m):
    return (n + m - 1) // m * m


def _dot(a, b):
    return jnp.dot(a.astype(BF16), b.astype(BF16), preferred_element_type=F32)


def _dot_nt(a, b):
    return lax.dot_general(a.astype(BF16), b.astype(BF16), (((1,), (1,)), ((), ())),
                           preferred_element_type=F32)


def _split3(a):
    hi = a.astype(BF16)
    r1 = a - hi.astype(F32)
    mid = r1.astype(BF16)
    lo = (r1 - mid.astype(F32)).astype(BF16)
    return hi, mid, lo


def _dot_hi(a, b):
    ah, am, _ = _split3(a)
    bh, bm, _ = _split3(b)
    d = functools.partial(jnp.dot, preferred_element_type=F32)
    return d(ah, bh) + (d(am, bh) + d(ah, bm))


def _sigmoid(x):
    return 1.0 / (1.0 + jnp.exp(-x))


def _rms(x, g):
    ms = jnp.mean(x * x, axis=-1, keepdims=True)
    return x * lax.rsqrt(ms + RMS_EPS) * g


def _in_proj_kernel(x_ref, g_ref, w_ref, o_ref):
    xn = _rms(x_ref[...], g_ref[...]).astype(BF16)
    o_ref[...] = jnp.dot(xn, w_ref[...], preferred_element_type=F32)


def _in_proj(x2, g, w, tm, tn):
    t, d = x2.shape
    n = w.shape[1]
    return pl.pallas_call(
        _in_proj_kernel,
        out_shape=jax.ShapeDtypeStruct((t, n), F32),
        grid=(n // tn, t // tm),
        in_specs=[pl.BlockSpec((tm, d), lambda j, i: (i, 0)),
                  pl.BlockSpec((1, d), lambda j, i: (0, 0)),
                  pl.BlockSpec((d, tn), lambda j, i: (0, j))],
        out_specs=pl.BlockSpec((tm, tn), lambda j, i: (i, j)),
        compiler_params=pltpu.CompilerParams(
            dimension_semantics=("arbitrary", "arbitrary"), vmem_limit_bytes=VMEM_LIMIT),
        name="in_proj",
    )(x2, g, w)


def _moba_kernel(q_ref, k_ref, v_ref, o_ref, *, nblk, scale):
    bs = MOBA_BLOCK
    k = k_ref[...]
    kb = k.astype(BF16)
    vtb = v_ref[...].T.astype(BF16)
    qt = q_ref[...].T
    qtb = qt.astype(BF16)
    kmean = jnp.concatenate(
        [jnp.sum(k[n * bs:(n + 1) * bs, :], axis=0, keepdims=True) for n in range(nblk)], axis=0) * (1.0 / bs)
    gate_all = _dot_hi(kmean, qt)
    kpos = lax.broadcasted_iota(jnp.int32, (bs, bs), 0)
    qpos = lax.broadcasted_iota(jnp.int32, (bs, bs), 1)
    causal = kpos <= qpos

    for qi in range(nblk):
        qs = slice(qi * bs, (qi + 1) * bs)
        nk = (qi + 1) * bs
        s_all = jnp.dot(kb[0:nk, :], qtb[:, qs], preferred_element_type=F32) * scale
        blocks = []
        if qi > MOBA_TOPK:
            gate = gate_all[0:qi, qs]
            blk = lax.broadcasted_iota(jnp.int32, (qi, bs), 0)
        for n in range(qi):
            sn = s_all[n * bs:(n + 1) * bs, :]
            if qi > MOBA_TOPK:
                gn = gate[n:n + 1, :]
                beats = (gate > gn) | ((gate == gn) & (blk < n))
                rank = jnp.sum(jnp.where(beats, 1.0, 0.0), axis=0, keepdims=True)
                sn = jnp.where(rank < float(MOBA_TOPK), sn, NEG_BIG)
            blocks.append(sn)
        blocks.append(jnp.where(causal, s_all[qi * bs:nk, :], NEG_BIG))
        m = functools.reduce(jnp.maximum, [jnp.max(x, axis=0, keepdims=True) for x in blocks])
        ps = [jnp.exp(x - m) for x in blocks]
        l = functools.reduce(jnp.add, [jnp.sum(x, axis=0, keepdims=True) for x in ps])
        pb = jnp.concatenate([x.astype(BF16) for x in ps], axis=0)
        acc = jnp.dot(vtb[:, 0:nk], pb, preferred_element_type=F32)
        o_ref[qs, :] = (acc / l).T


def _moba(proj, b, s, heads, col0):
    d = ATT_HEAD_DIM
    nblk = s // MOBA_BLOCK
    kern = functools.partial(_moba_kernel, nblk=nblk, scale=1.0 / math.sqrt(d))
    return pl.pallas_call(
        kern,
        out_shape=jax.ShapeDtypeStruct((b * s, heads * d), F32),
        grid=(b, heads),
        in_specs=[pl.BlockSpec((s, d), lambda bi, h: (bi, col0 + h)),
                  pl.BlockSpec((s, d), lambda bi, h: (bi, col0 + heads + h)),
                  pl.BlockSpec((s, d), lambda bi, h: (bi, col0 + 2 * heads + h))],
        out_specs=pl.BlockSpec((s, d), lambda bi, h: (bi, h)),
        compiler_params=pltpu.CompilerParams(
            dimension_semantics=("arbitrary", "arbitrary"), vmem_limit_bytes=VMEM_LIMIT),
        name="moba",
    )(proj, proj, proj)


def _head_sum(x, lo):
    s0 = jnp.sum(jnp.where(lo, x, 0.0), axis=-1, keepdims=True)
    s1 = jnp.sum(jnp.where(lo, 0.0, x), axis=-1, keepdims=True)
    return jnp.where(lo, s0, s1)


def _stack_heads(x, lo):
    return jnp.concatenate([jnp.where(lo, x, 0.0), jnp.where(lo, 0.0, x)], axis=0)


def _rwkv_kernel(p_ref, mu_ref, w0_ref, w2_ref, a0_ref, a2_ref, g2_ref, kk_ref, ka_ref, rk_ref,
                 lnw_ref, lnb_ref, o_ref, carry_ref, state_ref, *, cw, pw, pa, pg):
    c = pl.program_id(1)
    C = RWKV_CHUNK
    C2 = 2 * C

    @pl.when(c == 0)
    def _():
        carry_ref[...] = jnp.zeros_like(carry_ref)
        state_ref[...] = jnp.zeros_like(state_ref)

    pr = p_ref[...]
    row = lax.broadcasted_iota(jnp.int32, pr.shape, 0)
    prev = jnp.where(row == 0, carry_ref[...], pltpu.roll(pr, 1, 0))
    carry_ref[...] = pr[C - 1:C, :]
    ps = pr + (prev - pr) * mu_ref[...]

    r = ps[:, 0:cw]
    k = ps[:, cw:2 * cw]
    v = ps[:, 2 * cw:3 * cw]
    o = 3 * cw
    xw = ps[:, o:o + pw]
    xa = ps[:, o + pw:o + pw + pa]
    xg = ps[:, o + pw + pa:o + pw + pa + pg]

    z = -(w0_ref[...] + _dot(jnp.tanh(xw), w2_ref[...]))
    softplus = jnp.maximum(z, 0.0) + jnp.log(1.0 + jnp.exp(-jnp.abs(z)))
    logw = -jnp.exp(-softplus - 0.5)
    asig = _sigmoid(a0_ref[...] + _dot(xa, a2_ref[...]))
    gate = _dot(_sigmoid(xg), g2_ref[...])
    kkr = k * kk_ref[...]
    kmod = k * (1.0 + (asig - 1.0) * ka_ref[...])

    ti = lax.broadcasted_iota(jnp.int32, (C, C), 0)
    si = lax.broadcasted_iota(jnp.int32, (C, C), 1)
    tril = jnp.where(si <= ti, 1.0, 0.0).astype(BF16)
    lh, lm, ll = _split3(logw)
    d32 = functools.partial(jnp.dot, preferred_element_type=F32)
    cum = d32(tril, lh) + (d32(tril, lm) + d32(tril, ll))
    cum_end = cum[C - 1:C, :]
    e_prev = jnp.exp(cum - logw)
    e_incl = jnp.exp(cum)
    e_inv = jnp.exp(-cum)
    e_end = jnp.exp(cum_end - cum)
    g_end = jnp.exp(cum_end)

    lane = lax.broadcasted_iota(jnp.int32, (C, LANES), 1)
    lo = lane < RWKV_HEAD_DIM
    rho = lax.broadcasted_iota(jnp.int32, (C2, C2), 0)
    sig = lax.broadcasted_iota(jnp.int32, (C2, C2), 1)
    same_head = (rho >= C) == (sig >= C)
    strict = same_head & ((sig % C) < (rho % C))
    incl = same_head & ((sig % C) <= (rho % C))
    eye = jnp.where(rho == sig, 1.0, 0.0)

    pairs = range(cw // LANES)
    sls = [slice(p * LANES, (p + 1) * LANES) for p in pairs]
    incl2 = jnp.concatenate([incl, incl], axis=1)
    strict2 = jnp.concatenate([strict, strict], axis=1)

    b2, r2, k2, v2, at, rt, v2s, gm = [], [], [], [], [], [], [], []
    for sl in sls:
        kk2 = kkr[:, sl]
        nrm = jnp.sqrt(_head_sum(kk2 * kk2, lo))
        kkn = kk2 / jnp.maximum(nrm, 1e-12)
        b2.append(kkn * asig[:, sl])
        r2.append(r[:, sl])
        k2.append(kmod[:, sl])
        v2.append(v[:, sl])
        at.append(-kkn * e_prev[:, sl])
        rt.append(r2[-1] * e_incl[:, sl])
        bt = b2[-1] * e_inv[:, sl]
        kt = k2[-1] * e_inv[:, sl]
        v2s.append(_stack_heads(v2[-1], lo))
        lhs = jnp.concatenate([_stack_heads(at[-1], lo), _stack_heads(rt[-1], lo)], axis=0)
        rhs = jnp.concatenate([bt, bt, kt, kt], axis=0)
        gm.append(_dot_nt(lhs, rhs))

    s0 = [state_ref[p] for p in pairs]
    ars = [_dot_nt(jnp.concatenate([at[p], rt[p]], axis=0), s0[p]) for p in pairs]
    labk = [jnp.where(strict2, g[0:C2, :], 0.0) for g in gm]
    mrbk = [jnp.where(incl2, g[C2:2 * C2, :], 0.0) for g in gm]
    rhs_u = [_stack_heads(ars[p][0:C], lo) + _dot(labk[p][:, C2:2 * C2], v2s[p]) for p in pairs]

    lab = [m[:, 0:C2] for m in labk]
    tinv = [eye + m for m in lab]
    pw2 = [_dot(m, m) for m in lab]
    n_pow = int(math.log2(C))
    for k in range(1, n_pow):
        if k < n_pow - 1:
            prod = [_dot(jnp.concatenate([t, m], axis=0), m) for t, m in zip(tinv, pw2)]
            tinv = [t + pr[0:C2] for t, pr in zip(tinv, prod)]
            pw2 = [pr[C2:2 * C2] for pr in prod]
        else:
            tinv = [t + _dot(t, m) for t, m in zip(tinv, pw2)]

    u2s = [_dot(t, x) for t, x in zip(tinv, rhs_u)]
    y2s = [_stack_heads(ars[p][C:C2], lo) + _dot(mrbk[p], jnp.concatenate([u2s[p], v2s[p]], axis=0))
           for p in pairs]
    u2 = [m[0:C] + m[C:C2] for m in u2s]
    y2 = [m[0:C] + m[C:C2] for m in y2s]

    for p, sl in zip(pairs, sls):
        uv = jnp.concatenate([u2[p], v2[p]], axis=0)
        bk = jnp.concatenate([b2[p] * e_end[:, sl], k2[p] * e_end[:, sl]], axis=0)
        upd = _dot(uv.T, bk)
        state_ref[p] = s0[p] * g_end[:, sl] + jnp.where(same_head, upd, 0.0)

    for p, sl in zip(pairs, sls):
        mean = _head_sum(y2[p], lo) * (1.0 / RWKV_HEAD_DIM)
        yc = y2[p] - mean
        var = _head_sum(yc * yc, lo) * (1.0 / RWKV_HEAD_DIM)
        yn = yc * lax.rsqrt(var + GN_EPS) * lnw_ref[:, sl] + lnb_ref[:, sl]
        bonus = _head_sum(r2[p] * k2[p] * rk_ref[:, sl], lo) * v2[p]
        o_ref[:, sl] = (yn + bonus) * gate[:, sl]


def _rwkv(proj, b, s, cw, wpad, params):
    (mu, w0, w2, a0, a2, g2, kk, ka, rk, lnw, lnb) = params
    C = RWKV_CHUNK
    nch = s // C
    pw, pa, pg = w2.shape[0], a2.shape[0], g2.shape[0]
    kern = functools.partial(_rwkv_kernel, cw=cw, pw=pw, pa=pa, pg=pg)
    row = lambda n: pl.BlockSpec((1, n), lambda bi, c: (0, 0))
    full = lambda a: pl.BlockSpec(a.shape, lambda bi, c: (0, 0))
    return pl.pallas_call(
        kern,
        out_shape=jax.ShapeDtypeStruct((b * s, cw), F32),
        grid=(b, nch),
        in_specs=[pl.BlockSpec((C, wpad), lambda bi, c: (bi * nch + c, 0)),
                  row(wpad), row(cw), full(w2), row(cw), full(a2), full(g2),
                  row(cw), row(cw), row(cw), row(cw), row(cw)],
        out_specs=pl.BlockSpec((C, cw), lambda bi, c: (bi * nch + c, 0)),
        scratch_shapes=[pltpu.VMEM((1, wpad), F32),
                        pltpu.VMEM((cw // LANES, LANES, LANES), F32)],
        compiler_params=pltpu.CompilerParams(
            dimension_semantics=("arbitrary", "arbitrary"), vmem_limit_bytes=VMEM_LIMIT),
        name="rwkv",
    )(proj, mu, w0, w2, a0, a2, g2, kk, ka, rk, lnw, lnb)


def _out_proj_kernel(att_ref, rw_ref, x_ref, ga_ref, woa_ref, wor_ref, g2_ref, wrh_ref, wrm_ref, br_ref,
                     x1_ref, hn_ref, route_ref, counts_ref):
    att = _rms(att_ref[...], ga_ref[...])
    y = (jnp.dot(att.astype(BF16), woa_ref[...], preferred_element_type=F32)
         + jnp.dot(rw_ref[...].astype(BF16), wor_ref[...], preferred_element_type=F32))
    x1 = x_ref[...] + y
    x1_ref[...] = x1
    hn = _rms(x1, g2_ref[...])
    _store_slabs(hn_ref, hn)

    hh, hm, _ = _split3(hn)
    d32 = functools.partial(jnp.dot, preferred_element_type=F32)
    logits = d32(hh, wrh_ref[...]) + (d32(hm, wrh_ref[...]) + d32(hh, wrm_ref[...])) + br_ref[...]
    lane = lax.broadcasted_iota(jnp.int32, logits.shape, 1)
    lanef = lane.astype(F32)
    big = float(LANES)

    is_g = lane < N_GROUPS
    gmax = jnp.max(jnp.where(is_g, logits, -jnp.inf), axis=-1, keepdims=True)
    gidx = jnp.min(jnp.where(is_g & (logits == gmax), lanef, big), axis=-1, keepdims=True)
    gsum = jnp.sum(jnp.where(is_g, jnp.exp(logits - gmax), 0.0), axis=-1, keepdims=True)
    g_gate = 1.0 / gsum

    e_lo = N_GROUPS + EXPERTS_PER_GROUP * gidx
    in_grp = (lanef >= e_lo) & (lanef < e_lo + EXPERTS_PER_GROUP)
    e1 = jnp.max(jnp.where(in_grp, logits, -jnp.inf), axis=-1, keepdims=True)
    i1 = jnp.min(jnp.where(in_grp & (logits == e1), lanef, big), axis=-1, keepdims=True)
    rest = in_grp & (lanef != i1)
    e2 = jnp.max(jnp.where(rest, logits, -jnp.inf), axis=-1, keepdims=True)
    i2 = jnp.min(jnp.where(rest & (logits == e2), lanef, big), axis=-1, keepdims=True)
    t = jnp.exp(e2 - e1)
    w1 = g_gate * (1.0 / (1.0 + t))
    w2 = g_gate * (t / (1.0 + t))

    @pl.when(pl.program_id(0) == 0)
    def _():
        counts_ref[...] = jnp.zeros_like(counts_ref)

    tm = logits.shape[0]
    pick1 = lanef == i1
    pick2 = lanef == i2
    onehot = jnp.where(pick1 | pick2, 1.0, 0.0)
    ti = lax.broadcasted_iota(jnp.int32, (tm, tm), 0)
    si = lax.broadcasted_iota(jnp.int32, (tm, tm), 1)
    before = _dot(jnp.where(si < ti, 1.0, 0.0), onehot) + counts_ref[...]
    r1 = jnp.sum(jnp.where(pick1, before, 0.0), axis=-1, keepdims=True)
    r2 = jnp.sum(jnp.where(pick2, before, 0.0), axis=-1, keepdims=True)
    counts_ref[...] = counts_ref[...] + jnp.sum(onehot, axis=0, keepdims=True)

    route_ref[...] = jnp.where(lane == 0, i1 - N_GROUPS,
                     jnp.where(lane == 1, i2 - N_GROUPS,
                     jnp.where(lane == 2, w1,
                     jnp.where(lane == 3, w2,
                     jnp.where(lane == 4, r1,
                     jnp.where(lane == 5, r2, 0.0))))))


def _out_proj(att, rw, x2, ga, woa, wor, g2, wrh, wrm, br, tm):
    t, d = x2.shape
    wa = att.shape[1]
    wr = rw.shape[1]
    sr = d // SLAB_LANES
    c = lambda shape: pl.BlockSpec(shape, lambda i: (0, 0))
    return pl.pallas_call(
        _out_proj_kernel,
        out_shape=(jax.ShapeDtypeStruct((t, d), F32), jax.ShapeDtypeStruct((t * sr, SLAB_LANES), F32),
                   jax.ShapeDtypeStruct((t, LANES), F32), jax.ShapeDtypeStruct((1, LANES), F32)),
        grid=(t // tm,),
        in_specs=[pl.BlockSpec((tm, wa), lambda i: (i, 0)),
                  pl.BlockSpec((tm, wr), lambda i: (i, 0)),
                  pl.BlockSpec((tm, d), lambda i: (i, 0)),
                  c((1, wa)), c((wa, d)), c((wr, d)), c((1, d)), c((d, LANES)), c((d, LANES)), c((1, LANES))],
        out_specs=(pl.BlockSpec((tm, d), lambda i: (i, 0)), pl.BlockSpec((tm * sr, SLAB_LANES), lambda i: (i, 0)),
                   pl.BlockSpec((tm, LANES), lambda i: (i, 0)), c((1, LANES))),
        compiler_params=pltpu.CompilerParams(
            dimension_semantics=("arbitrary",), vmem_limit_bytes=VMEM_LIMIT),
        name="out_proj",
    )(att, rw, x2, ga, woa, wor, g2, wrh, wrm, br)


def _store_slabs(ref, x):
    rows, d = x.shape
    sr = d // SLAB_LANES
    for s in range(sr):
        ref[pl.ds(s, rows, stride=sr), :] = x[:, s * SLAB_LANES:(s + 1) * SLAB_LANES]


def _load_slabs(ref, slot, first, rows, sr):
    return jnp.concatenate([ref[slot, pl.ds(first * sr + s, rows, stride=sr), :] for s in range(sr)], axis=1)


def _gather_rows(i, nt, idx_hbm, src_hbm, idx_ref, buf_ref, isem, gsem, rows, sr):
    def idx_copy(tile, slot):
        return pltpu.make_async_copy(idx_hbm.at[pl.ds(tile, 1)], idx_ref.at[pl.ds(slot, 1)], isem.at[slot])

    def start_rows(slot):
        for r in range(rows):
            src = pl.multiple_of(idx_ref[slot, r], sr)
            pltpu.make_async_copy(src_hbm.at[pl.ds(src, sr)], buf_ref.at[slot, pl.ds(r * sr, sr)],
                                  gsem.at[slot]).start()

    slot = jnp.bitwise_and(i, 1)

    @pl.when(i == 0)
    def _():
        idx_copy(0, 0).start()
        idx_copy(0, 0).wait()
        start_rows(0)
        if nt > 1:
            idx_copy(1, 1).start()

    @pl.when(i + 1 < nt)
    def _():
        idx_copy(i + 1, 1 - slot).wait()
        start_rows(1 - slot)

    @pl.when(i + 2 < nt)
    def _():
        idx_copy(i + 2, slot).start()

    pltpu.make_async_copy(src_hbm.at[pl.ds(0, rows * sr)], buf_ref.at[slot], gsem.at[slot]).wait()
    return slot


def _dispatch_kernel(idx_hbm, x_hbm, o_hbm, idx_ref, tile_ref, zero_ref, isem, lsem, csem,
                     *, nt, n_tok, n_pad, sr):
    i = pl.program_id(0)
    slot = jnp.bitwise_and(i, 1)
    other = 1 - slot
    tg = n_tok // TOP_K
    n_all = n_tok + n_pad
    stage = lax.rem(i, 3)

    def idx_copy(tile, s):
        return pltpu.make_async_copy(idx_hbm.at[pl.ds(tile, 1)], idx_ref.at[pl.ds(s, 1)], isem.at[s])

    def load(tile, s):
        return pltpu.make_async_copy(x_hbm.at[pl.ds(tile * tg * sr, tg * sr)], tile_ref.at[s], lsem.at[s])

    def copies_done(s):
        return pltpu.make_async_copy(x_hbm.at[pl.ds(0, n_all * sr)], o_hbm.at[pl.ds(0, n_all * sr)], csem.at[s])

    @pl.when(i == 0)
    def _():
        zero_ref[...] = jnp.zeros_like(zero_ref)
        idx_copy(0, 0).start()
        load(0, 0).start()

    idx_copy(i, slot).wait()
    load(i, stage).wait()

    @pl.when(i + 1 < nt)
    def _():
        idx_copy(i + 1, other).start()
        load(i + 1, lax.rem(i + 1, 3)).start()

    for r in range(n_tok):
        dst = pl.multiple_of(idx_ref[slot, r], sr)
        pltpu.make_async_copy(tile_ref.at[stage, pl.ds((r % tg) * sr, sr)], o_hbm.at[pl.ds(dst, sr)],
                              csem.at[slot]).start()
    for r in range(n_tok, n_all):
        dst = pl.multiple_of(idx_ref[slot, r], sr)
        pltpu.make_async_copy(zero_ref, o_hbm.at[pl.ds(dst, sr)], csem.at[slot]).start()

    @pl.when(i > 0)
    def _():
        copies_done(other).wait()

    @pl.when(i == nt - 1)
    def _():
        copies_done(slot).wait()


def _dispatch(idx, hn_slabs, n_rows, sr):
    nt, n_all = idx.shape
    n_tok = TOP_K * COMBINE_ROWS
    hbm = pl.BlockSpec(memory_space=pl.ANY)
    return pl.pallas_call(
        functools.partial(_dispatch_kernel, nt=nt, n_tok=n_tok, n_pad=n_all - n_tok, sr=sr),
        out_shape=jax.ShapeDtypeStruct((n_rows * sr, SLAB_LANES), F32),
        grid=(nt,),
        in_specs=[hbm, hbm],
        out_specs=hbm,
        scratch_shapes=[pltpu.SMEM((2, n_all), jnp.int32),
                        pltpu.VMEM((3, COMBINE_ROWS * sr, SLAB_LANES), F32), pltpu.VMEM((sr, SLAB_LANES), F32),
                        pltpu.SemaphoreType.DMA((2,)), pltpu.SemaphoreType.DMA((3,)),
                        pltpu.SemaphoreType.DMA((2,))],
        compiler_params=pltpu.CompilerParams(dimension_semantics=("arbitrary",)),
        name="dispatch",
    )(idx, hn_slabs)


def _expert_kernel(te_ref, first_ref, next_ref, nu_ref, x_ref, w1_hbm, w3_hbm, w2_hbm, o_ref,
                   stage_refs, wb_refs, wsem, *, sr):
    i = pl.program_id(0)
    tm = EXPERT_ROWS
    w_hbm = (w1_hbm, w3_hbm, w2_hbm)
    used = i < nu_ref[0]

    def weight_copies(e):
        return [pltpu.make_async_copy(w.at[e], st, wsem.at[j]) for j, (w, st) in enumerate(zip(w_hbm, stage_refs))]

    @pl.when(i == 0)
    def _():
        for cp in weight_copies(te_ref[0]):
            cp.start()

    @pl.when(used & (first_ref[i] == 1))
    def _():
        for cp in weight_copies(te_ref[i]):
            cp.wait()
        for st, wb in zip(stage_refs, wb_refs):
            rows = st.shape[0] // WEIGHT_CAST_CHUNKS

            def cast_chunk(c, carry, st=st, wb=wb, rows=rows):
                r0 = pl.multiple_of(c * rows, rows)
                wb[pl.ds(r0, rows), :] = st[pl.ds(r0, rows), :].astype(BF16)
                return carry

            lax.fori_loop(0, WEIGHT_CAST_CHUNKS, cast_chunk, 0)

        @pl.when(next_ref[i] >= 0)
        def _():
            for cp in weight_copies(next_ref[i]):
                cp.start()

    @pl.when(used)
    def _():
        xb = jnp.concatenate([x_ref[pl.ds(s, tm, stride=sr), :] for s in range(sr)], axis=1).astype(BF16)
        h1 = jnp.dot(xb, wb_refs[0][...], preferred_element_type=F32)
        h3 = jnp.dot(xb, wb_refs[1][...], preferred_element_type=F32)
        hid = (h1 * _sigmoid(h1)) * h3
        _store_slabs(o_ref, jnp.dot(hid.astype(BF16), wb_refs[2][...], preferred_element_type=F32))

    @pl.when(jnp.logical_not(used))
    def _():
        o_ref[...] = jnp.zeros_like(o_ref)


def _experts(tile_e, tile_first, tile_next, n_used, x_slabs, w1, w3, w2):
    nt = tile_e.shape[0]
    d, de = w1.shape[1], w1.shape[2]
    sr = d // SLAB_LANES
    tm = EXPERT_ROWS
    hbm = pl.BlockSpec(memory_space=pl.ANY)
    grid_spec = pltpu.PrefetchScalarGridSpec(
        num_scalar_prefetch=4,
        grid=(nt,),
        in_specs=[pl.BlockSpec((tm * sr, SLAB_LANES), lambda i, te, fi, nx, nu: (jnp.minimum(i, nu[0] - 1), 0)),
                  hbm, hbm, hbm],
        out_specs=pl.BlockSpec((tm * sr, SLAB_LANES), lambda i, *_: (i, 0)),
        scratch_shapes=[[pltpu.VMEM((d, de), F32), pltpu.VMEM((d, de), F32), pltpu.VMEM((de, d), F32)],
                        [pltpu.VMEM((d, de), BF16), pltpu.VMEM((d, de), BF16), pltpu.VMEM((de, d), BF16)],
                        pltpu.SemaphoreType.DMA((3,))],
    )
    return pl.pallas_call(
        functools.partial(_expert_kernel, sr=sr),
        out_shape=jax.ShapeDtypeStruct((nt * tm * sr, SLAB_LANES), F32),
        grid_spec=grid_spec,
        compiler_params=pltpu.CompilerParams(
            dimension_semantics=("arbitrary",), vmem_limit_bytes=EXPERT_VMEM_LIMIT),
        name="experts",
    )(tile_e, tile_first, tile_next, n_used, x_slabs, w1, w3, w2)


def _combine_kernel(dest_hbm, y_hbm, x1_ref, route_ref, fg_ref, o_ref, idx_ref, buf_ref, isem, gsem,
                    *, nt, sr):
    i = pl.program_id(0)
    tg = COMBINE_ROWS
    slot = _gather_rows(i, nt, dest_hbm, y_hbm, idx_ref, buf_ref, isem, gsem, TOP_K * tg, sr)
    route = route_ref[...]
    y = (_load_slabs(buf_ref, slot, 0, tg, sr) * route[:, 2:3]
         + _load_slabs(buf_ref, slot, tg, tg, sr) * route[:, 3:4])
    o_ref[...] = _rms(x1_ref[...] + y, fg_ref[...])


def _combine(dest, ybuf, x1, route, fg):
    t, d = x1.shape
    tg = COMBINE_ROWS
    sr = d // SLAB_LANES
    return pl.pallas_call(
        functools.partial(_combine_kernel, nt=t // tg, sr=sr),
        out_shape=jax.ShapeDtypeStruct((t, d), F32),
        grid=(t // tg,),
        in_specs=[pl.BlockSpec(memory_space=pl.ANY),
                  pl.BlockSpec(memory_space=pl.ANY),
                  pl.BlockSpec((tg, d), lambda i: (i, 0)),
                  pl.BlockSpec((tg, LANES), lambda i: (i, 0)),
                  pl.BlockSpec((1, d), lambda i: (0, 0))],
        out_specs=pl.BlockSpec((tg, d), lambda i: (i, 0)),
        scratch_shapes=[pltpu.SMEM((2, TOP_K * tg), jnp.int32),
                        pltpu.VMEM((2, TOP_K * tg * sr, SLAB_LANES), F32),
                        pltpu.SemaphoreType.DMA((2,)), pltpu.SemaphoreType.DMA((2,))],
        compiler_params=pltpu.CompilerParams(
            dimension_semantics=("arbitrary",), vmem_limit_bytes=VMEM_LIMIT),
        name="combine",
    )(dest, ybuf, x1, route, fg)


def _pad_cols(m, n):
    return jnp.pad(m, ((0, 0), (0, n - m.shape[1])))


def _pad_rows(m, n):
    return jnp.pad(m, ((0, n - m.shape[0]), (0, 0)))


def _pick_tile(n, cap):
    for unit in (MXU_WIDTH, LANES):
        fits = [c for c in range(unit, cap + 1, unit) if n % c == 0]
        if fits:
            return fits[-1]
    raise ValueError(f"no lane-aligned tile divides {n}")


def _dispatch_plan(experts, rank, counts, n_tok):
    tm = EXPERT_ROWS
    nt = n_tok * TOP_K // tm + N_EXPERTS
    tiles_e = (counts + tm - 1) // tm
    tile_end = jnp.cumsum(tiles_e)
    tile_start = tile_end - tiles_e
    dest = (tile_start[experts] * tm + rank).astype(jnp.int32)
    n_used = tile_end[-1:].astype(jnp.int32)
    tiles = jnp.arange(nt, dtype=jnp.int32)
    tile_e = jnp.minimum(jnp.sum((tile_end[None, :] <= tiles[:, None]).astype(jnp.int32), axis=1),
                         N_EXPERTS - 1).astype(jnp.int32)
    tile_first = ((tiles == 0) | (tile_e != jnp.roll(tile_e, 1))).astype(jnp.int32)
    next_start = tile_end[tile_e]
    tile_next = jnp.where(next_start < n_used[0], tile_e[jnp.minimum(next_start, nt - 1)], -1).astype(jnp.int32)

    seg_len = jnp.concatenate([tiles_e * tm - counts, (nt - n_used) * tm])
    seg_first = jnp.concatenate([tile_start * tm + counts, n_used * tm])
    seg_stop = jnp.cumsum(seg_len)
    j = jnp.arange(N_EXPERTS * tm, dtype=jnp.int32)
    seg = jnp.sum((seg_stop[None, :] <= j[:, None]).astype(jnp.int32), axis=1)
    pad_rows = (seg_first[seg] + j - (seg_stop - seg_len)[seg]).astype(jnp.int32)
    return tile_e, tile_first, tile_next, n_used, dest, pad_rows


def _layer(x2, b, s, norm1_g, w_in, attn_out_g, mu, w0, w2, a0, a2, g2, k_k, k_a, r_k, lnx_w, lnx_b,
           w_out, norm2_g, wg, bg, we, be, ew1, ew3, ew2, final_g):
    t, d = x2.shape
    aw = attn_out_g.shape[0]
    cw = w0.shape[0]
    heads = aw // ATT_HEAD_DIM
    dl, al, gl = w2.shape[0], a2.shape[0], g2.shape[0]
    pw, pa, pg = _round_up(dl, LANES), _round_up(al, LANES), _round_up(gl, LANES)
    wpad = 3 * cw + pw + pa + pg

    sh = w_in[:, 3 * aw:]
    o = 3 * cw
    w_arr = jnp.concatenate(
        [sh[:, :o], _pad_cols(sh[:, o:o + dl], pw), _pad_cols(sh[:, o + dl:o + dl + al], pa),
         _pad_cols(sh[:, o + dl + al:o + dl + al + gl], pg), w_in[:, :3 * aw]], axis=1).astype(BF16)
    mu_arr = jnp.concatenate(
        [mu[:o], jnp.pad(mu[o:o + dl], (0, pw - dl)), jnp.pad(mu[o + dl:o + dl + al], (0, pa - al)),
         jnp.pad(mu[o + dl + al:], (0, pg - gl))])[None, :]

    n_all = w_arr.shape[1]
    proj = _in_proj(x2, norm1_g[None, :], w_arr, tm=min(512, t), tn=_pick_tile(n_all, 4096))

    att = _moba(proj, b, s, heads, wpad // LANES)
    r1 = lambda a: a.reshape(1, -1)
    rw = _rwkv(proj, b, s, cw, wpad,
               (mu_arr, r1(w0), _pad_rows(w2, pw).astype(BF16), r1(a0), _pad_rows(a2, pa).astype(BF16),
                _pad_rows(g2, pg).astype(BF16), r1(k_k), r1(k_a), r1(r_k), r1(lnx_w), r1(lnx_b)))

    wr = _pad_cols(jnp.concatenate([wg, we], axis=1), LANES)
    wrh = wr.astype(BF16)
    wrm = (wr - wrh.astype(F32)).astype(BF16)
    br = jnp.pad(jnp.concatenate([bg, be]), (0, LANES - N_GROUPS - N_EXPERTS))[None, :]
    wo = w_out.astype(BF16)
    x1, hn, route, counts = _out_proj(att, rw, x2, r1(attn_out_g), wo[:aw], wo[aw:], r1(norm2_g), wrh, wrm, br,
                                      tm=min(256, t))

    plan_in = route[:, :6].astype(jnp.int32)
    counts = counts[0, N_GROUPS:N_GROUPS + N_EXPERTS].astype(jnp.int32)
    tg = COMBINE_ROWS
    n_tiles = t // tg
    pad_per_tile = N_EXPERTS * EXPERT_ROWS // n_tiles
    tile_e, tile_first, tile_next, n_used, dest, pad_rows = _dispatch_plan(
        plan_in[:, 0:TOP_K], plan_in[:, 4:4 + TOP_K], counts, t)

    sr = d // SLAB_LANES
    dest_tiles = dest.reshape(n_tiles, tg, TOP_K).transpose(0, 2, 1).reshape(n_tiles, TOP_K * tg) * sr
    disp_idx = jnp.concatenate([dest_tiles, pad_rows.reshape(n_tiles, pad_per_tile) * sr], axis=1)
    xbuf = _dispatch(disp_idx, hn, tile_e.shape[0] * EXPERT_ROWS, sr)
    ybuf = _experts(tile_e, tile_first, tile_next, n_used, xbuf, ew1, ew3, ew2)
    return _combine(dest_tiles, ybuf, x1, route, r1(final_g))


def kernel(x, norm1_g, w_in, attn_out_g, rwkv_mu, rwkv_w0, rwkv_w2, rwkv_a0, rwkv_a2, rwkv_g2, rwkv_k_k,
           rwkv_k_a, rwkv_r_k, rwkv_lnx_w, rwkv_lnx_b, w_out, norm2_g, router_group_w, router_group_b,
           router_expert_w, router_expert_b, expert_w1, expert_w3, expert_w2, final_g):
    b, s, d = x.shape
    assert norm1_g.shape[0] == 1, "single-layer block"
    assert s % MOBA_BLOCK == 0 and s % RWKV_CHUNK == 0
    assert (b * s) % max(EXPERT_ROWS, COMBINE_ROWS) == 0
    assert (N_EXPERTS * EXPERT_ROWS) % ((b * s) // COMBINE_ROWS) == 0, "zero-fill rows split evenly over tiles"
    out = _layer(x.reshape(b * s, d), b, s, norm1_g[0], w_in[0], attn_out_g[0], rwkv_mu[0], rwkv_w0[0],
                 rwkv_w2[0], rwkv_a0[0], rwkv_a2[0], rwkv_g2[0], rwkv_k_k[0], rwkv_k_a[0], rwkv_r_k[0],
                 rwkv_lnx_w[0], rwkv_lnx_b[0], w_out[0], norm2_g[0], router_group_w[0], router_group_b[0],
                 router_expert_w[0], router_expert_b[0], expert_w1[0], expert_w3[0], expert_w2[0], final_g)
    return out.reshape(b, s, d)
```

```python
import functools
import math

import jax
import jax.numpy as jnp
from jax import lax
from jax.experimental import pallas as pl
from jax.experimental.pallas import tpu as pltpu

F32 = jnp.float32
BF16 = jnp.bfloat16

LANES = 128
MXU_WIDTH = 256
SLAB_LANES = 128
ATT_HEAD_DIM = 128
RWKV_HEAD_DIM = 64
MOBA_BLOCK = 256
MOBA_TOPK = 3
N_GROUPS = 4
EXPERTS_PER_GROUP = 8
N_EXPERTS = N_GROUPS * EXPERTS_PER_GROUP
TOP_K = 2
RMS_EPS = 1e-6
GN_EPS = 64e-5
NEG_BIG = -1e30
RWKV_CHUNK = 64
RWKV_CHUNKS_PER_STEP = 4
RWKV_CHUNK_LAG = 12
EXPERT_ROWS = 256
COMBINE_ROWS = 256
VMEM_LIMIT = 56 * 1024 * 1024
EXPERT_VMEM_LIMIT = 60 * 1024 * 1024
WEIGHT_CAST_CHUNKS = 16


def _round_up(n, m):
    return (n + m - 1) // m * m


def _dot(a, b):
    return jnp.dot(a.astype(BF16), b.astype(BF16), preferred_element_type=F32)


def _dot_nt(a, b):
    return lax.dot_general(a.astype(BF16), b.astype(BF16), (((1,), (1,)), ((), ())),
                           preferred_element_type=F32)


def _split3(a):
    hi = a.astype(BF16)
    r1 = a - hi.astype(F32)
    mid = r1.astype(BF16)
    lo = (r1 - mid.astype(F32)).astype(BF16)
    return hi, mid, lo


def _dot_hi(a, b):
    ah, am, _ = _split3(a)
    bh, bm, _ = _split3(b)
    d = functools.partial(jnp.dot, preferred_element_type=F32)
    return d(ah, bh) + (d(am, bh) + d(ah, bm))


def _sigmoid(x):
    return 1.0 / (1.0 + jnp.exp(-x))


def _rms(x, g):
    ms = jnp.mean(x * x, axis=-1, keepdims=True)
    return x * lax.rsqrt(ms + RMS_EPS) * g


def _in_proj_kernel(x_ref, g_ref, w_ref, o_ref):
    xn = _rms(x_ref[...], g_ref[...]).astype(BF16)
    o_ref[...] = jnp.dot(xn, w_ref[...], preferred_element_type=F32)


def _in_proj(x2, g, w, tm, tn):
    t, d = x2.shape
    n = w.shape[1]
    return pl.pallas_call(
        _in_proj_kernel,
        out_shape=jax.ShapeDtypeStruct((t, n), F32),
        grid=(n // tn, t // tm),
        in_specs=[pl.BlockSpec((tm, d), lambda j, i: (i, 0)),
                  pl.BlockSpec((1, d), lambda j, i: (0, 0)),
                  pl.BlockSpec((d, tn), lambda j, i: (0, j))],
        out_specs=pl.BlockSpec((tm, tn), lambda j, i: (i, j)),
        compiler_params=pltpu.CompilerParams(
            dimension_semantics=("arbitrary", "arbitrary"), vmem_limit_bytes=VMEM_LIMIT),
        name="in_proj",
    )(x2, g, w)


def _moba_kernel(q_ref, k_ref, v_ref, o_ref, *, nblk, scale):
    bs = MOBA_BLOCK
    k = k_ref[...]
    kb = k.astype(BF16)
    vtb = v_ref[...].T.astype(BF16)
    qt = q_ref[...].T
    qtb = qt.astype(BF16)
    kmean = jnp.concatenate(
        [jnp.sum(k[n * bs:(n + 1) * bs, :], axis=0, keepdims=True) for n in range(nblk)], axis=0) * (1.0 / bs)
    gate_all = _dot_hi(kmean, qt)
    kpos = lax.broadcasted_iota(jnp.int32, (bs, bs), 0)
    qpos = lax.broadcasted_iota(jnp.int32, (bs, bs), 1)
    causal = kpos <= qpos

    def scores(qi):
        return jnp.dot(kb[0:(qi + 1) * bs, :], qtb[:, qi * bs:(qi + 1) * bs], preferred_element_type=F32)

    exp_scale = scale * math.log2(math.e)
    s_next = scores(0)
    for qi in range(nblk):
        qs = slice(qi * bs, (qi + 1) * bs)
        nk = (qi + 1) * bs
        s_all = s_next
        if qi + 1 < nblk:
            s_next = scores(qi + 1)
        blocks = []
        if qi > MOBA_TOPK:
            gate = gate_all[0:qi, qs]
            blk = lax.broadcasted_iota(jnp.int32, (qi, bs), 0)
        for n in range(qi):
            sn = s_all[n * bs:(n + 1) * bs, :]
            if qi > MOBA_TOPK:
                gn = gate[n:n + 1, :]
                beats = (gate > gn) | ((gate == gn) & (blk < n))
                rank = jnp.sum(jnp.where(beats, 1.0, 0.0), axis=0, keepdims=True)
                sn = jnp.where(rank < float(MOBA_TOPK), sn, NEG_BIG)
            blocks.append(sn)
        blocks.append(jnp.where(causal, s_all[qi * bs:nk, :], NEG_BIG))
        m = functools.reduce(jnp.maximum, [jnp.max(x, axis=0, keepdims=True) for x in blocks])
        ps = [jnp.exp2((x - m) * exp_scale) for x in blocks]
        l = functools.reduce(jnp.add, [jnp.sum(x, axis=0, keepdims=True) for x in ps])
        pb = jnp.concatenate([x.astype(BF16) for x in ps], axis=0)
        acc = jnp.dot(vtb[:, 0:nk], pb, preferred_element_type=F32)
        o_ref[qs, :] = (acc / l).T


def _moba(proj, b, s, heads, col0):
    d = ATT_HEAD_DIM
    nblk = s // MOBA_BLOCK
    kern = functools.partial(_moba_kernel, nblk=nblk, scale=1.0 / math.sqrt(d))
    return pl.pallas_call(
        kern,
        out_shape=jax.ShapeDtypeStruct((b * s, heads * d), F32),
        grid=(b, heads),
        in_specs=[pl.BlockSpec((s, d), lambda bi, h: (bi, col0 + h)),
                  pl.BlockSpec((s, d), lambda bi, h: (bi, col0 + heads + h)),
                  pl.BlockSpec((s, d), lambda bi, h: (bi, col0 + 2 * heads + h))],
        out_specs=pl.BlockSpec((s, d), lambda bi, h: (bi, h)),
        compiler_params=pltpu.CompilerParams(
            dimension_semantics=("arbitrary", "arbitrary"), vmem_limit_bytes=VMEM_LIMIT),
        name="moba",
    )(proj, proj, proj)


def _head_sum(x, lo):
    s0 = jnp.sum(jnp.where(lo, x, 0.0), axis=-1, keepdims=True)
    s1 = jnp.sum(jnp.where(lo, 0.0, x), axis=-1, keepdims=True)
    return jnp.where(lo, s0, s1)


def _stack_heads(x, lo):
    return jnp.concatenate([jnp.where(lo, x, 0.0), jnp.where(lo, 0.0, x)], axis=0)


def _rwkv_kernel(p_ref, mu_ref, w0_ref, w2_ref, a0_ref, a2_ref, g2_ref, kk_ref, ka_ref, rk_ref,
                 lnw_ref, lnb_ref, o_ref, carry_ref, state_ref, *, n_sub, **dims):
    @pl.when(pl.program_id(1) == 0)
    def _():
        carry_ref[...] = jnp.zeros_like(carry_ref)
        state_ref[...] = jnp.zeros_like(state_ref)

    C = RWKV_CHUNK
    chunks = [_rwkv_chunk(p_ref.at[pl.ds(h * C, C)], mu_ref, w0_ref, w2_ref, a0_ref, a2_ref, g2_ref, kk_ref,
                          ka_ref, rk_ref, lnw_ref, lnb_ref, o_ref.at[pl.ds(h * C, C)], carry_ref, state_ref,
                          **dims) for h in range(n_sub)]
    _interleave(chunks, RWKV_CHUNK_LAG)


def _interleave(chunks, lag):
    n = len(chunks)
    steps, done, wrote, waiting = [0] * n, [False] * n, [False] * n, [False] * n

    def advance(j):
        if waiting[j] and j > 0:
            while not (wrote[j - 1] or done[j - 1]):
                advance(j - 1)
        waiting[j] = False
        try:
            tag = next(chunks[j])
        except StopIteration:
            done[j] = True
            return
        steps[j] += 1
        waiting[j] = tag == "need_state"
        wrote[j] = wrote[j] or tag == "state_done"

    while not all(done):
        for j in range(n):
            if not done[j] and (j == 0 or done[j - 1] or steps[j - 1] >= lag):
                advance(j)


def _rwkv_chunk(p_ref, mu_ref, w0_ref, w2_ref, a0_ref, a2_ref, g2_ref, kk_ref, ka_ref, rk_ref,
                lnw_ref, lnb_ref, o_ref, carry_ref, state_ref, *, cw, pw, pa, pg):
    C = RWKV_CHUNK
    C2 = 2 * C
    pr = p_ref[...]
    row = lax.broadcasted_iota(jnp.int32, pr.shape, 0)
    prev = jnp.where(row == 0, carry_ref[...], pltpu.roll(pr, 1, 0))
    carry_ref[...] = pr[C - 1:C, :]
    ps = pr + (prev - pr) * mu_ref[...]
    yield "step"

    r = ps[:, 0:cw]
    k = ps[:, cw:2 * cw]
    v = ps[:, 2 * cw:3 * cw]
    o = 3 * cw
    xw = ps[:, o:o + pw]
    xa = ps[:, o + pw:o + pw + pa]
    xg = ps[:, o + pw + pa:o + pw + pa + pg]

    z = -(w0_ref[...] + _dot(jnp.tanh(xw), w2_ref[...]))
    softplus = jnp.maximum(z, 0.0) + jnp.log(1.0 + jnp.exp(-jnp.abs(z)))
    logw = -jnp.exp(-softplus - 0.5)
    asig = _sigmoid(a0_ref[...] + _dot(xa, a2_ref[...]))
    gate = _dot(_sigmoid(xg), g2_ref[...])
    kkr = k * kk_ref[...]
    kmod = k * (1.0 + (asig - 1.0) * ka_ref[...])
    yield "step"

    ti = lax.broadcasted_iota(jnp.int32, (C, C), 0)
    si = lax.broadcasted_iota(jnp.int32, (C, C), 1)
    tril = jnp.where(si <= ti, 1.0, 0.0).astype(BF16)
    lh, lm, ll = _split3(logw)
    d32 = functools.partial(jnp.dot, preferred_element_type=F32)
    cum = d32(tril, lh) + (d32(tril, lm) + d32(tril, ll))
    cum_end = cum[C - 1:C, :]
    e_prev = jnp.exp(cum - logw)
    e_incl = jnp.exp(cum)
    e_inv = jnp.exp(-cum)
    e_end = jnp.exp(cum_end - cum)
    g_end = jnp.exp(cum_end)
    yield "step"

    lane = lax.broadcasted_iota(jnp.int32, (C, LANES), 1)
    lo = lane < RWKV_HEAD_DIM
    rho = lax.broadcasted_iota(jnp.int32, (C2, C2), 0)
    sig = lax.broadcasted_iota(jnp.int32, (C2, C2), 1)
    same_head = (rho >= C) == (sig >= C)
    strict = same_head & ((sig % C) < (rho % C))
    incl = same_head & ((sig % C) <= (rho % C))
    eye = jnp.where(rho == sig, 1.0, 0.0)

    pairs = range(cw // LANES)
    sls = [slice(p * LANES, (p + 1) * LANES) for p in pairs]
    incl2 = jnp.concatenate([incl, incl], axis=1)
    strict2 = jnp.concatenate([strict, strict], axis=1)

    b2, r2, k2, v2, at, rt, v2s, gm = [], [], [], [], [], [], [], []
    for sl in sls:
        kk2 = kkr[:, sl]
        nrm = jnp.sqrt(_head_sum(kk2 * kk2, lo))
        kkn = kk2 / jnp.maximum(nrm, 1e-12)
        b2.append(kkn * asig[:, sl])
        r2.append(r[:, sl])
        k2.append(kmod[:, sl])
        v2.append(v[:, sl])
        at.append(-kkn * e_prev[:, sl])
        rt.append(r2[-1] * e_incl[:, sl])
        bt = b2[-1] * e_inv[:, sl]
        kt = k2[-1] * e_inv[:, sl]
        v2s.append(_stack_heads(v2[-1], lo))
        lhs = jnp.concatenate([_stack_heads(at[-1], lo), _stack_heads(rt[-1], lo)], axis=0)
        rhs = jnp.concatenate([bt, bt, kt, kt], axis=0)
        gm.append(_dot_nt(lhs, rhs))
        yield "step"

    labk = [jnp.where(strict2, g[0:C2, :], 0.0) for g in gm]
    mrbk = [jnp.where(incl2, g[C2:2 * C2, :], 0.0) for g in gm]
    lakv = [_dot(labk[p][:, C2:2 * C2], v2s[p]) for p in pairs]
    yield "step"

    lab = [m[:, 0:C2] for m in labk]
    tinv = [eye + m for m in lab]
    pw2 = [_dot(m, m) for m in lab]
    yield "step"
    n_pow = int(math.log2(C))
    for k in range(1, n_pow):
        if k < n_pow - 1:
            prod = [_dot(jnp.concatenate([t, m], axis=0), m) for t, m in zip(tinv, pw2)]
            tinv = [t + pr[0:C2] for t, pr in zip(tinv, prod)]
            pw2 = [pr[C2:2 * C2] for pr in prod]
        else:
            tinv = [t + _dot(t, m) for t, m in zip(tinv, pw2)]
        yield "step"

    yield "need_state"
    s0 = [state_ref[p] for p in pairs]
    ars = [_dot_nt(jnp.concatenate([at[p], rt[p]], axis=0), s0[p]) for p in pairs]
    yield "step"
    u2s = [_dot(t, _stack_heads(ars[p][0:C], lo) + lakv[p]) for p, t in zip(pairs, tinv)]
    yield "step"
    y2s = [_stack_heads(ars[p][C:C2], lo) + _dot(mrbk[p], jnp.concatenate([u2s[p], v2s[p]], axis=0))
           for p in pairs]
    u2 = [m[0:C] + m[C:C2] for m in u2s]
    y2 = [m[0:C] + m[C:C2] for m in y2s]
    yield "step"

    for p, sl in zip(pairs, sls):
        uv = jnp.concatenate([u2[p], v2[p]], axis=0)
        bk = jnp.concatenate([b2[p] * e_end[:, sl], k2[p] * e_end[:, sl]], axis=0)
        upd = _dot(uv.T, bk)
        state_ref[p] = s0[p] * g_end[:, sl] + jnp.where(same_head, upd, 0.0)
    yield "state_done"

    for p, sl in zip(pairs, sls):
        mean = _head_sum(y2[p], lo) * (1.0 / RWKV_HEAD_DIM)
        yc = y2[p] - mean
        var = _head_sum(yc * yc, lo) * (1.0 / RWKV_HEAD_DIM)
        yn = yc * lax.rsqrt(var + GN_EPS) * lnw_ref[:, sl] + lnb_ref[:, sl]
        bonus = _head_sum(r2[p] * k2[p] * rk_ref[:, sl], lo) * v2[p]
        o_ref[:, sl] = (yn + bonus) * gate[:, sl]
        yield "step"


def _rwkv(proj, b, s, cw, wpad, params):
    (mu, w0, w2, a0, a2, g2, kk, ka, rk, lnw, lnb) = params
    n_sub = RWKV_CHUNKS_PER_STEP if s % (RWKV_CHUNKS_PER_STEP * RWKV_CHUNK) == 0 else 1
    C = n_sub * RWKV_CHUNK
    nch = s // C
    pw, pa, pg = w2.shape[0], a2.shape[0], g2.shape[0]
    kern = functools.partial(_rwkv_kernel, n_sub=n_sub, cw=cw, pw=pw, pa=pa, pg=pg)
    row = lambda n: pl.BlockSpec((1, n), lambda bi, c: (0, 0))
    full = lambda a: pl.BlockSpec(a.shape, lambda bi, c: (0, 0))
    return pl.pallas_call(
        kern,
        out_shape=jax.ShapeDtypeStruct((b * s, cw), F32),
        grid=(b, nch),
        in_specs=[pl.BlockSpec((C, wpad), lambda bi, c: (bi * nch + c, 0)),
                  row(wpad), row(cw), full(w2), row(cw), full(a2), full(g2),
                  row(cw), row(cw), row(cw), row(cw), row(cw)],
        out_specs=pl.BlockSpec((C, cw), lambda bi, c: (bi * nch + c, 0)),
        scratch_shapes=[pltpu.VMEM((1, wpad), F32),
                        pltpu.VMEM((cw // LANES, LANES, LANES), F32)],
        compiler_params=pltpu.CompilerParams(
            dimension_semantics=("arbitrary", "arbitrary"), vmem_limit_bytes=VMEM_LIMIT),
        name="rwkv",
    )(proj, mu, w0, w2, a0, a2, g2, kk, ka, rk, lnw, lnb)


def _out_proj_kernel(att_ref, rw_ref, x_ref, ga_ref, woa_ref, wor_ref, g2_ref, wrh_ref, wrm_ref, br_ref,
                     x1_ref, hn_ref, route_ref, counts_ref):
    att = _rms(att_ref[...], ga_ref[...])
    y = (jnp.dot(att.astype(BF16), woa_ref[...], preferred_element_type=F32)
         + jnp.dot(rw_ref[...].astype(BF16), wor_ref[...], preferred_element_type=F32))
    x1 = x_ref[...] + y
    x1_ref[...] = x1
    hn = _rms(x1, g2_ref[...])
    _store_slabs(hn_ref, hn)

    hh, hm, _ = _split3(hn)
    d32 = functools.partial(jnp.dot, preferred_element_type=F32)
    logits = d32(hh, wrh_ref[...]) + (d32(hm, wrh_ref[...]) + d32(hh, wrm_ref[...])) + br_ref[...]
    lane = lax.broadcasted_iota(jnp.int32, logits.shape, 1)
    lanef = lane.astype(F32)
    big = float(LANES)

    is_g = lane < N_GROUPS
    gmax = jnp.max(jnp.where(is_g, logits, -jnp.inf), axis=-1, keepdims=True)
    gidx = jnp.min(jnp.where(is_g & (logits == gmax), lanef, big), axis=-1, keepdims=True)
    gsum = jnp.sum(jnp.where(is_g, jnp.exp(logits - gmax), 0.0), axis=-1, keepdims=True)
    g_gate = 1.0 / gsum

    e_lo = N_GROUPS + EXPERTS_PER_GROUP * gidx
    in_grp = (lanef >= e_lo) & (lanef < e_lo + EXPERTS_PER_GROUP)
    e1 = jnp.max(jnp.where(in_grp, logits, -jnp.inf), axis=-1, keepdims=True)
    i1 = jnp.min(jnp.where(in_grp & (logits == e1), lanef, big), axis=-1, keepdims=True)
    rest = in_grp & (lanef != i1)
    e2 = jnp.max(jnp.where(rest, logits, -jnp.inf), axis=-1, keepdims=True)
    i2 = jnp.min(jnp.where(rest & (logits == e2), lanef, big), axis=-1, keepdims=True)
    t = jnp.exp(e2 - e1)
    w1 = g_gate * (1.0 / (1.0 + t))
    w2 = g_gate * (t / (1.0 + t))

    @pl.when(pl.program_id(0) == 0)
    def _():
        counts_ref[...] = jnp.zeros_like(counts_ref)

    tm = logits.shape[0]
    pick1 = lanef == i1
    pick2 = lanef == i2
    onehot = jnp.where(pick1 | pick2, 1.0, 0.0)
    ti = lax.broadcasted_iota(jnp.int32, (tm, tm), 0)
    si = lax.broadcasted_iota(jnp.int32, (tm, tm), 1)
    before = _dot(jnp.where(si < ti, 1.0, 0.0), onehot) + counts_ref[...]
    r1 = jnp.sum(jnp.where(pick1, before, 0.0), axis=-1, keepdims=True)
    r2 = jnp.sum(jnp.where(pick2, before, 0.0), axis=-1, keepdims=True)
    counts_ref[...] = counts_ref[...] + jnp.sum(onehot, axis=0, keepdims=True)

    route_ref[...] = jnp.where(lane == 0, i1 - N_GROUPS,
                     jnp.where(lane == 1, i2 - N_GROUPS,
                     jnp.where(lane == 2, w1,
                     jnp.where(lane == 3, w2,
                     jnp.where(lane == 4, r1,
                     jnp.where(lane == 5, r2, 0.0))))))


def _out_proj(att, rw, x2, ga, woa, wor, g2, wrh, wrm, br, tm):
    t, d = x2.shape
    wa = att.shape[1]
    wr = rw.shape[1]
    sr = d // SLAB_LANES
    c = lambda shape: pl.BlockSpec(shape, lambda i: (0, 0))
    return pl.pallas_call(
        _out_proj_kernel,
        out_shape=(jax.ShapeDtypeStruct((t, d), F32), jax.ShapeDtypeStruct((t * sr, SLAB_LANES), F32),
                   jax.ShapeDtypeStruct((t, LANES), F32), jax.ShapeDtypeStruct((1, LANES), F32)),
        grid=(t // tm,),
        in_specs=[pl.BlockSpec((tm, wa), lambda i: (i, 0)),
                  pl.BlockSpec((tm, wr), lambda i: (i, 0)),
                  pl.BlockSpec((tm, d), lambda i: (i, 0)),
                  c((1, wa)), c((wa, d)), c((wr, d)), c((1, d)), c((d, LANES)), c((d, LANES)), c((1, LANES))],
        out_specs=(pl.BlockSpec((tm, d), lambda i: (i, 0)), pl.BlockSpec((tm * sr, SLAB_LANES), lambda i: (i, 0)),
                   pl.BlockSpec((tm, LANES), lambda i: (i, 0)), c((1, LANES))),
        compiler_params=pltpu.CompilerParams(
            dimension_semantics=("arbitrary",), vmem_limit_bytes=VMEM_LIMIT),
        name="out_proj",
    )(att, rw, x2, ga, woa, wor, g2, wrh, wrm, br)


def _store_slabs(ref, x):
    rows, d = x.shape
    sr = d // SLAB_LANES
    for s in range(sr):
        ref[pl.ds(s, rows, stride=sr), :] = x[:, s * SLAB_LANES:(s + 1) * SLAB_LANES]


def _load_slabs(ref, slot, first, rows, sr):
    return jnp.concatenate([ref[slot, pl.ds(first * sr + s, rows, stride=sr), :] for s in range(sr)], axis=1)


def _gather_rows(i, nt, idx_hbm, src_hbm, idx_ref, buf_ref, isem, gsem, rows, sr):
    def idx_copy(tile, slot):
        return pltpu.make_async_copy(idx_hbm.at[pl.ds(tile, 1)], idx_ref.at[pl.ds(slot, 1)], isem.at[slot])

    def start_rows(slot):
        for r in range(rows):
            src = pl.multiple_of(idx_ref[slot, r], sr)
            pltpu.make_async_copy(src_hbm.at[pl.ds(src, sr)], buf_ref.at[slot, pl.ds(r * sr, sr)],
                                  gsem.at[slot]).start()

    slot = jnp.bitwise_and(i, 1)

    @pl.when(i == 0)
    def _():
        idx_copy(0, 0).start()
        idx_copy(0, 0).wait()
        start_rows(0)
        if nt > 1:
            idx_copy(1, 1).start()

    @pl.when(i + 1 < nt)
    def _():
        idx_copy(i + 1, 1 - slot).wait()
        start_rows(1 - slot)

    @pl.when(i + 2 < nt)
    def _():
        idx_copy(i + 2, slot).start()

    pltpu.make_async_copy(src_hbm.at[pl.ds(0, rows * sr)], buf_ref.at[slot], gsem.at[slot]).wait()
    return slot


def _dispatch_kernel(idx_hbm, x_hbm, o_hbm, idx_ref, tile_ref, zero_ref, isem, lsem, csem,
                     *, nt, n_tok, n_pad, sr):
    i = pl.program_id(0)
    slot = jnp.bitwise_and(i, 1)
    other = 1 - slot
    tg = n_tok // TOP_K
    n_all = n_tok + n_pad
    stage = lax.rem(i, 3)

    def idx_copy(tile, s):
        return pltpu.make_async_copy(idx_hbm.at[pl.ds(tile, 1)], idx_ref.at[pl.ds(s, 1)], isem.at[s])

    def load(tile, s):
        return pltpu.make_async_copy(x_hbm.at[pl.ds(tile * tg * sr, tg * sr)], tile_ref.at[s], lsem.at[s])

    def copies_done(s):
        return pltpu.make_async_copy(x_hbm.at[pl.ds(0, n_all * sr)], o_hbm.at[pl.ds(0, n_all * sr)], csem.at[s])

    @pl.when(i == 0)
    def _():
        zero_ref[...] = jnp.zeros_like(zero_ref)
        idx_copy(0, 0).start()
        load(0, 0).start()

    idx_copy(i, slot).wait()
    load(i, stage).wait()

    @pl.when(i + 1 < nt)
    def _():
        idx_copy(i + 1, other).start()
        load(i + 1, lax.rem(i + 1, 3)).start()

    for r in range(n_tok):
        dst = pl.multiple_of(idx_ref[slot, r], sr)
        pltpu.make_async_copy(tile_ref.at[stage, pl.ds((r % tg) * sr, sr)], o_hbm.at[pl.ds(dst, sr)],
                              csem.at[slot]).start()
    for r in range(n_tok, n_all):
        dst = pl.multiple_of(idx_ref[slot, r], sr)
        pltpu.make_async_copy(zero_ref, o_hbm.at[pl.ds(dst, sr)], csem.at[slot]).start()

    @pl.when(i > 0)
    def _():
        copies_done(other).wait()

    @pl.when(i == nt - 1)
    def _():
        copies_done(slot).wait()


def _dispatch(idx, hn_slabs, n_rows, sr):
    nt, n_all = idx.shape
    n_tok = TOP_K * COMBINE_ROWS
    hbm = pl.BlockSpec(memory_space=pl.ANY)
    return pl.pallas_call(
        functools.partial(_dispatch_kernel, nt=nt, n_tok=n_tok, n_pad=n_all - n_tok, sr=sr),
        out_shape=jax.ShapeDtypeStruct((n_rows * sr, SLAB_LANES), F32),
        grid=(nt,),
        in_specs=[hbm, hbm],
        out_specs=hbm,
        scratch_shapes=[pltpu.SMEM((2, n_all), jnp.int32),
                        pltpu.VMEM((3, COMBINE_ROWS * sr, SLAB_LANES), F32), pltpu.VMEM((sr, SLAB_LANES), F32),
                        pltpu.SemaphoreType.DMA((2,)), pltpu.SemaphoreType.DMA((3,)),
                        pltpu.SemaphoreType.DMA((2,))],
        compiler_params=pltpu.CompilerParams(dimension_semantics=("arbitrary",)),
        name="dispatch",
    )(idx, hn_slabs)


def _expert_kernel(te_ref, first_ref, next_ref, nu_ref, x_ref, w1_hbm, w3_hbm, w2_hbm, o_ref,
                   stage_refs, wb_refs, wsem, *, sr):
    i = pl.program_id(0)
    tm = EXPERT_ROWS
    w_hbm = (w1_hbm, w3_hbm, w2_hbm)
    used = i < nu_ref[0]

    def weight_copies(e):
        out = []
        for j, (w, st) in enumerate(zip(w_hbm, stage_refs)):
            half = st.shape[0] // 2
            for q in range(2):
                rows = pl.ds(q * half, half)
                out.append((pltpu.make_async_copy(w.at[e, rows], st.at[rows], wsem.at[2 * j + q]), q))
        return out

    @pl.when(i == 0)
    def _():
        for cp, q in weight_copies(te_ref[0]):
            cp.start(priority=q)

    @pl.when(used & (first_ref[i] == 1))
    def _():
        for cp, _ in weight_copies(te_ref[i]):
            cp.wait()
        for st, wb in zip(stage_refs, wb_refs):
            rows = st.shape[0] // WEIGHT_CAST_CHUNKS

            def cast_chunk(c, carry, st=st, wb=wb, rows=rows):
                r0 = pl.multiple_of(c * rows, rows)
                wb[pl.ds(r0, rows), :] = st[pl.ds(r0, rows), :].astype(BF16)
                return carry

            lax.fori_loop(0, WEIGHT_CAST_CHUNKS, cast_chunk, 0)

        @pl.when(next_ref[i] >= 0)
        def _():
            for cp, q in weight_copies(next_ref[i]):
                cp.start(priority=q)

    @pl.when(used)
    def _():
        xb = jnp.concatenate([x_ref[pl.ds(s, tm, stride=sr), :] for s in range(sr)], axis=1).astype(BF16)
        h1 = jnp.dot(xb, wb_refs[0][...], preferred_element_type=F32)
        h3 = jnp.dot(xb, wb_refs[1][...], preferred_element_type=F32)
        hid = (h1 * _sigmoid(h1)) * h3
        _store_slabs(o_ref, jnp.dot(hid.astype(BF16), wb_refs[2][...], preferred_element_type=F32))

    @pl.when(jnp.logical_not(used))
    def _():
        o_ref[...] = jnp.zeros_like(o_ref)


def _experts(tile_e, tile_first, tile_next, n_used, x_slabs, w1, w3, w2):
    nt = tile_e.shape[0]
    d, de = w1.shape[1], w1.shape[2]
    sr = d // SLAB_LANES
    tm = EXPERT_ROWS
    hbm = pl.BlockSpec(memory_space=pl.ANY)
    grid_spec = pltpu.PrefetchScalarGridSpec(
        num_scalar_prefetch=4,
        grid=(nt,),
        in_specs=[pl.BlockSpec((tm * sr, SLAB_LANES), lambda i, te, fi, nx, nu: (jnp.minimum(i, nu[0] - 1), 0)),
                  hbm, hbm, hbm],
        out_specs=pl.BlockSpec((tm * sr, SLAB_LANES), lambda i, *_: (i, 0)),
        scratch_shapes=[[pltpu.VMEM((d, de), F32), pltpu.VMEM((d, de), F32), pltpu.VMEM((de, d), F32)],
                        [pltpu.VMEM((d, de), BF16), pltpu.VMEM((d, de), BF16), pltpu.VMEM((de, d), BF16)],
                        pltpu.SemaphoreType.DMA((6,))],
    )
    return pl.pallas_call(
        functools.partial(_expert_kernel, sr=sr),
        out_shape=jax.ShapeDtypeStruct((nt * tm * sr, SLAB_LANES), F32),
        grid_spec=grid_spec,
        compiler_params=pltpu.CompilerParams(
            dimension_semantics=("arbitrary",), vmem_limit_bytes=EXPERT_VMEM_LIMIT),
        name="experts",
    )(tile_e, tile_first, tile_next, n_used, x_slabs, w1, w3, w2)


def _combine_kernel(dest_hbm, y_hbm, x1_ref, route_ref, fg_ref, o_ref, idx_ref, buf_ref, isem, gsem,
                    *, nt, sr):
    i = pl.program_id(0)
    tg = COMBINE_ROWS
    slot = _gather_rows(i, nt, dest_hbm, y_hbm, idx_ref, buf_ref, isem, gsem, TOP_K * tg, sr)
    route = route_ref[...]
    y = (_load_slabs(buf_ref, slot, 0, tg, sr) * route[:, 2:3]
         + _load_slabs(buf_ref, slot, tg, tg, sr) * route[:, 3:4])
    o_ref[...] = _rms(x1_ref[...] + y, fg_ref[...])


def _combine(dest, ybuf, x1, route, fg):
    t, d = x1.shape
    tg = COMBINE_ROWS
    sr = d // SLAB_LANES
    return pl.pallas_call(
        functools.partial(_combine_kernel, nt=t // tg, sr=sr),
        out_shape=jax.ShapeDtypeStruct((t, d), F32),
        grid=(t // tg,),
        in_specs=[pl.BlockSpec(memory_space=pl.ANY),
                  pl.BlockSpec(memory_space=pl.ANY),
                  pl.BlockSpec((tg, d), lambda i: (i, 0)),
                  pl.BlockSpec((tg, LANES), lambda i: (i, 0)),
                  pl.BlockSpec((1, d), lambda i: (0, 0))],
        out_specs=pl.BlockSpec((tg, d), lambda i: (i, 0)),
        scratch_shapes=[pltpu.SMEM((2, TOP_K * tg), jnp.int32),
                        pltpu.VMEM((2, TOP_K * tg * sr, SLAB_LANES), F32),
                        pltpu.SemaphoreType.DMA((2,)), pltpu.SemaphoreType.DMA((2,))],
        compiler_params=pltpu.CompilerParams(
            dimension_semantics=("arbitrary",), vmem_limit_bytes=VMEM_LIMIT),
        name="combine",
    )(dest, ybuf, x1, route, fg)


def _pad_cols(m, n):
    return jnp.pad(m, ((0, 0), (0, n - m.shape[1])))


def _pad_rows(m, n):
    return jnp.pad(m, ((0, n - m.shape[0]), (0, 0)))


def _pick_tile(n, cap):
    for unit in (MXU_WIDTH, LANES):
        fits = [c for c in range(unit, cap + 1, unit) if n % c == 0]
        if fits:
            return fits[-1]
    raise ValueError(f"no lane-aligned tile divides {n}")


def _dispatch_plan(experts, rank, counts, n_tok):
    tm = EXPERT_ROWS
    nt = n_tok * TOP_K // tm + N_EXPERTS
    tiles_e = (counts + tm - 1) // tm
    tile_end = jnp.cumsum(tiles_e)
    tile_start = tile_end - tiles_e
    dest = (tile_start[experts] * tm + rank).astype(jnp.int32)
    n_used = tile_end[-1:].astype(jnp.int32)
    tiles = jnp.arange(nt, dtype=jnp.int32)
    tile_e = jnp.minimum(jnp.sum((tile_end[None, :] <= tiles[:, None]).astype(jnp.int32), axis=1),
                         N_EXPERTS - 1).astype(jnp.int32)
    tile_first = ((tiles == 0) | (tile_e != jnp.roll(tile_e, 1))).astype(jnp.int32)
    next_start = tile_end[tile_e]
    tile_next = jnp.where(next_start < n_used[0], tile_e[jnp.minimum(next_start, nt - 1)], -1).astype(jnp.int32)

    seg_len = jnp.concatenate([tiles_e * tm - counts, (nt - n_used) * tm])
    seg_first = jnp.concatenate([tile_start * tm + counts, n_used * tm])
    seg_stop = jnp.cumsum(seg_len)
    j = jnp.arange(N_EXPERTS * tm, dtype=jnp.int32)
    seg = jnp.sum((seg_stop[None, :] <= j[:, None]).astype(jnp.int32), axis=1)
    pad_rows = (seg_first[seg] + j - (seg_stop - seg_len)[seg]).astype(jnp.int32)
    return tile_e, tile_first, tile_next, n_used, dest, pad_rows


def _layer(x2, b, s, norm1_g, w_in, attn_out_g, mu, w0, w2, a0, a2, g2, k_k, k_a, r_k, lnx_w, lnx_b,
           w_out, norm2_g, wg, bg, we, be, ew1, ew3, ew2, final_g):
    t, d = x2.shape
    aw = attn_out_g.shape[0]
    cw = w0.shape[0]
    heads = aw // ATT_HEAD_DIM
    dl, al, gl = w2.shape[0], a2.shape[0], g2.shape[0]
    pw, pa, pg = _round_up(dl, LANES), _round_up(al, LANES), _round_up(gl, LANES)
    wpad = 3 * cw + pw + pa + pg

    sh = w_in[:, 3 * aw:]
    o = 3 * cw
    w_arr = jnp.concatenate(
        [sh[:, :o], _pad_cols(sh[:, o:o + dl], pw), _pad_cols(sh[:, o + dl:o + dl + al], pa),
         _pad_cols(sh[:, o + dl + al:o + dl + al + gl], pg), w_in[:, :3 * aw]], axis=1).astype(BF16)
    mu_arr = jnp.concatenate(
        [mu[:o], jnp.pad(mu[o:o + dl], (0, pw - dl)), jnp.pad(mu[o + dl:o + dl + al], (0, pa - al)),
         jnp.pad(mu[o + dl + al:], (0, pg - gl))])[None, :]

    n_all = w_arr.shape[1]
    proj = _in_proj(x2, norm1_g[None, :], w_arr, tm=min(512, t), tn=_pick_tile(n_all, 4096))

    att = _moba(proj, b, s, heads, wpad // LANES)
    r1 = lambda a: a.reshape(1, -1)
    rw = _rwkv(proj, b, s, cw, wpad,
               (mu_arr, r1(w0), _pad_rows(w2, pw).astype(BF16), r1(a0), _pad_rows(a2, pa).astype(BF16),
                _pad_rows(g2, pg).astype(BF16), r1(k_k), r1(k_a), r1(r_k), r1(lnx_w), r1(lnx_b)))

    wr = _pad_cols(jnp.concatenate([wg, we], axis=1), LANES)
    wrh = wr.astype(BF16)
    wrm = (wr - wrh.astype(F32)).astype(BF16)
    br = jnp.pad(jnp.concatenate([bg, be]), (0, LANES - N_GROUPS - N_EXPERTS))[None, :]
    wo = w_out.astype(BF16)
    x1, hn, route, counts = _out_proj(att, rw, x2, r1(attn_out_g), wo[:aw], wo[aw:], r1(norm2_g), wrh, wrm, br,
                                      tm=min(256, t))

    plan_in = route[:, :6].astype(jnp.int32)
    counts = counts[0, N_GROUPS:N_GROUPS + N_EXPERTS].astype(jnp.int32)
    tg = COMBINE_ROWS
    n_tiles = t // tg
    pad_per_tile = N_EXPERTS * EXPERT_ROWS // n_tiles
    tile_e, tile_first, tile_next, n_used, dest, pad_rows = _dispatch_plan(
        plan_in[:, 0:TOP_K], plan_in[:, 4:4 + TOP_K], counts, t)

    sr = d // SLAB_LANES
    dest_tiles = dest.reshape(n_tiles, tg, TOP_K).transpose(0, 2, 1).reshape(n_tiles, TOP_K * tg) * sr
    disp_idx = jnp.concatenate([dest_tiles, pad_rows.reshape(n_tiles, pad_per_tile) * sr], axis=1)
    xbuf = _dispatch(disp_idx, hn, tile_e.shape[0] * EXPERT_ROWS, sr)
    ybuf = _experts(tile_e, tile_first, tile_next, n_used, xbuf, ew1, ew3, ew2)
    return _combine(dest_tiles, ybuf, x1, route, r1(final_g))


def kernel(x, norm1_g, w_in, attn_out_g, rwkv_mu, rwkv_w0, rwkv_w2, rwkv_a0, rwkv_a2, rwkv_g2, rwkv_k_k,
           rwkv_k_a, rwkv_r_k, rwkv_lnx_w, rwkv_lnx_b, w_out, norm2_g, router_group_w, router_group_b,
           router_expert_w, router_expert_b, expert_w1, expert_w3, expert_w2, final_g):
    b, s, d = x.shape
    assert norm1_g.shape[0] == 1, "single-layer block"
    assert s % MOBA_BLOCK == 0 and s % RWKV_CHUNK == 0
    assert (b * s) % max(EXPERT_ROWS, COMBINE_ROWS) == 0
    assert (N_EXPERTS * EXPERT_ROWS) % ((b * s) // COMBINE_ROWS) == 0, "zero-fill rows split evenly over tiles"
    out = _layer(x.reshape(b * s, d), b, s, norm1_g[0], w_in[0], attn_out_g[0], rwkv_mu[0], rwkv_w0[0],
                 rwkv_w2[0], rwkv_a0[0], rwkv_a2[0], rwkv_g2[0], rwkv_k_k[0], rwkv_k_a[0], rwkv_r_k[0],
                 rwkv_lnx_w[0], rwkv_lnx_b[0], w_out[0], norm2_g[0], router_group_w[0], router_group_b[0],
                 router_expert_w[0], router_expert_b[0], expert_w1[0], expert_w3[0], expert_w2[0], final_g)
    return out.reshape(b, s, d)
```

```python
import functools
import math

import jax
import jax.numpy as jnp
from jax import lax
from jax.experimental import pallas as pl
from jax.experimental.pallas import tpu as pltpu

F32 = jnp.float32
BF16 = jnp.bfloat16

LANES = 128
MXU_WIDTH = 256
SLAB_LANES = 128
ATT_HEAD_DIM = 128
RWKV_HEAD_DIM = 64
MOBA_BLOCK = 256
MOBA_TOPK = 3
N_GROUPS = 4
EXPERTS_PER_GROUP = 8
N_EXPERTS = N_GROUPS * EXPERTS_PER_GROUP
TOP_K = 2
RMS_EPS = 1e-6
GN_EPS = 64e-5
NEG_BIG = -1e30
RWKV_CHUNK = 64
RWKV_CHUNKS_PER_STEP = 4
RWKV_CHUNK_LAG = 8
EXPERT_ROWS = 256
COMBINE_ROWS = 256
VMEM_LIMIT = 56 * 1024 * 1024
EXPERT_VMEM_LIMIT = 60 * 1024 * 1024
WEIGHT_CAST_CHUNKS = 16


def _round_up(n, m):
    return (n + m - 1) // m * m


def _dot(a, b):
    return jnp.dot(a.astype(BF16), b.astype(BF16), preferred_element_type=F32)


def _dot_nt(a, b):
    return lax.dot_general(a.astype(BF16), b.astype(BF16), (((1,), (1,)), ((), ())),
                           preferred_element_type=F32)


def _split3(a):
    hi = a.astype(BF16)
    r1 = a - hi.astype(F32)
    mid = r1.astype(BF16)
    lo = (r1 - mid.astype(F32)).astype(BF16)
    return hi, mid, lo


def _dot_hi(a, b):
    ah, am, _ = _split3(a)
    bh, bm, _ = _split3(b)
    d = functools.partial(jnp.dot, preferred_element_type=F32)
    return d(ah, bh) + (d(am, bh) + d(ah, bm))


def _sigmoid(x):
    return 1.0 / (1.0 + jnp.exp(-x))


def _rms(x, g):
    ms = jnp.mean(x * x, axis=-1, keepdims=True)
    return x * lax.rsqrt(ms + RMS_EPS) * g


def _in_proj_kernel(x_ref, g_ref, w_ref, o_ref):
    xn = _rms(x_ref[...], g_ref[...]).astype(BF16)
    o_ref[...] = jnp.dot(xn, w_ref[...], preferred_element_type=F32)


def _in_proj(x2, g, w, tm, tn):
    t, d = x2.shape
    n = w.shape[1]
    return pl.pallas_call(
        _in_proj_kernel,
        out_shape=jax.ShapeDtypeStruct((t, n), F32),
        grid=(n // tn, t // tm),
        in_specs=[pl.BlockSpec((tm, d), lambda j, i: (i, 0)),
                  pl.BlockSpec((1, d), lambda j, i: (0, 0)),
                  pl.BlockSpec((d, tn), lambda j, i: (0, j))],
        out_specs=pl.BlockSpec((tm, tn), lambda j, i: (i, j)),
        compiler_params=pltpu.CompilerParams(
            dimension_semantics=("arbitrary", "arbitrary"), vmem_limit_bytes=VMEM_LIMIT),
        name="in_proj",
    )(x2, g, w)


def _moba_kernel(q_ref, k_ref, v_ref, o_ref, *, nblk, scale):
    bs = MOBA_BLOCK
    k = k_ref[...]
    kb = k.astype(BF16)
    vtb = v_ref[...].T.astype(BF16)
    qt = q_ref[...].T
    qtb = qt.astype(BF16)
    kmean = jnp.concatenate(
        [jnp.sum(k[n * bs:(n + 1) * bs, :], axis=0, keepdims=True) for n in range(nblk)], axis=0) * (1.0 / bs)
    gate_all = _dot_hi(kmean, qt)
    kpos = lax.broadcasted_iota(jnp.int32, (bs, bs), 0)
    qpos = lax.broadcasted_iota(jnp.int32, (bs, bs), 1)
    causal = kpos <= qpos

    def scores(qi):
        return jnp.dot(kb[0:(qi + 1) * bs, :], qtb[:, qi * bs:(qi + 1) * bs], preferred_element_type=F32)

    exp_scale = scale * math.log2(math.e)
    s_next = scores(0)
    for qi in range(nblk):
        qs = slice(qi * bs, (qi + 1) * bs)
        nk = (qi + 1) * bs
        s_all = s_next
        if qi + 1 < nblk:
            s_next = scores(qi + 1)
        blocks = []
        if qi > MOBA_TOPK:
            gate = gate_all[0:qi, qs]
            blk = lax.broadcasted_iota(jnp.int32, (qi, bs), 0)
        for n in range(qi):
            sn = s_all[n * bs:(n + 1) * bs, :]
            if qi > MOBA_TOPK:
                gn = gate[n:n + 1, :]
                beats = (gate > gn) | ((gate == gn) & (blk < n))
                rank = jnp.sum(jnp.where(beats, 1.0, 0.0), axis=0, keepdims=True)
                sn = jnp.where(rank < float(MOBA_TOPK), sn, NEG_BIG)
            blocks.append(sn)
        blocks.append(jnp.where(causal, s_all[qi * bs:nk, :], NEG_BIG))
        m = functools.reduce(jnp.maximum, [jnp.max(x, axis=0, keepdims=True) for x in blocks])
        ps = [jnp.exp2((x - m) * exp_scale) for x in blocks]
        l = functools.reduce(jnp.add, [jnp.sum(x, axis=0, keepdims=True) for x in ps])
        pb = jnp.concatenate([x.astype(BF16) for x in ps], axis=0)
        acc = jnp.dot(vtb[:, 0:nk], pb, preferred_element_type=F32)
        o_ref[qs, :] = (acc / l).T


def _moba(proj, b, s, heads, col0):
    d = ATT_HEAD_DIM
    nblk = s // MOBA_BLOCK
    kern = functools.partial(_moba_kernel, nblk=nblk, scale=1.0 / math.sqrt(d))
    return pl.pallas_call(
        kern,
        out_shape=jax.ShapeDtypeStruct((b * s, heads * d), F32),
        grid=(b, heads),
        in_specs=[pl.BlockSpec((s, d), lambda bi, h: (bi, col0 + h)),
                  pl.BlockSpec((s, d), lambda bi, h: (bi, col0 + heads + h)),
                  pl.BlockSpec((s, d), lambda bi, h: (bi, col0 + 2 * heads + h))],
        out_specs=pl.BlockSpec((s, d), lambda bi, h: (bi, h)),
        compiler_params=pltpu.CompilerParams(
            dimension_semantics=("arbitrary", "arbitrary"), vmem_limit_bytes=VMEM_LIMIT),
        name="moba",
    )(proj, proj, proj)


def _head_sum(x, lo):
    s0 = jnp.sum(jnp.where(lo, x, 0.0), axis=-1, keepdims=True)
    s1 = jnp.sum(jnp.where(lo, 0.0, x), axis=-1, keepdims=True)
    return jnp.where(lo, s0, s1)


def _stack_heads(x, lo):
    return jnp.concatenate([jnp.where(lo, x, 0.0), jnp.where(lo, 0.0, x)], axis=0)


def _rwkv_kernel(p_ref, mu_ref, w0_ref, w2_ref, a0_ref, a2_ref, g2_ref, kk_ref, ka_ref, rk_ref,
                 lnw_ref, lnb_ref, o_ref, carry_ref, state_ref, *, n_sub, **dims):
    @pl.when(pl.program_id(1) == 0)
    def _():
        carry_ref[...] = jnp.zeros_like(carry_ref)
        state_ref[...] = jnp.zeros_like(state_ref)

    C = RWKV_CHUNK
    chunks = [_rwkv_chunk(p_ref.at[pl.ds(h * C, C)], mu_ref, w0_ref, w2_ref, a0_ref, a2_ref, g2_ref, kk_ref,
                          ka_ref, rk_ref, lnw_ref, lnb_ref, o_ref.at[pl.ds(h * C, C)], carry_ref, state_ref,
                          **dims) for h in range(n_sub)]
    _interleave(chunks, RWKV_CHUNK_LAG)


def _interleave(chunks, lag):
    n = len(chunks)
    steps, done, wrote, waiting = [0] * n, [False] * n, [False] * n, [False] * n

    def advance(j):
        if waiting[j] and j > 0:
            while not (wrote[j - 1] or done[j - 1]):
                advance(j - 1)
        waiting[j] = False
        try:
            tag = next(chunks[j])
        except StopIteration:
            done[j] = True
            return
        steps[j] += 1
        waiting[j] = tag == "need_state"
        wrote[j] = wrote[j] or tag == "state_done"

    while not all(done):
        for j in range(n):
            if not done[j] and (j == 0 or done[j - 1] or steps[j - 1] >= lag):
                advance(j)


def _rwkv_chunk(p_ref, mu_ref, w0_ref, w2_ref, a0_ref, a2_ref, g2_ref, kk_ref, ka_ref, rk_ref,
                lnw_ref, lnb_ref, o_ref, carry_ref, state_ref, *, cw, pw, pa, pg):
    C = RWKV_CHUNK
    C2 = 2 * C
    pr = p_ref[...]
    row = lax.broadcasted_iota(jnp.int32, pr.shape, 0)
    prev = jnp.where(row == 0, carry_ref[...], pltpu.roll(pr, 1, 0))
    carry_ref[...] = pr[C - 1:C, :]
    ps = pr + (prev - pr) * mu_ref[...]
    yield "step"

    r = ps[:, 0:cw]
    k = ps[:, cw:2 * cw]
    v = ps[:, 2 * cw:3 * cw]
    o = 3 * cw
    xw = ps[:, o:o + pw]
    xa = ps[:, o + pw:o + pw + pa]
    xg = ps[:, o + pw + pa:o + pw + pa + pg]

    z = -(w0_ref[...] + _dot(jnp.tanh(xw), w2_ref[...]))
    softplus = jnp.maximum(z, 0.0) + jnp.log(1.0 + jnp.exp(-jnp.abs(z)))
    logw = -jnp.exp(-softplus - 0.5)
    asig = _sigmoid(a0_ref[...] + _dot(xa, a2_ref[...]))
    gate = _dot(_sigmoid(xg), g2_ref[...])
    kkr = k * kk_ref[...]
    kmod = k * (1.0 + (asig - 1.0) * ka_ref[...])
    yield "step"

    ti = lax.broadcasted_iota(jnp.int32, (C, C), 0)
    si = lax.broadcasted_iota(jnp.int32, (C, C), 1)
    tril = jnp.where(si <= ti, 1.0, 0.0).astype(BF16)
    lh, lm, ll = _split3(logw)
    d32 = functools.partial(jnp.dot, preferred_element_type=F32)
    cum = d32(tril, lh) + (d32(tril, lm) + d32(tril, ll))
    cum_end = cum[C - 1:C, :]
    e_prev = jnp.exp(cum - logw)
    e_incl = jnp.exp(cum)
    e_inv = jnp.exp(-cum)
    e_end = jnp.exp(cum_end - cum)
    g_end = jnp.exp(cum_end)
    yield "step"

    lane = lax.broadcasted_iota(jnp.int32, (C, LANES), 1)
    lo = lane < RWKV_HEAD_DIM
    rho = lax.broadcasted_iota(jnp.int32, (C2, C2), 0)
    sig = lax.broadcasted_iota(jnp.int32, (C2, C2), 1)
    same_head = (rho >= C) == (sig >= C)
    strict = same_head & ((sig % C) < (rho % C))
    incl = same_head & ((sig % C) <= (rho % C))
    eye = jnp.where(rho == sig, 1.0, 0.0)

    pairs = range(cw // LANES)
    sls = [slice(p * LANES, (p + 1) * LANES) for p in pairs]
    incl2 = jnp.concatenate([incl, incl], axis=1)
    strict2 = jnp.concatenate([strict, strict], axis=1)

    b2, r2, k2, v2, at, rt, v2s, gm = [], [], [], [], [], [], [], []
    for sl in sls:
        kk2 = kkr[:, sl]
        nrm = jnp.sqrt(_head_sum(kk2 * kk2, lo))
        kkn = kk2 / jnp.maximum(nrm, 1e-12)
        b2.append(kkn * asig[:, sl])
        r2.append(r[:, sl])
        k2.append(kmod[:, sl])
        v2.append(v[:, sl])
        at.append(-kkn * e_prev[:, sl])
        rt.append(r2[-1] * e_incl[:, sl])
        bt = b2[-1] * e_inv[:, sl]
        kt = k2[-1] * e_inv[:, sl]
        v2s.append(_stack_heads(v2[-1], lo))
        lhs = jnp.concatenate([_stack_heads(at[-1], lo), _stack_heads(rt[-1], lo)], axis=0)
        rhs = jnp.concatenate([bt, bt, kt, kt], axis=0)
        gm.append(_dot_nt(lhs, rhs))
        yield "step"

    labk = [jnp.where(strict2, g[0:C2, :], 0.0) for g in gm]
    mrbk = [jnp.where(incl2, g[C2:2 * C2, :], 0.0) for g in gm]
    lakv = [_dot(labk[p][:, C2:2 * C2], v2s[p]) for p in pairs]
    yield "step"

    lab = [m[:, 0:C2] for m in labk]
    tinv = [eye + m for m in lab]
    pw2 = [_dot(m, m) for m in lab]
    yield "step"
    n_pow = int(math.log2(C))
    for k in range(1, n_pow):
        if k < n_pow - 1:
            prod = [_dot(jnp.concatenate([t, m], axis=0), m) for t, m in zip(tinv, pw2)]
            tinv = [t + pr[0:C2] for t, pr in zip(tinv, prod)]
            pw2 = [pr[C2:2 * C2] for pr in prod]
        else:
            tinv = [t + _dot(t, m) for t, m in zip(tinv, pw2)]
        yield "step"

    yield "need_state"
    s0 = [state_ref[p] for p in pairs]
    ars = [_dot_nt(jnp.concatenate([at[p], rt[p]], axis=0), s0[p]) for p in pairs]
    yield "step"
    u2s = [_dot(t, _stack_heads(ars[p][0:C], lo) + lakv[p]) for p, t in zip(pairs, tinv)]
    yield "step"
    y2s = [_stack_heads(ars[p][C:C2], lo) + _dot(mrbk[p], jnp.concatenate([u2s[p], v2s[p]], axis=0))
           for p in pairs]
    u2 = [m[0:C] + m[C:C2] for m in u2s]
    y2 = [m[0:C] + m[C:C2] for m in y2s]
    yield "step"

    for p, sl in zip(pairs, sls):
        uv = jnp.concatenate([u2[p], v2[p]], axis=0)
        bk = jnp.concatenate([b2[p] * e_end[:, sl], k2[p] * e_end[:, sl]], axis=0)
        upd = _dot(uv.T, bk)
        state_ref[p] = s0[p] * g_end[:, sl] + jnp.where(same_head, upd, 0.0)
    yield "state_done"

    for p, sl in zip(pairs, sls):
        mean = _head_sum(y2[p], lo) * (1.0 / RWKV_HEAD_DIM)
        yc = y2[p] - mean
        var = _head_sum(yc * yc, lo) * (1.0 / RWKV_HEAD_DIM)
        yn = yc * lax.rsqrt(var + GN_EPS) * lnw_ref[:, sl] + lnb_ref[:, sl]
        bonus = _head_sum(r2[p] * k2[p] * rk_ref[:, sl], lo) * v2[p]
        o_ref[:, sl] = (yn + bonus) * gate[:, sl]
        yield "step"


def _rwkv(proj, b, s, cw, wpad, params):
    (mu, w0, w2, a0, a2, g2, kk, ka, rk, lnw, lnb) = params
    n_sub = RWKV_CHUNKS_PER_STEP if s % (RWKV_CHUNKS_PER_STEP * RWKV_CHUNK) == 0 else 1
    C = n_sub * RWKV_CHUNK
    nch = s // C
    pw, pa, pg = w2.shape[0], a2.shape[0], g2.shape[0]
    kern = functools.partial(_rwkv_kernel, n_sub=n_sub, cw=cw, pw=pw, pa=pa, pg=pg)
    row = lambda n: pl.BlockSpec((1, n), lambda bi, c: (0, 0))
    full = lambda a: pl.BlockSpec(a.shape, lambda bi, c: (0, 0))
    return pl.pallas_call(
        kern,
        out_shape=jax.ShapeDtypeStruct((b * s, cw), F32),
        grid=(b, nch),
        in_specs=[pl.BlockSpec((C, wpad), lambda bi, c: (bi * nch + c, 0)),
                  row(wpad), row(cw), full(w2), row(cw), full(a2), full(g2),
                  row(cw), row(cw), row(cw), row(cw), row(cw)],
        out_specs=pl.BlockSpec((C, cw), lambda bi, c: (bi * nch + c, 0)),
        scratch_shapes=[pltpu.VMEM((1, wpad), F32),
                        pltpu.VMEM((cw // LANES, LANES, LANES), F32)],
        compiler_params=pltpu.CompilerParams(
            dimension_semantics=("arbitrary", "arbitrary"), vmem_limit_bytes=VMEM_LIMIT),
        name="rwkv",
    )(proj, mu, w0, w2, a0, a2, g2, kk, ka, rk, lnw, lnb)


def _out_proj_kernel(att_ref, rw_ref, x_ref, ga_ref, woa_ref, wor_ref, g2_ref, wrh_ref, wrm_ref, br_ref,
                     x1_ref, hn_ref, route_ref, counts_ref):
    att = _rms(att_ref[...], ga_ref[...])
    y = (jnp.dot(att.astype(BF16), woa_ref[...], preferred_element_type=F32)
         + jnp.dot(rw_ref[...].astype(BF16), wor_ref[...], preferred_element_type=F32))
    x1 = x_ref[...] + y
    x1_ref[...] = x1
    hn = _rms(x1, g2_ref[...])
    _store_slabs(hn_ref, hn)

    hh, hm, _ = _split3(hn)
    d32 = functools.partial(jnp.dot, preferred_element_type=F32)
    logits = d32(hh, wrh_ref[...]) + (d32(hm, wrh_ref[...]) + d32(hh, wrm_ref[...])) + br_ref[...]
    lane = lax.broadcasted_iota(jnp.int32, logits.shape, 1)
    lanef = lane.astype(F32)
    big = float(LANES)

    is_g = lane < N_GROUPS
    gmax = jnp.max(jnp.where(is_g, logits, -jnp.inf), axis=-1, keepdims=True)
    gidx = jnp.min(jnp.where(is_g & (logits == gmax), lanef, big), axis=-1, keepdims=True)
    gsum = jnp.sum(jnp.where(is_g, jnp.exp(logits - gmax), 0.0), axis=-1, keepdims=True)
    g_gate = 1.0 / gsum

    e_lo = N_GROUPS + EXPERTS_PER_GROUP * gidx
    in_grp = (lanef >= e_lo) & (lanef < e_lo + EXPERTS_PER_GROUP)
    e1 = jnp.max(jnp.where(in_grp, logits, -jnp.inf), axis=-1, keepdims=True)
    i1 = jnp.min(jnp.where(in_grp & (logits == e1), lanef, big), axis=-1, keepdims=True)
    rest = in_grp & (lanef != i1)
    e2 = jnp.max(jnp.where(rest, logits, -jnp.inf), axis=-1, keepdims=True)
    i2 = jnp.min(jnp.where(rest & (logits == e2), lanef, big), axis=-1, keepdims=True)
    t = jnp.exp(e2 - e1)
    w1 = g_gate * (1.0 / (1.0 + t))
    w2 = g_gate * (t / (1.0 + t))

    @pl.when(pl.program_id(0) == 0)
    def _():
        counts_ref[...] = jnp.zeros_like(counts_ref)

    tm = logits.shape[0]
    pick1 = lanef == i1
    pick2 = lanef == i2
    onehot = jnp.where(pick1 | pick2, 1.0, 0.0)
    ti = lax.broadcasted_iota(jnp.int32, (tm, tm), 0)
    si = lax.broadcasted_iota(jnp.int32, (tm, tm), 1)
    before = _dot(jnp.where(si < ti, 1.0, 0.0), onehot) + counts_ref[...]
    r1 = jnp.sum(jnp.where(pick1, before, 0.0), axis=-1, keepdims=True)
    r2 = jnp.sum(jnp.where(pick2, before, 0.0), axis=-1, keepdims=True)
    counts_ref[...] = counts_ref[...] + jnp.sum(onehot, axis=0, keepdims=True)

    route_ref[...] = jnp.where(lane == 0, i1 - N_GROUPS,
                     jnp.where(lane == 1, i2 - N_GROUPS,
                     jnp.where(lane == 2, w1,
                     jnp.where(lane == 3, w2,
                     jnp.where(lane == 4, r1,
                     jnp.where(lane == 5, r2, 0.0))))))


def _out_proj(att, rw, x2, ga, woa, wor, g2, wrh, wrm, br, tm):
    t, d = x2.shape
    wa = att.shape[1]
    wr = rw.shape[1]
    sr = d // SLAB_LANES
    c = lambda shape: pl.BlockSpec(shape, lambda i: (0, 0))
    return pl.pallas_call(
        _out_proj_kernel,
        out_shape=(jax.ShapeDtypeStruct((t, d), F32), jax.ShapeDtypeStruct((t * sr, SLAB_LANES), F32),
                   jax.ShapeDtypeStruct((t, LANES), F32), jax.ShapeDtypeStruct((1, LANES), F32)),
        grid=(t // tm,),
        in_specs=[pl.BlockSpec((tm, wa), lambda i: (i, 0)),
                  pl.BlockSpec((tm, wr), lambda i: (i, 0)),
                  pl.BlockSpec((tm, d), lambda i: (i, 0)),
                  c((1, wa)), c((wa, d)), c((wr, d)), c((1, d)), c((d, LANES)), c((d, LANES)), c((1, LANES))],
        out_specs=(pl.BlockSpec((tm, d), lambda i: (i, 0)), pl.BlockSpec((tm * sr, SLAB_LANES), lambda i: (i, 0)),
                   pl.BlockSpec((tm, LANES), lambda i: (i, 0)), c((1, LANES))),
        compiler_params=pltpu.CompilerParams(
            dimension_semantics=("arbitrary",), vmem_limit_bytes=VMEM_LIMIT),
        name="out_proj",
    )(att, rw, x2, ga, woa, wor, g2, wrh, wrm, br)


def _store_slabs(ref, x):
    rows, d = x.shape
    sr = d // SLAB_LANES
    for s in range(sr):
        ref[pl.ds(s, rows, stride=sr), :] = x[:, s * SLAB_LANES:(s + 1) * SLAB_LANES]


def _load_slabs(ref, slot, first, rows, sr):
    return jnp.concatenate([ref[slot, pl.ds(first * sr + s, rows, stride=sr), :] for s in range(sr)], axis=1)


def _gather_rows(i, nt, idx_hbm, src_hbm, idx_ref, buf_ref, isem, gsem, rows, sr):
    def idx_copy(tile, slot):
        return pltpu.make_async_copy(idx_hbm.at[pl.ds(tile, 1)], idx_ref.at[pl.ds(slot, 1)], isem.at[slot])

    def start_rows(slot):
        for r in range(rows):
            src = pl.multiple_of(idx_ref[slot, r], sr)
            pltpu.make_async_copy(src_hbm.at[pl.ds(src, sr)], buf_ref.at[slot, pl.ds(r * sr, sr)],
                                  gsem.at[slot]).start()

    slot = jnp.bitwise_and(i, 1)

    @pl.when(i == 0)
    def _():
        idx_copy(0, 0).start()
        idx_copy(0, 0).wait()
        start_rows(0)
        if nt > 1:
            idx_copy(1, 1).start()

    @pl.when(i + 1 < nt)
    def _():
        idx_copy(i + 1, 1 - slot).wait()
        start_rows(1 - slot)

    @pl.when(i + 2 < nt)
    def _():
        idx_copy(i + 2, slot).start()

    pltpu.make_async_copy(src_hbm.at[pl.ds(0, rows * sr)], buf_ref.at[slot], gsem.at[slot]).wait()
    return slot


def _dispatch_kernel(idx_hbm, x_hbm, o_hbm, idx_ref, tile_ref, zero_ref, isem, lsem, csem,
                     *, nt, n_tok, n_pad, sr):
    i = pl.program_id(0)
    slot = jnp.bitwise_and(i, 1)
    other = 1 - slot
    tg = n_tok // TOP_K
    n_all = n_tok + n_pad
    stage = lax.rem(i, 3)

    def idx_copy(tile, s):
        return pltpu.make_async_copy(idx_hbm.at[pl.ds(tile, 1)], idx_ref.at[pl.ds(s, 1)], isem.at[s])

    def load(tile, s):
        return pltpu.make_async_copy(x_hbm.at[pl.ds(tile * tg * sr, tg * sr)], tile_ref.at[s], lsem.at[s])

    def copies_done(s):
        return pltpu.make_async_copy(x_hbm.at[pl.ds(0, n_all * sr)], o_hbm.at[pl.ds(0, n_all * sr)], csem.at[s])

    @pl.when(i == 0)
    def _():
        zero_ref[...] = jnp.zeros_like(zero_ref)
        idx_copy(0, 0).start()
        load(0, 0).start()

    idx_copy(i, slot).wait()
    load(i, stage).wait()

    @pl.when(i + 1 < nt)
    def _():
        idx_copy(i + 1, other).start()
        load(i + 1, lax.rem(i + 1, 3)).start()

    for r in range(n_tok):
        dst = pl.multiple_of(idx_ref[slot, r], sr)
        pltpu.make_async_copy(tile_ref.at[stage, pl.ds((r % tg) * sr, sr)], o_hbm.at[pl.ds(dst, sr)],
                              csem.at[slot]).start()
    for r in range(n_tok, n_all):
        dst = pl.multiple_of(idx_ref[slot, r], sr)
        pltpu.make_async_copy(zero_ref, o_hbm.at[pl.ds(dst, sr)], csem.at[slot]).start()

    @pl.when(i > 0)
    def _():
        copies_done(other).wait()

    @pl.when(i == nt - 1)
    def _():
        copies_done(slot).wait()


def _dispatch(idx, hn_slabs, n_rows, sr):
    nt, n_all = idx.shape
    n_tok = TOP_K * COMBINE_ROWS
    hbm = pl.BlockSpec(memory_space=pl.ANY)
    return pl.pallas_call(
        functools.partial(_dispatch_kernel, nt=nt, n_tok=n_tok, n_pad=n_all - n_tok, sr=sr),
        out_shape=jax.ShapeDtypeStruct((n_rows * sr, SLAB_LANES), F32),
        grid=(nt,),
        in_specs=[hbm, hbm],
        out_specs=hbm,
        scratch_shapes=[pltpu.SMEM((2, n_all), jnp.int32),
                        pltpu.VMEM((3, COMBINE_ROWS * sr, SLAB_LANES), F32), pltpu.VMEM((sr, SLAB_LANES), F32),
                        pltpu.SemaphoreType.DMA((2,)), pltpu.SemaphoreType.DMA((3,)),
                        pltpu.SemaphoreType.DMA((2,))],
        compiler_params=pltpu.CompilerParams(dimension_semantics=("arbitrary",)),
        name="dispatch",
    )(idx, hn_slabs)


def _expert_kernel(te_ref, first_ref, next_ref, nu_ref, x_ref, w1_hbm, w3_hbm, w2_hbm, o_ref,
                   stage_refs, wb_refs, wsem, *, sr):
    i = pl.program_id(0)
    tm = EXPERT_ROWS
    w_hbm = (w1_hbm, w3_hbm, w2_hbm)
    used = i < nu_ref[0]

    def weight_copies(e):
        out = []
        for j, (w, st) in enumerate(zip(w_hbm, stage_refs)):
            half = st.shape[0] // 2
            for q in range(2):
                rows = pl.ds(q * half, half)
                out.append((pltpu.make_async_copy(w.at[e, rows], st.at[rows], wsem.at[2 * j + q]), q))
        return out

    @pl.when(i == 0)
    def _():
        for cp, q in weight_copies(te_ref[0]):
            cp.start(priority=q)

    @pl.when(used & (first_ref[i] == 1))
    def _():
        for cp, _ in weight_copies(te_ref[i]):
            cp.wait()
        for st, wb in zip(stage_refs, wb_refs):
            rows = st.shape[0] // WEIGHT_CAST_CHUNKS

            def cast_chunk(c, carry, st=st, wb=wb, rows=rows):
                r0 = pl.multiple_of(c * rows, rows)
                wb[pl.ds(r0, rows), :] = st[pl.ds(r0, rows), :].astype(BF16)
                return carry

            lax.fori_loop(0, WEIGHT_CAST_CHUNKS, cast_chunk, 0)

        @pl.when(next_ref[i] >= 0)
        def _():
            for cp, q in weight_copies(next_ref[i]):
                cp.start(priority=q)

    @pl.when(used)
    def _():
        xb = jnp.concatenate([x_ref[pl.ds(s, tm, stride=sr), :] for s in range(sr)], axis=1).astype(BF16)
        h1 = jnp.dot(xb, wb_refs[0][...], preferred_element_type=F32)
        h3 = jnp.dot(xb, wb_refs[1][...], preferred_element_type=F32)
        hid = (h1 * _sigmoid(h1)) * h3
        _store_slabs(o_ref, jnp.dot(hid.astype(BF16), wb_refs[2][...], preferred_element_type=F32))

    @pl.when(jnp.logical_not(used))
    def _():
        o_ref[...] = jnp.zeros_like(o_ref)


def _experts(tile_e, tile_first, tile_next, n_used, x_slabs, w1, w3, w2):
    nt = tile_e.shape[0]
    d, de = w1.shape[1], w1.shape[2]
    sr = d // SLAB_LANES
    tm = EXPERT_ROWS
    hbm = pl.BlockSpec(memory_space=pl.ANY)
    grid_spec = pltpu.PrefetchScalarGridSpec(
        num_scalar_prefetch=4,
        grid=(nt,),
        in_specs=[pl.BlockSpec((tm * sr, SLAB_LANES), lambda i, te, fi, nx, nu: (jnp.minimum(i, nu[0] - 1), 0)),
                  hbm, hbm, hbm],
        out_specs=pl.BlockSpec((tm * sr, SLAB_LANES), lambda i, *_: (i, 0)),
        scratch_shapes=[[pltpu.VMEM((d, de), F32), pltpu.VMEM((d, de), F32), pltpu.VMEM((de, d), F32)],
                        [pltpu.VMEM((d, de), BF16), pltpu.VMEM((d, de), BF16), pltpu.VMEM((de, d), BF16)],
                        pltpu.SemaphoreType.DMA((6,))],
    )
    return pl.pallas_call(
        functools.partial(_expert_kernel, sr=sr),
        out_shape=jax.ShapeDtypeStruct((nt * tm * sr, SLAB_LANES), F32),
        grid_spec=grid_spec,
        compiler_params=pltpu.CompilerParams(
            dimension_semantics=("arbitrary",), vmem_limit_bytes=EXPERT_VMEM_LIMIT),
        name="experts",
    )(tile_e, tile_first, tile_next, n_used, x_slabs, w1, w3, w2)


def _combine_kernel(dest_hbm, y_hbm, x1_ref, route_ref, fg_ref, o_ref, idx_ref, buf_ref, isem, gsem,
                    *, nt, sr):
    i = pl.program_id(0)
    tg = COMBINE_ROWS
    slot = _gather_rows(i, nt, dest_hbm, y_hbm, idx_ref, buf_ref, isem, gsem, TOP_K * tg, sr)
    route = route_ref[...]
    y = (_load_slabs(buf_ref, slot, 0, tg, sr) * route[:, 2:3]
         + _load_slabs(buf_ref, slot, tg, tg, sr) * route[:, 3:4])
    o_ref[...] = _rms(x1_ref[...] + y, fg_ref[...])


def _combine(dest, ybuf, x1, route, fg):
    t, d = x1.shape
    tg = COMBINE_ROWS
    sr = d // SLAB_LANES
    return pl.pallas_call(
        functools.partial(_combine_kernel, nt=t // tg, sr=sr),
        out_shape=jax.ShapeDtypeStruct((t, d), F32),
        grid=(t // tg,),
        in_specs=[pl.BlockSpec(memory_space=pl.ANY),
                  pl.BlockSpec(memory_space=pl.ANY),
                  pl.BlockSpec((tg, d), lambda i: (i, 0)),
                  pl.BlockSpec((tg, LANES), lambda i: (i, 0)),
                  pl.BlockSpec((1, d), lambda i: (0, 0))],
        out_specs=pl.BlockSpec((tg, d), lambda i: (i, 0)),
        scratch_shapes=[pltpu.SMEM((2, TOP_K * tg), jnp.int32),
                        pltpu.VMEM((2, TOP_K * tg * sr, SLAB_LANES), F32),
                        pltpu.SemaphoreType.DMA((2,)), pltpu.SemaphoreType.DMA((2,))],
        compiler_params=pltpu.CompilerParams(
            dimension_semantics=("arbitrary",), vmem_limit_bytes=VMEM_LIMIT),
        name="combine",
    )(dest, ybuf, x1, route, fg)


def _pad_cols(m, n):
    return jnp.pad(m, ((0, 0), (0, n - m.shape[1])))


def _pad_rows(m, n):
    return jnp.pad(m, ((0, n - m.shape[0]), (0, 0)))


def _pick_tile(n, cap):
    for unit in (MXU_WIDTH, LANES):
        fits = [c for c in range(unit, cap + 1, unit) if n % c == 0]
        if fits:
            return fits[-1]
    raise ValueError(f"no lane-aligned tile divides {n}")


def _dispatch_plan(experts, rank, counts, n_tok):
    tm = EXPERT_ROWS
    nt = n_tok * TOP_K // tm + N_EXPERTS
    tiles_e = (counts + tm - 1) // tm
    tile_end = jnp.cumsum(tiles_e)
    tile_start = tile_end - tiles_e
    dest = (tile_start[experts] * tm + rank).astype(jnp.int32)
    n_used = tile_end[-1:].astype(jnp.int32)
    tiles = jnp.arange(nt, dtype=jnp.int32)
    tile_e = jnp.minimum(jnp.sum((tile_end[None, :] <= tiles[:, None]).astype(jnp.int32), axis=1),
                         N_EXPERTS - 1).astype(jnp.int32)
    tile_first = ((tiles == 0) | (tile_e != jnp.roll(tile_e, 1))).astype(jnp.int32)
    next_start = tile_end[tile_e]
    tile_next = jnp.where(next_start < n_used[0], tile_e[jnp.minimum(next_start, nt - 1)], -1).astype(jnp.int32)

    seg_len = jnp.concatenate([tiles_e * tm - counts, (nt - n_used) * tm])
    seg_first = jnp.concatenate([tile_start * tm + counts, n_used * tm])
    seg_stop = jnp.cumsum(seg_len)
    j = jnp.arange(N_EXPERTS * tm, dtype=jnp.int32)
    seg = jnp.sum((seg_stop[None, :] <= j[:, None]).astype(jnp.int32), axis=1)
    pad_rows = (seg_first[seg] + j - (seg_stop - seg_len)[seg]).astype(jnp.int32)
    return tile_e, tile_first, tile_next, n_used, dest, pad_rows


def _layer(x2, b, s, norm1_g, w_in, attn_out_g, mu, w0, w2, a0, a2, g2, k_k, k_a, r_k, lnx_w, lnx_b,
           w_out, norm2_g, wg, bg, we, be, ew1, ew3, ew2, final_g):
    t, d = x2.shape
    aw = attn_out_g.shape[0]
    cw = w0.shape[0]
    heads = aw // ATT_HEAD_DIM
    dl, al, gl = w2.shape[0], a2.shape[0], g2.shape[0]
    pw, pa, pg = _round_up(dl, LANES), _round_up(al, LANES), _round_up(gl, LANES)
    wpad = 3 * cw + pw + pa + pg

    w_in = w_in.astype(BF16)
    sh = w_in[:, 3 * aw:]
    o = 3 * cw
    w_arr = jnp.concatenate(
        [sh[:, :o], _pad_cols(sh[:, o:o + dl], pw), _pad_cols(sh[:, o + dl:o + dl + al], pa),
         _pad_cols(sh[:, o + dl + al:o + dl + al + gl], pg), w_in[:, :3 * aw]], axis=1)
    mu_arr = jnp.concatenate(
        [mu[:o], jnp.pad(mu[o:o + dl], (0, pw - dl)), jnp.pad(mu[o + dl:o + dl + al], (0, pa - al)),
         jnp.pad(mu[o + dl + al:], (0, pg - gl))])[None, :]

    n_all = w_arr.shape[1]
    proj = _in_proj(x2, norm1_g[None, :], w_arr, tm=min(512, t), tn=_pick_tile(n_all, 4096))

    att = _moba(proj, b, s, heads, wpad // LANES)
    r1 = lambda a: a.reshape(1, -1)
    rw = _rwkv(proj, b, s, cw, wpad,
               (mu_arr, r1(w0), _pad_rows(w2, pw).astype(BF16), r1(a0), _pad_rows(a2, pa).astype(BF16),
                _pad_rows(g2, pg).astype(BF16), r1(k_k), r1(k_a), r1(r_k), r1(lnx_w), r1(lnx_b)))

    wr = _pad_cols(jnp.concatenate([wg, we], axis=1), LANES)
    wrh = wr.astype(BF16)
    wrm = (wr - wrh.astype(F32)).astype(BF16)
    br = jnp.pad(jnp.concatenate([bg, be]), (0, LANES - N_GROUPS - N_EXPERTS))[None, :]
    wo = w_out.astype(BF16)
    x1, hn, route, counts = _out_proj(att, rw, x2, r1(attn_out_g), wo[:aw], wo[aw:], r1(norm2_g), wrh, wrm, br,
                                      tm=min(512, t))

    plan_in = route[:, :6].astype(jnp.int32)
    counts = counts[0, N_GROUPS:N_GROUPS + N_EXPERTS].astype(jnp.int32)
    tg = COMBINE_ROWS
    n_tiles = t // tg
    pad_per_tile = N_EXPERTS * EXPERT_ROWS // n_tiles
    tile_e, tile_first, tile_next, n_used, dest, pad_rows = _dispatch_plan(
        plan_in[:, 0:TOP_K], plan_in[:, 4:4 + TOP_K], counts, t)

    sr = d // SLAB_LANES
    dest_tiles = dest.reshape(n_tiles, tg, TOP_K).transpose(0, 2, 1).reshape(n_tiles, TOP_K * tg) * sr
    disp_idx = jnp.concatenate([dest_tiles, pad_rows.reshape(n_tiles, pad_per_tile) * sr], axis=1)
    xbuf = _dispatch(disp_idx, hn, tile_e.shape[0] * EXPERT_ROWS, sr)
    ybuf = _experts(tile_e, tile_first, tile_next, n_used, xbuf, ew1, ew3, ew2)
    return _combine(dest_tiles, ybuf, x1, route, r1(final_g))


def kernel(x, norm1_g, w_in, attn_out_g, rwkv_mu, rwkv_w0, rwkv_w2, rwkv_a0, rwkv_a2, rwkv_g2, rwkv_k_k,
           rwkv_k_a, rwkv_r_k, rwkv_lnx_w, rwkv_lnx_b, w_out, norm2_g, router_group_w, router_group_b,
           router_expert_w, router_expert_b, expert_w1, expert_w3, expert_w2, final_g):
    b, s, d = x.shape
    assert norm1_g.shape[0] == 1, "single-layer block"
    assert s % MOBA_BLOCK == 0 and s % RWKV_CHUNK == 0
    assert (b * s) % max(EXPERT_ROWS, COMBINE_ROWS) == 0
    assert (N_EXPERTS * EXPERT_ROWS) % ((b * s) // COMBINE_ROWS) == 0, "zero-fill rows split evenly over tiles"
    out = _layer(x.reshape(b * s, d), b, s, norm1_g[0], w_in[0], attn_out_g[0], rwkv_mu[0], rwkv_w0[0],
                 rwkv_w2[0], rwkv_a0[0], rwkv_a2[0], rwkv_g2[0], rwkv_k_k[0], rwkv_k_a[0], rwkv_r_k[0],
                 rwkv_lnx_w[0], rwkv_lnx_b[0], w_out[0], norm2_g[0], router_group_w[0], router_group_b[0],
                 router_expert_w[0], router_expert_b[0], expert_w1[0], expert_w3[0], expert_w2[0], final_g)
    return out.reshape(b, s, d)
```

```python
import functools
import math

import jax
import jax.numpy as jnp
from jax import lax
from jax.experimental import pallas as pl
from jax.experimental.pallas import tpu as pltpu

F32 = jnp.float32
BF16 = jnp.bfloat16

LANES = 128
MXU_WIDTH = 256
SLAB_LANES = 128
ATT_HEAD_DIM = 128
RWKV_HEAD_DIM = 64
MOBA_BLOCK = 256
MOBA_TOPK = 3
N_GROUPS = 4
EXPERTS_PER_GROUP = 8
N_EXPERTS = N_GROUPS * EXPERTS_PER_GROUP
TOP_K = 2
RMS_EPS = 1e-6
GN_EPS = 64e-5
NEG_BIG = -1e30
RWKV_CHUNK = 64
RWKV_CHUNKS_PER_STEP = 4
RWKV_CHUNK_LAG = 8
EXPERT_ROWS = 256
COMBINE_ROWS = 256
VMEM_LIMIT = 56 * 1024 * 1024
EXPERT_VMEM_LIMIT = 60 * 1024 * 1024
WEIGHT_CAST_CHUNKS = 16


def _round_up(n, m):
    return (n + m - 1) // m * m


def _dot(a, b):
    return jnp.dot(a.astype(BF16), b.astype(BF16), preferred_element_type=F32)


def _dot_nt(a, b):
    return lax.dot_general(a.astype(BF16), b.astype(BF16), (((1,), (1,)), ((), ())),
                           preferred_element_type=F32)


def _split3(a):
    hi = a.astype(BF16)
    r1 = a - hi.astype(F32)
    mid = r1.astype(BF16)
    lo = (r1 - mid.astype(F32)).astype(BF16)
    return hi, mid, lo


def _dot_hi(a, b):
    ah, am, _ = _split3(a)
    bh, bm, _ = _split3(b)
    d = functools.partial(jnp.dot, preferred_element_type=F32)
    return d(ah, bh) + (d(am, bh) + d(ah, bm))


def _sigmoid(x):
    return 1.0 / (1.0 + jnp.exp(-x))


def _rms(x, g):
    ms = jnp.mean(x * x, axis=-1, keepdims=True)
    return x * lax.rsqrt(ms + RMS_EPS) * g


def _in_proj_kernel(x_ref, g_ref, w_ref, o_ref):
    xn = _rms(x_ref[...], g_ref[...]).astype(BF16)
    o_ref[...] = jnp.dot(xn, w_ref[...], preferred_element_type=F32)


def _in_proj(x2, g, w, tm, tn):
    t, d = x2.shape
    n = w.shape[1]
    return pl.pallas_call(
        _in_proj_kernel,
        out_shape=jax.ShapeDtypeStruct((t, n), F32),
        grid=(n // tn, t // tm),
        in_specs=[pl.BlockSpec((tm, d), lambda j, i: (i, 0)),
                  pl.BlockSpec((1, d), lambda j, i: (0, 0)),
                  pl.BlockSpec((d, tn), lambda j, i: (0, j))],
        out_specs=pl.BlockSpec((tm, tn), lambda j, i: (i, j)),
        compiler_params=pltpu.CompilerParams(
            dimension_semantics=("arbitrary", "arbitrary"), vmem_limit_bytes=VMEM_LIMIT),
        name="in_proj",
    )(x2, g, w)


def _moba_kernel(q_ref, k_ref, v_ref, o_ref, *, nblk, scale):
    bs = MOBA_BLOCK
    k = k_ref[...]
    kb = k.astype(BF16)
    vtb = v_ref[...].T.astype(BF16)
    qt = q_ref[...].T
    qtb = qt.astype(BF16)
    kmean = jnp.concatenate(
        [jnp.sum(k[n * bs:(n + 1) * bs, :], axis=0, keepdims=True) for n in range(nblk)], axis=0) * (1.0 / bs)
    gate_all = _dot_hi(kmean, qt)
    kpos = lax.broadcasted_iota(jnp.int32, (bs, bs), 0)
    qpos = lax.broadcasted_iota(jnp.int32, (bs, bs), 1)
    causal = kpos <= qpos

    def scores(qi):
        return jnp.dot(kb[0:(qi + 1) * bs, :], qtb[:, qi * bs:(qi + 1) * bs], preferred_element_type=F32)

    exp_scale = scale * math.log2(math.e)
    s_next = scores(0)
    for qi in range(nblk):
        qs = slice(qi * bs, (qi + 1) * bs)
        nk = (qi + 1) * bs
        s_all = s_next
        if qi + 1 < nblk:
            s_next = scores(qi + 1)
        blocks = []
        if qi > MOBA_TOPK:
            gate = gate_all[0:qi, qs]
            blk = lax.broadcasted_iota(jnp.int32, (qi, bs), 0)
        for n in range(qi):
            sn = s_all[n * bs:(n + 1) * bs, :]
            if qi > MOBA_TOPK:
                gn = gate[n:n + 1, :]
                beats = (gate > gn) | ((gate == gn) & (blk < n))
                rank = jnp.sum(jnp.where(beats, 1.0, 0.0), axis=0, keepdims=True)
                sn = jnp.where(rank < float(MOBA_TOPK), sn, NEG_BIG)
            blocks.append(sn)
        blocks.append(jnp.where(causal, s_all[qi * bs:nk, :], NEG_BIG))
        m = functools.reduce(jnp.maximum, [jnp.max(x, axis=0, keepdims=True) for x in blocks])
        ps = [jnp.exp2((x - m) * exp_scale) for x in blocks]
        l = functools.reduce(jnp.add, [jnp.sum(x, axis=0, keepdims=True) for x in ps])
        pb = jnp.concatenate([x.astype(BF16) for x in ps], axis=0)
        acc = jnp.dot(vtb[:, 0:nk], pb, preferred_element_type=F32)
        o_ref[qs, :] = (acc / l).T


def _moba(proj, b, s, heads, col0):
    d = ATT_HEAD_DIM
    nblk = s // MOBA_BLOCK
    kern = functools.partial(_moba_kernel, nblk=nblk, scale=1.0 / math.sqrt(d))
    return pl.pallas_call(
        kern,
        out_shape=jax.ShapeDtypeStruct((b * s, heads * d), F32),
        grid=(b, heads),
        in_specs=[pl.BlockSpec((s, d), lambda bi, h: (bi, col0 + h)),
                  pl.BlockSpec((s, d), lambda bi, h: (bi, col0 + heads + h)),
                  pl.BlockSpec((s, d), lambda bi, h: (bi, col0 + 2 * heads + h))],
        out_specs=pl.BlockSpec((s, d), lambda bi, h: (bi, h)),
        compiler_params=pltpu.CompilerParams(
            dimension_semantics=("arbitrary", "arbitrary"), vmem_limit_bytes=VMEM_LIMIT),
        name="moba",
    )(proj, proj, proj)


def _head_sum(x, lo):
    s0 = jnp.sum(jnp.where(lo, x, 0.0), axis=-1, keepdims=True)
    s1 = jnp.sum(jnp.where(lo, 0.0, x), axis=-1, keepdims=True)
    return jnp.where(lo, s0, s1)


def _stack_heads(x, lo):
    return jnp.concatenate([jnp.where(lo, x, 0.0), jnp.where(lo, 0.0, x)], axis=0)


def _rwkv_kernel(p_ref, mu_ref, w0_ref, w2_ref, a0_ref, a2_ref, g2_ref, kk_ref, ka_ref, rk_ref,
                 lnw_ref, lnb_ref, o_ref, carry_ref, state_ref, *, n_sub, **dims):
    @pl.when(pl.program_id(1) == 0)
    def _():
        carry_ref[...] = jnp.zeros_like(carry_ref)
        state_ref[...] = jnp.zeros_like(state_ref)

    C = RWKV_CHUNK
    chunks = [_rwkv_chunk(p_ref.at[pl.ds(h * C, C)], mu_ref, w0_ref, w2_ref, a0_ref, a2_ref, g2_ref, kk_ref,
                          ka_ref, rk_ref, lnw_ref, lnb_ref, o_ref.at[pl.ds(h * C, C)], carry_ref, state_ref,
                          **dims) for h in range(n_sub)]
    _interleave(chunks, RWKV_CHUNK_LAG)


def _interleave(chunks, lag):
    n = len(chunks)
    steps, done, wrote, waiting = [0] * n, [False] * n, [False] * n, [False] * n

    def advance(j):
        if waiting[j] and j > 0:
            while not (wrote[j - 1] or done[j - 1]):
                advance(j - 1)
        waiting[j] = False
        try:
            tag = next(chunks[j])
        except StopIteration:
            done[j] = True
            return
        steps[j] += 1
        waiting[j] = tag == "need_state"
        wrote[j] = wrote[j] or tag == "state_done"

    while not all(done):
        for j in range(n):
            if not done[j] and (j == 0 or done[j - 1] or steps[j - 1] >= lag):
                advance(j)


def _rwkv_chunk(p_ref, mu_ref, w0_ref, w2_ref, a0_ref, a2_ref, g2_ref, kk_ref, ka_ref, rk_ref,
                lnw_ref, lnb_ref, o_ref, carry_ref, state_ref, *, cw, pw, pa, pg):
    C = RWKV_CHUNK
    C2 = 2 * C
    pr = p_ref[...]
    row = lax.broadcasted_iota(jnp.int32, pr.shape, 0)
    prev = jnp.where(row == 0, carry_ref[...], pltpu.roll(pr, 1, 0))
    carry_ref[...] = pr[C - 1:C, :]
    ps = pr + (prev - pr) * mu_ref[...]
    yield "step"

    r = ps[:, 0:cw]
    k = ps[:, cw:2 * cw]
    v = ps[:, 2 * cw:3 * cw]
    o = 3 * cw
    xw = ps[:, o:o + pw]
    xa = ps[:, o + pw:o + pw + pa]
    xg = ps[:, o + pw + pa:o + pw + pa + pg]

    z = -(w0_ref[...] + _dot(jnp.tanh(xw), w2_ref[...]))
    softplus = jnp.maximum(z, 0.0) + jnp.log(1.0 + jnp.exp(-jnp.abs(z)))
    logw = -jnp.exp(-softplus - 0.5)
    asig = _sigmoid(a0_ref[...] + _dot(xa, a2_ref[...]))
    gate = _dot(_sigmoid(xg), g2_ref[...])
    kkr = k * kk_ref[...]
    kmod = k * (1.0 + (asig - 1.0) * ka_ref[...])
    yield "step"

    ti = lax.broadcasted_iota(jnp.int32, (C, C), 0)
    si = lax.broadcasted_iota(jnp.int32, (C, C), 1)
    tril = jnp.where(si <= ti, 1.0, 0.0).astype(BF16)
    lh, lm, ll = _split3(logw)
    d32 = functools.partial(jnp.dot, preferred_element_type=F32)
    cum = d32(tril, lh) + (d32(tril, lm) + d32(tril, ll))
    cum_end = cum[C - 1:C, :]
    e_prev = jnp.exp(cum - logw)
    e_incl = jnp.exp(cum)
    e_inv = jnp.exp(-cum)
    e_end = jnp.exp(cum_end - cum)
    g_end = jnp.exp(cum_end)
    yield "step"

    lane = lax.broadcasted_iota(jnp.int32, (C, LANES), 1)
    lo = lane < RWKV_HEAD_DIM
    rho = lax.broadcasted_iota(jnp.int32, (C2, C2), 0)
    sig = lax.broadcasted_iota(jnp.int32, (C2, C2), 1)
    same_head = (rho >= C) == (sig >= C)
    strict = same_head & ((sig % C) < (rho % C))
    incl = same_head & ((sig % C) <= (rho % C))
    eye = jnp.where(rho == sig, 1.0, 0.0)

    pairs = range(cw // LANES)
    sls = [slice(p * LANES, (p + 1) * LANES) for p in pairs]
    incl2 = jnp.concatenate([incl, incl], axis=1)
    strict2 = jnp.concatenate([strict, strict], axis=1)

    b2, r2, k2, v2, at, rt, v2s, gm = [], [], [], [], [], [], [], []
    for sl in sls:
        kk2 = kkr[:, sl]
        nrm = jnp.sqrt(_head_sum(kk2 * kk2, lo))
        kkn = kk2 / jnp.maximum(nrm, 1e-12)
        b2.append(kkn * asig[:, sl])
        r2.append(r[:, sl])
        k2.append(kmod[:, sl])
        v2.append(v[:, sl])
        at.append(-kkn * e_prev[:, sl])
        rt.append(r2[-1] * e_incl[:, sl])
        bt = b2[-1] * e_inv[:, sl]
        kt = k2[-1] * e_inv[:, sl]
        v2s.append(_stack_heads(v2[-1], lo))
        lhs = jnp.concatenate([_stack_heads(at[-1], lo), _stack_heads(rt[-1], lo)], axis=0)
        rhs = jnp.concatenate([bt, bt, kt, kt], axis=0)
        gm.append(_dot_nt(lhs, rhs))
        yield "step"

    labk = [jnp.where(strict2, g[0:C2, :], 0.0) for g in gm]
    mrbk = [jnp.where(incl2, g[C2:2 * C2, :], 0.0) for g in gm]
    lakv = [_dot(labk[p][:, C2:2 * C2], v2s[p]) for p in pairs]
    yield "step"

    lab = [m[:, 0:C2] for m in labk]
    tinv = [eye + m for m in lab]
    pw2 = [_dot(m, m) for m in lab]
    yield "step"
    n_pow = int(math.log2(C))
    for k in range(1, n_pow):
        if k < n_pow - 1:
            prod = [_dot(jnp.concatenate([t, m], axis=0), m) for t, m in zip(tinv, pw2)]
            tinv = [t + pr[0:C2] for t, pr in zip(tinv, prod)]
            pw2 = [pr[C2:2 * C2] for pr in prod]
        else:
            tinv = [t + _dot(t, m) for t, m in zip(tinv, pw2)]
        yield "step"

    yield "need_state"
    s0 = [state_ref[p] for p in pairs]
    ars = [_dot_nt(jnp.concatenate([at[p], rt[p]], axis=0), s0[p]) for p in pairs]
    yield "step"
    u2s = [_dot(t, _stack_heads(ars[p][0:C], lo) + lakv[p]) for p, t in zip(pairs, tinv)]
    yield "step"
    y2s = [_stack_heads(ars[p][C:C2], lo) + _dot(mrbk[p], jnp.concatenate([u2s[p], v2s[p]], axis=0))
           for p in pairs]
    u2 = [m[0:C] + m[C:C2] for m in u2s]
    y2 = [m[0:C] + m[C:C2] for m in y2s]
    yield "step"

    for p, sl in zip(pairs, sls):
        uv = jnp.concatenate([u2[p], v2[p]], axis=0)
        bk = jnp.concatenate([b2[p] * e_end[:, sl], k2[p] * e_end[:, sl]], axis=0)
        upd = _dot(uv.T, bk)
        state_ref[p] = s0[p] * g_end[:, sl] + jnp.where(same_head, upd, 0.0)
    yield "state_done"

    for p, sl in zip(pairs, sls):
        mean = _head_sum(y2[p], lo) * (1.0 / RWKV_HEAD_DIM)
        yc = y2[p] - mean
        var = _head_sum(yc * yc, lo) * (1.0 / RWKV_HEAD_DIM)
        yn = yc * lax.rsqrt(var + GN_EPS) * lnw_ref[:, sl] + lnb_ref[:, sl]
        bonus = _head_sum(r2[p] * k2[p] * rk_ref[:, sl], lo) * v2[p]
        o_ref[:, sl] = (yn + bonus) * gate[:, sl]
        yield "step"


def _rwkv(proj, b, s, cw, wpad, params):
    (mu, w0, w2, a0, a2, g2, kk, ka, rk, lnw, lnb) = params
    n_sub = RWKV_CHUNKS_PER_STEP if s % (RWKV_CHUNKS_PER_STEP * RWKV_CHUNK) == 0 else 1
    C = n_sub * RWKV_CHUNK
    nch = s // C
    pw, pa, pg = w2.shape[0], a2.shape[0], g2.shape[0]
    kern = functools.partial(_rwkv_kernel, n_sub=n_sub, cw=cw, pw=pw, pa=pa, pg=pg)
    row = lambda n: pl.BlockSpec((1, n), lambda bi, c: (0, 0))
    full = lambda a: pl.BlockSpec(a.shape, lambda bi, c: (0, 0))
    return pl.pallas_call(
        kern,
        out_shape=jax.ShapeDtypeStruct((b * s, cw), F32),
        grid=(b, nch),
        in_specs=[pl.BlockSpec((C, wpad), lambda bi, c: (bi * nch + c, 0)),
                  row(wpad), row(cw), full(w2), row(cw), full(a2), full(g2),
                  row(cw), row(cw), row(cw), row(cw), row(cw)],
        out_specs=pl.BlockSpec((C, cw), lambda bi, c: (bi * nch + c, 0)),
        scratch_shapes=[pltpu.VMEM((1, wpad), F32),
                        pltpu.VMEM((cw // LANES, LANES, LANES), F32)],
        compiler_params=pltpu.CompilerParams(
            dimension_semantics=("arbitrary", "arbitrary"), vmem_limit_bytes=VMEM_LIMIT),
        name="rwkv",
    )(proj, mu, w0, w2, a0, a2, g2, kk, ka, rk, lnw, lnb)


def _out_proj_kernel(att_ref, rw_ref, x_ref, ga_ref, woa_ref, wor_ref, g2_ref, wrh_ref, wrm_ref, br_ref,
                     x1_ref, hn_ref, route_ref, counts_ref):
    att = _rms(att_ref[...], ga_ref[...])
    y = (jnp.dot(att.astype(BF16), woa_ref[...], preferred_element_type=F32)
         + jnp.dot(rw_ref[...].astype(BF16), wor_ref[...], preferred_element_type=F32))
    x1 = x_ref[...] + y
    x1_ref[...] = x1
    hn = _rms(x1, g2_ref[...])
    _store_slabs(hn_ref, hn)

    hh, hm, _ = _split3(hn)
    d32 = functools.partial(jnp.dot, preferred_element_type=F32)
    logits = d32(hh, wrh_ref[...]) + (d32(hm, wrh_ref[...]) + d32(hh, wrm_ref[...])) + br_ref[...]
    lane = lax.broadcasted_iota(jnp.int32, logits.shape, 1)
    lanef = lane.astype(F32)
    big = float(LANES)

    is_g = lane < N_GROUPS
    gmax = jnp.max(jnp.where(is_g, logits, -jnp.inf), axis=-1, keepdims=True)
    gidx = jnp.min(jnp.where(is_g & (logits == gmax), lanef, big), axis=-1, keepdims=True)
    gsum = jnp.sum(jnp.where(is_g, jnp.exp(logits - gmax), 0.0), axis=-1, keepdims=True)
    g_gate = 1.0 / gsum

    e_lo = N_GROUPS + EXPERTS_PER_GROUP * gidx
    in_grp = (lanef >= e_lo) & (lanef < e_lo + EXPERTS_PER_GROUP)
    e1 = jnp.max(jnp.where(in_grp, logits, -jnp.inf), axis=-1, keepdims=True)
    i1 = jnp.min(jnp.where(in_grp & (logits == e1), lanef, big), axis=-1, keepdims=True)
    rest = in_grp & (lanef != i1)
    e2 = jnp.max(jnp.where(rest, logits, -jnp.inf), axis=-1, keepdims=True)
    i2 = jnp.min(jnp.where(rest & (logits == e2), lanef, big), axis=-1, keepdims=True)
    t = jnp.exp(e2 - e1)
    w1 = g_gate * (1.0 / (1.0 + t))
    w2 = g_gate * (t / (1.0 + t))

    @pl.when(pl.program_id(0) == 0)
    def _():
        counts_ref[...] = jnp.zeros_like(counts_ref)

    tm = logits.shape[0]
    pick1 = lanef == i1
    pick2 = lanef == i2
    onehot = jnp.where(pick1 | pick2, 1.0, 0.0)
    ti = lax.broadcasted_iota(jnp.int32, (tm, tm), 0)
    si = lax.broadcasted_iota(jnp.int32, (tm, tm), 1)
    before = _dot(jnp.where(si < ti, 1.0, 0.0), onehot) + counts_ref[...]
    r1 = jnp.sum(jnp.where(pick1, before, 0.0), axis=-1, keepdims=True)
    r2 = jnp.sum(jnp.where(pick2, before, 0.0), axis=-1, keepdims=True)
    counts_ref[...] = counts_ref[...] + jnp.sum(onehot, axis=0, keepdims=True)

    route_ref[...] = jnp.where(lane == 0, i1 - N_GROUPS,
                     jnp.where(lane == 1, i2 - N_GROUPS,
                     jnp.where(lane == 2, w1,
                     jnp.where(lane == 3, w2,
                     jnp.where(lane == 4, r1,
                     jnp.where(lane == 5, r2, 0.0))))))


def _out_proj(att, rw, x2, ga, woa, wor, g2, wrh, wrm, br, tm):
    t, d = x2.shape
    wa = att.shape[1]
    wr = rw.shape[1]
    sr = d // SLAB_LANES
    c = lambda shape: pl.BlockSpec(shape, lambda i: (0, 0))
    return pl.pallas_call(
        _out_proj_kernel,
        out_shape=(jax.ShapeDtypeStruct((t, d), F32), jax.ShapeDtypeStruct((t * sr, SLAB_LANES), F32),
                   jax.ShapeDtypeStruct((t, LANES), F32), jax.ShapeDtypeStruct((1, LANES), F32)),
        grid=(t // tm,),
        in_specs=[pl.BlockSpec((tm, wa), lambda i: (i, 0)),
                  pl.BlockSpec((tm, wr), lambda i: (i, 0)),
                  pl.BlockSpec((tm, d), lambda i: (i, 0)),
                  c((1, wa)), c((wa, d)), c((wr, d)), c((1, d)), c((d, LANES)), c((d, LANES)), c((1, LANES))],
        out_specs=(pl.BlockSpec((tm, d), lambda i: (i, 0)), pl.BlockSpec((tm * sr, SLAB_LANES), lambda i: (i, 0)),
                   pl.BlockSpec((tm, LANES), lambda i: (i, 0)), c((1, LANES))),
        compiler_params=pltpu.CompilerParams(
            dimension_semantics=("arbitrary",), vmem_limit_bytes=VMEM_LIMIT),
        name="out_proj",
    )(att, rw, x2, ga, woa, wor, g2, wrh, wrm, br)


def _store_slabs(ref, x):
    rows, d = x.shape
    sr = d // SLAB_LANES
    for s in range(sr):
        ref[pl.ds(s, rows, stride=sr), :] = x[:, s * SLAB_LANES:(s + 1) * SLAB_LANES]


def _load_slabs(ref, slot, first, rows, sr):
    return jnp.concatenate([ref[slot, pl.ds(first * sr + s, rows, stride=sr), :] for s in range(sr)], axis=1)


def _gather_rows(i, nt, idx_hbm, src_hbm, idx_ref, buf_ref, isem, gsem, rows, sr):
    def idx_copy(tile, slot):
        return pltpu.make_async_copy(idx_hbm.at[pl.ds(tile, 1)], idx_ref.at[pl.ds(slot, 1)], isem.at[slot])

    def start_rows(slot):
        for r in range(rows):
            src = pl.multiple_of(idx_ref[slot, r], sr)
            pltpu.make_async_copy(src_hbm.at[pl.ds(src, sr)], buf_ref.at[slot, pl.ds(r * sr, sr)],
                                  gsem.at[slot]).start(priority=r % 2)

    slot = jnp.bitwise_and(i, 1)

    @pl.when(i == 0)
    def _():
        idx_copy(0, 0).start()
        idx_copy(0, 0).wait()
        start_rows(0)
        if nt > 1:
            idx_copy(1, 1).start()

    @pl.when(i + 1 < nt)
    def _():
        idx_copy(i + 1, 1 - slot).wait()
        start_rows(1 - slot)

    @pl.when(i + 2 < nt)
    def _():
        idx_copy(i + 2, slot).start()

    pltpu.make_async_copy(src_hbm.at[pl.ds(0, rows * sr)], buf_ref.at[slot], gsem.at[slot]).wait()
    return slot


def _dispatch_kernel(idx_hbm, x_hbm, o_hbm, idx_ref, tile_ref, zero_ref, isem, lsem, csem,
                     *, nt, n_tok, n_pad, sr):
    i = pl.program_id(0)
    slot = jnp.bitwise_and(i, 1)
    other = 1 - slot
    tg = n_tok // TOP_K
    n_all = n_tok + n_pad
    stage = lax.rem(i, 3)

    def idx_copy(tile, s):
        return pltpu.make_async_copy(idx_hbm.at[pl.ds(tile, 1)], idx_ref.at[pl.ds(s, 1)], isem.at[s])

    def load(tile, s):
        return pltpu.make_async_copy(x_hbm.at[pl.ds(tile * tg * sr, tg * sr)], tile_ref.at[s], lsem.at[s])

    def copies_done(s):
        return pltpu.make_async_copy(x_hbm.at[pl.ds(0, n_all * sr)], o_hbm.at[pl.ds(0, n_all * sr)], csem.at[s])

    @pl.when(i == 0)
    def _():
        zero_ref[...] = jnp.zeros_like(zero_ref)
        idx_copy(0, 0).start()
        load(0, 0).start()

    idx_copy(i, slot).wait()
    load(i, stage).wait()

    @pl.when(i + 1 < nt)
    def _():
        idx_copy(i + 1, other).start()
        load(i + 1, lax.rem(i + 1, 3)).start()

    for r in range(n_tok):
        dst = pl.multiple_of(idx_ref[slot, r], sr)
        pltpu.make_async_copy(tile_ref.at[stage, pl.ds((r % tg) * sr, sr)], o_hbm.at[pl.ds(dst, sr)],
                              csem.at[slot]).start(priority=r % 2)
    for r in range(n_tok, n_all):
        dst = pl.multiple_of(idx_ref[slot, r], sr)
        pltpu.make_async_copy(zero_ref, o_hbm.at[pl.ds(dst, sr)], csem.at[slot]).start(priority=r % 2)

    @pl.when(i > 0)
    def _():
        copies_done(other).wait()

    @pl.when(i == nt - 1)
    def _():
        copies_done(slot).wait()


def _dispatch(idx, hn_slabs, n_rows, sr):
    nt, n_all = idx.shape
    n_tok = TOP_K * COMBINE_ROWS
    hbm = pl.BlockSpec(memory_space=pl.ANY)
    return pl.pallas_call(
        functools.partial(_dispatch_kernel, nt=nt, n_tok=n_tok, n_pad=n_all - n_tok, sr=sr),
        out_shape=jax.ShapeDtypeStruct((n_rows * sr, SLAB_LANES), F32),
        grid=(nt,),
        in_specs=[hbm, hbm],
        out_specs=hbm,
        scratch_shapes=[pltpu.SMEM((2, n_all), jnp.int32),
                        pltpu.VMEM((3, COMBINE_ROWS * sr, SLAB_LANES), F32), pltpu.VMEM((sr, SLAB_LANES), F32),
                        pltpu.SemaphoreType.DMA((2,)), pltpu.SemaphoreType.DMA((3,)),
                        pltpu.SemaphoreType.DMA((2,))],
        compiler_params=pltpu.CompilerParams(dimension_semantics=("arbitrary",)),
        name="dispatch",
    )(idx, hn_slabs)


def _expert_kernel(te_ref, first_ref, next_ref, nu_ref, x_ref, w1_hbm, w3_hbm, w2_hbm, o_ref,
                   stage_refs, wb_refs, wsem, *, sr):
    i = pl.program_id(0)
    tm = EXPERT_ROWS
    w_hbm = (w1_hbm, w3_hbm, w2_hbm)
    used = i < nu_ref[0]

    def weight_copies(e):
        out = []
        for j, (w, st) in enumerate(zip(w_hbm, stage_refs)):
            half = st.shape[0] // 2
            for q in range(2):
                rows = pl.ds(q * half, half)
                out.append((pltpu.make_async_copy(w.at[e, rows], st.at[rows], wsem.at[2 * j + q]), q))
        return out

    @pl.when(i == 0)
    def _():
        for cp, q in weight_copies(te_ref[0]):
            cp.start(priority=q)

    @pl.when(used & (first_ref[i] == 1))
    def _():
        copies = weight_copies(te_ref[i])
        for j, (st, wb) in enumerate(zip(stage_refs, wb_refs)):
            for cp, _ in copies[2 * j:2 * j + 2]:
                cp.wait()
            rows = st.shape[0] // WEIGHT_CAST_CHUNKS

            def cast_chunk(c, carry, st=st, wb=wb, rows=rows):
                r0 = pl.multiple_of(c * rows, rows)
                wb[pl.ds(r0, rows), :] = st[pl.ds(r0, rows), :].astype(BF16)
                return carry

            lax.fori_loop(0, WEIGHT_CAST_CHUNKS, cast_chunk, 0)

        @pl.when(next_ref[i] >= 0)
        def _():
            for cp, q in weight_copies(next_ref[i]):
                cp.start(priority=q)

    @pl.when(used)
    def _():
        xb = jnp.concatenate([x_ref[pl.ds(s, tm, stride=sr), :] for s in range(sr)], axis=1).astype(BF16)
        h1 = jnp.dot(xb, wb_refs[0][...], preferred_element_type=F32)
        h3 = jnp.dot(xb, wb_refs[1][...], preferred_element_type=F32)
        hid = (h1 * _sigmoid(h1)) * h3
        _store_slabs(o_ref, jnp.dot(hid.astype(BF16), wb_refs[2][...], preferred_element_type=F32))

    @pl.when(jnp.logical_not(used))
    def _():
        o_ref[...] = jnp.zeros_like(o_ref)


def _experts(tile_e, tile_first, tile_next, n_used, x_slabs, w1, w3, w2):
    nt = tile_e.shape[0]
    d, de = w1.shape[1], w1.shape[2]
    sr = d // SLAB_LANES
    tm = EXPERT_ROWS
    hbm = pl.BlockSpec(memory_space=pl.ANY)
    grid_spec = pltpu.PrefetchScalarGridSpec(
        num_scalar_prefetch=4,
        grid=(nt,),
        in_specs=[pl.BlockSpec((tm * sr, SLAB_LANES), lambda i, te, fi, nx, nu: (jnp.minimum(i, nu[0] - 1), 0)),
                  hbm, hbm, hbm],
        out_specs=pl.BlockSpec((tm * sr, SLAB_LANES), lambda i, *_: (i, 0)),
        scratch_shapes=[[pltpu.VMEM((d, de), F32), pltpu.VMEM((d, de), F32), pltpu.VMEM((de, d), F32)],
                        [pltpu.VMEM((d, de), BF16), pltpu.VMEM((d, de), BF16), pltpu.VMEM((de, d), BF16)],
                        pltpu.SemaphoreType.DMA((6,))],
    )
    return pl.pallas_call(
        functools.partial(_expert_kernel, sr=sr),
        out_shape=jax.ShapeDtypeStruct((nt * tm * sr, SLAB_LANES), F32),
        grid_spec=grid_spec,
        compiler_params=pltpu.CompilerParams(
            dimension_semantics=("arbitrary",), vmem_limit_bytes=EXPERT_VMEM_LIMIT),
        name="experts",
    )(tile_e, tile_first, tile_next, n_used, x_slabs, w1, w3, w2)


def _combine_kernel(dest_hbm, y_hbm, x1_ref, route_ref, fg_ref, o_ref, idx_ref, buf_ref, isem, gsem,
                    *, nt, sr):
    i = pl.program_id(0)
    tg = COMBINE_ROWS
    slot = _gather_rows(i, nt, dest_hbm, y_hbm, idx_ref, buf_ref, isem, gsem, TOP_K * tg, sr)
    route = route_ref[...]
    y = (_load_slabs(buf_ref, slot, 0, tg, sr) * route[:, 2:3]
         + _load_slabs(buf_ref, slot, tg, tg, sr) * route[:, 3:4])
    o_ref[...] = _rms(x1_ref[...] + y, fg_ref[...])


def _combine(dest, ybuf, x1, route, fg):
    t, d = x1.shape
    tg = COMBINE_ROWS
    sr = d // SLAB_LANES
    return pl.pallas_call(
        functools.partial(_combine_kernel, nt=t // tg, sr=sr),
        out_shape=jax.ShapeDtypeStruct((t, d), F32),
        grid=(t // tg,),
        in_specs=[pl.BlockSpec(memory_space=pl.ANY),
                  pl.BlockSpec(memory_space=pl.ANY),
                  pl.BlockSpec((tg, d), lambda i: (i, 0)),
                  pl.BlockSpec((tg, LANES), lambda i: (i, 0)),
                  pl.BlockSpec((1, d), lambda i: (0, 0))],
        out_specs=pl.BlockSpec((tg, d), lambda i: (i, 0)),
        scratch_shapes=[pltpu.SMEM((2, TOP_K * tg), jnp.int32),
                        pltpu.VMEM((2, TOP_K * tg * sr, SLAB_LANES), F32),
                        pltpu.SemaphoreType.DMA((2,)), pltpu.SemaphoreType.DMA((2,))],
        compiler_params=pltpu.CompilerParams(
            dimension_semantics=("arbitrary",), vmem_limit_bytes=VMEM_LIMIT),
        name="combine",
    )(dest, ybuf, x1, route, fg)


def _pad_cols(m, n):
    return jnp.pad(m, ((0, 0), (0, n - m.shape[1])))


def _pad_rows(m, n):
    return jnp.pad(m, ((0, n - m.shape[0]), (0, 0)))


def _pick_tile(n, cap):
    for unit in (MXU_WIDTH, LANES):
        fits = [c for c in range(unit, cap + 1, unit) if n % c == 0]
        if fits:
            return fits[-1]
    raise ValueError(f"no lane-aligned tile divides {n}")


def _dispatch_plan(experts, rank, counts, n_tok):
    tm = EXPERT_ROWS
    nt = n_tok * TOP_K // tm + N_EXPERTS
    tiles_e = (counts + tm - 1) // tm
    tile_end = jnp.cumsum(tiles_e)
    tile_start = tile_end - tiles_e
    dest = (tile_start[experts] * tm + rank).astype(jnp.int32)
    n_used = tile_end[-1:].astype(jnp.int32)
    tiles = jnp.arange(nt, dtype=jnp.int32)
    tile_e = jnp.minimum(jnp.sum((tile_end[None, :] <= tiles[:, None]).astype(jnp.int32), axis=1),
                         N_EXPERTS - 1).astype(jnp.int32)
    tile_first = ((tiles == 0) | (tile_e != jnp.roll(tile_e, 1))).astype(jnp.int32)
    next_start = tile_end[tile_e]
    tile_next = jnp.where(next_start < n_used[0], tile_e[jnp.minimum(next_start, nt - 1)], -1).astype(jnp.int32)

    seg_len = jnp.concatenate([tiles_e * tm - counts, (nt - n_used) * tm])
    seg_first = jnp.concatenate([tile_start * tm + counts, n_used * tm])
    seg_stop = jnp.cumsum(seg_len)
    j = jnp.arange(N_EXPERTS * tm, dtype=jnp.int32)
    seg = jnp.sum((seg_stop[None, :] <= j[:, None]).astype(jnp.int32), axis=1)
    pad_rows = (seg_first[seg] + j - (seg_stop - seg_len)[seg]).astype(jnp.int32)
    return tile_e, tile_first, tile_next, n_used, dest, pad_rows


def _layer(x2, b, s, norm1_g, w_in, attn_out_g, mu, w0, w2, a0, a2, g2, k_k, k_a, r_k, lnx_w, lnx_b,
           w_out, norm2_g, wg, bg, we, be, ew1, ew3, ew2, final_g):
    t, d = x2.shape
    aw = attn_out_g.shape[0]
    cw = w0.shape[0]
    heads = aw // ATT_HEAD_DIM
    dl, al, gl = w2.shape[0], a2.shape[0], g2.shape[0]
    pw, pa, pg = _round_up(dl, LANES), _round_up(al, LANES), _round_up(gl, LANES)
    wpad = 3 * cw + pw + pa + pg

    w_in = w_in.astype(BF16)
    sh = w_in[:, 3 * aw:]
    o = 3 * cw
    w_arr = jnp.concatenate(
        [sh[:, :o], _pad_cols(sh[:, o:o + dl], pw), _pad_cols(sh[:, o + dl:o + dl + al], pa),
         _pad_cols(sh[:, o + dl + al:o + dl + al + gl], pg), w_in[:, :3 * aw]], axis=1)
    mu_arr = jnp.concatenate(
        [mu[:o], jnp.pad(mu[o:o + dl], (0, pw - dl)), jnp.pad(mu[o + dl:o + dl + al], (0, pa - al)),
         jnp.pad(mu[o + dl + al:], (0, pg - gl))])[None, :]

    n_all = w_arr.shape[1]
    proj = _in_proj(x2, norm1_g[None, :], w_arr, tm=min(512, t), tn=_pick_tile(n_all, 4096))

    att = _moba(proj, b, s, heads, wpad // LANES)
    r1 = lambda a: a.reshape(1, -1)
    rw = _rwkv(proj, b, s, cw, wpad,
               (mu_arr, r1(w0), _pad_rows(w2, pw).astype(BF16), r1(a0), _pad_rows(a2, pa).astype(BF16),
                _pad_rows(g2, pg).astype(BF16), r1(k_k), r1(k_a), r1(r_k), r1(lnx_w), r1(lnx_b)))

    wr = _pad_cols(jnp.concatenate([wg, we], axis=1), LANES)
    wrh = wr.astype(BF16)
    wrm = (wr - wrh.astype(F32)).astype(BF16)
    br = jnp.pad(jnp.concatenate([bg, be]), (0, LANES - N_GROUPS - N_EXPERTS))[None, :]
    wo = w_out.astype(BF16)
    x1, hn, route, counts = _out_proj(att, rw, x2, r1(attn_out_g), wo[:aw], wo[aw:], r1(norm2_g), wrh, wrm, br,
                                      tm=min(512, t))

    plan_in = route[:, :6].astype(jnp.int32)
    counts = counts[0, N_GROUPS:N_GROUPS + N_EXPERTS].astype(jnp.int32)
    tg = COMBINE_ROWS
    n_tiles = t // tg
    pad_per_tile = N_EXPERTS * EXPERT_ROWS // n_tiles
    tile_e, tile_first, tile_next, n_used, dest, pad_rows = _dispatch_plan(
        plan_in[:, 0:TOP_K], plan_in[:, 4:4 + TOP_K], counts, t)

    sr = d // SLAB_LANES
    dest_tiles = dest.reshape(n_tiles, tg, TOP_K).transpose(0, 2, 1).reshape(n_tiles, TOP_K * tg) * sr
    disp_idx = jnp.concatenate([dest_tiles, pad_rows.reshape(n_tiles, pad_per_tile) * sr], axis=1)
    xbuf = _dispatch(disp_idx, hn, tile_e.shape[0] * EXPERT_ROWS, sr)
    ybuf = _experts(tile_e, tile_first, tile_next, n_used, xbuf, ew1, ew3, ew2)
    return _combine(dest_tiles, ybuf, x1, route, r1(final_g))


def kernel(x, norm1_g, w_in, attn_out_g, rwkv_mu, rwkv_w0, rwkv_w2, rwkv_a0, rwkv_a2, rwkv_g2, rwkv_k_k,
           rwkv_k_a, rwkv_r_k, rwkv_lnx_w, rwkv_lnx_b, w_out, norm2_g, router_group_w, router_group_b,
           router_expert_w, router_expert_b, expert_w1, expert_w3, expert_w2, final_g):
    b, s, d = x.shape
    assert norm1_g.shape[0] == 1, "single-layer block"
    assert s % MOBA_BLOCK == 0 and s % RWKV_CHUNK == 0
    assert (b * s) % max(EXPERT_ROWS, COMBINE_ROWS) == 0
    assert (N_EXPERTS * EXPERT_ROWS) % ((b * s) // COMBINE_ROWS) == 0, "zero-fill rows split evenly over tiles"
    out = _layer(x.reshape(b * s, d), b, s, norm1_g[0], w_in[0], attn_out_g[0], rwkv_mu[0], rwkv_w0[0],
                 rwkv_w2[0], rwkv_a0[0], rwkv_a2[0], rwkv_g2[0], rwkv_k_k[0], rwkv_k_a[0], rwkv_r_k[0],
                 rwkv_lnx_w[0], rwkv_lnx_b[0], w_out[0], norm2_g[0], router_group_w[0], router_group_b[0],
                 router_expert_w[0], router_expert_b[0], expert_w1[0], expert_w3[0], expert_w2[0], final_g)
    return out.reshape(b, s, d)
```

```python
import functools
import math

import jax
import jax.numpy as jnp
from jax import lax
from jax.experimental import pallas as pl
from jax.experimental.pallas import tpu as pltpu

F32 = jnp.float32
BF16 = jnp.bfloat16

LANES = 128
MXU_WIDTH = 256
SLAB_LANES = 128
ATT_HEAD_DIM = 128
RWKV_HEAD_DIM = 64
MOBA_BLOCK = 256
MOBA_TOPK = 3
N_GROUPS = 4
EXPERTS_PER_GROUP = 8
N_EXPERTS = N_GROUPS * EXPERTS_PER_GROUP
TOP_K = 2
RMS_EPS = 1e-6
GN_EPS = 64e-5
NEG_BIG = -1e30
RWKV_CHUNK = 64
RWKV_CHUNKS_PER_STEP = 4
RWKV_CHUNK_LAG = 8
EXPERT_ROWS = 256
COMBINE_ROWS = 256
VMEM_LIMIT = 56 * 1024 * 1024
EXPERT_VMEM_LIMIT = 60 * 1024 * 1024
WEIGHT_CAST_CHUNKS = 16


def _round_up(n, m):
    return (n + m - 1) // m * m


def _dot(a, b):
    return jnp.dot(a.astype(BF16), b.astype(BF16), preferred_element_type=F32)


def _dot_nt(a, b):
    return lax.dot_general(a.astype(BF16), b.astype(BF16), (((1,), (1,)), ((), ())),
                           preferred_element_type=F32)


def _split3(a):
    hi = a.astype(BF16)
    r1 = a - hi.astype(F32)
    mid = r1.astype(BF16)
    lo = (r1 - mid.astype(F32)).astype(BF16)
    return hi, mid, lo


def _dot_hi(a, b):
    ah, am, _ = _split3(a)
    bh, bm, _ = _split3(b)
    d = functools.partial(jnp.dot, preferred_element_type=F32)
    return d(ah, bh) + (d(am, bh) + d(ah, bm))


def _sigmoid(x):
    return 1.0 / (1.0 + jnp.exp(-x))


def _rms(x, g):
    ms = jnp.mean(x * x, axis=-1, keepdims=True)
    return x * lax.rsqrt(ms + RMS_EPS) * g


def _in_proj_kernel(x_ref, g_ref, w_ref, o_ref):
    xn = _rms(x_ref[...], g_ref[...]).astype(BF16)
    o_ref[...] = jnp.dot(xn, w_ref[...], preferred_element_type=F32)


def _in_proj(x2, g, w, tm, tn):
    t, d = x2.shape
    n = w.shape[1]
    return pl.pallas_call(
        _in_proj_kernel,
        out_shape=jax.ShapeDtypeStruct((t, n), F32),
        grid=(n // tn, t // tm),
        in_specs=[pl.BlockSpec((tm, d), lambda j, i: (i, 0)),
                  pl.BlockSpec((1, d), lambda j, i: (0, 0)),
                  pl.BlockSpec((d, tn), lambda j, i: (0, j))],
        out_specs=pl.BlockSpec((tm, tn), lambda j, i: (i, j)),
        compiler_params=pltpu.CompilerParams(
            dimension_semantics=("arbitrary", "arbitrary"), vmem_limit_bytes=VMEM_LIMIT),
        name="in_proj",
    )(x2, g, w)


def _moba_kernel(q_ref, k_ref, v_ref, o_ref, *, nblk, scale):
    bs = MOBA_BLOCK
    k = k_ref[...]
    kb = k.astype(BF16)
    vtb = v_ref[...].T.astype(BF16)
    qt = q_ref[...].T
    qtb = qt.astype(BF16)
    kmean = jnp.concatenate(
        [jnp.sum(k[n * bs:(n + 1) * bs, :], axis=0, keepdims=True) for n in range(nblk)], axis=0) * (1.0 / bs)
    gate_all = _dot_hi(kmean, qt)
    kpos = lax.broadcasted_iota(jnp.int32, (bs, bs), 0)
    qpos = lax.broadcasted_iota(jnp.int32, (bs, bs), 1)
    causal = kpos <= qpos

    def scores(qi):
        return jnp.dot(kb[0:(qi + 1) * bs, :], qtb[:, qi * bs:(qi + 1) * bs], preferred_element_type=F32)

    exp_scale = scale * math.log2(math.e)
    s_next = scores(0)
    for qi in range(nblk):
        qs = slice(qi * bs, (qi + 1) * bs)
        nk = (qi + 1) * bs
        s_all = s_next
        if qi + 1 < nblk:
            s_next = scores(qi + 1)
        blocks = []
        if qi > MOBA_TOPK:
            gate = gate_all[0:qi, qs]
            blk = lax.broadcasted_iota(jnp.int32, (qi, bs), 0)
        for n in range(qi):
            sn = s_all[n * bs:(n + 1) * bs, :]
            if qi > MOBA_TOPK:
                gn = gate[n:n + 1, :]
                beats = (gate > gn) | ((gate == gn) & (blk < n))
                rank = jnp.sum(jnp.where(beats, 1.0, 0.0), axis=0, keepdims=True)
                sn = jnp.where(rank < float(MOBA_TOPK), sn, NEG_BIG)
            blocks.append(sn)
        blocks.append(jnp.where(causal, s_all[qi * bs:nk, :], NEG_BIG))
        m = functools.reduce(jnp.maximum, [jnp.max(x, axis=0, keepdims=True) for x in blocks])
        ps = [jnp.exp2((x - m) * exp_scale) for x in blocks]
        l = functools.reduce(jnp.add, [jnp.sum(x, axis=0, keepdims=True) for x in ps])
        pb = jnp.concatenate([x.astype(BF16) for x in ps], axis=0)
        acc = jnp.dot(vtb[:, 0:nk], pb, preferred_element_type=F32)
        o_ref[qs, :] = (acc / l).T


def _moba(proj, b, s, heads, col0):
    d = ATT_HEAD_DIM
    nblk = s // MOBA_BLOCK
    kern = functools.partial(_moba_kernel, nblk=nblk, scale=1.0 / math.sqrt(d))
    return pl.pallas_call(
        kern,
        out_shape=jax.ShapeDtypeStruct((b * s, heads * d), F32),
        grid=(b, heads),
        in_specs=[pl.BlockSpec((s, d), lambda bi, h: (bi, col0 + h)),
                  pl.BlockSpec((s, d), lambda bi, h: (bi, col0 + heads + h)),
                  pl.BlockSpec((s, d), lambda bi, h: (bi, col0 + 2 * heads + h))],
        out_specs=pl.BlockSpec((s, d), lambda bi, h: (bi, h)),
        compiler_params=pltpu.CompilerParams(
            dimension_semantics=("arbitrary", "arbitrary"), vmem_limit_bytes=VMEM_LIMIT),
        name="moba",
    )(proj, proj, proj)


def _head_sum(x, lo):
    s0 = jnp.sum(jnp.where(lo, x, 0.0), axis=-1, keepdims=True)
    s1 = jnp.sum(jnp.where(lo, 0.0, x), axis=-1, keepdims=True)
    return jnp.where(lo, s0, s1)


def _stack_heads(x, lo):
    return jnp.concatenate([jnp.where(lo, x, 0.0), jnp.where(lo, 0.0, x)], axis=0)


def _rwkv_kernel(p_ref, mu_ref, w0_ref, w2_ref, a0_ref, a2_ref, g2_ref, kk_ref, ka_ref, rk_ref,
                 lnw_ref, lnb_ref, o_ref, carry_ref, state_ref, *, n_sub, **dims):
    @pl.when(pl.program_id(1) == 0)
    def _():
        carry_ref[...] = jnp.zeros_like(carry_ref)
        state_ref[...] = jnp.zeros_like(state_ref)

    C = RWKV_CHUNK
    chunks = [_rwkv_chunk(p_ref.at[pl.ds(h * C, C)], mu_ref, w0_ref, w2_ref, a0_ref, a2_ref, g2_ref, kk_ref,
                          ka_ref, rk_ref, lnw_ref, lnb_ref, o_ref.at[pl.ds(h * C, C)], carry_ref, state_ref,
                          **dims) for h in range(n_sub)]
    _interleave(chunks, RWKV_CHUNK_LAG)


def _interleave(chunks, lag):
    n = len(chunks)
    steps, done, wrote, waiting = [0] * n, [False] * n, [False] * n, [False] * n

    def advance(j):
        if waiting[j] and j > 0:
            while not (wrote[j - 1] or done[j - 1]):
                advance(j - 1)
        waiting[j] = False
        try:
            tag = next(chunks[j])
        except StopIteration:
            done[j] = True
            return
        steps[j] += 1
        waiting[j] = tag == "need_state"
        wrote[j] = wrote[j] or tag == "state_done"

    while not all(done):
        for j in range(n):
            if not done[j] and (j == 0 or done[j - 1] or steps[j - 1] >= lag):
                advance(j)


def _rwkv_chunk(p_ref, mu_ref, w0_ref, w2_ref, a0_ref, a2_ref, g2_ref, kk_ref, ka_ref, rk_ref,
                lnw_ref, lnb_ref, o_ref, carry_ref, state_ref, *, cw, pw, pa, pg):
    C = RWKV_CHUNK
    C2 = 2 * C
    pr = p_ref[...]
    row = lax.broadcasted_iota(jnp.int32, pr.shape, 0)
    prev = jnp.where(row == 0, carry_ref[...], pltpu.roll(pr, 1, 0))
    carry_ref[...] = pr[C - 1:C, :]
    ps = pr + (prev - pr) * mu_ref[...]
    yield "step"

    r = ps[:, 0:cw]
    k = ps[:, cw:2 * cw]
    v = ps[:, 2 * cw:3 * cw]
    o = 3 * cw
    xw = ps[:, o:o + pw]
    xa = ps[:, o + pw:o + pw + pa]
    xg = ps[:, o + pw + pa:o + pw + pa + pg]

    z = -(w0_ref[...] + _dot(jnp.tanh(xw), w2_ref[...]))
    softplus = jnp.maximum(z, 0.0) + jnp.log(1.0 + jnp.exp(-jnp.abs(z)))
    logw = -jnp.exp(-softplus - 0.5)
    asig = _sigmoid(a0_ref[...] + _dot(xa, a2_ref[...]))
    gate = _dot(_sigmoid(xg), g2_ref[...])
    kkr = k * kk_ref[...]
    kmod = k * (1.0 + (asig - 1.0) * ka_ref[...])
    yield "step"

    ti = lax.broadcasted_iota(jnp.int32, (C, C), 0)
    si = lax.broadcasted_iota(jnp.int32, (C, C), 1)
    tril = jnp.where(si <= ti, 1.0, 0.0).astype(BF16)
    lh, lm, ll = _split3(logw)
    d32 = functools.partial(jnp.dot, preferred_element_type=F32)
    cum = d32(tril, lh) + (d32(tril, lm) + d32(tril, ll))
    cum_end = cum[C - 1:C, :]
    e_prev = jnp.exp(cum - logw)
    e_incl = jnp.exp(cum)
    e_inv = jnp.exp(-cum)
    e_end = jnp.exp(cum_end - cum)
    g_end = jnp.exp(cum_end)
    yield "step"

    lane = lax.broadcasted_iota(jnp.int32, (C, LANES), 1)
    lo = lane < RWKV_HEAD_DIM
    tok = lax.broadcasted_iota(jnp.int32, (C, LANES), 0)
    src = jnp.bitwise_and(lane, RWKV_HEAD_DIM - 1)
    strict = src < tok
    incl = src <= tok
    eye = jnp.where(src == tok, 1.0, 0.0)
    rho = lax.broadcasted_iota(jnp.int32, (C2, C2), 0)
    sig = lax.broadcasted_iota(jnp.int32, (C2, C2), 1)
    same_head = (rho >= C) == (sig >= C)

    pairs = range(cw // LANES)
    sls = [slice(p * LANES, (p + 1) * LANES) for p in pairs]
    incl2 = jnp.concatenate([incl, incl], axis=1)
    strict2 = jnp.concatenate([strict, strict], axis=1)

    b2, r2, k2, v2, at, rt, v2s, gm = [], [], [], [], [], [], [], []
    for sl in sls:
        kk2 = kkr[:, sl]
        nrm = jnp.sqrt(_head_sum(kk2 * kk2, lo))
        kkn = kk2 / jnp.maximum(nrm, 1e-12)
        b2.append(kkn * asig[:, sl])
        r2.append(r[:, sl])
        k2.append(kmod[:, sl])
        v2.append(v[:, sl])
        at.append(-kkn * e_prev[:, sl])
        rt.append(r2[-1] * e_incl[:, sl])
        bt = b2[-1] * e_inv[:, sl]
        kt = k2[-1] * e_inv[:, sl]
        v2s.append(_stack_heads(v2[-1], lo))
        rhs = jnp.concatenate([_stack_heads(bt, lo), _stack_heads(kt, lo)], axis=0)
        gm.append(_dot_nt(jnp.concatenate([at[-1], rt[-1]], axis=0), rhs))
        yield "step"

    labk = [jnp.where(strict2, g[0:C, :], 0.0) for g in gm]
    mrbk = [jnp.where(incl2, g[C:C2, :], 0.0) for g in gm]
    lakv = [_dot(labk[p][:, C2:2 * C2], v2s[p]) for p in pairs]
    yield "step"

    lab = [m[:, 0:C2] for m in labk]
    tinv = [eye + m for m in lab]
    pw2 = [_dot(m, _stack_heads(m, lo)) for m in lab]
    yield "step"
    n_pow = int(math.log2(C))
    for k in range(1, n_pow):
        if k < n_pow - 1:
            prod = [_dot(jnp.concatenate([t, m], axis=0), _stack_heads(m, lo)) for t, m in zip(tinv, pw2)]
            tinv = [t + pr[0:C] for t, pr in zip(tinv, prod)]
            pw2 = [pr[C:C2] for pr in prod]
        else:
            tinv = [t + _dot(t, _stack_heads(m, lo)) for t, m in zip(tinv, pw2)]
        yield "step"

    yield "need_state"
    s0 = [state_ref[p] for p in pairs]
    ars = [_dot_nt(jnp.concatenate([at[p], rt[p]], axis=0), s0[p]) for p in pairs]
    yield "step"
    u2 = [_dot(t, _stack_heads(ars[p][0:C] + lakv[p], lo)) for p, t in zip(pairs, tinv)]
    yield "step"
    y2 = [ars[p][C:C2] + _dot(mrbk[p], jnp.concatenate([_stack_heads(u2[p], lo), v2s[p]], axis=0))
          for p in pairs]
    yield "step"

    for p, sl in zip(pairs, sls):
        uv = jnp.concatenate([u2[p], v2[p]], axis=0)
        bk = jnp.concatenate([b2[p] * e_end[:, sl], k2[p] * e_end[:, sl]], axis=0)
        upd = _dot(uv.T, bk)
        state_ref[p] = s0[p] * g_end[:, sl] + jnp.where(same_head, upd, 0.0)
    yield "state_done"

    for p, sl in zip(pairs, sls):
        mean = _head_sum(y2[p], lo) * (1.0 / RWKV_HEAD_DIM)
        yc = y2[p] - mean
        var = _head_sum(yc * yc, lo) * (1.0 / RWKV_HEAD_DIM)
        yn = yc * lax.rsqrt(var + GN_EPS) * lnw_ref[:, sl] + lnb_ref[:, sl]
        bonus = _head_sum(r2[p] * k2[p] * rk_ref[:, sl], lo) * v2[p]
        o_ref[:, sl] = (yn + bonus) * gate[:, sl]
        yield "step"


def _rwkv(proj, b, s, cw, wpad, params):
    (mu, w0, w2, a0, a2, g2, kk, ka, rk, lnw, lnb) = params
    n_sub = RWKV_CHUNKS_PER_STEP if s % (RWKV_CHUNKS_PER_STEP * RWKV_CHUNK) == 0 else 1
    C = n_sub * RWKV_CHUNK
    nch = s // C
    pw, pa, pg = w2.shape[0], a2.shape[0], g2.shape[0]
    kern = functools.partial(_rwkv_kernel, n_sub=n_sub, cw=cw, pw=pw, pa=pa, pg=pg)
    row = lambda n: pl.BlockSpec((1, n), lambda bi, c: (0, 0))
    full = lambda a: pl.BlockSpec(a.shape, lambda bi, c: (0, 0))
    return pl.pallas_call(
        kern,
        out_shape=jax.ShapeDtypeStruct((b * s, cw), F32),
        grid=(b, nch),
        in_specs=[pl.BlockSpec((C, wpad), lambda bi, c: (bi * nch + c, 0)),
                  row(wpad), row(cw), full(w2), row(cw), full(a2), full(g2),
                  row(cw), row(cw), row(cw), row(cw), row(cw)],
        out_specs=pl.BlockSpec((C, cw), lambda bi, c: (bi * nch + c, 0)),
        scratch_shapes=[pltpu.VMEM((1, wpad), F32),
                        pltpu.VMEM((cw // LANES, LANES, LANES), F32)],
        compiler_params=pltpu.CompilerParams(
            dimension_semantics=("arbitrary", "arbitrary"), vmem_limit_bytes=VMEM_LIMIT),
        name="rwkv",
    )(proj, mu, w0, w2, a0, a2, g2, kk, ka, rk, lnw, lnb)


def _out_proj_kernel(att_ref, rw_ref, x_ref, ga_ref, woa_ref, wor_ref, g2_ref, wrh_ref, wrm_ref, br_ref,
                     x1_ref, hn_ref, route_ref, counts_ref):
    att = _rms(att_ref[...], ga_ref[...])
    y = (jnp.dot(att.astype(BF16), woa_ref[...], preferred_element_type=F32)
         + jnp.dot(rw_ref[...].astype(BF16), wor_ref[...], preferred_element_type=F32))
    x1 = x_ref[...] + y
    x1_ref[...] = x1
    hn = _rms(x1, g2_ref[...])
    _store_slabs(hn_ref, hn)

    hh, hm, _ = _split3(hn)
    d32 = functools.partial(jnp.dot, preferred_element_type=F32)
    logits = d32(hh, wrh_ref[...]) + (d32(hm, wrh_ref[...]) + d32(hh, wrm_ref[...])) + br_ref[...]
    lane = lax.broadcasted_iota(jnp.int32, logits.shape, 1)
    lanef = lane.astype(F32)
    big = float(LANES)

    is_g = lane < N_GROUPS
    gmax = jnp.max(jnp.where(is_g, logits, -jnp.inf), axis=-1, keepdims=True)
    gidx = jnp.min(jnp.where(is_g & (logits == gmax), lanef, big), axis=-1, keepdims=True)
    gsum = jnp.sum(jnp.where(is_g, jnp.exp(logits - gmax), 0.0), axis=-1, keepdims=True)
    g_gate = 1.0 / gsum

    e_lo = N_GROUPS + EXPERTS_PER_GROUP * gidx
    in_grp = (lanef >= e_lo) & (lanef < e_lo + EXPERTS_PER_GROUP)
    e1 = jnp.max(jnp.where(in_grp, logits, -jnp.inf), axis=-1, keepdims=True)
    i1 = jnp.min(jnp.where(in_grp & (logits == e1), lanef, big), axis=-1, keepdims=True)
    rest = in_grp & (lanef != i1)
    e2 = jnp.max(jnp.where(rest, logits, -jnp.inf), axis=-1, keepdims=True)
    i2 = jnp.min(jnp.where(rest & (logits == e2), lanef, big), axis=-1, keepdims=True)
    t = jnp.exp(e2 - e1)
    w1 = g_gate * (1.0 / (1.0 + t))
    w2 = g_gate * (t / (1.0 + t))

    @pl.when(pl.program_id(0) == 0)
    def _():
        counts_ref[...] = jnp.zeros_like(counts_ref)

    tm = logits.shape[0]
    pick1 = lanef == i1
    pick2 = lanef == i2
    onehot = jnp.where(pick1 | pick2, 1.0, 0.0)
    ti = lax.broadcasted_iota(jnp.int32, (tm, tm), 0)
    si = lax.broadcasted_iota(jnp.int32, (tm, tm), 1)
    before = _dot(jnp.where(si < ti, 1.0, 0.0), onehot) + counts_ref[...]
    r1 = jnp.sum(jnp.where(pick1, before, 0.0), axis=-1, keepdims=True)
    r2 = jnp.sum(jnp.where(pick2, before, 0.0), axis=-1, keepdims=True)
    counts_ref[...] = counts_ref[...] + jnp.sum(onehot, axis=0, keepdims=True)

    route_ref[...] = jnp.where(lane == 0, i1 - N_GROUPS,
                     jnp.where(lane == 1, i2 - N_GROUPS,
                     jnp.where(lane == 2, w1,
                     jnp.where(lane == 3, w2,
                     jnp.where(lane == 4, r1,
                     jnp.where(lane == 5, r2, 0.0))))))


def _out_proj(att, rw, x2, ga, woa, wor, g2, wrh, wrm, br, tm):
    t, d = x2.shape
    wa = att.shape[1]
    wr = rw.shape[1]
    sr = d // SLAB_LANES
    c = lambda shape: pl.BlockSpec(shape, lambda i: (0, 0))
    return pl.pallas_call(
        _out_proj_kernel,
        out_shape=(jax.ShapeDtypeStruct((t, d), F32), jax.ShapeDtypeStruct((t * sr, SLAB_LANES), F32),
                   jax.ShapeDtypeStruct((t, LANES), F32), jax.ShapeDtypeStruct((1, LANES), F32)),
        grid=(t // tm,),
        in_specs=[pl.BlockSpec((tm, wa), lambda i: (i, 0)),
                  pl.BlockSpec((tm, wr), lambda i: (i, 0)),
                  pl.BlockSpec((tm, d), lambda i: (i, 0)),
                  c((1, wa)), c((wa, d)), c((wr, d)), c((1, d)), c((d, LANES)), c((d, LANES)), c((1, LANES))],
        out_specs=(pl.BlockSpec((tm, d), lambda i: (i, 0)), pl.BlockSpec((tm * sr, SLAB_LANES), lambda i: (i, 0)),
                   pl.BlockSpec((tm, LANES), lambda i: (i, 0)), c((1, LANES))),
        compiler_params=pltpu.CompilerParams(
            dimension_semantics=("arbitrary",), vmem_limit_bytes=VMEM_LIMIT),
        name="out_proj",
    )(att, rw, x2, ga, woa, wor, g2, wrh, wrm, br)


def _store_slabs(ref, x):
    rows, d = x.shape
    sr = d // SLAB_LANES
    for s in range(sr):
        ref[pl.ds(s, rows, stride=sr), :] = x[:, s * SLAB_LANES:(s + 1) * SLAB_LANES]


def _load_slabs(ref, slot, first, rows, sr):
    return jnp.concatenate([ref[slot, pl.ds(first * sr + s, rows, stride=sr), :] for s in range(sr)], axis=1)


def _gather_rows(i, nt, idx_hbm, src_hbm, idx_ref, buf_ref, isem, gsem, rows, sr):
    def idx_copy(tile, slot):
        return pltpu.make_async_copy(idx_hbm.at[pl.ds(tile, 1)], idx_ref.at[pl.ds(slot, 1)], isem.at[slot])

    def start_rows(slot):
        for r in range(rows):
            src = pl.multiple_of(idx_ref[slot, r], sr)
            pltpu.make_async_copy(src_hbm.at[pl.ds(src, sr)], buf_ref.at[slot, pl.ds(r * sr, sr)],
                                  gsem.at[slot]).start()

    slot = jnp.bitwise_and(i, 1)

    @pl.when(i == 0)
    def _():
        idx_copy(0, 0).start()
        idx_copy(0, 0).wait()
        start_rows(0)
        if nt > 1:
            idx_copy(1, 1).start()

    @pl.when(i + 1 < nt)
    def _():
        idx_copy(i + 1, 1 - slot).wait()
        start_rows(1 - slot)

    @pl.when(i + 2 < nt)
    def _():
        idx_copy(i + 2, slot).start()

    pltpu.make_async_copy(src_hbm.at[pl.ds(0, rows * sr)], buf_ref.at[slot], gsem.at[slot]).wait()
    return slot


def _dispatch_kernel(idx_hbm, x_hbm, o_hbm, idx_ref, tile_ref, zero_ref, isem, lsem, csem,
                     *, nt, n_tok, n_pad, sr):
    i = pl.program_id(0)
    slot = jnp.bitwise_and(i, 1)
    other = 1 - slot
    tg = n_tok // TOP_K
    n_all = n_tok + n_pad
    stage = lax.rem(i, 3)

    def idx_copy(tile, s):
        return pltpu.make_async_copy(idx_hbm.at[pl.ds(tile, 1)], idx_ref.at[pl.ds(s, 1)], isem.at[s])

    def load(tile, s):
        return pltpu.make_async_copy(x_hbm.at[pl.ds(tile * tg * sr, tg * sr)], tile_ref.at[s], lsem.at[s])

    def copies_done(s):
        return pltpu.make_async_copy(x_hbm.at[pl.ds(0, n_all * sr)], o_hbm.at[pl.ds(0, n_all * sr)], csem.at[s])

    @pl.when(i == 0)
    def _():
        zero_ref[...] = jnp.zeros_like(zero_ref)
        idx_copy(0, 0).start()
        load(0, 0).start()

    idx_copy(i, slot).wait()
    load(i, stage).wait()

    @pl.when(i + 1 < nt)
    def _():
        idx_copy(i + 1, other).start()
        load(i + 1, lax.rem(i + 1, 3)).start()

    for r in range(n_tok):
        dst = pl.multiple_of(idx_ref[slot, r], sr)
        pltpu.make_async_copy(tile_ref.at[stage, pl.ds((r % tg) * sr, sr)], o_hbm.at[pl.ds(dst, sr)],
                              csem.at[slot]).start()
    for r in range(n_tok, n_all):
        dst = pl.multiple_of(idx_ref[slot, r], sr)
        pltpu.make_async_copy(zero_ref, o_hbm.at[pl.ds(dst, sr)], csem.at[slot]).start()

    @pl.when(i > 0)
    def _():
        copies_done(other).wait()

    @pl.when(i == nt - 1)
    def _():
        copies_done(slot).wait()


def _dispatch(idx, hn_slabs, n_rows, sr):
    nt, n_all = idx.shape
    n_tok = TOP_K * COMBINE_ROWS
    hbm = pl.BlockSpec(memory_space=pl.ANY)
    return pl.pallas_call(
        functools.partial(_dispatch_kernel, nt=nt, n_tok=n_tok, n_pad=n_all - n_tok, sr=sr),
        out_shape=jax.ShapeDtypeStruct((n_rows * sr, SLAB_LANES), F32),
        grid=(nt,),
        in_specs=[hbm, hbm],
        out_specs=hbm,
        scratch_shapes=[pltpu.SMEM((2, n_all), jnp.int32),
                        pltpu.VMEM((3, COMBINE_ROWS * sr, SLAB_LANES), F32), pltpu.VMEM((sr, SLAB_LANES), F32),
                        pltpu.SemaphoreType.DMA((2,)), pltpu.SemaphoreType.DMA((3,)),
                        pltpu.SemaphoreType.DMA((2,))],
        compiler_params=pltpu.CompilerParams(dimension_semantics=("arbitrary",)),
        name="dispatch",
    )(idx, hn_slabs)


def _expert_kernel(te_ref, first_ref, next_ref, nu_ref, x_ref, w1_hbm, w3_hbm, w2_hbm, o_ref,
                   stage_refs, wb_refs, wsem, *, sr):
    i = pl.program_id(0)
    tm = EXPERT_ROWS
    w_hbm = (w1_hbm, w3_hbm, w2_hbm)
    used = i < nu_ref[0]

    def weight_copies(e):
        out = []
        for j, (w, st) in enumerate(zip(w_hbm, stage_refs)):
            half = st.shape[0] // 2
            for q in range(2):
                rows = pl.ds(q * half, half)
                out.append((pltpu.make_async_copy(w.at[e, rows], st.at[rows], wsem.at[2 * j + q]), q))
        return out

    @pl.when(i == 0)
    def _():
        for cp, q in weight_copies(te_ref[0]):
            cp.start(priority=q)

    @pl.when(used & (first_ref[i] == 1))
    def _():
        for cp, _ in weight_copies(te_ref[i]):
            cp.wait()
        for st, wb in zip(stage_refs, wb_refs):
            rows = st.shape[0] // WEIGHT_CAST_CHUNKS

            def cast_chunk(c, carry, st=st, wb=wb, rows=rows):
                r0 = pl.multiple_of(c * rows, rows)
                wb[pl.ds(r0, rows), :] = st[pl.ds(r0, rows), :].astype(BF16)
                return carry

            lax.fori_loop(0, WEIGHT_CAST_CHUNKS, cast_chunk, 0)

        @pl.when(next_ref[i] >= 0)
        def _():
            for cp, q in weight_copies(next_ref[i]):
                cp.start(priority=q)

    @pl.when(used)
    def _():
        xb = jnp.concatenate([x_ref[pl.ds(s, tm, stride=sr), :] for s in range(sr)], axis=1).astype(BF16)
        h1 = jnp.dot(xb, wb_refs[0][...], preferred_element_type=F32)
        h3 = jnp.dot(xb, wb_refs[1][...], preferred_element_type=F32)
        hid = (h1 * _sigmoid(h1)) * h3
        _store_slabs(o_ref, jnp.dot(hid.astype(BF16), wb_refs[2][...], preferred_element_type=F32))

    @pl.when(jnp.logical_not(used))
    def _():
        o_ref[...] = jnp.zeros_like(o_ref)


def _experts(tile_e, tile_first, tile_next, n_used, x_slabs, w1, w3, w2):
    nt = tile_e.shape[0]
    d, de = w1.shape[1], w1.shape[2]
    sr = d // SLAB_LANES
    tm = EXPERT_ROWS
    hbm = pl.BlockSpec(memory_space=pl.ANY)
    grid_spec = pltpu.PrefetchScalarGridSpec(
        num_scalar_prefetch=4,
        grid=(nt,),
        in_specs=[pl.BlockSpec((tm * sr, SLAB_LANES), lambda i, te, fi, nx, nu: (jnp.minimum(i, nu[0] - 1), 0)),
                  hbm, hbm, hbm],
        out_specs=pl.BlockSpec((tm * sr, SLAB_LANES), lambda i, *_: (i, 0)),
        scratch_shapes=[[pltpu.VMEM((d, de), F32), pltpu.VMEM((d, de), F32), pltpu.VMEM((de, d), F32)],
                        [pltpu.VMEM((d, de), BF16), pltpu.VMEM((d, de), BF16), pltpu.VMEM((de, d), BF16)],
                        pltpu.SemaphoreType.DMA((6,))],
    )
    return pl.pallas_call(
        functools.partial(_expert_kernel, sr=sr),
        out_shape=jax.ShapeDtypeStruct((nt * tm * sr, SLAB_LANES), F32),
        grid_spec=grid_spec,
        compiler_params=pltpu.CompilerParams(
            dimension_semantics=("arbitrary",), vmem_limit_bytes=EXPERT_VMEM_LIMIT),
        name="experts",
    )(tile_e, tile_first, tile_next, n_used, x_slabs, w1, w3, w2)


def _combine_kernel(dest_hbm, y_hbm, x1_ref, route_ref, fg_ref, o_ref, idx_ref, buf_ref, isem, gsem,
                    *, nt, sr):
    i = pl.program_id(0)
    tg = COMBINE_ROWS
    slot = _gather_rows(i, nt, dest_hbm, y_hbm, idx_ref, buf_ref, isem, gsem, TOP_K * tg, sr)
    route = route_ref[...]
    y = (_load_slabs(buf_ref, slot, 0, tg, sr) * route[:, 2:3]
         + _load_slabs(buf_ref, slot, tg, tg, sr) * route[:, 3:4])
    o_ref[...] = _rms(x1_ref[...] + y, fg_ref[...])


def _combine(dest, ybuf, x1, route, fg):
    t, d = x1.shape
    tg = COMBINE_ROWS
    sr = d // SLAB_LANES
    return pl.pallas_call(
        functools.partial(_combine_kernel, nt=t // tg, sr=sr),
        out_shape=jax.ShapeDtypeStruct((t, d), F32),
        grid=(t // tg,),
        in_specs=[pl.BlockSpec(memory_space=pl.ANY),
                  pl.BlockSpec(memory_space=pl.ANY),
                  pl.BlockSpec((tg, d), lambda i: (i, 0)),
                  pl.BlockSpec((tg, LANES), lambda i: (i, 0)),
                  pl.BlockSpec((1, d), lambda i: (0, 0))],
        out_specs=pl.BlockSpec((tg, d), lambda i: (i, 0)),
        scratch_shapes=[pltpu.SMEM((2, TOP_K * tg), jnp.int32),
                        pltpu.VMEM((2, TOP_K * tg * sr, SLAB_LANES), F32),
                        pltpu.SemaphoreType.DMA((2,)), pltpu.SemaphoreType.DMA((2,))],
        compiler_params=pltpu.CompilerParams(
            dimension_semantics=("arbitrary",), vmem_limit_bytes=VMEM_LIMIT),
        name="combine",
    )(dest, ybuf, x1, route, fg)


def _pad_cols(m, n):
    return jnp.pad(m, ((0, 0), (0, n - m.shape[1])))


def _pad_rows(m, n):
    return jnp.pad(m, ((0, n - m.shape[0]), (0, 0)))


def _pick_tile(n, cap):
    for unit in (MXU_WIDTH, LANES):
        fits = [c for c in range(unit, cap + 1, unit) if n % c == 0]
        if fits:
            return fits[-1]
    raise ValueError(f"no lane-aligned tile divides {n}")


def _dispatch_plan(experts, rank, counts, n_tok):
    tm = EXPERT_ROWS
    nt = n_tok * TOP_K // tm + N_EXPERTS
    tiles_e = (counts + tm - 1) // tm
    tile_end = jnp.cumsum(tiles_e)
    tile_start = tile_end - tiles_e
    dest = (tile_start[experts] * tm + rank).astype(jnp.int32)
    n_used = tile_end[-1:].astype(jnp.int32)
    tiles = jnp.arange(nt, dtype=jnp.int32)
    tile_e = jnp.minimum(jnp.sum((tile_end[None, :] <= tiles[:, None]).astype(jnp.int32), axis=1),
                         N_EXPERTS - 1).astype(jnp.int32)
    tile_first = ((tiles == 0) | (tile_e != jnp.roll(tile_e, 1))).astype(jnp.int32)
    next_start = tile_end[tile_e]
    tile_next = jnp.where(next_start < n_used[0], tile_e[jnp.minimum(next_start, nt - 1)], -1).astype(jnp.int32)

    seg_len = jnp.concatenate([tiles_e * tm - counts, (nt - n_used) * tm])
    seg_first = jnp.concatenate([tile_start * tm + counts, n_used * tm])
    seg_stop = jnp.cumsum(seg_len)
    j = jnp.arange(N_EXPERTS * tm, dtype=jnp.int32)
    seg = jnp.sum((seg_stop[None, :] <= j[:, None]).astype(jnp.int32), axis=1)
    pad_rows = (seg_first[seg] + j - (seg_stop - seg_len)[seg]).astype(jnp.int32)
    return tile_e, tile_first, tile_next, n_used, dest, pad_rows


def _layer(x2, b, s, norm1_g, w_in, attn_out_g, mu, w0, w2, a0, a2, g2, k_k, k_a, r_k, lnx_w, lnx_b,
           w_out, norm2_g, wg, bg, we, be, ew1, ew3, ew2, final_g):
    t, d = x2.shape
    aw = attn_out_g.shape[0]
    cw = w0.shape[0]
    heads = aw // ATT_HEAD_DIM
    dl, al, gl = w2.shape[0], a2.shape[0], g2.shape[0]
    pw, pa, pg = _round_up(dl, LANES), _round_up(al, LANES), _round_up(gl, LANES)
    wpad = 3 * cw + pw + pa + pg

    w_in = w_in.astype(BF16)
    sh = w_in[:, 3 * aw:]
    o = 3 * cw
    w_arr = jnp.concatenate(
        [sh[:, :o], _pad_cols(sh[:, o:o + dl], pw), _pad_cols(sh[:, o + dl:o + dl + al], pa),
         _pad_cols(sh[:, o + dl + al:o + dl + al + gl], pg), w_in[:, :3 * aw]], axis=1)
    mu_arr = jnp.concatenate(
        [mu[:o], jnp.pad(mu[o:o + dl], (0, pw - dl)), jnp.pad(mu[o + dl:o + dl + al], (0, pa - al)),
         jnp.pad(mu[o + dl + al:], (0, pg - gl))])[None, :]

    n_all = w_arr.shape[1]
    proj = _in_proj(x2, norm1_g[None, :], w_arr, tm=min(512, t), tn=_pick_tile(n_all, 4096))

    att = _moba(proj, b, s, heads, wpad // LANES)
    r1 = lambda a: a.reshape(1, -1)
    rw = _rwkv(proj, b, s, cw, wpad,
               (mu_arr, r1(w0), _pad_rows(w2, pw).astype(BF16), r1(a0), _pad_rows(a2, pa).astype(BF16),
                _pad_rows(g2, pg).astype(BF16), r1(k_k), r1(k_a), r1(r_k), r1(lnx_w), r1(lnx_b)))

    wr = _pad_cols(jnp.concatenate([wg, we], axis=1), LANES)
    wrh = wr.astype(BF16)
    wrm = (wr - wrh.astype(F32)).astype(BF16)
    br = jnp.pad(jnp.concatenate([bg, be]), (0, LANES - N_GROUPS - N_EXPERTS))[None, :]
    wo = w_out.astype(BF16)
    x1, hn, route, counts = _out_proj(att, rw, x2, r1(attn_out_g), wo[:aw], wo[aw:], r1(norm2_g), wrh, wrm, br,
                                      tm=min(512, t))

    plan_in = route[:, :6].astype(jnp.int32)
    counts = counts[0, N_GROUPS:N_GROUPS + N_EXPERTS].astype(jnp.int32)
    tg = COMBINE_ROWS
    n_tiles = t // tg
    pad_per_tile = N_EXPERTS * EXPERT_ROWS // n_tiles
    tile_e, tile_first, tile_next, n_used, dest, pad_rows = _dispatch_plan(
        plan_in[:, 0:TOP_K], plan_in[:, 4:4 + TOP_K], counts, t)

    sr = d // SLAB_LANES
    dest_tiles = dest.reshape(n_tiles, tg, TOP_K).transpose(0, 2, 1).reshape(n_tiles, TOP_K * tg) * sr
    disp_idx = jnp.concatenate([dest_tiles, pad_rows.reshape(n_tiles, pad_per_tile) * sr], axis=1)
    xbuf = _dispatch(disp_idx, hn, tile_e.shape[0] * EXPERT_ROWS, sr)
    ybuf = _experts(tile_e, tile_first, tile_next, n_used, xbuf, ew1, ew3, ew2)
    return _combine(dest_tiles, ybuf, x1, route, r1(final_g))


def kernel(x, norm1_g, w_in, attn_out_g, rwkv_mu, rwkv_w0, rwkv_w2, rwkv_a0, rwkv_a2, rwkv_g2, rwkv_k_k,
           rwkv_k_a, rwkv_r_k, rwkv_lnx_w, rwkv_lnx_b, w_out, norm2_g, router_group_w, router_group_b,
           router_expert_w, router_expert_b, expert_w1, expert_w3, expert_w2, final_g):
    b, s, d = x.shape
    assert norm1_g.shape[0] == 1, "single-layer block"
    assert s % MOBA_BLOCK == 0 and s % RWKV_CHUNK == 0
    assert (b * s) % max(EXPERT_ROWS, COMBINE_ROWS) == 0
    assert 2 * RWKV_CHUNK == LANES and RWKV_CHUNK == RWKV_HEAD_DIM, "rwkv packs two CxC head matrices per vreg row"
    assert (N_EXPERTS * EXPERT_ROWS) % ((b * s) // COMBINE_ROWS) == 0, "zero-fill rows split evenly over tiles"
    out = _layer(x.reshape(b * s, d), b, s, norm1_g[0], w_in[0], attn_out_g[0], rwkv_mu[0], rwkv_w0[0],
                 rwkv_w2[0], rwkv_a0[0], rwkv_a2[0], rwkv_g2[0], rwkv_k_k[0], rwkv_k_a[0], rwkv_r_k[0],
                 rwkv_lnx_w[0], rwkv_lnx_b[0], w_out[0], norm2_g[0], router_group_w[0], router_group_b[0],
                 router_expert_w[0], router_expert_b[0], expert_w1[0], expert_w3[0], expert_w2[0], final_g)
    return out.reshape(b, s, d)
```

```python
import functools
import math

import jax
import jax.numpy as jnp
from jax import lax
from jax.experimental import pallas as pl
from jax.experimental.pallas import tpu as pltpu

F32 = jnp.float32
BF16 = jnp.bfloat16

LANES = 128
MXU_WIDTH = 256
BF16_SUBLANES = 16
SLAB_LANES = 128
ATT_HEAD_DIM = 128
RWKV_HEAD_DIM = 64
MOBA_BLOCK = 256
MOBA_TOPK = 3
N_GROUPS = 4
EXPERTS_PER_GROUP = 8
N_EXPERTS = N_GROUPS * EXPERTS_PER_GROUP
TOP_K = 2
RMS_EPS = 1e-6
GN_EPS = 64e-5
NEG_BIG = -1e30
RWKV_CHUNK = 64
RWKV_CHUNKS_PER_STEP = 4
RWKV_CHUNK_LAG = 8
EXPERT_ROWS = 256
COMBINE_ROWS = 256
VMEM_LIMIT = 56 * 1024 * 1024
EXPERT_VMEM_LIMIT = 60 * 1024 * 1024
WEIGHT_CAST_CHUNKS = 16


def _round_up(n, m):
    return (n + m - 1) // m * m


def _dot(a, b):
    return jnp.dot(a.astype(BF16), b.astype(BF16), preferred_element_type=F32)


def _dot_nt(a, b):
    return lax.dot_general(a.astype(BF16), b.astype(BF16), (((1,), (1,)), ((), ())),
                           preferred_element_type=F32)


def _split3(a):
    hi = a.astype(BF16)
    r1 = a - hi.astype(F32)
    mid = r1.astype(BF16)
    lo = (r1 - mid.astype(F32)).astype(BF16)
    return hi, mid, lo


def _dot_hi(a, b):
    ah, am, _ = _split3(a)
    bh, bm, _ = _split3(b)
    d = functools.partial(jnp.dot, preferred_element_type=F32)
    return d(ah, bh) + (d(am, bh) + d(ah, bm))


def _sigmoid(x):
    return 1.0 / (1.0 + jnp.exp(-x))


def _rms(x, g):
    ms = jnp.mean(x * x, axis=-1, keepdims=True)
    return x * lax.rsqrt(ms + RMS_EPS) * g


def _in_proj_kernel(x_ref, g_ref, w_ref, o_ref):
    xn = _rms(x_ref[...], g_ref[...]).astype(BF16)
    o_ref[...] = jnp.dot(xn, w_ref[...], preferred_element_type=F32)


def _in_proj(x2, g, w, tm, tn):
    t, d = x2.shape
    n = w.shape[1]
    return pl.pallas_call(
        _in_proj_kernel,
        out_shape=jax.ShapeDtypeStruct((t, n), F32),
        grid=(n // tn, t // tm),
        in_specs=[pl.BlockSpec((tm, d), lambda j, i: (i, 0)),
                  pl.BlockSpec((1, d), lambda j, i: (0, 0)),
                  pl.BlockSpec((d, tn), lambda j, i: (0, j))],
        out_specs=pl.BlockSpec((tm, tn), lambda j, i: (i, j)),
        compiler_params=pltpu.CompilerParams(
            dimension_semantics=("arbitrary", "arbitrary"), vmem_limit_bytes=VMEM_LIMIT),
        name="in_proj",
    )(x2, g, w)


def _moba_kernel(q_ref, k_ref, v_ref, o_ref, *, nblk, scale):
    bs = MOBA_BLOCK
    k = k_ref[...]
    kb = k.astype(BF16)
    d = v_ref.shape[1]
    vtb = jnp.concatenate([v_ref[...].T, jnp.ones((BF16_SUBLANES, v_ref.shape[0]), F32)], axis=0).astype(BF16)
    qt = q_ref[...].T
    qtb = qt.astype(BF16)
    kmean = jnp.concatenate(
        [jnp.sum(k[n * bs:(n + 1) * bs, :], axis=0, keepdims=True) for n in range(nblk)], axis=0) * (1.0 / bs)
    gate_all = _dot_hi(kmean, qt)
    kpos = lax.broadcasted_iota(jnp.int32, (bs, bs), 0)
    qpos = lax.broadcasted_iota(jnp.int32, (bs, bs), 1)
    causal = kpos <= qpos

    def scores(qi):
        return jnp.dot(kb[0:(qi + 1) * bs, :], qtb[:, qi * bs:(qi + 1) * bs], preferred_element_type=F32)

    exp_scale = scale * math.log2(math.e)
    s_next = scores(0)
    for qi in range(nblk):
        qs = slice(qi * bs, (qi + 1) * bs)
        nk = (qi + 1) * bs
        s_all = s_next
        if qi + 1 < nblk:
            s_next = scores(qi + 1)
        blocks = []
        if qi > MOBA_TOPK:
            gate = gate_all[0:qi, qs]
            blk = lax.broadcasted_iota(jnp.int32, (qi, bs), 0)
        for n in range(qi):
            sn = s_all[n * bs:(n + 1) * bs, :]
            if qi > MOBA_TOPK:
                gn = gate[n:n + 1, :]
                beats = (gate > gn) | ((gate == gn) & (blk < n))
                rank = jnp.sum(jnp.where(beats, 1.0, 0.0), axis=0, keepdims=True)
                sn = jnp.where(rank < float(MOBA_TOPK), sn, NEG_BIG)
            blocks.append(sn)
        blocks.append(jnp.where(causal, s_all[qi * bs:nk, :], NEG_BIG))
        m = functools.reduce(jnp.maximum, [jnp.max(x, axis=0, keepdims=True) for x in blocks])
        pb = jnp.concatenate([jnp.exp2((x - m) * exp_scale).astype(BF16) for x in blocks], axis=0)
        acc = jnp.dot(vtb[:, 0:nk], pb, preferred_element_type=F32)
        o_ref[qs, :] = (acc[0:d, :] / acc[d:d + 1, :]).T


def _moba(proj, b, s, heads, col0):
    d = ATT_HEAD_DIM
    nblk = s // MOBA_BLOCK
    kern = functools.partial(_moba_kernel, nblk=nblk, scale=1.0 / math.sqrt(d))
    return pl.pallas_call(
        kern,
        out_shape=jax.ShapeDtypeStruct((b * s, heads * d), F32),
        grid=(b, heads),
        in_specs=[pl.BlockSpec((s, d), lambda bi, h: (bi, col0 + h)),
                  pl.BlockSpec((s, d), lambda bi, h: (bi, col0 + heads + h)),
                  pl.BlockSpec((s, d), lambda bi, h: (bi, col0 + 2 * heads + h))],
        out_specs=pl.BlockSpec((s, d), lambda bi, h: (bi, h)),
        compiler_params=pltpu.CompilerParams(
            dimension_semantics=("arbitrary", "arbitrary"), vmem_limit_bytes=VMEM_LIMIT),
        name="moba",
    )(proj, proj, proj)


def _head_sum(x, lo):
    s0 = jnp.sum(jnp.where(lo, x, 0.0), axis=-1, keepdims=True)
    s1 = jnp.sum(jnp.where(lo, 0.0, x), axis=-1, keepdims=True)
    return jnp.where(lo, s0, s1)


def _stack_heads(x, lo):
    return jnp.concatenate([jnp.where(lo, x, 0.0), jnp.where(lo, 0.0, x)], axis=0)


def _rwkv_kernel(p_ref, mu_ref, w0_ref, w2_ref, a0_ref, a2_ref, g2_ref, kk_ref, ka_ref, rk_ref,
                 lnw_ref, lnb_ref, o_ref, carry_ref, state_ref, *, n_sub, **dims):
    @pl.when(pl.program_id(1) == 0)
    def _():
        carry_ref[...] = jnp.zeros_like(carry_ref)
        state_ref[...] = jnp.zeros_like(state_ref)

    C = RWKV_CHUNK
    chunks = [_rwkv_chunk(p_ref.at[pl.ds(h * C, C)], mu_ref, w0_ref, w2_ref, a0_ref, a2_ref, g2_ref, kk_ref,
                          ka_ref, rk_ref, lnw_ref, lnb_ref, o_ref.at[pl.ds(h * C, C)], carry_ref, state_ref,
                          **dims) for h in range(n_sub)]
    _interleave(chunks, RWKV_CHUNK_LAG)


def _interleave(chunks, lag):
    n = len(chunks)
    steps, done, wrote, waiting = [0] * n, [False] * n, [False] * n, [False] * n

    def advance(j):
        if waiting[j] and j > 0:
            while not (wrote[j - 1] or done[j - 1]):
                advance(j - 1)
        waiting[j] = False
        try:
            tag = next(chunks[j])
        except StopIteration:
            done[j] = True
            return
        steps[j] += 1
        waiting[j] = tag == "need_state"
        wrote[j] = wrote[j] or tag == "state_done"

    while not all(done):
        for j in range(n):
            if not done[j] and (j == 0 or done[j - 1] or steps[j - 1] >= lag):
                advance(j)


def _rwkv_chunk(p_ref, mu_ref, w0_ref, w2_ref, a0_ref, a2_ref, g2_ref, kk_ref, ka_ref, rk_ref,
                lnw_ref, lnb_ref, o_ref, carry_ref, state_ref, *, cw, pw, pa, pg):
    C = RWKV_CHUNK
    C2 = 2 * C
    pr = p_ref[...]
    row = lax.broadcasted_iota(jnp.int32, pr.shape, 0)
    prev = jnp.where(row == 0, carry_ref[...], pltpu.roll(pr, 1, 0))
    carry_ref[...] = pr[C - 1:C, :]
    ps = pr + (prev - pr) * mu_ref[...]
    yield "step"

    r = ps[:, 0:cw]
    k = ps[:, cw:2 * cw]
    v = ps[:, 2 * cw:3 * cw]
    o = 3 * cw
    xw = ps[:, o:o + pw]
    xa = ps[:, o + pw:o + pw + pa]
    xg = ps[:, o + pw + pa:o + pw + pa + pg]

    z = -(w0_ref[...] + _dot(jnp.tanh(xw), w2_ref[...]))
    softplus = jnp.maximum(z, 0.0) + jnp.log(1.0 + jnp.exp(-jnp.abs(z)))
    logw = -jnp.exp(-softplus - 0.5)
    asig = _sigmoid(a0_ref[...] + _dot(xa, a2_ref[...]))
    gate = _dot(_sigmoid(xg), g2_ref[...])
    kkr = k * kk_ref[...]
    kmod = k * (1.0 + (asig - 1.0) * ka_ref[...])
    yield "step"

    ti = lax.broadcasted_iota(jnp.int32, (C, C), 0)
    si = lax.broadcasted_iota(jnp.int32, (C, C), 1)
    tril = jnp.where(si <= ti, 1.0, 0.0).astype(BF16)
    lh, lm, ll = _split3(logw)
    d32 = functools.partial(jnp.dot, preferred_element_type=F32)
    cum = d32(tril, lh) + (d32(tril, lm) + d32(tril, ll))
    cum_end = cum[C - 1:C, :]
    e_prev = jnp.exp(cum - logw)
    e_incl = jnp.exp(cum)
    e_inv = jnp.exp(-cum)
    e_end = jnp.exp(cum_end - cum)
    g_end = jnp.exp(cum_end)
    yield "step"

    lane = lax.broadcasted_iota(jnp.int32, (C, LANES), 1)
    lo = lane < RWKV_HEAD_DIM
    tok = lax.broadcasted_iota(jnp.int32, (C, LANES), 0)
    src = jnp.bitwise_and(lane, RWKV_HEAD_DIM - 1)
    strict = src < tok
    incl = src <= tok
    eye = jnp.where(src == tok, 1.0, 0.0)
    rho = lax.broadcasted_iota(jnp.int32, (C2, C2), 0)
    sig = lax.broadcasted_iota(jnp.int32, (C2, C2), 1)
    same_head = (rho >= C) == (sig >= C)

    pairs = range(cw // LANES)
    sls = [slice(p * LANES, (p + 1) * LANES) for p in pairs]
    incl2 = jnp.concatenate([incl, incl], axis=1)
    strict2 = jnp.concatenate([strict, strict], axis=1)

    b2, r2, k2, v2, at, rt, v2s, gm = [], [], [], [], [], [], [], []
    for sl in sls:
        kk2 = kkr[:, sl]
        nrm = jnp.sqrt(_head_sum(kk2 * kk2, lo))
        kkn = kk2 / jnp.maximum(nrm, 1e-12)
        b2.append(kkn * asig[:, sl])
        r2.append(r[:, sl])
        k2.append(kmod[:, sl])
        v2.append(v[:, sl])
        at.append(-kkn * e_prev[:, sl])
        rt.append(r2[-1] * e_incl[:, sl])
        bt = b2[-1] * e_inv[:, sl]
        kt = k2[-1] * e_inv[:, sl]
        v2s.append(_stack_heads(v2[-1], lo))
        rhs = jnp.concatenate([_stack_heads(bt, lo), _stack_heads(kt, lo)], axis=0)
        gm.append(_dot_nt(jnp.concatenate([at[-1], rt[-1]], axis=0), rhs))
        yield "step"

    labk = [jnp.where(strict2, g[0:C, :], 0.0) for g in gm]
    mrbk = [jnp.where(incl2, g[C:C2, :], 0.0) for g in gm]
    lakv = [_dot(labk[p][:, C2:2 * C2], v2s[p]) for p in pairs]
    yield "step"

    lab = [m[:, 0:C2] for m in labk]
    tinv = [eye + m for m in lab]
    pw2 = [_dot(m, _stack_heads(m, lo)) for m in lab]
    yield "step"
    n_pow = int(math.log2(C))
    for k in range(1, n_pow):
        if k < n_pow - 1:
            prod = [_dot(jnp.concatenate([t, m], axis=0), _stack_heads(m, lo)) for t, m in zip(tinv, pw2)]
            tinv = [t + pr[0:C] for t, pr in zip(tinv, prod)]
            pw2 = [pr[C:C2] for pr in prod]
        else:
            tinv = [t + _dot(t, _stack_heads(m, lo)) for t, m in zip(tinv, pw2)]
        yield "step"

    yield "need_state"
    s0 = [state_ref[p] for p in pairs]
    ars = [_dot_nt(jnp.concatenate([at[p], rt[p]], axis=0), s0[p]) for p in pairs]
    yield "step"
    u2 = [_dot(t, _stack_heads(ars[p][0:C] + lakv[p], lo)) for p, t in zip(pairs, tinv)]
    yield "step"
    y2 = [ars[p][C:C2] + _dot(mrbk[p], jnp.concatenate([_stack_heads(u2[p], lo), v2s[p]], axis=0))
          for p in pairs]
    yield "step"

    for p, sl in zip(pairs, sls):
        uv = jnp.concatenate([u2[p], v2[p]], axis=0)
        bk = jnp.concatenate([b2[p] * e_end[:, sl], k2[p] * e_end[:, sl]], axis=0)
        upd = _dot(uv.T, bk)
        state_ref[p] = s0[p] * g_end[:, sl] + jnp.where(same_head, upd, 0.0)
    yield "state_done"

    for p, sl in zip(pairs, sls):
        mean = _head_sum(y2[p], lo) * (1.0 / RWKV_HEAD_DIM)
        yc = y2[p] - mean
        var = _head_sum(yc * yc, lo) * (1.0 / RWKV_HEAD_DIM)
        yn = yc * lax.rsqrt(var + GN_EPS) * lnw_ref[:, sl] + lnb_ref[:, sl]
        bonus = _head_sum(r2[p] * k2[p] * rk_ref[:, sl], lo) * v2[p]
        o_ref[:, sl] = (yn + bonus) * gate[:, sl]
        yield "step"


def _rwkv(proj, b, s, cw, wpad, params):
    (mu, w0, w2, a0, a2, g2, kk, ka, rk, lnw, lnb) = params
    n_sub = RWKV_CHUNKS_PER_STEP if s % (RWKV_CHUNKS_PER_STEP * RWKV_CHUNK) == 0 else 1
    C = n_sub * RWKV_CHUNK
    nch = s // C
    pw, pa, pg = w2.shape[0], a2.shape[0], g2.shape[0]
    kern = functools.partial(_rwkv_kernel, n_sub=n_sub, cw=cw, pw=pw, pa=pa, pg=pg)
    row = lambda n: pl.BlockSpec((1, n), lambda bi, c: (0, 0))
    full = lambda a: pl.BlockSpec(a.shape, lambda bi, c: (0, 0))
    return pl.pallas_call(
        kern,
        out_shape=jax.ShapeDtypeStruct((b * s, cw), F32),
        grid=(b, nch),
        in_specs=[pl.BlockSpec((C, wpad), lambda bi, c: (bi * nch + c, 0)),
                  row(wpad), row(cw), full(w2), row(cw), full(a2), full(g2),
                  row(cw), row(cw), row(cw), row(cw), row(cw)],
        out_specs=pl.BlockSpec((C, cw), lambda bi, c: (bi * nch + c, 0)),
        scratch_shapes=[pltpu.VMEM((1, wpad), F32),
                        pltpu.VMEM((cw // LANES, LANES, LANES), F32)],
        compiler_params=pltpu.CompilerParams(
            dimension_semantics=("arbitrary", "arbitrary"), vmem_limit_bytes=VMEM_LIMIT),
        name="rwkv",
    )(proj, mu, w0, w2, a0, a2, g2, kk, ka, rk, lnw, lnb)


def _out_proj_kernel(att_ref, rw_ref, x_ref, ga_ref, woa_ref, wor_ref, g2_ref, wrh_ref, wrm_ref, br_ref,
                     x1_ref, hn_ref, route_ref, counts_ref):
    att = _rms(att_ref[...], ga_ref[...])
    y = (jnp.dot(att.astype(BF16), woa_ref[...], preferred_element_type=F32)
         + jnp.dot(rw_ref[...].astype(BF16), wor_ref[...], preferred_element_type=F32))
    x1 = x_ref[...] + y
    x1_ref[...] = x1
    hn = _rms(x1, g2_ref[...])
    _store_slabs(hn_ref, hn)

    hh, hm, _ = _split3(hn)
    d32 = functools.partial(jnp.dot, preferred_element_type=F32)
    logits = d32(hh, wrh_ref[...]) + (d32(hm, wrh_ref[...]) + d32(hh, wrm_ref[...])) + br_ref[...]
    lane = lax.broadcasted_iota(jnp.int32, logits.shape, 1)
    lanef = lane.astype(F32)
    big = float(LANES)

    is_g = lane < N_GROUPS
    gmax = jnp.max(jnp.where(is_g, logits, -jnp.inf), axis=-1, keepdims=True)
    gidx = jnp.min(jnp.where(is_g & (logits == gmax), lanef, big), axis=-1, keepdims=True)
    gsum = jnp.sum(jnp.where(is_g, jnp.exp(logits - gmax), 0.0), axis=-1, keepdims=True)
    g_gate = 1.0 / gsum

    e_lo = N_GROUPS + EXPERTS_PER_GROUP * gidx
    in_grp = (lanef >= e_lo) & (lanef < e_lo + EXPERTS_PER_GROUP)
    e1 = jnp.max(jnp.where(in_grp, logits, -jnp.inf), axis=-1, keepdims=True)
    i1 = jnp.min(jnp.where(in_grp & (logits == e1), lanef, big), axis=-1, keepdims=True)
    rest = in_grp & (lanef != i1)
    e2 = jnp.max(jnp.where(rest, logits, -jnp.inf), axis=-1, keepdims=True)
    i2 = jnp.min(jnp.where(rest & (logits == e2), lanef, big), axis=-1, keepdims=True)
    t = jnp.exp(e2 - e1)
    w1 = g_gate * (1.0 / (1.0 + t))
    w2 = g_gate * (t / (1.0 + t))

    @pl.when(pl.program_id(0) == 0)
    def _():
        counts_ref[...] = jnp.zeros_like(counts_ref)

    tm = logits.shape[0]
    pick1 = lanef == i1
    pick2 = lanef == i2
    onehot = jnp.where(pick1 | pick2, 1.0, 0.0)
    ti = lax.broadcasted_iota(jnp.int32, (tm, tm), 0)
    si = lax.broadcasted_iota(jnp.int32, (tm, tm), 1)
    before = _dot(jnp.where(si < ti, 1.0, 0.0), onehot) + counts_ref[...]
    r1 = jnp.sum(jnp.where(pick1, before, 0.0), axis=-1, keepdims=True)
    r2 = jnp.sum(jnp.where(pick2, before, 0.0), axis=-1, keepdims=True)
    counts_ref[...] = counts_ref[...] + jnp.sum(onehot, axis=0, keepdims=True)

    route_ref[...] = jnp.where(lane == 0, i1 - N_GROUPS,
                     jnp.where(lane == 1, i2 - N_GROUPS,
                     jnp.where(lane == 2, w1,
                     jnp.where(lane == 3, w2,
                     jnp.where(lane == 4, r1,
                     jnp.where(lane == 5, r2, 0.0))))))


def _out_proj(att, rw, x2, ga, woa, wor, g2, wrh, wrm, br, tm):
    t, d = x2.shape
    wa = att.shape[1]
    wr = rw.shape[1]
    sr = d // SLAB_LANES
    c = lambda shape: pl.BlockSpec(shape, lambda i: (0, 0))
    return pl.pallas_call(
        _out_proj_kernel,
        out_shape=(jax.ShapeDtypeStruct((t, d), F32), jax.ShapeDtypeStruct((t * sr, SLAB_LANES), F32),
                   jax.ShapeDtypeStruct((t, LANES), F32), jax.ShapeDtypeStruct((1, LANES), F32)),
        grid=(t // tm,),
        in_specs=[pl.BlockSpec((tm, wa), lambda i: (i, 0)),
                  pl.BlockSpec((tm, wr), lambda i: (i, 0)),
                  pl.BlockSpec((tm, d), lambda i: (i, 0)),
                  c((1, wa)), c((wa, d)), c((wr, d)), c((1, d)), c((d, LANES)), c((d, LANES)), c((1, LANES))],
        out_specs=(pl.BlockSpec((tm, d), lambda i: (i, 0)), pl.BlockSpec((tm * sr, SLAB_LANES), lambda i: (i, 0)),
                   pl.BlockSpec((tm, LANES), lambda i: (i, 0)), c((1, LANES))),
        compiler_params=pltpu.CompilerParams(
            dimension_semantics=("arbitrary",), vmem_limit_bytes=VMEM_LIMIT),
        name="out_proj",
    )(att, rw, x2, ga, woa, wor, g2, wrh, wrm, br)


def _store_slabs(ref, x):
    rows, d = x.shape
    sr = d // SLAB_LANES
    for s in range(sr):
        ref[pl.ds(s, rows, stride=sr), :] = x[:, s * SLAB_LANES:(s + 1) * SLAB_LANES]


def _load_slabs(ref, slot, first, rows, sr):
    return jnp.concatenate([ref[slot, pl.ds(first * sr + s, rows, stride=sr), :] for s in range(sr)], axis=1)


def _gather_rows(i, nt, idx_hbm, src_hbm, idx_ref, buf_ref, isem, gsem, rows, sr):
    def idx_copy(tile, slot):
        return pltpu.make_async_copy(idx_hbm.at[pl.ds(tile, 1)], idx_ref.at[pl.ds(slot, 1)], isem.at[slot])

    def start_rows(slot):
        for r in range(rows):
            src = pl.multiple_of(idx_ref[slot, r], sr)
            pltpu.make_async_copy(src_hbm.at[pl.ds(src, sr)], buf_ref.at[slot, pl.ds(r * sr, sr)],
                                  gsem.at[slot]).start()

    slot = jnp.bitwise_and(i, 1)

    @pl.when(i == 0)
    def _():
        idx_copy(0, 0).start()
        idx_copy(0, 0).wait()
        start_rows(0)
        if nt > 1:
            idx_copy(1, 1).start()

    @pl.when(i + 1 < nt)
    def _():
        idx_copy(i + 1, 1 - slot).wait()
        start_rows(1 - slot)

    @pl.when(i + 2 < nt)
    def _():
        idx_copy(i + 2, slot).start()

    pltpu.make_async_copy(src_hbm.at[pl.ds(0, rows * sr)], buf_ref.at[slot], gsem.at[slot]).wait()
    return slot


def _dispatch_kernel(idx_hbm, x_hbm, o_hbm, idx_ref, tile_ref, zero_ref, isem, lsem, csem,
                     *, nt, n_tok, n_pad, sr):
    i = pl.program_id(0)
    slot = jnp.bitwise_and(i, 1)
    other = 1 - slot
    tg = n_tok // TOP_K
    n_all = n_tok + n_pad
    stage = lax.rem(i, 3)

    def idx_copy(tile, s):
        return pltpu.make_async_copy(idx_hbm.at[pl.ds(tile, 1)], idx_ref.at[pl.ds(s, 1)], isem.at[s])

    def load(tile, s):
        return pltpu.make_async_copy(x_hbm.at[pl.ds(tile * tg * sr, tg * sr)], tile_ref.at[s], lsem.at[s])

    def copies_done(s):
        return pltpu.make_async_copy(x_hbm.at[pl.ds(0, n_all * sr)], o_hbm.at[pl.ds(0, n_all * sr)], csem.at[s])

    @pl.when(i == 0)
    def _():
        zero_ref[...] = jnp.zeros_like(zero_ref)
        idx_copy(0, 0).start()
        load(0, 0).start()

    idx_copy(i, slot).wait()
    load(i, stage).wait()

    @pl.when(i + 1 < nt)
    def _():
        idx_copy(i + 1, other).start()
        load(i + 1, lax.rem(i + 1, 3)).start()

    for r in range(n_tok):
        dst = pl.multiple_of(idx_ref[slot, r], sr)
        pltpu.make_async_copy(tile_ref.at[stage, pl.ds((r % tg) * sr, sr)], o_hbm.at[pl.ds(dst, sr)],
                              csem.at[slot]).start()
    for r in range(n_tok, n_all):
        dst = pl.multiple_of(idx_ref[slot, r], sr)
        pltpu.make_async_copy(zero_ref, o_hbm.at[pl.ds(dst, sr)], csem.at[slot]).start()

    @pl.when(i > 0)
    def _():
        copies_done(other).wait()

    @pl.when(i == nt - 1)
    def _():
        copies_done(slot).wait()


def _dispatch(idx, hn_slabs, n_rows, sr):
    nt, n_all = idx.shape
    n_tok = TOP_K * COMBINE_ROWS
    hbm = pl.BlockSpec(memory_space=pl.ANY)
    return pl.pallas_call(
        functools.partial(_dispatch_kernel, nt=nt, n_tok=n_tok, n_pad=n_all - n_tok, sr=sr),
        out_shape=jax.ShapeDtypeStruct((n_rows * sr, SLAB_LANES), F32),
        grid=(nt,),
        in_specs=[hbm, hbm],
        out_specs=hbm,
        scratch_shapes=[pltpu.SMEM((2, n_all), jnp.int32),
                        pltpu.VMEM((3, COMBINE_ROWS * sr, SLAB_LANES), F32), pltpu.VMEM((sr, SLAB_LANES), F32),
                        pltpu.SemaphoreType.DMA((2,)), pltpu.SemaphoreType.DMA((3,)),
                        pltpu.SemaphoreType.DMA((2,))],
        compiler_params=pltpu.CompilerParams(dimension_semantics=("arbitrary",)),
        name="dispatch",
    )(idx, hn_slabs)


def _expert_kernel(te_ref, first_ref, next_ref, nu_ref, x_ref, w1_hbm, w3_hbm, w2_hbm, o_ref,
                   stage_refs, wb_refs, wsem, *, sr):
    i = pl.program_id(0)
    tm = EXPERT_ROWS
    w_hbm = (w1_hbm, w3_hbm, w2_hbm)
    used = i < nu_ref[0]

    def weight_copies(e):
        out = []
        for j, (w, st) in enumerate(zip(w_hbm, stage_refs)):
            half = st.shape[0] // 2
            for q in range(2):
                rows = pl.ds(q * half, half)
                out.append((pltpu.make_async_copy(w.at[e, rows], st.at[rows], wsem.at[2 * j + q]), q))
        return out

    @pl.when(i == 0)
    def _():
        for cp, q in weight_copies(te_ref[0]):
            cp.start(priority=q)

    @pl.when(used & (first_ref[i] == 1))
    def _():
        for cp, _ in weight_copies(te_ref[i]):
            cp.wait()
        for st, wb in zip(stage_refs, wb_refs):
            rows = st.shape[0] // WEIGHT_CAST_CHUNKS

            def cast_chunk(c, carry, st=st, wb=wb, rows=rows):
                r0 = pl.multiple_of(c * rows, rows)
                wb[pl.ds(r0, rows), :] = st[pl.ds(r0, rows), :].astype(BF16)
                return carry

            lax.fori_loop(0, WEIGHT_CAST_CHUNKS, cast_chunk, 0)

        @pl.when(next_ref[i] >= 0)
        def _():
            for cp, q in weight_copies(next_ref[i]):
                cp.start(priority=q)

    @pl.when(used)
    def _():
        xb = jnp.concatenate([x_ref[pl.ds(s, tm, stride=sr), :] for s in range(sr)], axis=1).astype(BF16)
        h1 = jnp.dot(xb, wb_refs[0][...], preferred_element_type=F32)
        h3 = jnp.dot(xb, wb_refs[1][...], preferred_element_type=F32)
        hid = (h1 * _sigmoid(h1)) * h3
        _store_slabs(o_ref, jnp.dot(hid.astype(BF16), wb_refs[2][...], preferred_element_type=F32))

    @pl.when(jnp.logical_not(used))
    def _():
        o_ref[...] = jnp.zeros_like(o_ref)


def _experts(tile_e, tile_first, tile_next, n_used, x_slabs, w1, w3, w2):
    nt = tile_e.shape[0]
    d, de = w1.shape[1], w1.shape[2]
    sr = d // SLAB_LANES
    tm = EXPERT_ROWS
    hbm = pl.BlockSpec(memory_space=pl.ANY)
    grid_spec = pltpu.PrefetchScalarGridSpec(
        num_scalar_prefetch=4,
        grid=(nt,),
        in_specs=[pl.BlockSpec((tm * sr, SLAB_LANES), lambda i, te, fi, nx, nu: (jnp.minimum(i, nu[0] - 1), 0)),
                  hbm, hbm, hbm],
        out_specs=pl.BlockSpec((tm * sr, SLAB_LANES), lambda i, *_: (i, 0)),
        scratch_shapes=[[pltpu.VMEM((d, de), F32), pltpu.VMEM((d, de), F32), pltpu.VMEM((de, d), F32)],
                        [pltpu.VMEM((d, de), BF16), pltpu.VMEM((d, de), BF16), pltpu.VMEM((de, d), BF16)],
                        pltpu.SemaphoreType.DMA((6,))],
    )
    return pl.pallas_call(
        functools.partial(_expert_kernel, sr=sr),
        out_shape=jax.ShapeDtypeStruct((nt * tm * sr, SLAB_LANES), F32),
        grid_spec=grid_spec,
        compiler_params=pltpu.CompilerParams(
            dimension_semantics=("arbitrary",), vmem_limit_bytes=EXPERT_VMEM_LIMIT),
        name="experts",
    )(tile_e, tile_first, tile_next, n_used, x_slabs, w1, w3, w2)


def _combine_kernel(dest_hbm, y_hbm, x1_ref, route_ref, fg_ref, o_ref, idx_ref, buf_ref, isem, gsem,
                    *, nt, sr):
    i = pl.program_id(0)
    tg = COMBINE_ROWS
    slot = _gather_rows(i, nt, dest_hbm, y_hbm, idx_ref, buf_ref, isem, gsem, TOP_K * tg, sr)
    route = route_ref[...]
    y = (_load_slabs(buf_ref, slot, 0, tg, sr) * route[:, 2:3]
         + _load_slabs(buf_ref, slot, tg, tg, sr) * route[:, 3:4])
    o_ref[...] = _rms(x1_ref[...] + y, fg_ref[...])


def _combine(dest, ybuf, x1, route, fg):
    t, d = x1.shape
    tg = COMBINE_ROWS
    sr = d // SLAB_LANES
    return pl.pallas_call(
        functools.partial(_combine_kernel, nt=t // tg, sr=sr),
        out_shape=jax.ShapeDtypeStruct((t, d), F32),
        grid=(t // tg,),
        in_specs=[pl.BlockSpec(memory_space=pl.ANY),
                  pl.BlockSpec(memory_space=pl.ANY),
                  pl.BlockSpec((tg, d), lambda i: (i, 0)),
                  pl.BlockSpec((tg, LANES), lambda i: (i, 0)),
                  pl.BlockSpec((1, d), lambda i: (0, 0))],
        out_specs=pl.BlockSpec((tg, d), lambda i: (i, 0)),
        scratch_shapes=[pltpu.SMEM((2, TOP_K * tg), jnp.int32),
                        pltpu.VMEM((2, TOP_K * tg * sr, SLAB_LANES), F32),
                        pltpu.SemaphoreType.DMA((2,)), pltpu.SemaphoreType.DMA((2,))],
        compiler_params=pltpu.CompilerParams(
            dimension_semantics=("arbitrary",), vmem_limit_bytes=VMEM_LIMIT),
        name="combine",
    )(dest, ybuf, x1, route, fg)


def _pad_cols(m, n):
    return jnp.pad(m, ((0, 0), (0, n - m.shape[1])))


def _pad_rows(m, n):
    return jnp.pad(m, ((0, n - m.shape[0]), (0, 0)))


def _pick_tile(n, cap):
    for unit in (MXU_WIDTH, LANES):
        fits = [c for c in range(unit, cap + 1, unit) if n % c == 0]
        if fits:
            return fits[-1]
    raise ValueError(f"no lane-aligned tile divides {n}")


def _dispatch_plan(experts, rank, counts, n_tok):
    tm = EXPERT_ROWS
    nt = n_tok * TOP_K // tm + N_EXPERTS
    tiles_e = (counts + tm - 1) // tm
    tile_end = jnp.cumsum(tiles_e)
    tile_start = tile_end - tiles_e
    dest = (tile_start[experts] * tm + rank).astype(jnp.int32)
    n_used = tile_end[-1:].astype(jnp.int32)
    tiles = jnp.arange(nt, dtype=jnp.int32)
    tile_e = jnp.minimum(jnp.sum((tile_end[None, :] <= tiles[:, None]).astype(jnp.int32), axis=1),
                         N_EXPERTS - 1).astype(jnp.int32)
    tile_first = ((tiles == 0) | (tile_e != jnp.roll(tile_e, 1))).astype(jnp.int32)
    next_start = tile_end[tile_e]
    tile_next = jnp.where(next_start < n_used[0], tile_e[jnp.minimum(next_start, nt - 1)], -1).astype(jnp.int32)

    seg_len = jnp.concatenate([tiles_e * tm - counts, (nt - n_used) * tm])
    seg_first = jnp.concatenate([tile_start * tm + counts, n_used * tm])
    seg_stop = jnp.cumsum(seg_len)
    j = jnp.arange(N_EXPERTS * tm, dtype=jnp.int32)
    seg = jnp.sum((seg_stop[None, :] <= j[:, None]).astype(jnp.int32), axis=1)
    pad_rows = (seg_first[seg] + j - (seg_stop - seg_len)[seg]).astype(jnp.int32)
    return tile_e, tile_first, tile_next, n_used, dest, pad_rows


def _layer(x2, b, s, norm1_g, w_in, attn_out_g, mu, w0, w2, a0, a2, g2, k_k, k_a, r_k, lnx_w, lnx_b,
           w_out, norm2_g, wg, bg, we, be, ew1, ew3, ew2, final_g):
    t, d = x2.shape
    aw = attn_out_g.shape[0]
    cw = w0.shape[0]
    heads = aw // ATT_HEAD_DIM
    dl, al, gl = w2.shape[0], a2.shape[0], g2.shape[0]
    pw, pa, pg = _round_up(dl, LANES), _round_up(al, LANES), _round_up(gl, LANES)
    wpad = 3 * cw + pw + pa + pg

    w_in = w_in.astype(BF16)
    sh = w_in[:, 3 * aw:]
    o = 3 * cw
    w_arr = jnp.concatenate(
        [sh[:, :o], _pad_cols(sh[:, o:o + dl], pw), _pad_cols(sh[:, o + dl:o + dl + al], pa),
         _pad_cols(sh[:, o + dl + al:o + dl + al + gl], pg), w_in[:, :3 * aw]], axis=1)
    mu_arr = jnp.concatenate(
        [mu[:o], jnp.pad(mu[o:o + dl], (0, pw - dl)), jnp.pad(mu[o + dl:o + dl + al], (0, pa - al)),
         jnp.pad(mu[o + dl + al:], (0, pg - gl))])[None, :]

    n_all = w_arr.shape[1]
    proj = _in_proj(x2, norm1_g[None, :], w_arr, tm=min(512, t), tn=_pick_tile(n_all, 4096))

    att = _moba(proj, b, s, heads, wpad // LANES)
    r1 = lambda a: a.reshape(1, -1)
    rw = _rwkv(proj, b, s, cw, wpad,
               (mu_arr, r1(w0), _pad_rows(w2, pw).astype(BF16), r1(a0), _pad_rows(a2, pa).astype(BF16),
                _pad_rows(g2, pg).astype(BF16), r1(k_k), r1(k_a), r1(r_k), r1(lnx_w), r1(lnx_b)))

    wr = _pad_cols(jnp.concatenate([wg, we], axis=1), LANES)
    wrh = wr.astype(BF16)
    wrm = (wr - wrh.astype(F32)).astype(BF16)
    br = jnp.pad(jnp.concatenate([bg, be]), (0, LANES - N_GROUPS - N_EXPERTS))[None, :]
    wo = w_out.astype(BF16)
    x1, hn, route, counts = _out_proj(att, rw, x2, r1(attn_out_g), wo[:aw], wo[aw:], r1(norm2_g), wrh, wrm, br,
                                      tm=min(512, t))

    plan_in = route[:, :6].astype(jnp.int32)
    counts = counts[0, N_GROUPS:N_GROUPS + N_EXPERTS].astype(jnp.int32)
    tg = COMBINE_ROWS
    n_tiles = t // tg
    pad_per_tile = N_EXPERTS * EXPERT_ROWS // n_tiles
    tile_e, tile_first, tile_next, n_used, dest, pad_rows = _dispatch_plan(
        plan_in[:, 0:TOP_K], plan_in[:, 4:4 + TOP_K], counts, t)

    sr = d // SLAB_LANES
    dest_tiles = dest.reshape(n_tiles, tg, TOP_K).transpose(0, 2, 1).reshape(n_tiles, TOP_K * tg) * sr
    disp_idx = jnp.concatenate([dest_tiles, pad_rows.reshape(n_tiles, pad_per_tile) * sr], axis=1)
    xbuf = _dispatch(disp_idx, hn, tile_e.shape[0] * EXPERT_ROWS, sr)
    ybuf = _experts(tile_e, tile_first, tile_next, n_used, xbuf, ew1, ew3, ew2)
    return _combine(dest_tiles, ybuf, x1, route, r1(final_g))


def kernel(x, norm1_g, w_in, attn_out_g, rwkv_mu, rwkv_w0, rwkv_w2, rwkv_a0, rwkv_a2, rwkv_g2, rwkv_k_k,
           rwkv_k_a, rwkv_r_k, rwkv_lnx_w, rwkv_lnx_b, w_out, norm2_g, router_group_w, router_group_b,
           router_expert_w, router_expert_b, expert_w1, expert_w3, expert_w2, final_g):
    b, s, d = x.shape
    assert norm1_g.shape[0] == 1, "single-layer block"
    assert s % MOBA_BLOCK == 0 and s % RWKV_CHUNK == 0
    assert (b * s) % max(EXPERT_ROWS, COMBINE_ROWS) == 0
    assert 2 * RWKV_CHUNK == LANES and RWKV_CHUNK == RWKV_HEAD_DIM, "rwkv packs two CxC head matrices per vreg row"
    assert (N_EXPERTS * EXPERT_ROWS) % ((b * s) // COMBINE_ROWS) == 0, "zero-fill rows split evenly over tiles"
    out = _layer(x.reshape(b * s, d), b, s, norm1_g[0], w_in[0], attn_out_g[0], rwkv_mu[0], rwkv_w0[0],
                 rwkv_w2[0], rwkv_a0[0], rwkv_a2[0], rwkv_g2[0], rwkv_k_k[0], rwkv_k_a[0], rwkv_r_k[0],
                 rwkv_lnx_w[0], rwkv_lnx_b[0], w_out[0], norm2_g[0], router_group_w[0], router_group_b[0],
                 router_expert_w[0], router_expert_b[0], expert_w1[0], expert_w3[0], expert_w2[0], final_g)
    return out.reshape(b, s, d)
```

```python
import functools
import math

import jax
import jax.numpy as jnp
from jax import lax
from jax.experimental import pallas as pl
from jax.experimental.pallas import tpu as pltpu

F32 = jnp.float32
BF16 = jnp.bfloat16

LANES = 128
MXU_WIDTH = 256
BF16_SUBLANES = 16
SLAB_LANES = 128
ATT_HEAD_DIM = 128
RWKV_HEAD_DIM = 64
MOBA_BLOCK = 256
MOBA_TOPK = 3
N_GROUPS = 4
EXPERTS_PER_GROUP = 8
N_EXPERTS = N_GROUPS * EXPERTS_PER_GROUP
TOP_K = 2
RMS_EPS = 1e-6
GN_EPS = 64e-5
NEG_BIG = -1e30
RWKV_CHUNK = 64
RWKV_CHUNKS_PER_STEP = 4
RWKV_CHUNK_LAG = 8
EXPERT_ROWS = 256
COMBINE_ROWS = 256
VMEM_LIMIT = 56 * 1024 * 1024
EXPERT_VMEM_LIMIT = 60 * 1024 * 1024
WEIGHT_CAST_CHUNKS = 16


def _round_up(n, m):
    return (n + m - 1) // m * m


def _dot(a, b):
    return jnp.dot(a.astype(BF16), b.astype(BF16), preferred_element_type=F32)


def _dot_nt(a, b):
    return lax.dot_general(a.astype(BF16), b.astype(BF16), (((1,), (1,)), ((), ())),
                           preferred_element_type=F32)


def _split3(a):
    hi = a.astype(BF16)
    r1 = a - hi.astype(F32)
    mid = r1.astype(BF16)
    lo = (r1 - mid.astype(F32)).astype(BF16)
    return hi, mid, lo


def _dot_hi(a, b):
    ah, am, _ = _split3(a)
    bh, bm, _ = _split3(b)
    d = functools.partial(jnp.dot, preferred_element_type=F32)
    return d(ah, bh) + (d(am, bh) + d(ah, bm))


def _sigmoid(x):
    return 1.0 / (1.0 + jnp.exp(-x))


def _rms(x, g):
    ms = jnp.mean(x * x, axis=-1, keepdims=True)
    return x * lax.rsqrt(ms + RMS_EPS) * g


def _in_proj_kernel(x_ref, g_ref, w_ref, o_ref):
    xn = _rms(x_ref[...], g_ref[...]).astype(BF16)
    o_ref[...] = jnp.dot(xn, w_ref[...], preferred_element_type=F32)


def _in_proj(x2, g, w, tm, tn):
    t, d = x2.shape
    n = w.shape[1]
    return pl.pallas_call(
        _in_proj_kernel,
        out_shape=jax.ShapeDtypeStruct((t, n), F32),
        grid=(n // tn, t // tm),
        in_specs=[pl.BlockSpec((tm, d), lambda j, i: (i, 0)),
                  pl.BlockSpec((1, d), lambda j, i: (0, 0)),
                  pl.BlockSpec((d, tn), lambda j, i: (0, j))],
        out_specs=pl.BlockSpec((tm, tn), lambda j, i: (i, j)),
        compiler_params=pltpu.CompilerParams(
            dimension_semantics=("arbitrary", "arbitrary"), vmem_limit_bytes=VMEM_LIMIT),
        name="in_proj",
    )(x2, g, w)


def _moba_kernel(q_ref, k_ref, v_ref, o_ref, *, nblk, scale):
    bs = MOBA_BLOCK
    k = k_ref[...]
    kb = k.astype(BF16)
    d = v_ref.shape[1]
    vtb = jnp.concatenate([v_ref[...].T, jnp.ones((BF16_SUBLANES, v_ref.shape[0]), F32)], axis=0).astype(BF16)
    qt = q_ref[...].T
    qtb = qt.astype(BF16)
    kmean = jnp.concatenate(
        [jnp.sum(k[n * bs:(n + 1) * bs, :], axis=0, keepdims=True) for n in range(nblk)], axis=0) * (1.0 / bs)
    gate_all = _dot_hi(kmean, qt)
    kpos = lax.broadcasted_iota(jnp.int32, (bs, bs), 0)
    qpos = lax.broadcasted_iota(jnp.int32, (bs, bs), 1)
    causal = kpos <= qpos

    def scores(qi):
        return jnp.dot(kb[0:(qi + 1) * bs, :], qtb[:, qi * bs:(qi + 1) * bs], preferred_element_type=F32)

    exp_scale = scale * math.log2(math.e)
    s_next = scores(0)
    for qi in range(nblk):
        qs = slice(qi * bs, (qi + 1) * bs)
        nk = (qi + 1) * bs
        s_all = s_next
        if qi + 1 < nblk:
            s_next = scores(qi + 1)
        blocks = []
        if qi > MOBA_TOPK:
            gate = gate_all[0:qi, qs]
            blk = lax.broadcasted_iota(jnp.int32, (qi, bs), 0)
        for n in range(qi):
            sn = s_all[n * bs:(n + 1) * bs, :]
            if qi > MOBA_TOPK:
                gn = gate[n:n + 1, :]
                beats = (gate > gn) | ((gate == gn) & (blk < n))
                rank = jnp.sum(jnp.where(beats, 1.0, 0.0), axis=0, keepdims=True)
                sn = jnp.where(rank < float(MOBA_TOPK), sn, NEG_BIG)
            blocks.append(sn)
        blocks.append(jnp.where(causal, s_all[qi * bs:nk, :], NEG_BIG))
        m = functools.reduce(jnp.maximum, [jnp.max(x, axis=0, keepdims=True) for x in blocks])
        pb = jnp.concatenate([jnp.exp2((x - m) * exp_scale).astype(BF16) for x in blocks], axis=0)
        acc = jnp.dot(vtb[:, 0:nk], pb, preferred_element_type=F32)
        o_ref[qs, :] = (acc[0:d, :] / acc[d:d + 1, :]).T


def _moba(proj, b, s, heads, col0):
    d = ATT_HEAD_DIM
    nblk = s // MOBA_BLOCK
    kern = functools.partial(_moba_kernel, nblk=nblk, scale=1.0 / math.sqrt(d))
    return pl.pallas_call(
        kern,
        out_shape=jax.ShapeDtypeStruct((b * s, heads * d), F32),
        grid=(b, heads),
        in_specs=[pl.BlockSpec((s, d), lambda bi, h: (bi, col0 + h)),
                  pl.BlockSpec((s, d), lambda bi, h: (bi, col0 + heads + h)),
                  pl.BlockSpec((s, d), lambda bi, h: (bi, col0 + 2 * heads + h))],
        out_specs=pl.BlockSpec((s, d), lambda bi, h: (bi, h)),
        compiler_params=pltpu.CompilerParams(
            dimension_semantics=("arbitrary", "arbitrary"), vmem_limit_bytes=VMEM_LIMIT),
        name="moba",
    )(proj, proj, proj)


def _head_sum(x, lo):
    s0 = jnp.sum(jnp.where(lo, x, 0.0), axis=-1, keepdims=True)
    s1 = jnp.sum(jnp.where(lo, 0.0, x), axis=-1, keepdims=True)
    return jnp.where(lo, s0, s1)


def _stack_heads(x, lo):
    return jnp.concatenate([jnp.where(lo, x, 0.0), jnp.where(lo, 0.0, x)], axis=0)


def _rwkv_kernel(p_ref, mu_ref, w0_ref, w2_ref, a0_ref, a2_ref, g2_ref, kk_ref, ka_ref, rk_ref,
                 lnw_ref, lnb_ref, o_ref, carry_ref, state_ref, *, n_sub, **dims):
    @pl.when(pl.program_id(1) == 0)
    def _():
        carry_ref[...] = jnp.zeros_like(carry_ref)
        state_ref[...] = jnp.zeros_like(state_ref)

    C = RWKV_CHUNK
    chunks = [_rwkv_chunk(p_ref.at[pl.ds(h * C, C)], mu_ref, w0_ref, w2_ref, a0_ref, a2_ref, g2_ref, kk_ref,
                          ka_ref, rk_ref, lnw_ref, lnb_ref, o_ref.at[pl.ds(h * C, C)], carry_ref, state_ref,
                          **dims) for h in range(n_sub)]
    _interleave(chunks, RWKV_CHUNK_LAG)


def _interleave(chunks, lag):
    n = len(chunks)
    steps, done, wrote, waiting = [0] * n, [False] * n, [False] * n, [False] * n

    def advance(j):
        if waiting[j] and j > 0:
            while not (wrote[j - 1] or done[j - 1]):
                advance(j - 1)
        waiting[j] = False
        try:
            tag = next(chunks[j])
        except StopIteration:
            done[j] = True
            return
        steps[j] += 1
        waiting[j] = tag == "need_state"
        wrote[j] = wrote[j] or tag == "state_done"

    while not all(done):
        for j in range(n):
            if not done[j] and (j == 0 or done[j - 1] or steps[j - 1] >= lag):
                advance(j)


def _rwkv_chunk(p_ref, mu_ref, w0_ref, w2_ref, a0_ref, a2_ref, g2_ref, kk_ref, ka_ref, rk_ref,
                lnw_ref, lnb_ref, o_ref, carry_ref, state_ref, *, cw, pw, pa, pg):
    C = RWKV_CHUNK
    C2 = 2 * C
    pr = p_ref[...]
    row = lax.broadcasted_iota(jnp.int32, pr.shape, 0)
    prev = jnp.where(row == 0, carry_ref[...], pltpu.roll(pr, 1, 0))
    carry_ref[...] = pr[C - 1:C, :]
    ps = pr + (prev - pr) * mu_ref[...]
    yield "step"

    r = ps[:, 0:cw]
    k = ps[:, cw:2 * cw]
    v = ps[:, 2 * cw:3 * cw]
    o = 3 * cw
    xw = ps[:, o:o + pw]
    xa = ps[:, o + pw:o + pw + pa]
    xg = ps[:, o + pw + pa:o + pw + pa + pg]

    z = -(w0_ref[...] + _dot(jnp.tanh(xw), w2_ref[...]))
    softplus = jnp.maximum(z, 0.0) + jnp.log(1.0 + jnp.exp(-jnp.abs(z)))
    logw = -jnp.exp(-softplus - 0.5)
    asig = _sigmoid(a0_ref[...] + _dot(xa, a2_ref[...]))
    gate = _dot(_sigmoid(xg), g2_ref[...])
    kkr = k * kk_ref[...]
    kmod = k * (1.0 + (asig - 1.0) * ka_ref[...])
    yield "step"

    ti = lax.broadcasted_iota(jnp.int32, (C, C), 0)
    si = lax.broadcasted_iota(jnp.int32, (C, C), 1)
    tril = jnp.where(si <= ti, 1.0, 0.0).astype(BF16)
    lh, lm, ll = _split3(logw)
    d32 = functools.partial(jnp.dot, preferred_element_type=F32)
    cum = d32(tril, lh) + (d32(tril, lm) + d32(tril, ll))
    cum_end = cum[C - 1:C, :]
    e_prev = jnp.exp(cum - logw)
    e_incl = jnp.exp(cum)
    e_inv = jnp.exp(-cum)
    g_end = jnp.exp(cum_end)
    e_end = g_end * e_inv
    yield "step"

    lane = lax.broadcasted_iota(jnp.int32, (C, LANES), 1)
    lo = lane < RWKV_HEAD_DIM
    tok = lax.broadcasted_iota(jnp.int32, (C, LANES), 0)
    src = jnp.bitwise_and(lane, RWKV_HEAD_DIM - 1)
    strict = src < tok
    incl = src <= tok
    eye = jnp.where(src == tok, 1.0, 0.0)
    rho = lax.broadcasted_iota(jnp.int32, (C2, C2), 0)
    sig = lax.broadcasted_iota(jnp.int32, (C2, C2), 1)
    same_head = (rho >= C) == (sig >= C)

    pairs = range(cw // LANES)
    sls = [slice(p * LANES, (p + 1) * LANES) for p in pairs]
    incl2 = jnp.concatenate([incl, incl], axis=1)
    strict2 = jnp.concatenate([strict, strict], axis=1)

    b2, r2, k2, v2, at, rt, v2s, gm = [], [], [], [], [], [], [], []
    for sl in sls:
        kk2 = kkr[:, sl]
        nrm = jnp.sqrt(_head_sum(kk2 * kk2, lo))
        kkn = kk2 / jnp.maximum(nrm, 1e-12)
        b2.append(kkn * asig[:, sl])
        r2.append(r[:, sl])
        k2.append(kmod[:, sl])
        v2.append(v[:, sl])
        at.append(-kkn * e_prev[:, sl])
        rt.append(r2[-1] * e_incl[:, sl])
        bt = b2[-1] * e_inv[:, sl]
        kt = k2[-1] * e_inv[:, sl]
        v2s.append(_stack_heads(v2[-1], lo))
        rhs = jnp.concatenate([_stack_heads(bt, lo), _stack_heads(kt, lo)], axis=0)
        gm.append(_dot_nt(jnp.concatenate([at[-1], rt[-1]], axis=0), rhs))
        yield "step"

    labk = [jnp.where(strict2, g[0:C, :], 0.0) for g in gm]
    mrbk = [jnp.where(incl2, g[C:C2, :], 0.0) for g in gm]
    lakv = [_dot(labk[p][:, C2:2 * C2], v2s[p]) for p in pairs]
    yield "step"

    lab = [m[:, 0:C2] for m in labk]
    tinv = [eye + m for m in lab]
    pw2 = [_dot(m, _stack_heads(m, lo)) for m in lab]
    yield "step"
    n_pow = int(math.log2(C))
    for k in range(1, n_pow):
        if k < n_pow - 1:
            prod = [_dot(jnp.concatenate([t, m], axis=0), _stack_heads(m, lo)) for t, m in zip(tinv, pw2)]
            tinv = [t + pr[0:C] for t, pr in zip(tinv, prod)]
            pw2 = [pr[C:C2] for pr in prod]
        else:
            tinv = [t + _dot(t, _stack_heads(m, lo)) for t, m in zip(tinv, pw2)]
        yield "step"

    yield "need_state"
    s0 = [state_ref[p] for p in pairs]
    ars = [_dot_nt(jnp.concatenate([at[p], rt[p]], axis=0), s0[p]) for p in pairs]
    yield "step"
    u2 = [_dot(t, _stack_heads(ars[p][0:C] + lakv[p], lo)) for p, t in zip(pairs, tinv)]
    yield "step"
    y2 = [ars[p][C:C2] + _dot(mrbk[p], jnp.concatenate([_stack_heads(u2[p], lo), v2s[p]], axis=0))
          for p in pairs]
    yield "step"

    for p, sl in zip(pairs, sls):
        uv = jnp.concatenate([u2[p], v2[p]], axis=0)
        bk = jnp.concatenate([b2[p] * e_end[:, sl], k2[p] * e_end[:, sl]], axis=0)
        upd = _dot(uv.T, bk)
        state_ref[p] = s0[p] * g_end[:, sl] + jnp.where(same_head, upd, 0.0)
    yield "state_done"

    for p, sl in zip(pairs, sls):
        mean = _head_sum(y2[p], lo) * (1.0 / RWKV_HEAD_DIM)
        yc = y2[p] - mean
        var = _head_sum(yc * yc, lo) * (1.0 / RWKV_HEAD_DIM)
        yn = yc * lax.rsqrt(var + GN_EPS) * lnw_ref[:, sl] + lnb_ref[:, sl]
        bonus = _head_sum(r2[p] * k2[p] * rk_ref[:, sl], lo) * v2[p]
        o_ref[:, sl] = (yn + bonus) * gate[:, sl]
        yield "step"


def _rwkv(proj, b, s, cw, wpad, params):
    (mu, w0, w2, a0, a2, g2, kk, ka, rk, lnw, lnb) = params
    n_sub = RWKV_CHUNKS_PER_STEP if s % (RWKV_CHUNKS_PER_STEP * RWKV_CHUNK) == 0 else 1
    C = n_sub * RWKV_CHUNK
    nch = s // C
    pw, pa, pg = w2.shape[0], a2.shape[0], g2.shape[0]
    kern = functools.partial(_rwkv_kernel, n_sub=n_sub, cw=cw, pw=pw, pa=pa, pg=pg)
    row = lambda n: pl.BlockSpec((1, n), lambda bi, c: (0, 0))
    full = lambda a: pl.BlockSpec(a.shape, lambda bi, c: (0, 0))
    return pl.pallas_call(
        kern,
        out_shape=jax.ShapeDtypeStruct((b * s, cw), F32),
        grid=(b, nch),
        in_specs=[pl.BlockSpec((C, wpad), lambda bi, c: (bi * nch + c, 0)),
                  row(wpad), row(cw), full(w2), row(cw), full(a2), full(g2),
                  row(cw), row(cw), row(cw), row(cw), row(cw)],
        out_specs=pl.BlockSpec((C, cw), lambda bi, c: (bi * nch + c, 0)),
        scratch_shapes=[pltpu.VMEM((1, wpad), F32),
                        pltpu.VMEM((cw // LANES, LANES, LANES), F32)],
        compiler_params=pltpu.CompilerParams(
            dimension_semantics=("arbitrary", "arbitrary"), vmem_limit_bytes=VMEM_LIMIT),
        name="rwkv",
    )(proj, mu, w0, w2, a0, a2, g2, kk, ka, rk, lnw, lnb)


def _out_proj_kernel(att_ref, rw_ref, x_ref, ga_ref, woa_ref, wor_ref, g2_ref, wrh_ref, wrm_ref, br_ref,
                     x1_ref, hn_ref, route_ref, counts_ref):
    att = _rms(att_ref[...], ga_ref[...])
    y = (jnp.dot(att.astype(BF16), woa_ref[...], preferred_element_type=F32)
         + jnp.dot(rw_ref[...].astype(BF16), wor_ref[...], preferred_element_type=F32))
    x1 = x_ref[...] + y
    x1_ref[...] = x1
    hn = _rms(x1, g2_ref[...])
    _store_slabs(hn_ref, hn)

    hh, hm, _ = _split3(hn)
    d32 = functools.partial(jnp.dot, preferred_element_type=F32)
    logits = d32(hh, wrh_ref[...]) + (d32(hm, wrh_ref[...]) + d32(hh, wrm_ref[...])) + br_ref[...]
    lane = lax.broadcasted_iota(jnp.int32, logits.shape, 1)
    lanef = lane.astype(F32)
    big = float(LANES)

    is_g = lane < N_GROUPS
    gmax = jnp.max(jnp.where(is_g, logits, -jnp.inf), axis=-1, keepdims=True)
    gidx = jnp.min(jnp.where(is_g & (logits == gmax), lanef, big), axis=-1, keepdims=True)
    gsum = jnp.sum(jnp.where(is_g, jnp.exp(logits - gmax), 0.0), axis=-1, keepdims=True)
    g_gate = 1.0 / gsum

    e_lo = N_GROUPS + EXPERTS_PER_GROUP * gidx
    in_grp = (lanef >= e_lo) & (lanef < e_lo + EXPERTS_PER_GROUP)
    e1 = jnp.max(jnp.where(in_grp, logits, -jnp.inf), axis=-1, keepdims=True)
    i1 = jnp.min(jnp.where(in_grp & (logits == e1), lanef, big), axis=-1, keepdims=True)
    rest = in_grp & (lanef != i1)
    e2 = jnp.max(jnp.where(rest, logits, -jnp.inf), axis=-1, keepdims=True)
    i2 = jnp.min(jnp.where(rest & (logits == e2), lanef, big), axis=-1, keepdims=True)
    t = jnp.exp(e2 - e1)
    w1 = g_gate * (1.0 / (1.0 + t))
    w2 = g_gate * (t / (1.0 + t))

    @pl.when(pl.program_id(0) == 0)
    def _():
        counts_ref[...] = jnp.zeros_like(counts_ref)

    tm = logits.shape[0]
    pick1 = lanef == i1
    pick2 = lanef == i2
    onehot = jnp.where(pick1 | pick2, 1.0, 0.0)
    ti = lax.broadcasted_iota(jnp.int32, (tm, tm), 0)
    si = lax.broadcasted_iota(jnp.int32, (tm, tm), 1)
    before = _dot(jnp.where(si < ti, 1.0, 0.0), onehot) + counts_ref[...]
    r1 = jnp.sum(jnp.where(pick1, before, 0.0), axis=-1, keepdims=True)
    r2 = jnp.sum(jnp.where(pick2, before, 0.0), axis=-1, keepdims=True)
    counts_ref[...] = counts_ref[...] + jnp.sum(onehot, axis=0, keepdims=True)

    route_ref[...] = jnp.where(lane == 0, i1 - N_GROUPS,
                     jnp.where(lane == 1, i2 - N_GROUPS,
                     jnp.where(lane == 2, w1,
                     jnp.where(lane == 3, w2,
                     jnp.where(lane == 4, r1,
                     jnp.where(lane == 5, r2, 0.0))))))


def _out_proj(att, rw, x2, ga, woa, wor, g2, wrh, wrm, br, tm):
    t, d = x2.shape
    wa = att.shape[1]
    wr = rw.shape[1]
    sr = d // SLAB_LANES
    c = lambda shape: pl.BlockSpec(shape, lambda i: (0, 0))
    return pl.pallas_call(
        _out_proj_kernel,
        out_shape=(jax.ShapeDtypeStruct((t, d), F32), jax.ShapeDtypeStruct((t * sr, SLAB_LANES), F32),
                   jax.ShapeDtypeStruct((t, LANES), F32), jax.ShapeDtypeStruct((1, LANES), F32)),
        grid=(t // tm,),
        in_specs=[pl.BlockSpec((tm, wa), lambda i: (i, 0)),
                  pl.BlockSpec((tm, wr), lambda i: (i, 0)),
                  pl.BlockSpec((tm, d), lambda i: (i, 0)),
                  c((1, wa)), c((wa, d)), c((wr, d)), c((1, d)), c((d, LANES)), c((d, LANES)), c((1, LANES))],
        out_specs=(pl.BlockSpec((tm, d), lambda i: (i, 0)), pl.BlockSpec((tm * sr, SLAB_LANES), lambda i: (i, 0)),
                   pl.BlockSpec((tm, LANES), lambda i: (i, 0)), c((1, LANES))),
        compiler_params=pltpu.CompilerParams(
            dimension_semantics=("arbitrary",), vmem_limit_bytes=VMEM_LIMIT),
        name="out_proj",
    )(att, rw, x2, ga, woa, wor, g2, wrh, wrm, br)


def _store_slabs(ref, x):
    rows, d = x.shape
    sr = d // SLAB_LANES
    for s in range(sr):
        ref[pl.ds(s, rows, stride=sr), :] = x[:, s * SLAB_LANES:(s + 1) * SLAB_LANES]


def _load_slabs(ref, slot, first, rows, sr):
    return jnp.concatenate([ref[slot, pl.ds(first * sr + s, rows, stride=sr), :] for s in range(sr)], axis=1)


def _gather_rows(i, nt, idx_hbm, src_hbm, idx_ref, buf_ref, isem, gsem, rows, sr):
    def idx_copy(tile, slot):
        return pltpu.make_async_copy(idx_hbm.at[pl.ds(tile, 1)], idx_ref.at[pl.ds(slot, 1)], isem.at[slot])

    def start_rows(slot):
        for r in range(rows):
            src = pl.multiple_of(idx_ref[slot, r], sr)
            pltpu.make_async_copy(src_hbm.at[pl.ds(src, sr)], buf_ref.at[slot, pl.ds(r * sr, sr)],
                                  gsem.at[slot]).start()

    slot = jnp.bitwise_and(i, 1)

    @pl.when(i == 0)
    def _():
        idx_copy(0, 0).start()
        idx_copy(0, 0).wait()
        start_rows(0)
        if nt > 1:
            idx_copy(1, 1).start()

    @pl.when(i + 1 < nt)
    def _():
        idx_copy(i + 1, 1 - slot).wait()
        start_rows(1 - slot)

    @pl.when(i + 2 < nt)
    def _():
        idx_copy(i + 2, slot).start()

    pltpu.make_async_copy(src_hbm.at[pl.ds(0, rows * sr)], buf_ref.at[slot], gsem.at[slot]).wait()
    return slot


def _dispatch_kernel(idx_hbm, x_hbm, o_hbm, idx_ref, tile_ref, zero_ref, isem, lsem, csem,
                     *, nt, n_tok, n_pad, sr):
    i = pl.program_id(0)
    slot = jnp.bitwise_and(i, 1)
    other = 1 - slot
    tg = n_tok // TOP_K
    n_all = n_tok + n_pad
    stage = lax.rem(i, 3)

    def idx_copy(tile, s):
        return pltpu.make_async_copy(idx_hbm.at[pl.ds(tile, 1)], idx_ref.at[pl.ds(s, 1)], isem.at[s])

    def load(tile, s):
        return pltpu.make_async_copy(x_hbm.at[pl.ds(tile * tg * sr, tg * sr)], tile_ref.at[s], lsem.at[s])

    def copies_done(s):
        return pltpu.make_async_copy(x_hbm.at[pl.ds(0, n_all * sr)], o_hbm.at[pl.ds(0, n_all * sr)], csem.at[s])

    @pl.when(i == 0)
    def _():
        zero_ref[...] = jnp.zeros_like(zero_ref)
        idx_copy(0, 0).start()
        load(0, 0).start()

    idx_copy(i, slot).wait()
    load(i, stage).wait()

    @pl.when(i + 1 < nt)
    def _():
        idx_copy(i + 1, other).start()
        load(i + 1, lax.rem(i + 1, 3)).start()

    for r in range(n_tok):
        dst = pl.multiple_of(idx_ref[slot, r], sr)
        pltpu.make_async_copy(tile_ref.at[stage, pl.ds((r % tg) * sr, sr)], o_hbm.at[pl.ds(dst, sr)],
                              csem.at[slot]).start()
    for r in range(n_tok, n_all):
        dst = pl.multiple_of(idx_ref[slot, r], sr)
        pltpu.make_async_copy(zero_ref, o_hbm.at[pl.ds(dst, sr)], csem.at[slot]).start()

    @pl.when(i > 0)
    def _():
        copies_done(other).wait()

    @pl.when(i == nt - 1)
    def _():
        copies_done(slot).wait()


def _dispatch(idx, hn_slabs, n_rows, sr):
    nt, n_all = idx.shape
    n_tok = TOP_K * COMBINE_ROWS
    hbm = pl.BlockSpec(memory_space=pl.ANY)
    return pl.pallas_call(
        functools.partial(_dispatch_kernel, nt=nt, n_tok=n_tok, n_pad=n_all - n_tok, sr=sr),
        out_shape=jax.ShapeDtypeStruct((n_rows * sr, SLAB_LANES), F32),
        grid=(nt,),
        in_specs=[hbm, hbm],
        out_specs=hbm,
        scratch_shapes=[pltpu.SMEM((2, n_all), jnp.int32),
                        pltpu.VMEM((3, COMBINE_ROWS * sr, SLAB_LANES), F32), pltpu.VMEM((sr, SLAB_LANES), F32),
                        pltpu.SemaphoreType.DMA((2,)), pltpu.SemaphoreType.DMA((3,)),
                        pltpu.SemaphoreType.DMA((2,))],
        compiler_params=pltpu.CompilerParams(dimension_semantics=("arbitrary",)),
        name="dispatch",
    )(idx, hn_slabs)


def _expert_kernel(te_ref, first_ref, next_ref, nu_ref, x_ref, w1_hbm, w3_hbm, w2_hbm, o_ref,
                   stage_refs, wb_refs, wsem, *, sr):
    i = pl.program_id(0)
    tm = EXPERT_ROWS
    w_hbm = (w1_hbm, w3_hbm, w2_hbm)
    used = i < nu_ref[0]

    def weight_copies(e):
        out = []
        for j, (w, st) in enumerate(zip(w_hbm, stage_refs)):
            half = st.shape[0] // 2
            for q in range(2):
                rows = pl.ds(q * half, half)
                out.append((pltpu.make_async_copy(w.at[e, rows], st.at[rows], wsem.at[2 * j + q]), q))
        return out

    @pl.when(i == 0)
    def _():
        for cp, q in weight_copies(te_ref[0]):
            cp.start(priority=q)

    def cast_stage(j):
        st, wb = stage_refs[j], wb_refs[j]
        rows = st.shape[0] // WEIGHT_CAST_CHUNKS
        for c in range(WEIGHT_CAST_CHUNKS):
            wb[c * rows:(c + 1) * rows, :] = st[c * rows:(c + 1) * rows, :].astype(BF16)

    def tile_mlp(before_down_proj=None):
        xb = jnp.concatenate([x_ref[pl.ds(s, tm, stride=sr), :] for s in range(sr)], axis=1).astype(BF16)
        h1 = jnp.dot(xb, wb_refs[0][...], preferred_element_type=F32)
        h3 = jnp.dot(xb, wb_refs[1][...], preferred_element_type=F32)
        if before_down_proj is not None:
            before_down_proj()
        hid = (h1 * _sigmoid(h1)) * h3
        _store_slabs(o_ref, jnp.dot(hid.astype(BF16), wb_refs[2][...], preferred_element_type=F32))

    is_first = first_ref[i] == 1

    @pl.when(used & is_first)
    def _():
        for cp, _ in weight_copies(te_ref[i]):
            cp.wait()
        cast_stage(0)
        cast_stage(1)

        def finish_weights():
            cast_stage(2)

            @pl.when(next_ref[i] >= 0)
            def _():
                for cp, q in weight_copies(next_ref[i]):
                    cp.start(priority=q)

        tile_mlp(finish_weights)

    @pl.when(used & jnp.logical_not(is_first))
    def _():
        tile_mlp()

    @pl.when(jnp.logical_not(used))
    def _():
        o_ref[...] = jnp.zeros_like(o_ref)


def _experts(tile_e, tile_first, tile_next, n_used, x_slabs, w1, w3, w2):
    nt = tile_e.shape[0]
    d, de = w1.shape[1], w1.shape[2]
    sr = d // SLAB_LANES
    tm = EXPERT_ROWS
    hbm = pl.BlockSpec(memory_space=pl.ANY)
    grid_spec = pltpu.PrefetchScalarGridSpec(
        num_scalar_prefetch=4,
        grid=(nt,),
        in_specs=[pl.BlockSpec((tm * sr, SLAB_LANES), lambda i, te, fi, nx, nu: (jnp.minimum(i, nu[0] - 1), 0)),
                  hbm, hbm, hbm],
        out_specs=pl.BlockSpec((tm * sr, SLAB_LANES), lambda i, *_: (i, 0)),
        scratch_shapes=[[pltpu.VMEM((d, de), F32), pltpu.VMEM((d, de), F32), pltpu.VMEM((de, d), F32)],
                        [pltpu.VMEM((d, de), BF16), pltpu.VMEM((d, de), BF16), pltpu.VMEM((de, d), BF16)],
                        pltpu.SemaphoreType.DMA((6,))],
    )
    return pl.pallas_call(
        functools.partial(_expert_kernel, sr=sr),
        out_shape=jax.ShapeDtypeStruct((nt * tm * sr, SLAB_LANES), F32),
        grid_spec=grid_spec,
        compiler_params=pltpu.CompilerParams(
            dimension_semantics=("arbitrary",), vmem_limit_bytes=EXPERT_VMEM_LIMIT),
        name="experts",
    )(tile_e, tile_first, tile_next, n_used, x_slabs, w1, w3, w2)


def _combine_kernel(dest_hbm, y_hbm, x1_ref, route_ref, fg_ref, o_ref, idx_ref, buf_ref, isem, gsem,
                    *, nt, sr):
    i = pl.program_id(0)
    tg = COMBINE_ROWS
    slot = _gather_rows(i, nt, dest_hbm, y_hbm, idx_ref, buf_ref, isem, gsem, TOP_K * tg, sr)
    route = route_ref[...]
    y = (_load_slabs(buf_ref, slot, 0, tg, sr) * route[:, 2:3]
         + _load_slabs(buf_ref, slot, tg, tg, sr) * route[:, 3:4])
    o_ref[...] = _rms(x1_ref[...] + y, fg_ref[...])


def _combine(dest, ybuf, x1, route, fg):
    t, d = x1.shape
    tg = COMBINE_ROWS
    sr = d // SLAB_LANES
    return pl.pallas_call(
        functools.partial(_combine_kernel, nt=t // tg, sr=sr),
        out_shape=jax.ShapeDtypeStruct((t, d), F32),
        grid=(t // tg,),
        in_specs=[pl.BlockSpec(memory_space=pl.ANY),
                  pl.BlockSpec(memory_space=pl.ANY),
                  pl.BlockSpec((tg, d), lambda i: (i, 0)),
                  pl.BlockSpec((tg, LANES), lambda i: (i, 0)),
                  pl.BlockSpec((1, d), lambda i: (0, 0))],
        out_specs=pl.BlockSpec((tg, d), lambda i: (i, 0)),
        scratch_shapes=[pltpu.SMEM((2, TOP_K * tg), jnp.int32),
                        pltpu.VMEM((2, TOP_K * tg * sr, SLAB_LANES), F32),
                        pltpu.SemaphoreType.DMA((2,)), pltpu.SemaphoreType.DMA((2,))],
        compiler_params=pltpu.CompilerParams(
            dimension_semantics=("arbitrary",), vmem_limit_bytes=VMEM_LIMIT),
        name="combine",
    )(dest, ybuf, x1, route, fg)


def _pad_cols(m, n):
    return jnp.pad(m, ((0, 0), (0, n - m.shape[1])))


def _pad_rows(m, n):
    return jnp.pad(m, ((0, n - m.shape[0]), (0, 0)))


def _pick_tile(n, cap):
    for unit in (MXU_WIDTH, LANES):
        fits = [c for c in range(unit, cap + 1, unit) if n % c == 0]
        if fits:
            return fits[-1]
    raise ValueError(f"no lane-aligned tile divides {n}")


def _dispatch_plan(experts, rank, counts, n_tok):
    tm = EXPERT_ROWS
    nt = n_tok * TOP_K // tm + N_EXPERTS
    tiles_e = (counts + tm - 1) // tm
    tile_end = jnp.cumsum(tiles_e)
    tile_start = tile_end - tiles_e
    dest = (tile_start[experts] * tm + rank).astype(jnp.int32)
    n_used = tile_end[-1:].astype(jnp.int32)
    tiles = jnp.arange(nt, dtype=jnp.int32)
    tile_e = jnp.minimum(jnp.sum((tile_end[None, :] <= tiles[:, None]).astype(jnp.int32), axis=1),
                         N_EXPERTS - 1).astype(jnp.int32)
    tile_first = ((tiles == 0) | (tile_e != jnp.roll(tile_e, 1))).astype(jnp.int32)
    next_start = tile_end[tile_e]
    tile_next = jnp.where(next_start < n_used[0], tile_e[jnp.minimum(next_start, nt - 1)], -1).astype(jnp.int32)

    seg_len = jnp.concatenate([tiles_e * tm - counts, (nt - n_used) * tm])
    seg_first = jnp.concatenate([tile_start * tm + counts, n_used * tm])
    seg_stop = jnp.cumsum(seg_len)
    j = jnp.arange(N_EXPERTS * tm, dtype=jnp.int32)
    seg = jnp.sum((seg_stop[None, :] <= j[:, None]).astype(jnp.int32), axis=1)
    pad_rows = (seg_first[seg] + j - (seg_stop - seg_len)[seg]).astype(jnp.int32)
    return tile_e, tile_first, tile_next, n_used, dest, pad_rows


def _layer(x2, b, s, norm1_g, w_in, attn_out_g, mu, w0, w2, a0, a2, g2, k_k, k_a, r_k, lnx_w, lnx_b,
           w_out, norm2_g, wg, bg, we, be, ew1, ew3, ew2, final_g):
    t, d = x2.shape
    aw = attn_out_g.shape[0]
    cw = w0.shape[0]
    heads = aw // ATT_HEAD_DIM
    dl, al, gl = w2.shape[0], a2.shape[0], g2.shape[0]
    pw, pa, pg = _round_up(dl, LANES), _round_up(al, LANES), _round_up(gl, LANES)
    wpad = 3 * cw + pw + pa + pg

    w_in = w_in.astype(BF16)
    sh = w_in[:, 3 * aw:]
    o = 3 * cw
    w_arr = jnp.concatenate(
        [sh[:, :o], _pad_cols(sh[:, o:o + dl], pw), _pad_cols(sh[:, o + dl:o + dl + al], pa),
         _pad_cols(sh[:, o + dl + al:o + dl + al + gl], pg), w_in[:, :3 * aw]], axis=1)
    mu_arr = jnp.concatenate(
        [mu[:o], jnp.pad(mu[o:o + dl], (0, pw - dl)), jnp.pad(mu[o + dl:o + dl + al], (0, pa - al)),
         jnp.pad(mu[o + dl + al:], (0, pg - gl))])[None, :]

    n_all = w_arr.shape[1]
    proj = _in_proj(x2, norm1_g[None, :], w_arr, tm=min(512, t), tn=_pick_tile(n_all, 4096))

    att = _moba(proj, b, s, heads, wpad // LANES)
    r1 = lambda a: a.reshape(1, -1)
    rw = _rwkv(proj, b, s, cw, wpad,
               (mu_arr, r1(w0), _pad_rows(w2, pw).astype(BF16), r1(a0), _pad_rows(a2, pa).astype(BF16),
                _pad_rows(g2, pg).astype(BF16), r1(k_k), r1(k_a), r1(r_k), r1(lnx_w), r1(lnx_b)))

    wr = _pad_cols(jnp.concatenate([wg, we], axis=1), LANES)
    wrh = wr.astype(BF16)
    wrm = (wr - wrh.astype(F32)).astype(BF16)
    br = jnp.pad(jnp.concatenate([bg, be]), (0, LANES - N_GROUPS - N_EXPERTS))[None, :]
    wo = w_out.astype(BF16)
    x1, hn, route, counts = _out_proj(att, rw, x2, r1(attn_out_g), wo[:aw], wo[aw:], r1(norm2_g), wrh, wrm, br,
                                      tm=min(512, t))

    plan_in = route[:, :6].astype(jnp.int32)
    counts = counts[0, N_GROUPS:N_GROUPS + N_EXPERTS].astype(jnp.int32)
    tg = COMBINE_ROWS
    n_tiles = t // tg
    pad_per_tile = N_EXPERTS * EXPERT_ROWS // n_tiles
    tile_e, tile_first, tile_next, n_used, dest, pad_rows = _dispatch_plan(
        plan_in[:, 0:TOP_K], plan_in[:, 4:4 + TOP_K], counts, t)

    sr = d // SLAB_LANES
    dest_tiles = dest.reshape(n_tiles, tg, TOP_K).transpose(0, 2, 1).reshape(n_tiles, TOP_K * tg) * sr
    disp_idx = jnp.concatenate([dest_tiles, pad_rows.reshape(n_tiles, pad_per_tile) * sr], axis=1)
    xbuf = _dispatch(disp_idx, hn, tile_e.shape[0] * EXPERT_ROWS, sr)
    ybuf = _experts(tile_e, tile_first, tile_next, n_used, xbuf, ew1, ew3, ew2)
    return _combine(dest_tiles, ybuf, x1, route, r1(final_g))


def kernel(x, norm1_g, w_in, attn_out_g, rwkv_mu, rwkv_w0, rwkv_w2, rwkv_a0, rwkv_a2, rwkv_g2, rwkv_k_k,
           rwkv_k_a, rwkv_r_k, rwkv_lnx_w, rwkv_lnx_b, w_out, norm2_g, router_group_w, router_group_b,
           router_expert_w, router_expert_b, expert_w1, expert_w3, expert_w2, final_g):
    b, s, d = x.shape
    assert norm1_g.shape[0] == 1, "single-layer block"
    assert s % MOBA_BLOCK == 0 and s % RWKV_CHUNK == 0
    assert (b * s) % max(EXPERT_ROWS, COMBINE_ROWS) == 0
    assert 2 * RWKV_CHUNK == LANES and RWKV_CHUNK == RWKV_HEAD_DIM, "rwkv packs two CxC head matrices per vreg row"
    assert (N_EXPERTS * EXPERT_ROWS) % ((b * s) // COMBINE_ROWS) == 0, "zero-fill rows split evenly over tiles"
    out = _layer(x.reshape(b * s, d), b, s, norm1_g[0], w_in[0], attn_out_g[0], rwkv_mu[0], rwkv_w0[0],
                 rwkv_w2[0], rwkv_a0[0], rwkv_a2[0], rwkv_g2[0], rwkv_k_k[0], rwkv_k_a[0], rwkv_r_k[0],
                 rwkv_lnx_w[0], rwkv_lnx_b[0], w_out[0], norm2_g[0], router_group_w[0], router_group_b[0],
                 router_expert_w[0], router_expert_b[0], expert_w1[0], expert_w3[0], expert_w2[0], final_g)
    return out.reshape(b, s, d)
```

```python
import functools
import math

import jax
import jax.numpy as jnp
from jax import lax
from jax.experimental import pallas as pl
from jax.experimental.pallas import tpu as pltpu

F32 = jnp.float32
BF16 = jnp.bfloat16

LANES = 128
MXU_WIDTH = 256
BF16_SUBLANES = 16
SLAB_LANES = 128
ATT_HEAD_DIM = 128
RWKV_HEAD_DIM = 64
MOBA_BLOCK = 256
MOBA_TOPK = 3
N_GROUPS = 4
EXPERTS_PER_GROUP = 8
N_EXPERTS = N_GROUPS * EXPERTS_PER_GROUP
TOP_K = 2
RMS_EPS = 1e-6
GN_EPS = 64e-5
NEG_BIG = -1e30
RWKV_CHUNK = 64
RWKV_CHUNKS_PER_STEP = 4
RWKV_CHUNK_LAG = 8
EXPERT_ROWS = 256
COMBINE_ROWS = 256
VMEM_LIMIT = 56 * 1024 * 1024
EXPERT_VMEM_LIMIT = 60 * 1024 * 1024
WEIGHT_CAST_CHUNKS = 16


def _round_up(n, m):
    return (n + m - 1) // m * m


def _dot(a, b):
    return jnp.dot(a.astype(BF16), b.astype(BF16), preferred_element_type=F32)


def _dot_nt(a, b):
    return lax.dot_general(a.astype(BF16), b.astype(BF16), (((1,), (1,)), ((), ())),
                           preferred_element_type=F32)


def _split3(a):
    hi = a.astype(BF16)
    r1 = a - hi.astype(F32)
    mid = r1.astype(BF16)
    lo = (r1 - mid.astype(F32)).astype(BF16)
    return hi, mid, lo


def _dot_hi(a, b):
    ah, am, _ = _split3(a)
    bh, bm, _ = _split3(b)
    d = functools.partial(jnp.dot, preferred_element_type=F32)
    return d(ah, bh) + (d(am, bh) + d(ah, bm))


def _sigmoid(x):
    return 1.0 / (1.0 + jnp.exp(-x))


def _rms(x, g):
    ms = jnp.mean(x * x, axis=-1, keepdims=True)
    return x * lax.rsqrt(ms + RMS_EPS) * g


def _in_proj_kernel(x_ref, g_ref, w_ref, o_ref):
    xn = _rms(x_ref[...], g_ref[...]).astype(BF16)
    o_ref[...] = jnp.dot(xn, w_ref[...], preferred_element_type=F32)


def _in_proj(x2, g, w, tm, tn):
    t, d = x2.shape
    n = w.shape[1]
    return pl.pallas_call(
        _in_proj_kernel,
        out_shape=jax.ShapeDtypeStruct((t, n), F32),
        grid=(n // tn, t // tm),
        in_specs=[pl.BlockSpec((tm, d), lambda j, i: (i, 0)),
                  pl.BlockSpec((1, d), lambda j, i: (0, 0)),
                  pl.BlockSpec((d, tn), lambda j, i: (0, j))],
        out_specs=pl.BlockSpec((tm, tn), lambda j, i: (i, j)),
        compiler_params=pltpu.CompilerParams(
            dimension_semantics=("arbitrary", "arbitrary"), vmem_limit_bytes=VMEM_LIMIT),
        name="in_proj",
    )(x2, g, w)


def _moba_kernel(q_ref, k_ref, v_ref, o_ref, *, nblk, scale):
    bs = MOBA_BLOCK
    k = k_ref[...]
    kb = k.astype(BF16)
    d = v_ref.shape[1]
    vtb = jnp.concatenate([v_ref[...].T, jnp.ones((BF16_SUBLANES, v_ref.shape[0]), F32)], axis=0).astype(BF16)
    qt = q_ref[...].T
    qtb = qt.astype(BF16)
    kmean = jnp.concatenate(
        [jnp.sum(k[n * bs:(n + 1) * bs, :], axis=0, keepdims=True) for n in range(nblk)], axis=0) * (1.0 / bs)
    gate_all = _dot_hi(kmean, qt)
    kpos = lax.broadcasted_iota(jnp.int32, (bs, bs), 0)
    qpos = lax.broadcasted_iota(jnp.int32, (bs, bs), 1)
    causal = kpos <= qpos

    def scores(qi):
        return jnp.dot(kb[0:(qi + 1) * bs, :], qtb[:, qi * bs:(qi + 1) * bs], preferred_element_type=F32)

    exp_scale = scale * math.log2(math.e)
    s_next = scores(0)
    for qi in range(nblk):
        qs = slice(qi * bs, (qi + 1) * bs)
        nk = (qi + 1) * bs
        s_all = s_next
        if qi + 1 < nblk:
            s_next = scores(qi + 1)
        blocks = []
        if qi > MOBA_TOPK:
            gate = gate_all[0:qi, qs]
            blk = lax.broadcasted_iota(jnp.int32, (qi, bs), 0)
        for n in range(qi):
            sn = s_all[n * bs:(n + 1) * bs, :]
            if qi > MOBA_TOPK:
                gn = gate[n:n + 1, :]
                beats = (gate > gn) | ((gate == gn) & (blk < n))
                rank = jnp.sum(jnp.where(beats, 1.0, 0.0), axis=0, keepdims=True)
                sn = jnp.where(rank < float(MOBA_TOPK), sn, NEG_BIG)
            blocks.append(sn)
        blocks.append(jnp.where(causal, s_all[qi * bs:nk, :], NEG_BIG))
        m = functools.reduce(jnp.maximum, [jnp.max(x, axis=0, keepdims=True) for x in blocks])
        pb = jnp.concatenate([jnp.exp2((x - m) * exp_scale).astype(BF16) for x in blocks], axis=0)
        acc = jnp.dot(vtb[:, 0:nk], pb, preferred_element_type=F32)
        o_ref[qs, :] = (acc[0:d, :] / acc[d:d + 1, :]).T


def _moba(proj, b, s, heads, col0):
    d = ATT_HEAD_DIM
    nblk = s // MOBA_BLOCK
    kern = functools.partial(_moba_kernel, nblk=nblk, scale=1.0 / math.sqrt(d))
    return pl.pallas_call(
        kern,
        out_shape=jax.ShapeDtypeStruct((b * s, heads * d), F32),
        grid=(b, heads),
        in_specs=[pl.BlockSpec((s, d), lambda bi, h: (bi, col0 + h)),
                  pl.BlockSpec((s, d), lambda bi, h: (bi, col0 + heads + h)),
                  pl.BlockSpec((s, d), lambda bi, h: (bi, col0 + 2 * heads + h))],
        out_specs=pl.BlockSpec((s, d), lambda bi, h: (bi, h)),
        compiler_params=pltpu.CompilerParams(
            dimension_semantics=("arbitrary", "arbitrary"), vmem_limit_bytes=VMEM_LIMIT),
        name="moba",
    )(proj, proj, proj)


def _head_sum(x, lo):
    s0 = jnp.sum(jnp.where(lo, x, 0.0), axis=-1, keepdims=True)
    s1 = jnp.sum(jnp.where(lo, 0.0, x), axis=-1, keepdims=True)
    return jnp.where(lo, s0, s1)


def _stack_heads(x, lo):
    return jnp.concatenate([jnp.where(lo, x, 0.0), jnp.where(lo, 0.0, x)], axis=0)


def _rwkv_kernel(p_ref, mu_ref, w0_ref, w2_ref, a0_ref, a2_ref, g2_ref, kk_ref, ka_ref, rk_ref,
                 lnw_ref, lnb_ref, o_ref, carry_ref, state_ref, *, n_sub, **dims):
    @pl.when(pl.program_id(1) == 0)
    def _():
        carry_ref[...] = jnp.zeros_like(carry_ref)
        state_ref[...] = jnp.zeros_like(state_ref)

    C = RWKV_CHUNK
    chunks = [_rwkv_chunk(p_ref.at[pl.ds(h * C, C)], mu_ref, w0_ref, w2_ref, a0_ref, a2_ref, g2_ref, kk_ref,
                          ka_ref, rk_ref, lnw_ref, lnb_ref, o_ref.at[pl.ds(h * C, C)], carry_ref, state_ref,
                          **dims) for h in range(n_sub)]
    _interleave(chunks, RWKV_CHUNK_LAG)


def _interleave(chunks, lag):
    n = len(chunks)
    steps, done, wrote, waiting = [0] * n, [False] * n, [False] * n, [False] * n

    def advance(j):
        if waiting[j] and j > 0:
            while not (wrote[j - 1] or done[j - 1]):
                advance(j - 1)
        waiting[j] = False
        try:
            tag = next(chunks[j])
        except StopIteration:
            done[j] = True
            return
        steps[j] += 1
        waiting[j] = tag == "need_state"
        wrote[j] = wrote[j] or tag == "state_done"

    while not all(done):
        for j in range(n):
            if not done[j] and (j == 0 or done[j - 1] or steps[j - 1] >= lag):
                advance(j)


def _rwkv_chunk(p_ref, mu_ref, w0_ref, w2_ref, a0_ref, a2_ref, g2_ref, kk_ref, ka_ref, rk_ref,
                lnw_ref, lnb_ref, o_ref, carry_ref, state_ref, *, cw, pw, pa, pg):
    C = RWKV_CHUNK
    C2 = 2 * C
    pr = p_ref[...]
    row = lax.broadcasted_iota(jnp.int32, pr.shape, 0)
    prev = jnp.where(row == 0, carry_ref[...], pltpu.roll(pr, 1, 0))
    carry_ref[...] = pr[C - 1:C, :]
    ps = pr + (prev - pr) * mu_ref[...]
    yield "step"

    r = ps[:, 0:cw]
    k = ps[:, cw:2 * cw]
    v = ps[:, 2 * cw:3 * cw]
    o = 3 * cw
    xw = ps[:, o:o + pw]
    xa = ps[:, o + pw:o + pw + pa]
    xg = ps[:, o + pw + pa:o + pw + pa + pg]

    z = -(w0_ref[...] + _dot(jnp.tanh(xw), w2_ref[...]))
    softplus = jnp.maximum(z, 0.0) + jnp.log(1.0 + jnp.exp(-jnp.abs(z)))
    logw = -jnp.exp(-softplus - 0.5)
    asig = _sigmoid(a0_ref[...] + _dot(xa, a2_ref[...]))
    gate = _dot(_sigmoid(xg), g2_ref[...])
    kkr = k * kk_ref[...]
    kmod = k * (1.0 + (asig - 1.0) * ka_ref[...])
    yield "step"

    ti = lax.broadcasted_iota(jnp.int32, (C, C), 0)
    si = lax.broadcasted_iota(jnp.int32, (C, C), 1)
    tril = jnp.where(si <= ti, 1.0, 0.0).astype(BF16)
    lh, lm, ll = _split3(logw)
    d32 = functools.partial(jnp.dot, preferred_element_type=F32)
    cum = d32(tril, lh) + (d32(tril, lm) + d32(tril, ll))
    cum_end = cum[C - 1:C, :]
    e_prev = jnp.exp(cum - logw)
    e_incl = jnp.exp(cum)
    e_inv = jnp.exp(-cum)
    g_end = jnp.exp(cum_end)
    e_end = g_end * e_inv
    yield "step"

    lane = lax.broadcasted_iota(jnp.int32, (C, LANES), 1)
    lo = lane < RWKV_HEAD_DIM
    tok = lax.broadcasted_iota(jnp.int32, (C, LANES), 0)
    src = jnp.bitwise_and(lane, RWKV_HEAD_DIM - 1)
    strict = src < tok
    incl = src <= tok
    eye = jnp.where(src == tok, 1.0, 0.0)
    rho = lax.broadcasted_iota(jnp.int32, (C2, C2), 0)
    sig = lax.broadcasted_iota(jnp.int32, (C2, C2), 1)
    same_head = (rho >= C) == (sig >= C)

    pairs = range(cw // LANES)
    sls = [slice(p * LANES, (p + 1) * LANES) for p in pairs]
    incl2 = jnp.concatenate([incl, incl], axis=1)
    strict2 = jnp.concatenate([strict, strict], axis=1)

    b2, r2, k2, v2, at, rt, v2s, gm = [], [], [], [], [], [], [], []
    for sl in sls:
        kk2 = kkr[:, sl]
        nrm = jnp.sqrt(_head_sum(kk2 * kk2, lo))
        kkn = kk2 / jnp.maximum(nrm, 1e-12)
        b2.append(kkn * asig[:, sl])
        r2.append(r[:, sl])
        k2.append(kmod[:, sl])
        v2.append(v[:, sl])
        at.append(-kkn * e_prev[:, sl])
        rt.append(r2[-1] * e_incl[:, sl])
        bt = b2[-1] * e_inv[:, sl]
        kt = k2[-1] * e_inv[:, sl]
        v2s.append(_stack_heads(v2[-1], lo))
        rhs = jnp.concatenate([_stack_heads(bt, lo), _stack_heads(kt, lo)], axis=0)
        gm.append(_dot_nt(jnp.concatenate([at[-1], rt[-1]], axis=0), rhs))
        yield "step"

    labk = [jnp.where(strict2, g[0:C, :], 0.0) for g in gm]
    mrbk = [jnp.where(incl2, g[C:C2, :], 0.0) for g in gm]
    lakv = [_dot(labk[p][:, C2:2 * C2], v2s[p]) for p in pairs]
    yield "step"

    lab = [m[:, 0:C2] for m in labk]
    tinv = [eye + m for m in lab]
    pw2 = [_dot(m, _stack_heads(m, lo)) for m in lab]
    yield "step"
    n_pow = int(math.log2(C))
    for k in range(1, n_pow):
        if k < n_pow - 1:
            prod = [_dot(jnp.concatenate([t, m], axis=0), _stack_heads(m, lo)) for t, m in zip(tinv, pw2)]
            tinv = [t + pr[0:C] for t, pr in zip(tinv, prod)]
            pw2 = [pr[C:C2] for pr in prod]
        else:
            tinv = [t + _dot(t, _stack_heads(m, lo)) for t, m in zip(tinv, pw2)]
        yield "step"

    yield "need_state"
    s0 = [state_ref[p] for p in pairs]
    ars = [_dot_nt(jnp.concatenate([at[p], rt[p]], axis=0), s0[p]) for p in pairs]
    yield "step"
    u2 = [_dot(t, _stack_heads(ars[p][0:C] + lakv[p], lo)) for p, t in zip(pairs, tinv)]
    yield "step"
    y2 = [ars[p][C:C2] + _dot(mrbk[p], jnp.concatenate([_stack_heads(u2[p], lo), v2s[p]], axis=0))
          for p in pairs]
    yield "step"

    for p, sl in zip(pairs, sls):
        uv = jnp.concatenate([u2[p], v2[p]], axis=0)
        bk = jnp.concatenate([b2[p] * e_end[:, sl], k2[p] * e_end[:, sl]], axis=0)
        upd = _dot(uv.T, bk)
        state_ref[p] = s0[p] * g_end[:, sl] + jnp.where(same_head, upd, 0.0)
    yield "state_done"

    for p, sl in zip(pairs, sls):
        mean = _head_sum(y2[p], lo) * (1.0 / RWKV_HEAD_DIM)
        yc = y2[p] - mean
        var = _head_sum(yc * yc, lo) * (1.0 / RWKV_HEAD_DIM)
        yn = yc * lax.rsqrt(var + GN_EPS) * lnw_ref[:, sl] + lnb_ref[:, sl]
        bonus = _head_sum(r2[p] * k2[p] * rk_ref[:, sl], lo) * v2[p]
        o_ref[:, sl] = (yn + bonus) * gate[:, sl]
        yield "step"


def _rwkv(proj, b, s, cw, wpad, params):
    (mu, w0, w2, a0, a2, g2, kk, ka, rk, lnw, lnb) = params
    n_sub = RWKV_CHUNKS_PER_STEP if s % (RWKV_CHUNKS_PER_STEP * RWKV_CHUNK) == 0 else 1
    C = n_sub * RWKV_CHUNK
    nch = s // C
    pw, pa, pg = w2.shape[0], a2.shape[0], g2.shape[0]
    kern = functools.partial(_rwkv_kernel, n_sub=n_sub, cw=cw, pw=pw, pa=pa, pg=pg)
    row = lambda n: pl.BlockSpec((1, n), lambda bi, c: (0, 0))
    full = lambda a: pl.BlockSpec(a.shape, lambda bi, c: (0, 0))
    return pl.pallas_call(
        kern,
        out_shape=jax.ShapeDtypeStruct((b * s, cw), F32),
        grid=(b, nch),
        in_specs=[pl.BlockSpec((C, wpad), lambda bi, c: (bi * nch + c, 0)),
                  row(wpad), row(cw), full(w2), row(cw), full(a2), full(g2),
                  row(cw), row(cw), row(cw), row(cw), row(cw)],
        out_specs=pl.BlockSpec((C, cw), lambda bi, c: (bi * nch + c, 0)),
        scratch_shapes=[pltpu.VMEM((1, wpad), F32),
                        pltpu.VMEM((cw // LANES, LANES, LANES), F32)],
        compiler_params=pltpu.CompilerParams(
            dimension_semantics=("arbitrary", "arbitrary"), vmem_limit_bytes=VMEM_LIMIT),
        name="rwkv",
    )(proj, mu, w0, w2, a0, a2, g2, kk, ka, rk, lnw, lnb)


def _out_proj_kernel(att_ref, rw_ref, x_ref, ga_ref, woa_ref, wor_ref, g2_ref, wrh_ref, wrm_ref, br_ref,
                     x1_ref, hn_ref, route_ref, counts_ref):
    att = _rms(att_ref[...], ga_ref[...])
    y = (jnp.dot(att.astype(BF16), woa_ref[...], preferred_element_type=F32)
         + jnp.dot(rw_ref[...].astype(BF16), wor_ref[...], preferred_element_type=F32))
    x1 = x_ref[...] + y
    x1_ref[...] = x1
    hn = _rms(x1, g2_ref[...])
    _store_slabs(hn_ref, _pack_bf16_pairs(hn))

    hh, hm, _ = _split3(hn)
    d32 = functools.partial(jnp.dot, preferred_element_type=F32)
    logits = d32(hh, wrh_ref[...]) + (d32(hm, wrh_ref[...]) + d32(hh, wrm_ref[...])) + br_ref[...]
    lane = lax.broadcasted_iota(jnp.int32, logits.shape, 1)
    lanef = lane.astype(F32)
    big = float(LANES)

    is_g = lane < N_GROUPS
    gmax = jnp.max(jnp.where(is_g, logits, -jnp.inf), axis=-1, keepdims=True)
    gidx = jnp.min(jnp.where(is_g & (logits == gmax), lanef, big), axis=-1, keepdims=True)
    gsum = jnp.sum(jnp.where(is_g, jnp.exp(logits - gmax), 0.0), axis=-1, keepdims=True)
    g_gate = 1.0 / gsum

    e_lo = N_GROUPS + EXPERTS_PER_GROUP * gidx
    in_grp = (lanef >= e_lo) & (lanef < e_lo + EXPERTS_PER_GROUP)
    e1 = jnp.max(jnp.where(in_grp, logits, -jnp.inf), axis=-1, keepdims=True)
    i1 = jnp.min(jnp.where(in_grp & (logits == e1), lanef, big), axis=-1, keepdims=True)
    rest = in_grp & (lanef != i1)
    e2 = jnp.max(jnp.where(rest, logits, -jnp.inf), axis=-1, keepdims=True)
    i2 = jnp.min(jnp.where(rest & (logits == e2), lanef, big), axis=-1, keepdims=True)
    t = jnp.exp(e2 - e1)
    w1 = g_gate * (1.0 / (1.0 + t))
    w2 = g_gate * (t / (1.0 + t))

    @pl.when(pl.program_id(0) == 0)
    def _():
        counts_ref[...] = jnp.zeros_like(counts_ref)

    tm = logits.shape[0]
    pick1 = lanef == i1
    pick2 = lanef == i2
    onehot = jnp.where(pick1 | pick2, 1.0, 0.0)
    ti = lax.broadcasted_iota(jnp.int32, (tm, tm), 0)
    si = lax.broadcasted_iota(jnp.int32, (tm, tm), 1)
    before = _dot(jnp.where(si < ti, 1.0, 0.0), onehot) + counts_ref[...]
    r1 = jnp.sum(jnp.where(pick1, before, 0.0), axis=-1, keepdims=True)
    r2 = jnp.sum(jnp.where(pick2, before, 0.0), axis=-1, keepdims=True)
    counts_ref[...] = counts_ref[...] + jnp.sum(onehot, axis=0, keepdims=True)

    route_ref[...] = jnp.where(lane == 0, i1 - N_GROUPS,
                     jnp.where(lane == 1, i2 - N_GROUPS,
                     jnp.where(lane == 2, w1,
                     jnp.where(lane == 3, w2,
                     jnp.where(lane == 4, r1,
                     jnp.where(lane == 5, r2, 0.0))))))


def _out_proj(att, rw, x2, ga, woa, wor, g2, wrh, wrm, br, tm):
    t, d = x2.shape
    wa = att.shape[1]
    wr = rw.shape[1]
    sr = d // 2 // SLAB_LANES
    c = lambda shape: pl.BlockSpec(shape, lambda i: (0, 0))
    return pl.pallas_call(
        _out_proj_kernel,
        out_shape=(jax.ShapeDtypeStruct((t, d), F32), jax.ShapeDtypeStruct((t * sr, SLAB_LANES), jnp.uint32),
                   jax.ShapeDtypeStruct((t, LANES), F32), jax.ShapeDtypeStruct((1, LANES), F32)),
        grid=(t // tm,),
        in_specs=[pl.BlockSpec((tm, wa), lambda i: (i, 0)),
                  pl.BlockSpec((tm, wr), lambda i: (i, 0)),
                  pl.BlockSpec((tm, d), lambda i: (i, 0)),
                  c((1, wa)), c((wa, d)), c((wr, d)), c((1, d)), c((d, LANES)), c((d, LANES)), c((1, LANES))],
        out_specs=(pl.BlockSpec((tm, d), lambda i: (i, 0)), pl.BlockSpec((tm * sr, SLAB_LANES), lambda i: (i, 0)),
                   pl.BlockSpec((tm, LANES), lambda i: (i, 0)), c((1, LANES))),
        compiler_params=pltpu.CompilerParams(
            dimension_semantics=("arbitrary",), vmem_limit_bytes=VMEM_LIMIT),
        name="out_proj",
    )(att, rw, x2, ga, woa, wor, g2, wrh, wrm, br)


def _pack_bf16_pairs(x):
    half = x.shape[1] // 2
    bits = pltpu.bitcast(x.astype(BF16).astype(F32), jnp.uint32)
    return (bits[:, half:] & jnp.uint32(0xFFFF0000)) | (bits[:, :half] >> jnp.uint32(16))


def _unpack_bf16_pairs(w):
    lo = pltpu.bitcast(w << jnp.uint32(16), F32)
    hi = pltpu.bitcast(w & jnp.uint32(0xFFFF0000), F32)
    return jnp.concatenate([lo, hi], axis=1).astype(BF16)


def _store_slabs(ref, x):
    rows, d = x.shape
    sr = d // SLAB_LANES
    for s in range(sr):
        ref[pl.ds(s, rows, stride=sr), :] = x[:, s * SLAB_LANES:(s + 1) * SLAB_LANES]


def _load_slabs(ref, slot, first, rows, sr):
    return jnp.concatenate([ref[slot, pl.ds(first * sr + s, rows, stride=sr), :] for s in range(sr)], axis=1)


def _gather_rows(i, nt, idx_hbm, src_hbm, idx_ref, buf_ref, isem, gsem, rows, sr):
    def idx_copy(tile, slot):
        return pltpu.make_async_copy(idx_hbm.at[pl.ds(tile, 1)], idx_ref.at[pl.ds(slot, 1)], isem.at[slot])

    def start_rows(slot):
        for r in range(rows):
            src = pl.multiple_of(idx_ref[slot, r], sr)
            pltpu.make_async_copy(src_hbm.at[pl.ds(src, sr)], buf_ref.at[slot, pl.ds(r * sr, sr)],
                                  gsem.at[slot]).start()

    slot = jnp.bitwise_and(i, 1)

    @pl.when(i == 0)
    def _():
        idx_copy(0, 0).start()
        idx_copy(0, 0).wait()
        start_rows(0)
        if nt > 1:
            idx_copy(1, 1).start()

    @pl.when(i + 1 < nt)
    def _():
        idx_copy(i + 1, 1 - slot).wait()
        start_rows(1 - slot)

    @pl.when(i + 2 < nt)
    def _():
        idx_copy(i + 2, slot).start()

    pltpu.make_async_copy(src_hbm.at[pl.ds(0, rows * sr)], buf_ref.at[slot], gsem.at[slot]).wait()
    return slot


def _dispatch_kernel(idx_hbm, x_hbm, o_hbm, idx_ref, tile_ref, zero_ref, isem, lsem, csem,
                     *, nt, n_tok, n_pad, sr):
    i = pl.program_id(0)
    slot = jnp.bitwise_and(i, 1)
    other = 1 - slot
    tg = n_tok // TOP_K
    n_all = n_tok + n_pad
    stage = lax.rem(i, 3)

    def idx_copy(tile, s):
        return pltpu.make_async_copy(idx_hbm.at[pl.ds(tile, 1)], idx_ref.at[pl.ds(s, 1)], isem.at[s])

    def load(tile, s):
        return pltpu.make_async_copy(x_hbm.at[pl.ds(tile * tg * sr, tg * sr)], tile_ref.at[s], lsem.at[s])

    def copies_done(s):
        return pltpu.make_async_copy(x_hbm.at[pl.ds(0, n_all * sr)], o_hbm.at[pl.ds(0, n_all * sr)], csem.at[s])

    @pl.when(i == 0)
    def _():
        zero_ref[...] = jnp.zeros_like(zero_ref)
        idx_copy(0, 0).start()
        load(0, 0).start()

    idx_copy(i, slot).wait()
    load(i, stage).wait()

    @pl.when(i + 1 < nt)
    def _():
        idx_copy(i + 1, other).start()
        load(i + 1, lax.rem(i + 1, 3)).start()

    for r in range(n_tok):
        dst = pl.multiple_of(idx_ref[slot, r], sr)
        pltpu.make_async_copy(tile_ref.at[stage, pl.ds((r % tg) * sr, sr)], o_hbm.at[pl.ds(dst, sr)],
                              csem.at[slot]).start()
    for r in range(n_tok, n_all):
        dst = pl.multiple_of(idx_ref[slot, r], sr)
        pltpu.make_async_copy(zero_ref, o_hbm.at[pl.ds(dst, sr)], csem.at[slot]).start()

    @pl.when(i > 0)
    def _():
        copies_done(other).wait()

    @pl.when(i == nt - 1)
    def _():
        copies_done(slot).wait()


def _dispatch(idx, hn_slabs, n_rows, sr):
    nt, n_all = idx.shape
    n_tok = TOP_K * COMBINE_ROWS
    hbm = pl.BlockSpec(memory_space=pl.ANY)
    return pl.pallas_call(
        functools.partial(_dispatch_kernel, nt=nt, n_tok=n_tok, n_pad=n_all - n_tok, sr=sr),
        out_shape=jax.ShapeDtypeStruct((n_rows * sr, SLAB_LANES), hn_slabs.dtype),
        grid=(nt,),
        in_specs=[hbm, hbm],
        out_specs=hbm,
        scratch_shapes=[pltpu.SMEM((2, n_all), jnp.int32),
                        pltpu.VMEM((3, COMBINE_ROWS * sr, SLAB_LANES), hn_slabs.dtype),
                        pltpu.VMEM((sr, SLAB_LANES), hn_slabs.dtype),
                        pltpu.SemaphoreType.DMA((2,)), pltpu.SemaphoreType.DMA((3,)),
                        pltpu.SemaphoreType.DMA((2,))],
        compiler_params=pltpu.CompilerParams(dimension_semantics=("arbitrary",)),
        name="dispatch",
    )(idx, hn_slabs)


def _expert_kernel(te_ref, first_ref, next_ref, nu_ref, x_ref, w1_hbm, w3_hbm, w2_hbm, o_ref,
                   stage_refs, wb_refs, wsem, *, sx):
    i = pl.program_id(0)
    tm = EXPERT_ROWS
    w_hbm = (w1_hbm, w3_hbm, w2_hbm)
    used = i < nu_ref[0]

    def weight_copies(e):
        out = []
        for j, (w, st) in enumerate(zip(w_hbm, stage_refs)):
            half = st.shape[0] // 2
            for q in range(2):
                rows = pl.ds(q * half, half)
                out.append((pltpu.make_async_copy(w.at[e, rows], st.at[rows], wsem.at[2 * j + q]), q))
        return out

    @pl.when(i == 0)
    def _():
        for cp, q in weight_copies(te_ref[0]):
            cp.start(priority=q)

    def cast_stage(j):
        st, wb = stage_refs[j], wb_refs[j]
        rows = st.shape[0] // WEIGHT_CAST_CHUNKS
        for c in range(WEIGHT_CAST_CHUNKS):
            wb[c * rows:(c + 1) * rows, :] = st[c * rows:(c + 1) * rows, :].astype(BF16)

    def tile_mlp(before_down_proj=None):
        xb = _unpack_bf16_pairs(jnp.concatenate([x_ref[pl.ds(s, tm, stride=sx), :] for s in range(sx)], axis=1))
        h1 = jnp.dot(xb, wb_refs[0][...], preferred_element_type=F32)
        h3 = jnp.dot(xb, wb_refs[1][...], preferred_element_type=F32)
        if before_down_proj is not None:
            before_down_proj()
        hid = (h1 * _sigmoid(h1)) * h3
        _store_slabs(o_ref, jnp.dot(hid.astype(BF16), wb_refs[2][...], preferred_element_type=F32))

    is_first = first_ref[i] == 1

    @pl.when(used & is_first)
    def _():
        for cp, _ in weight_copies(te_ref[i]):
            cp.wait()
        cast_stage(0)
        cast_stage(1)

        def finish_weights():
            cast_stage(2)

            @pl.when(next_ref[i] >= 0)
            def _():
                for cp, q in weight_copies(next_ref[i]):
                    cp.start(priority=q)

        tile_mlp(finish_weights)

    @pl.when(used & jnp.logical_not(is_first))
    def _():
        tile_mlp()

    @pl.when(jnp.logical_not(used))
    def _():
        o_ref[...] = jnp.zeros_like(o_ref)


def _experts(tile_e, tile_first, tile_next, n_used, x_slabs, w1, w3, w2):
    nt = tile_e.shape[0]
    d, de = w1.shape[1], w1.shape[2]
    sr = d // SLAB_LANES
    sx = d // 2 // SLAB_LANES
    tm = EXPERT_ROWS
    hbm = pl.BlockSpec(memory_space=pl.ANY)
    grid_spec = pltpu.PrefetchScalarGridSpec(
        num_scalar_prefetch=4,
        grid=(nt,),
        in_specs=[pl.BlockSpec((tm * sx, SLAB_LANES), lambda i, te, fi, nx, nu: (jnp.minimum(i, nu[0] - 1), 0)),
                  hbm, hbm, hbm],
        out_specs=pl.BlockSpec((tm * sr, SLAB_LANES), lambda i, *_: (i, 0)),
        scratch_shapes=[[pltpu.VMEM((d, de), F32), pltpu.VMEM((d, de), F32), pltpu.VMEM((de, d), F32)],
                        [pltpu.VMEM((d, de), BF16), pltpu.VMEM((d, de), BF16), pltpu.VMEM((de, d), BF16)],
                        pltpu.SemaphoreType.DMA((6,))],
    )
    return pl.pallas_call(
        functools.partial(_expert_kernel, sx=sx),
        out_shape=jax.ShapeDtypeStruct((nt * tm * sr, SLAB_LANES), F32),
        grid_spec=grid_spec,
        compiler_params=pltpu.CompilerParams(
            dimension_semantics=("arbitrary",), vmem_limit_bytes=EXPERT_VMEM_LIMIT),
        name="experts",
    )(tile_e, tile_first, tile_next, n_used, x_slabs, w1, w3, w2)


def _combine_kernel(dest_hbm, y_hbm, x1_ref, route_ref, fg_ref, o_ref, idx_ref, buf_ref, isem, gsem,
                    *, nt, sr):
    i = pl.program_id(0)
    tg = COMBINE_ROWS
    slot = _gather_rows(i, nt, dest_hbm, y_hbm, idx_ref, buf_ref, isem, gsem, TOP_K * tg, sr)
    route = route_ref[...]
    y = (_load_slabs(buf_ref, slot, 0, tg, sr) * route[:, 2:3]
         + _load_slabs(buf_ref, slot, tg, tg, sr) * route[:, 3:4])
    o_ref[...] = _rms(x1_ref[...] + y, fg_ref[...])


def _combine(dest, ybuf, x1, route, fg):
    t, d = x1.shape
    tg = COMBINE_ROWS
    sr = d // SLAB_LANES
    return pl.pallas_call(
        functools.partial(_combine_kernel, nt=t // tg, sr=sr),
        out_shape=jax.ShapeDtypeStruct((t, d), F32),
        grid=(t // tg,),
        in_specs=[pl.BlockSpec(memory_space=pl.ANY),
                  pl.BlockSpec(memory_space=pl.ANY),
                  pl.BlockSpec((tg, d), lambda i: (i, 0)),
                  pl.BlockSpec((tg, LANES), lambda i: (i, 0)),
                  pl.BlockSpec((1, d), lambda i: (0, 0))],
        out_specs=pl.BlockSpec((tg, d), lambda i: (i, 0)),
        scratch_shapes=[pltpu.SMEM((2, TOP_K * tg), jnp.int32),
                        pltpu.VMEM((2, TOP_K * tg * sr, SLAB_LANES), F32),
                        pltpu.SemaphoreType.DMA((2,)), pltpu.SemaphoreType.DMA((2,))],
        compiler_params=pltpu.CompilerParams(
            dimension_semantics=("arbitrary",), vmem_limit_bytes=VMEM_LIMIT),
        name="combine",
    )(dest, ybuf, x1, route, fg)


def _pad_cols(m, n):
    return jnp.pad(m, ((0, 0), (0, n - m.shape[1])))


def _pad_rows(m, n):
    return jnp.pad(m, ((0, n - m.shape[0]), (0, 0)))


def _pick_tile(n, cap):
    for unit in (MXU_WIDTH, LANES):
        fits = [c for c in range(unit, cap + 1, unit) if n % c == 0]
        if fits:
            return fits[-1]
    raise ValueError(f"no lane-aligned tile divides {n}")


def _dispatch_plan(experts, rank, counts, n_tok):
    tm = EXPERT_ROWS
    nt = n_tok * TOP_K // tm + N_EXPERTS
    tiles_e = (counts + tm - 1) // tm
    tile_end = jnp.cumsum(tiles_e)
    tile_start = tile_end - tiles_e
    dest = (tile_start[experts] * tm + rank).astype(jnp.int32)
    n_used = tile_end[-1:].astype(jnp.int32)
    tiles = jnp.arange(nt, dtype=jnp.int32)
    tile_e = jnp.minimum(jnp.sum((tile_end[None, :] <= tiles[:, None]).astype(jnp.int32), axis=1),
                         N_EXPERTS - 1).astype(jnp.int32)
    tile_first = ((tiles == 0) | (tile_e != jnp.roll(tile_e, 1))).astype(jnp.int32)
    next_start = tile_end[tile_e]
    tile_next = jnp.where(next_start < n_used[0], tile_e[jnp.minimum(next_start, nt - 1)], -1).astype(jnp.int32)

    seg_len = jnp.concatenate([tiles_e * tm - counts, (nt - n_used) * tm])
    seg_first = jnp.concatenate([tile_start * tm + counts, n_used * tm])
    seg_stop = jnp.cumsum(seg_len)
    j = jnp.arange(N_EXPERTS * tm, dtype=jnp.int32)
    seg = jnp.sum((seg_stop[None, :] <= j[:, None]).astype(jnp.int32), axis=1)
    pad_rows = (seg_first[seg] + j - (seg_stop - seg_len)[seg]).astype(jnp.int32)
    return tile_e, tile_first, tile_next, n_used, dest, pad_rows


def _layer(x2, b, s, norm1_g, w_in, attn_out_g, mu, w0, w2, a0, a2, g2, k_k, k_a, r_k, lnx_w, lnx_b,
           w_out, norm2_g, wg, bg, we, be, ew1, ew3, ew2, final_g):
    t, d = x2.shape
    aw = attn_out_g.shape[0]
    cw = w0.shape[0]
    heads = aw // ATT_HEAD_DIM
    dl, al, gl = w2.shape[0], a2.shape[0], g2.shape[0]
    pw, pa, pg = _round_up(dl, LANES), _round_up(al, LANES), _round_up(gl, LANES)
    wpad = 3 * cw + pw + pa + pg

    w_in = w_in.astype(BF16)
    sh = w_in[:, 3 * aw:]
    o = 3 * cw
    w_arr = jnp.concatenate(
        [sh[:, :o], _pad_cols(sh[:, o:o + dl], pw), _pad_cols(sh[:, o + dl:o + dl + al], pa),
         _pad_cols(sh[:, o + dl + al:o + dl + al + gl], pg), w_in[:, :3 * aw]], axis=1)
    mu_arr = jnp.concatenate(
        [mu[:o], jnp.pad(mu[o:o + dl], (0, pw - dl)), jnp.pad(mu[o + dl:o + dl + al], (0, pa - al)),
         jnp.pad(mu[o + dl + al:], (0, pg - gl))])[None, :]

    n_all = w_arr.shape[1]
    proj = _in_proj(x2, norm1_g[None, :], w_arr, tm=min(512, t), tn=_pick_tile(n_all, 4096))

    att = _moba(proj, b, s, heads, wpad // LANES)
    r1 = lambda a: a.reshape(1, -1)
    rw = _rwkv(proj, b, s, cw, wpad,
               (mu_arr, r1(w0), _pad_rows(w2, pw).astype(BF16), r1(a0), _pad_rows(a2, pa).astype(BF16),
                _pad_rows(g2, pg).astype(BF16), r1(k_k), r1(k_a), r1(r_k), r1(lnx_w), r1(lnx_b)))

    wr = _pad_cols(jnp.concatenate([wg, we], axis=1), LANES)
    wrh = wr.astype(BF16)
    wrm = (wr - wrh.astype(F32)).astype(BF16)
    br = jnp.pad(jnp.concatenate([bg, be]), (0, LANES - N_GROUPS - N_EXPERTS))[None, :]
    wo = w_out.astype(BF16)
    x1, hn, route, counts = _out_proj(att, rw, x2, r1(attn_out_g), wo[:aw], wo[aw:], r1(norm2_g), wrh, wrm, br,
                                      tm=min(512, t))

    plan_in = route[:, :6].astype(jnp.int32)
    counts = counts[0, N_GROUPS:N_GROUPS + N_EXPERTS].astype(jnp.int32)
    tg = COMBINE_ROWS
    n_tiles = t // tg
    pad_per_tile = N_EXPERTS * EXPERT_ROWS // n_tiles
    tile_e, tile_first, tile_next, n_used, dest, pad_rows = _dispatch_plan(
        plan_in[:, 0:TOP_K], plan_in[:, 4:4 + TOP_K], counts, t)

    sr = d // SLAB_LANES
    sx = sr // 2
    dest_tiles = dest.reshape(n_tiles, tg, TOP_K).transpose(0, 2, 1).reshape(n_tiles, TOP_K * tg)
    disp_idx = jnp.concatenate([dest_tiles, pad_rows.reshape(n_tiles, pad_per_tile)], axis=1) * sx
    xbuf = _dispatch(disp_idx, hn, tile_e.shape[0] * EXPERT_ROWS, sx)
    ybuf = _experts(tile_e, tile_first, tile_next, n_used, xbuf, ew1, ew3, ew2)
    return _combine(dest_tiles * sr, ybuf, x1, route, r1(final_g))


def kernel(x, norm1_g, w_in, attn_out_g, rwkv_mu, rwkv_w0, rwkv_w2, rwkv_a0, rwkv_a2, rwkv_g2, rwkv_k_k,
           rwkv_k_a, rwkv_r_k, rwkv_lnx_w, rwkv_lnx_b, w_out, norm2_g, router_group_w, router_group_b,
           router_expert_w, router_expert_b, expert_w1, expert_w3, expert_w2, final_g):
    b, s, d = x.shape
    assert norm1_g.shape[0] == 1, "single-layer block"
    assert s % MOBA_BLOCK == 0 and s % RWKV_CHUNK == 0
    assert (b * s) % max(EXPERT_ROWS, COMBINE_ROWS) == 0
    assert 2 * RWKV_CHUNK == LANES and RWKV_CHUNK == RWKV_HEAD_DIM, "rwkv packs two CxC head matrices per vreg row"
    assert (N_EXPERTS * EXPERT_ROWS) % ((b * s) // COMBINE_ROWS) == 0, "zero-fill rows split evenly over tiles"
    out = _layer(x.reshape(b * s, d), b, s, norm1_g[0], w_in[0], attn_out_g[0], rwkv_mu[0], rwkv_w0[0],
                 rwkv_w2[0], rwkv_a0[0], rwkv_a2[0], rwkv_g2[0], rwkv_k_k[0], rwkv_k_a[0], rwkv_r_k[0],
                 rwkv_lnx_w[0], rwkv_lnx_b[0], w_out[0], norm2_g[0], router_group_w[0], router_group_b[0],
                 router_expert_w[0], router_expert_b[0], expert_w1[0], expert_w3[0], expert_w2[0], final_g)
    return out.reshape(b, s, d)
```

```python
import functools
import math

import jax
import jax.numpy as jnp
from jax import lax
from jax.experimental import pallas as pl
from jax.experimental.pallas import tpu as pltpu

F32 = jnp.float32
BF16 = jnp.bfloat16

LANES = 128
MXU_WIDTH = 256
BF16_SUBLANES = 16
SLAB_LANES = 128
ATT_HEAD_DIM = 128
RWKV_HEAD_DIM = 64
MOBA_BLOCK = 256
MOBA_TOPK = 3
N_GROUPS = 4
EXPERTS_PER_GROUP = 8
N_EXPERTS = N_GROUPS * EXPERTS_PER_GROUP
TOP_K = 2
RMS_EPS = 1e-6
GN_EPS = 64e-5
NEG_BIG = -1e30
RWKV_CHUNK = 64
RWKV_CHUNKS_PER_STEP = 4
RWKV_CHUNK_LAG = 5
EXPERT_ROWS = 256
COMBINE_ROWS = 256
VMEM_LIMIT = 56 * 1024 * 1024
EXPERT_VMEM_LIMIT = 60 * 1024 * 1024
WEIGHT_CAST_CHUNKS = 16


def _round_up(n, m):
    return (n + m - 1) // m * m


def _dot(a, b):
    return jnp.dot(a.astype(BF16), b.astype(BF16), preferred_element_type=F32)


def _dot_nt(a, b):
    return lax.dot_general(a.astype(BF16), b.astype(BF16), (((1,), (1,)), ((), ())),
                           preferred_element_type=F32)


def _split3(a):
    hi = a.astype(BF16)
    r1 = a - hi.astype(F32)
    mid = r1.astype(BF16)
    lo = (r1 - mid.astype(F32)).astype(BF16)
    return hi, mid, lo


def _dot_hi(a, b):
    ah, am, _ = _split3(a)
    bh, bm, _ = _split3(b)
    d = functools.partial(jnp.dot, preferred_element_type=F32)
    return d(ah, bh) + (d(am, bh) + d(ah, bm))


def _sigmoid(x):
    return 1.0 / (1.0 + jnp.exp(-x))


def _rms(x, g):
    ms = jnp.mean(x * x, axis=-1, keepdims=True)
    return x * lax.rsqrt(ms + RMS_EPS) * g


def _in_proj_kernel(x_ref, g_ref, w_ref, o_ref):
    xn = _rms(x_ref[...], g_ref[...]).astype(BF16)
    o_ref[...] = jnp.dot(xn, w_ref[...], preferred_element_type=F32)


def _in_proj(x2, g, w, tm, tn):
    t, d = x2.shape
    n = w.shape[1]
    return pl.pallas_call(
        _in_proj_kernel,
        out_shape=jax.ShapeDtypeStruct((t, n), F32),
        grid=(n // tn, t // tm),
        in_specs=[pl.BlockSpec((tm, d), lambda j, i: (i, 0)),
                  pl.BlockSpec((1, d), lambda j, i: (0, 0)),
                  pl.BlockSpec((d, tn), lambda j, i: (0, j))],
        out_specs=pl.BlockSpec((tm, tn), lambda j, i: (i, j)),
        compiler_params=pltpu.CompilerParams(
            dimension_semantics=("arbitrary", "arbitrary"), vmem_limit_bytes=VMEM_LIMIT),
        name="in_proj",
    )(x2, g, w)


def _moba_kernel(q_ref, k_ref, v_ref, o_ref, *, nblk, scale):
    bs = MOBA_BLOCK
    k = k_ref[...]
    kb = k.astype(BF16)
    d = v_ref.shape[1]
    vtb = jnp.concatenate([v_ref[...].T, jnp.ones((BF16_SUBLANES, v_ref.shape[0]), F32)], axis=0).astype(BF16)
    qt = q_ref[...].T
    qtb = qt.astype(BF16)
    kmean = jnp.concatenate(
        [jnp.sum(k[n * bs:(n + 1) * bs, :], axis=0, keepdims=True) for n in range(nblk)], axis=0) * (1.0 / bs)
    gate_all = _dot_hi(kmean, qt)
    kpos = lax.broadcasted_iota(jnp.int32, (bs, bs), 0)
    qpos = lax.broadcasted_iota(jnp.int32, (bs, bs), 1)
    causal = kpos <= qpos

    def scores(qi):
        return jnp.dot(kb[0:(qi + 1) * bs, :], qtb[:, qi * bs:(qi + 1) * bs], preferred_element_type=F32)

    exp_scale = scale * math.log2(math.e)
    s_next = scores(0)
    for qi in range(nblk):
        qs = slice(qi * bs, (qi + 1) * bs)
        nk = (qi + 1) * bs
        s_all = s_next
        if qi + 1 < nblk:
            s_next = scores(qi + 1)
        blocks = []
        if qi > MOBA_TOPK:
            gate = gate_all[0:qi, qs]
            blk = lax.broadcasted_iota(jnp.int32, (qi, bs), 0)
        for n in range(qi):
            sn = s_all[n * bs:(n + 1) * bs, :]
            if qi > MOBA_TOPK:
                gn = gate[n:n + 1, :]
                beats = (gate > gn) | ((gate == gn) & (blk < n))
                rank = jnp.sum(jnp.where(beats, 1.0, 0.0), axis=0, keepdims=True)
                sn = jnp.where(rank < float(MOBA_TOPK), sn, NEG_BIG)
            blocks.append(sn)
        blocks.append(jnp.where(causal, s_all[qi * bs:nk, :], NEG_BIG))
        m = functools.reduce(jnp.maximum, [jnp.max(x, axis=0, keepdims=True) for x in blocks])
        pb = jnp.concatenate([jnp.exp2((x - m) * exp_scale).astype(BF16) for x in blocks], axis=0)
        acc = jnp.dot(vtb[:, 0:nk], pb, preferred_element_type=F32)
        o_ref[qs, :] = (acc[0:d, :] / acc[d:d + 1, :]).T


def _moba(proj, b, s, heads, col0):
    d = ATT_HEAD_DIM
    nblk = s // MOBA_BLOCK
    kern = functools.partial(_moba_kernel, nblk=nblk, scale=1.0 / math.sqrt(d))
    return pl.pallas_call(
        kern,
        out_shape=jax.ShapeDtypeStruct((b * s, heads * d), F32),
        grid=(b, heads),
        in_specs=[pl.BlockSpec((s, d), lambda bi, h: (bi, col0 + h)),
                  pl.BlockSpec((s, d), lambda bi, h: (bi, col0 + heads + h)),
                  pl.BlockSpec((s, d), lambda bi, h: (bi, col0 + 2 * heads + h))],
        out_specs=pl.BlockSpec((s, d), lambda bi, h: (bi, h)),
        compiler_params=pltpu.CompilerParams(
            dimension_semantics=("arbitrary", "arbitrary"), vmem_limit_bytes=VMEM_LIMIT),
        name="moba",
    )(proj, proj, proj)


def _head_sum(x, lo):
    s0 = jnp.sum(jnp.where(lo, x, 0.0), axis=-1, keepdims=True)
    s1 = jnp.sum(jnp.where(lo, 0.0, x), axis=-1, keepdims=True)
    return jnp.where(lo, s0, s1)


def _stack_heads(x, lo):
    return jnp.concatenate([jnp.where(lo, x, 0.0), jnp.where(lo, 0.0, x)], axis=0)


def _rwkv_kernel(p_ref, mu_ref, w0_ref, w2_ref, a0_ref, a2_ref, g2_ref, kk_ref, ka_ref, rk_ref,
                 lnw_ref, lnb_ref, o_ref, carry_ref, state_ref, *, n_sub, **dims):
    @pl.when(pl.program_id(1) == 0)
    def _():
        carry_ref[...] = jnp.zeros_like(carry_ref)
        state_ref[...] = jnp.zeros_like(state_ref)

    C = RWKV_CHUNK
    chunks = [_rwkv_chunk(p_ref.at[pl.ds(h * C, C)], mu_ref, w0_ref, w2_ref, a0_ref, a2_ref, g2_ref, kk_ref,
                          ka_ref, rk_ref, lnw_ref, lnb_ref, o_ref.at[pl.ds(h * C, C)], carry_ref, state_ref,
                          **dims) for h in range(n_sub)]
    _interleave(chunks, RWKV_CHUNK_LAG)


def _interleave(chunks, lag):
    n = len(chunks)
    steps, done, wrote, waiting = [0] * n, [False] * n, [False] * n, [False] * n

    def advance(j):
        if waiting[j] and j > 0:
            while not (wrote[j - 1] or done[j - 1]):
                advance(j - 1)
        waiting[j] = False
        try:
            tag = next(chunks[j])
        except StopIteration:
            done[j] = True
            return
        steps[j] += 1
        waiting[j] = tag == "need_state"
        wrote[j] = wrote[j] or tag == "state_done"

    while not all(done):
        for j in range(n):
            if not done[j] and (j == 0 or done[j - 1] or steps[j - 1] >= lag):
                advance(j)


def _rwkv_chunk(p_ref, mu_ref, w0_ref, w2_ref, a0_ref, a2_ref, g2_ref, kk_ref, ka_ref, rk_ref,
                lnw_ref, lnb_ref, o_ref, carry_ref, state_ref, *, cw, pw, pa, pg):
    C = RWKV_CHUNK
    C2 = 2 * C
    pr = p_ref[...]
    row = lax.broadcasted_iota(jnp.int32, pr.shape, 0)
    prev = jnp.where(row == 0, carry_ref[...], pltpu.roll(pr, 1, 0))
    carry_ref[...] = pr[C - 1:C, :]
    ps = pr + (prev - pr) * mu_ref[...]
    yield "step"

    r = ps[:, 0:cw]
    k = ps[:, cw:2 * cw]
    v = ps[:, 2 * cw:3 * cw]
    o = 3 * cw
    xw = ps[:, o:o + pw]
    xa = ps[:, o + pw:o + pw + pa]
    xg = ps[:, o + pw + pa:o + pw + pa + pg]

    z = -(w0_ref[...] + _dot(jnp.tanh(xw), w2_ref[...]))
    softplus = jnp.maximum(z, 0.0) + jnp.log(1.0 + jnp.exp(-jnp.abs(z)))
    logw = -jnp.exp(-softplus - 0.5)
    asig = _sigmoid(a0_ref[...] + _dot(xa, a2_ref[...]))
    gate = _dot(_sigmoid(xg), g2_ref[...])
    kkr = k * kk_ref[...]
    kmod = k * (1.0 + (asig - 1.0) * ka_ref[...])
    yield "step"

    ti = lax.broadcasted_iota(jnp.int32, (C, C), 0)
    si = lax.broadcasted_iota(jnp.int32, (C, C), 1)
    tril = jnp.where(si <= ti, 1.0, 0.0).astype(BF16)
    lh, lm, ll = _split3(logw)
    d32 = functools.partial(jnp.dot, preferred_element_type=F32)
    cum = d32(tril, lh) + (d32(tril, lm) + d32(tril, ll))
    cum_end = cum[C - 1:C, :]
    e_prev = jnp.exp(cum - logw)
    e_incl = jnp.exp(cum)
    e_inv = jnp.exp(-cum)
    g_end = jnp.exp(cum_end)
    e_end = g_end * e_inv
    yield "step"

    lane = lax.broadcasted_iota(jnp.int32, (C, LANES), 1)
    lo = lane < RWKV_HEAD_DIM
    tok = lax.broadcasted_iota(jnp.int32, (C, LANES), 0)
    src = jnp.bitwise_and(lane, RWKV_HEAD_DIM - 1)
    strict = src < tok
    incl = src <= tok
    eye = jnp.where(src == tok, 1.0, 0.0)
    rho = lax.broadcasted_iota(jnp.int32, (C2, C2), 0)
    sig = lax.broadcasted_iota(jnp.int32, (C2, C2), 1)
    same_head = (rho >= C) == (sig >= C)

    pairs = range(cw // LANES)
    sls = [slice(p * LANES, (p + 1) * LANES) for p in pairs]
    incl2 = jnp.concatenate([incl, incl], axis=1)
    strict2 = jnp.concatenate([strict, strict], axis=1)

    b2, r2, k2, v2, at, rt, v2s, gm = [], [], [], [], [], [], [], []
    for sl in sls:
        kk2 = kkr[:, sl]
        nrm = jnp.sqrt(_head_sum(kk2 * kk2, lo))
        kkn = kk2 / jnp.maximum(nrm, 1e-12)
        b2.append(kkn * asig[:, sl])
        r2.append(r[:, sl])
        k2.append(kmod[:, sl])
        v2.append(v[:, sl])
        at.append(-kkn * e_prev[:, sl])
        rt.append(r2[-1] * e_incl[:, sl])
        bt = b2[-1] * e_inv[:, sl]
        kt = k2[-1] * e_inv[:, sl]
        v2s.append(_stack_heads(v2[-1], lo))
        rhs = jnp.concatenate([_stack_heads(bt, lo), _stack_heads(kt, lo)], axis=0)
        gm.append(_dot_nt(jnp.concatenate([at[-1], rt[-1]], axis=0), rhs))
        yield "step"

    labk = [jnp.where(strict2, g[0:C, :], 0.0) for g in gm]
    mrbk = [jnp.where(incl2, g[C:C2, :], 0.0) for g in gm]
    lakv = [_dot(labk[p][:, C2:2 * C2], v2s[p]) for p in pairs]
    yield "step"

    lab = [m[:, 0:C2] for m in labk]
    tinv = [eye + m for m in lab]
    pw2 = [_dot(m, _stack_heads(m, lo)) for m in lab]
    yield "step"
    n_pow = int(math.log2(C))
    for k in range(1, n_pow):
        if k < n_pow - 1:
            prod = [_dot(jnp.concatenate([t, m], axis=0), _stack_heads(m, lo)) for t, m in zip(tinv, pw2)]
            tinv = [t + pr[0:C] for t, pr in zip(tinv, prod)]
            pw2 = [pr[C:C2] for pr in prod]
        else:
            tinv = [t + _dot(t, _stack_heads(m, lo)) for t, m in zip(tinv, pw2)]
        yield "step"

    yield "need_state"
    s0 = [state_ref[p] for p in pairs]
    ars = [_dot_nt(jnp.concatenate([at[p], rt[p]], axis=0), s0[p]) for p in pairs]
    yield "step"
    u2 = [_dot(t, _stack_heads(ars[p][0:C] + lakv[p], lo)) for p, t in zip(pairs, tinv)]
    yield "step"
    y2 = [ars[p][C:C2] + _dot(mrbk[p], jnp.concatenate([_stack_heads(u2[p], lo), v2s[p]], axis=0))
          for p in pairs]
    yield "step"

    for p, sl in zip(pairs, sls):
        uv = jnp.concatenate([u2[p], v2[p]], axis=0)
        bk = jnp.concatenate([b2[p] * e_end[:, sl], k2[p] * e_end[:, sl]], axis=0)
        upd = _dot(uv.T, bk)
        state_ref[p] = s0[p] * g_end[:, sl] + jnp.where(same_head, upd, 0.0)
    yield "state_done"

    for p, sl in zip(pairs, sls):
        mean = _head_sum(y2[p], lo) * (1.0 / RWKV_HEAD_DIM)
        yc = y2[p] - mean
        var = _head_sum(yc * yc, lo) * (1.0 / RWKV_HEAD_DIM)
        yn = yc * lax.rsqrt(var + GN_EPS) * lnw_ref[:, sl] + lnb_ref[:, sl]
        bonus = _head_sum(r2[p] * k2[p] * rk_ref[:, sl], lo) * v2[p]
        o_ref[:, sl] = (yn + bonus) * gate[:, sl]
        yield "step"


def _rwkv(proj, b, s, cw, wpad, params):
    (mu, w0, w2, a0, a2, g2, kk, ka, rk, lnw, lnb) = params
    n_sub = RWKV_CHUNKS_PER_STEP if s % (RWKV_CHUNKS_PER_STEP * RWKV_CHUNK) == 0 else 1
    C = n_sub * RWKV_CHUNK
    nch = s // C
    pw, pa, pg = w2.shape[0], a2.shape[0], g2.shape[0]
    kern = functools.partial(_rwkv_kernel, n_sub=n_sub, cw=cw, pw=pw, pa=pa, pg=pg)
    row = lambda n: pl.BlockSpec((1, n), lambda bi, c: (0, 0))
    full = lambda a: pl.BlockSpec(a.shape, lambda bi, c: (0, 0))
    return pl.pallas_call(
        kern,
        out_shape=jax.ShapeDtypeStruct((b * s, cw), F32),
        grid=(b, nch),
        in_specs=[pl.BlockSpec((C, wpad), lambda bi, c: (bi * nch + c, 0)),
                  row(wpad), row(cw), full(w2), row(cw), full(a2), full(g2),
                  row(cw), row(cw), row(cw), row(cw), row(cw)],
        out_specs=pl.BlockSpec((C, cw), lambda bi, c: (bi * nch + c, 0)),
        scratch_shapes=[pltpu.VMEM((1, wpad), F32),
                        pltpu.VMEM((cw // LANES, LANES, LANES), F32)],
        compiler_params=pltpu.CompilerParams(
            dimension_semantics=("arbitrary", "arbitrary"), vmem_limit_bytes=VMEM_LIMIT),
        name="rwkv",
    )(proj, mu, w0, w2, a0, a2, g2, kk, ka, rk, lnw, lnb)


def _out_proj_kernel(att_ref, rw_ref, x_ref, ga_ref, woa_ref, wor_ref, g2_ref, wrh_ref, wrm_ref, br_ref,
                     x1_ref, hn_ref, route_ref, counts_ref):
    att = _rms(att_ref[...], ga_ref[...])
    y = (jnp.dot(att.astype(BF16), woa_ref[...], preferred_element_type=F32)
         + jnp.dot(rw_ref[...].astype(BF16), wor_ref[...], preferred_element_type=F32))
    x1 = x_ref[...] + y
    x1_ref[...] = x1
    hn = _rms(x1, g2_ref[...])
    _store_slabs(hn_ref, _pack_bf16_pairs(hn))

    hh, hm, _ = _split3(hn)
    d32 = functools.partial(jnp.dot, preferred_element_type=F32)
    logits = d32(hh, wrh_ref[...]) + (d32(hm, wrh_ref[...]) + d32(hh, wrm_ref[...])) + br_ref[...]
    lane = lax.broadcasted_iota(jnp.int32, logits.shape, 1)
    lanef = lane.astype(F32)
    big = float(LANES)

    is_g = lane < N_GROUPS
    gmax = jnp.max(jnp.where(is_g, logits, -jnp.inf), axis=-1, keepdims=True)
    gidx = jnp.min(jnp.where(is_g & (logits == gmax), lanef, big), axis=-1, keepdims=True)
    gsum = jnp.sum(jnp.where(is_g, jnp.exp(logits - gmax), 0.0), axis=-1, keepdims=True)
    g_gate = 1.0 / gsum

    e_lo = N_GROUPS + EXPERTS_PER_GROUP * gidx
    in_grp = (lanef >= e_lo) & (lanef < e_lo + EXPERTS_PER_GROUP)
    e1 = jnp.max(jnp.where(in_grp, logits, -jnp.inf), axis=-1, keepdims=True)
    i1 = jnp.min(jnp.where(in_grp & (logits == e1), lanef, big), axis=-1, keepdims=True)
    rest = in_grp & (lanef != i1)
    e2 = jnp.max(jnp.where(rest, logits, -jnp.inf), axis=-1, keepdims=True)
    i2 = jnp.min(jnp.where(rest & (logits == e2), lanef, big), axis=-1, keepdims=True)
    t = jnp.exp(e2 - e1)
    w1 = g_gate * (1.0 / (1.0 + t))
    w2 = g_gate * (t / (1.0 + t))

    @pl.when(pl.program_id(0) == 0)
    def _():
        counts_ref[...] = jnp.zeros_like(counts_ref)

    tm = logits.shape[0]
    pick1 = lanef == i1
    pick2 = lanef == i2
    onehot = jnp.where(pick1 | pick2, 1.0, 0.0)
    ti = lax.broadcasted_iota(jnp.int32, (tm, tm), 0)
    si = lax.broadcasted_iota(jnp.int32, (tm, tm), 1)
    before = _dot(jnp.where(si < ti, 1.0, 0.0), onehot) + counts_ref[...]
    r1 = jnp.sum(jnp.where(pick1, before, 0.0), axis=-1, keepdims=True)
    r2 = jnp.sum(jnp.where(pick2, before, 0.0), axis=-1, keepdims=True)
    counts_ref[...] = counts_ref[...] + jnp.sum(onehot, axis=0, keepdims=True)

    route_ref[...] = jnp.where(lane == 0, i1 - N_GROUPS,
                     jnp.where(lane == 1, i2 - N_GROUPS,
                     jnp.where(lane == 2, w1,
                     jnp.where(lane == 3, w2,
                     jnp.where(lane == 4, r1,
                     jnp.where(lane == 5, r2, 0.0))))))


def _out_proj(att, rw, x2, ga, woa, wor, g2, wrh, wrm, br, tm):
    t, d = x2.shape
    wa = att.shape[1]
    wr = rw.shape[1]
    sr = d // 2 // SLAB_LANES
    c = lambda shape: pl.BlockSpec(shape, lambda i: (0, 0))
    return pl.pallas_call(
        _out_proj_kernel,
        out_shape=(jax.ShapeDtypeStruct((t, d), F32), jax.ShapeDtypeStruct((t * sr, SLAB_LANES), jnp.uint32),
                   jax.ShapeDtypeStruct((t, LANES), F32), jax.ShapeDtypeStruct((1, LANES), F32)),
        grid=(t // tm,),
        in_specs=[pl.BlockSpec((tm, wa), lambda i: (i, 0)),
                  pl.BlockSpec((tm, wr), lambda i: (i, 0)),
                  pl.BlockSpec((tm, d), lambda i: (i, 0)),
                  c((1, wa)), c((wa, d)), c((wr, d)), c((1, d)), c((d, LANES)), c((d, LANES)), c((1, LANES))],
        out_specs=(pl.BlockSpec((tm, d), lambda i: (i, 0)), pl.BlockSpec((tm * sr, SLAB_LANES), lambda i: (i, 0)),
                   pl.BlockSpec((tm, LANES), lambda i: (i, 0)), c((1, LANES))),
        compiler_params=pltpu.CompilerParams(
            dimension_semantics=("arbitrary",), vmem_limit_bytes=VMEM_LIMIT),
        name="out_proj",
    )(att, rw, x2, ga, woa, wor, g2, wrh, wrm, br)


def _pack_bf16_pairs(x):
    half = x.shape[1] // 2
    bits = pltpu.bitcast(x.astype(BF16).astype(F32), jnp.uint32)
    return (bits[:, half:] & jnp.uint32(0xFFFF0000)) | (bits[:, :half] >> jnp.uint32(16))


def _unpack_bf16_pairs(w):
    lo = pltpu.bitcast(w << jnp.uint32(16), F32)
    hi = pltpu.bitcast(w & jnp.uint32(0xFFFF0000), F32)
    return jnp.concatenate([lo, hi], axis=1).astype(BF16)


def _store_slabs(ref, x):
    rows, d = x.shape
    sr = d // SLAB_LANES
    for s in range(sr):
        ref[pl.ds(s, rows, stride=sr), :] = x[:, s * SLAB_LANES:(s + 1) * SLAB_LANES]


def _load_slabs(ref, slot, first, rows, sr):
    return jnp.concatenate([ref[slot, pl.ds(first * sr + s, rows, stride=sr), :] for s in range(sr)], axis=1)


def _gather_rows(i, nt, idx_hbm, src_hbm, idx_ref, buf_ref, isem, gsem, rows, sr):
    def idx_copy(tile, slot):
        return pltpu.make_async_copy(idx_hbm.at[pl.ds(tile, 1)], idx_ref.at[pl.ds(slot, 1)], isem.at[slot])

    def start_rows(slot):
        for r in range(rows):
            src = pl.multiple_of(idx_ref[slot, r], sr)
            pltpu.make_async_copy(src_hbm.at[pl.ds(src, sr)], buf_ref.at[slot, pl.ds(r * sr, sr)],
                                  gsem.at[slot]).start()

    slot = jnp.bitwise_and(i, 1)

    @pl.when(i == 0)
    def _():
        idx_copy(0, 0).start()
        idx_copy(0, 0).wait()
        start_rows(0)
        if nt > 1:
            idx_copy(1, 1).start()

    @pl.when(i + 1 < nt)
    def _():
        idx_copy(i + 1, 1 - slot).wait()
        start_rows(1 - slot)

    @pl.when(i + 2 < nt)
    def _():
        idx_copy(i + 2, slot).start()

    pltpu.make_async_copy(src_hbm.at[pl.ds(0, rows * sr)], buf_ref.at[slot], gsem.at[slot]).wait()
    return slot


def _dispatch_kernel(idx_hbm, x_hbm, o_hbm, idx_ref, tile_ref, zero_ref, isem, lsem, csem,
                     *, nt, n_tok, n_pad, sr):
    i = pl.program_id(0)
    slot = jnp.bitwise_and(i, 1)
    other = 1 - slot
    tg = n_tok // TOP_K
    n_all = n_tok + n_pad
    stage = lax.rem(i, 3)

    def idx_copy(tile, s):
        return pltpu.make_async_copy(idx_hbm.at[pl.ds(tile, 1)], idx_ref.at[pl.ds(s, 1)], isem.at[s])

    def load(tile, s):
        return pltpu.make_async_copy(x_hbm.at[pl.ds(tile * tg * sr, tg * sr)], tile_ref.at[s], lsem.at[s])

    def copies_done(s):
        return pltpu.make_async_copy(x_hbm.at[pl.ds(0, n_all * sr)], o_hbm.at[pl.ds(0, n_all * sr)], csem.at[s])

    @pl.when(i == 0)
    def _():
        zero_ref[...] = jnp.zeros_like(zero_ref)
        idx_copy(0, 0).start()
        load(0, 0).start()

    idx_copy(i, slot).wait()
    load(i, stage).wait()

    @pl.when(i + 1 < nt)
    def _():
        idx_copy(i + 1, other).start()
        load(i + 1, lax.rem(i + 1, 3)).start()

    for r in range(n_tok):
        dst = pl.multiple_of(idx_ref[slot, r], sr)
        pltpu.make_async_copy(tile_ref.at[stage, pl.ds((r % tg) * sr, sr)], o_hbm.at[pl.ds(dst, sr)],
                              csem.at[slot]).start()
    for r in range(n_tok, n_all):
        dst = pl.multiple_of(idx_ref[slot, r], sr)
        pltpu.make_async_copy(zero_ref, o_hbm.at[pl.ds(dst, sr)], csem.at[slot]).start()

    @pl.when(i > 0)
    def _():
        copies_done(other).wait()

    @pl.when(i == nt - 1)
    def _():
        copies_done(slot).wait()


def _dispatch(idx, hn_slabs, n_rows, sr):
    nt, n_all = idx.shape
    n_tok = TOP_K * COMBINE_ROWS
    hbm = pl.BlockSpec(memory_space=pl.ANY)
    return pl.pallas_call(
        functools.partial(_dispatch_kernel, nt=nt, n_tok=n_tok, n_pad=n_all - n_tok, sr=sr),
        out_shape=jax.ShapeDtypeStruct((n_rows * sr, SLAB_LANES), hn_slabs.dtype),
        grid=(nt,),
        in_specs=[hbm, hbm],
        out_specs=hbm,
        scratch_shapes=[pltpu.SMEM((2, n_all), jnp.int32),
                        pltpu.VMEM((3, COMBINE_ROWS * sr, SLAB_LANES), hn_slabs.dtype),
                        pltpu.VMEM((sr, SLAB_LANES), hn_slabs.dtype),
                        pltpu.SemaphoreType.DMA((2,)), pltpu.SemaphoreType.DMA((3,)),
                        pltpu.SemaphoreType.DMA((2,))],
        compiler_params=pltpu.CompilerParams(dimension_semantics=("arbitrary",)),
        name="dispatch",
    )(idx, hn_slabs)


def _expert_kernel(te_ref, first_ref, next_ref, nu_ref, x_ref, w1_hbm, w3_hbm, w2_hbm, o_ref,
                   stage_refs, wb_refs, wsem, *, sx):
    i = pl.program_id(0)
    tm = EXPERT_ROWS
    w_hbm = (w1_hbm, w3_hbm, w2_hbm)
    used = i < nu_ref[0]

    def weight_copies(e):
        out = []
        for j, (w, st) in enumerate(zip(w_hbm, stage_refs)):
            half = st.shape[0] // 2
            for q in range(2):
                rows = pl.ds(q * half, half)
                out.append((pltpu.make_async_copy(w.at[e, rows], st.at[rows], wsem.at[2 * j + q]), q))
        return out

    @pl.when(i == 0)
    def _():
        for cp, q in weight_copies(te_ref[0]):
            cp.start(priority=q)

    def cast_stage(j):
        st, wb = stage_refs[j], wb_refs[j]
        rows = st.shape[0] // WEIGHT_CAST_CHUNKS
        for c in range(WEIGHT_CAST_CHUNKS):
            wb[c * rows:(c + 1) * rows, :] = st[c * rows:(c + 1) * rows, :].astype(BF16)

    def tile_mlp(before_down_proj=None):
        xb = _unpack_bf16_pairs(jnp.concatenate([x_ref[pl.ds(s, tm, stride=sx), :] for s in range(sx)], axis=1))
        h1 = jnp.dot(xb, wb_refs[0][...], preferred_element_type=F32)
        h3 = jnp.dot(xb, wb_refs[1][...], preferred_element_type=F32)
        if before_down_proj is not None:
            before_down_proj()
        hid = (h1 * _sigmoid(h1)) * h3
        _store_slabs(o_ref, jnp.dot(hid.astype(BF16), wb_refs[2][...], preferred_element_type=F32))

    is_first = first_ref[i] == 1

    @pl.when(used & is_first)
    def _():
        for cp, _ in weight_copies(te_ref[i]):
            cp.wait()
        cast_stage(0)
        cast_stage(1)

        def finish_weights():
            cast_stage(2)

            @pl.when(next_ref[i] >= 0)
            def _():
                for cp, q in weight_copies(next_ref[i]):
                    cp.start(priority=q)

        tile_mlp(finish_weights)

    @pl.when(used & jnp.logical_not(is_first))
    def _():
        tile_mlp()

    @pl.when(jnp.logical_not(used))
    def _():
        o_ref[...] = jnp.zeros_like(o_ref)


def _experts(tile_e, tile_first, tile_next, n_used, x_slabs, w1, w3, w2):
    nt = tile_e.shape[0]
    d, de = w1.shape[1], w1.shape[2]
    sr = d // SLAB_LANES
    sx = d // 2 // SLAB_LANES
    tm = EXPERT_ROWS
    hbm = pl.BlockSpec(memory_space=pl.ANY)
    grid_spec = pltpu.PrefetchScalarGridSpec(
        num_scalar_prefetch=4,
        grid=(nt,),
        in_specs=[pl.BlockSpec((tm * sx, SLAB_LANES), lambda i, te, fi, nx, nu: (jnp.minimum(i, nu[0] - 1), 0)),
                  hbm, hbm, hbm],
        out_specs=pl.BlockSpec((tm * sr, SLAB_LANES), lambda i, *_: (i, 0)),
        scratch_shapes=[[pltpu.VMEM((d, de), F32), pltpu.VMEM((d, de), F32), pltpu.VMEM((de, d), F32)],
                        [pltpu.VMEM((d, de), BF16), pltpu.VMEM((d, de), BF16), pltpu.VMEM((de, d), BF16)],
                        pltpu.SemaphoreType.DMA((6,))],
    )
    return pl.pallas_call(
        functools.partial(_expert_kernel, sx=sx),
        out_shape=jax.ShapeDtypeStruct((nt * tm * sr, SLAB_LANES), F32),
        grid_spec=grid_spec,
        compiler_params=pltpu.CompilerParams(
            dimension_semantics=("arbitrary",), vmem_limit_bytes=EXPERT_VMEM_LIMIT),
        name="experts",
    )(tile_e, tile_first, tile_next, n_used, x_slabs, w1, w3, w2)


def _combine_kernel(dest_hbm, y_hbm, x1_ref, route_ref, fg_ref, o_ref, idx_ref, buf_ref, isem, gsem,
                    *, nt, sr):
    i = pl.program_id(0)
    tg = COMBINE_ROWS
    slot = _gather_rows(i, nt, dest_hbm, y_hbm, idx_ref, buf_ref, isem, gsem, TOP_K * tg, sr)
    route = route_ref[...]
    y = (_load_slabs(buf_ref, slot, 0, tg, sr) * route[:, 2:3]
         + _load_slabs(buf_ref, slot, tg, tg, sr) * route[:, 3:4])
    o_ref[...] = _rms(x1_ref[...] + y, fg_ref[...])


def _combine(dest, ybuf, x1, route, fg):
    t, d = x1.shape
    tg = COMBINE_ROWS
    sr = d // SLAB_LANES
    return pl.pallas_call(
        functools.partial(_combine_kernel, nt=t // tg, sr=sr),
        out_shape=jax.ShapeDtypeStruct((t, d), F32),
        grid=(t // tg,),
        in_specs=[pl.BlockSpec(memory_space=pl.ANY),
                  pl.BlockSpec(memory_space=pl.ANY),
                  pl.BlockSpec((tg, d), lambda i: (i, 0)),
                  pl.BlockSpec((tg, LANES), lambda i: (i, 0)),
                  pl.BlockSpec((1, d), lambda i: (0, 0))],
        out_specs=pl.BlockSpec((tg, d), lambda i: (i, 0)),
        scratch_shapes=[pltpu.SMEM((2, TOP_K * tg), jnp.int32),
                        pltpu.VMEM((2, TOP_K * tg * sr, SLAB_LANES), F32),
                        pltpu.SemaphoreType.DMA((2,)), pltpu.SemaphoreType.DMA((2,))],
        compiler_params=pltpu.CompilerParams(
            dimension_semantics=("arbitrary",), vmem_limit_bytes=VMEM_LIMIT),
        name="combine",
    )(dest, ybuf, x1, route, fg)


def _pad_cols(m, n):
    return jnp.pad(m, ((0, 0), (0, n - m.shape[1])))


def _pad_rows(m, n):
    return jnp.pad(m, ((0, n - m.shape[0]), (0, 0)))


def _pick_tile(n, cap):
    for unit in (MXU_WIDTH, LANES):
        fits = [c for c in range(unit, cap + 1, unit) if n % c == 0]
        if fits:
            return fits[-1]
    raise ValueError(f"no lane-aligned tile divides {n}")


def _dispatch_plan(experts, rank, counts, n_tok):
    tm = EXPERT_ROWS
    nt = n_tok * TOP_K // tm + N_EXPERTS
    tiles_e = (counts + tm - 1) // tm
    tile_end = jnp.cumsum(tiles_e)
    tile_start = tile_end - tiles_e
    dest = (tile_start[experts] * tm + rank).astype(jnp.int32)
    n_used = tile_end[-1:].astype(jnp.int32)
    tiles = jnp.arange(nt, dtype=jnp.int32)
    tile_e = jnp.minimum(jnp.sum((tile_end[None, :] <= tiles[:, None]).astype(jnp.int32), axis=1),
                         N_EXPERTS - 1).astype(jnp.int32)
    tile_first = ((tiles == 0) | (tile_e != jnp.roll(tile_e, 1))).astype(jnp.int32)
    next_start = tile_end[tile_e]
    tile_next = jnp.where(next_start < n_used[0], tile_e[jnp.minimum(next_start, nt - 1)], -1).astype(jnp.int32)

    seg_len = jnp.concatenate([tiles_e * tm - counts, (nt - n_used) * tm])
    seg_first = jnp.concatenate([tile_start * tm + counts, n_used * tm])
    seg_stop = jnp.cumsum(seg_len)
    j = jnp.arange(N_EXPERTS * tm, dtype=jnp.int32)
    seg = jnp.sum((seg_stop[None, :] <= j[:, None]).astype(jnp.int32), axis=1)
    pad_rows = (seg_first[seg] + j - (seg_stop - seg_len)[seg]).astype(jnp.int32)
    return tile_e, tile_first, tile_next, n_used, dest, pad_rows


def _layer(x2, b, s, norm1_g, w_in, attn_out_g, mu, w0, w2, a0, a2, g2, k_k, k_a, r_k, lnx_w, lnx_b,
           w_out, norm2_g, wg, bg, we, be, ew1, ew3, ew2, final_g):
    t, d = x2.shape
    aw = attn_out_g.shape[0]
    cw = w0.shape[0]
    heads = aw // ATT_HEAD_DIM
    dl, al, gl = w2.shape[0], a2.shape[0], g2.shape[0]
    pw, pa, pg = _round_up(dl, LANES), _round_up(al, LANES), _round_up(gl, LANES)
    wpad = 3 * cw + pw + pa + pg

    w_in = w_in.astype(BF16)
    sh = w_in[:, 3 * aw:]
    o = 3 * cw
    w_arr = jnp.concatenate(
        [sh[:, :o], _pad_cols(sh[:, o:o + dl], pw), _pad_cols(sh[:, o + dl:o + dl + al], pa),
         _pad_cols(sh[:, o + dl + al:o + dl + al + gl], pg), w_in[:, :3 * aw]], axis=1)
    mu_arr = jnp.concatenate(
        [mu[:o], jnp.pad(mu[o:o + dl], (0, pw - dl)), jnp.pad(mu[o + dl:o + dl + al], (0, pa - al)),
         jnp.pad(mu[o + dl + al:], (0, pg - gl))])[None, :]

    n_all = w_arr.shape[1]
    proj = _in_proj(x2, norm1_g[None, :], w_arr, tm=min(512, t), tn=_pick_tile(n_all, 4096))

    att = _moba(proj, b, s, heads, wpad // LANES)
    r1 = lambda a: a.reshape(1, -1)
    rw = _rwkv(proj, b, s, cw, wpad,
               (mu_arr, r1(w0), _pad_rows(w2, pw).astype(BF16), r1(a0), _pad_rows(a2, pa).astype(BF16),
                _pad_rows(g2, pg).astype(BF16), r1(k_k), r1(k_a), r1(r_k), r1(lnx_w), r1(lnx_b)))

    wr = _pad_cols(jnp.concatenate([wg, we], axis=1), LANES)
    wrh = wr.astype(BF16)
    wrm = (wr - wrh.astype(F32)).astype(BF16)
    br = jnp.pad(jnp.concatenate([bg, be]), (0, LANES - N_GROUPS - N_EXPERTS))[None, :]
    wo = w_out.astype(BF16)
    x1, hn, route, counts = _out_proj(att, rw, x2, r1(attn_out_g), wo[:aw], wo[aw:], r1(norm2_g), wrh, wrm, br,
                                      tm=min(512, t))

    plan_in = route[:, :6].astype(jnp.int32)
    counts = counts[0, N_GROUPS:N_GROUPS + N_EXPERTS].astype(jnp.int32)
    tg = COMBINE_ROWS
    n_tiles = t // tg
    pad_per_tile = N_EXPERTS * EXPERT_ROWS // n_tiles
    tile_e, tile_first, tile_next, n_used, dest, pad_rows = _dispatch_plan(
        plan_in[:, 0:TOP_K], plan_in[:, 4:4 + TOP_K], counts, t)

    sr = d // SLAB_LANES
    sx = sr // 2
    dest_tiles = dest.reshape(n_tiles, tg, TOP_K).transpose(0, 2, 1).reshape(n_tiles, TOP_K * tg)
    disp_idx = jnp.concatenate([dest_tiles, pad_rows.reshape(n_tiles, pad_per_tile)], axis=1) * sx
    xbuf = _dispatch(disp_idx, hn, tile_e.shape[0] * EXPERT_ROWS, sx)
    ybuf = _experts(tile_e, tile_first, tile_next, n_used, xbuf, ew1, ew3, ew2)
    return _combine(dest_tiles * sr, ybuf, x1, route, r1(final_g))


def kernel(x, norm1_g, w_in, attn_out_g, rwkv_mu, rwkv_w0, rwkv_w2, rwkv_a0, rwkv_a2, rwkv_g2, rwkv_k_k,
           rwkv_k_a, rwkv_r_k, rwkv_lnx_w, rwkv_lnx_b, w_out, norm2_g, router_group_w, router_group_b,
           router_expert_w, router_expert_b, expert_w1, expert_w3, expert_w2, final_g):
    b, s, d = x.shape
    assert norm1_g.shape[0] == 1, "single-layer block"
    assert s % MOBA_BLOCK == 0 and s % RWKV_CHUNK == 0
    assert (b * s) % max(EXPERT_ROWS, COMBINE_ROWS) == 0
    assert 2 * RWKV_CHUNK == LANES and RWKV_CHUNK == RWKV_HEAD_DIM, "rwkv packs two CxC head matrices per vreg row"
    assert (N_EXPERTS * EXPERT_ROWS) % ((b * s) // COMBINE_ROWS) == 0, "zero-fill rows split evenly over tiles"
    out = _layer(x.reshape(b * s, d), b, s, norm1_g[0], w_in[0], attn_out_g[0], rwkv_mu[0], rwkv_w0[0],
                 rwkv_w2[0], rwkv_a0[0], rwkv_a2[0], rwkv_g2[0], rwkv_k_k[0], rwkv_k_a[0], rwkv_r_k[0],
                 rwkv_lnx_w[0], rwkv_lnx_b[0], w_out[0], norm2_g[0], router_group_w[0], router_group_b[0],
                 router_expert_w[0], router_expert_b[0], expert_w1[0], expert_w3[0], expert_w2[0], final_g)
    return out.reshape(b, s, d)
```

```python
import functools
import math

import jax
import jax.numpy as jnp
from jax import lax
from jax.experimental import pallas as pl
from jax.experimental.pallas import tpu as pltpu

F32 = jnp.float32
BF16 = jnp.bfloat16

LANES = 128
MXU_WIDTH = 256
BF16_SUBLANES = 16
SLAB_LANES = 128
ATT_HEAD_DIM = 128
RWKV_HEAD_DIM = 64
MOBA_BLOCK = 256
MOBA_TOPK = 3
N_GROUPS = 4
EXPERTS_PER_GROUP = 8
N_EXPERTS = N_GROUPS * EXPERTS_PER_GROUP
TOP_K = 2
RMS_EPS = 1e-6
GN_EPS = 64e-5
NEG_BIG = -1e30
RWKV_CHUNK = 64
RWKV_CHUNKS_PER_STEP = 4
RWKV_CHUNK_LAG = 5
EXPERT_ROWS = 256
COMBINE_ROWS = 512
VMEM_LIMIT = 56 * 1024 * 1024
EXPERT_VMEM_LIMIT = 60 * 1024 * 1024
WEIGHT_CAST_CHUNKS = 16


def _round_up(n, m):
    return (n + m - 1) // m * m


def _dot(a, b):
    return jnp.dot(a.astype(BF16), b.astype(BF16), preferred_element_type=F32)


def _dot_nt(a, b):
    return lax.dot_general(a.astype(BF16), b.astype(BF16), (((1,), (1,)), ((), ())),
                           preferred_element_type=F32)


def _split3(a):
    hi = a.astype(BF16)
    r1 = a - hi.astype(F32)
    mid = r1.astype(BF16)
    lo = (r1 - mid.astype(F32)).astype(BF16)
    return hi, mid, lo


def _dot_hi(a, b):
    ah, am, _ = _split3(a)
    bh, bm, _ = _split3(b)
    d = functools.partial(jnp.dot, preferred_element_type=F32)
    return d(ah, bh) + (d(am, bh) + d(ah, bm))


def _sigmoid(x):
    return 1.0 / (1.0 + jnp.exp(-x))


def _rms(x, g):
    ms = jnp.mean(x * x, axis=-1, keepdims=True)
    return x * lax.rsqrt(ms + RMS_EPS) * g


def _in_proj_kernel(x_ref, g_ref, w_ref, o_ref):
    xn = _rms(x_ref[...], g_ref[...]).astype(BF16)
    o_ref[...] = jnp.dot(xn, w_ref[...], preferred_element_type=F32)


def _in_proj(x2, g, w, tm, tn):
    t, d = x2.shape
    n = w.shape[1]
    return pl.pallas_call(
        _in_proj_kernel,
        out_shape=jax.ShapeDtypeStruct((t, n), F32),
        grid=(n // tn, t // tm),
        in_specs=[pl.BlockSpec((tm, d), lambda j, i: (i, 0)),
                  pl.BlockSpec((1, d), lambda j, i: (0, 0)),
                  pl.BlockSpec((d, tn), lambda j, i: (0, j))],
        out_specs=pl.BlockSpec((tm, tn), lambda j, i: (i, j)),
        compiler_params=pltpu.CompilerParams(
            dimension_semantics=("arbitrary", "arbitrary"), vmem_limit_bytes=VMEM_LIMIT),
        name="in_proj",
    )(x2, g, w)


def _moba_kernel(q_ref, k_ref, v_ref, o_ref, *, nblk, scale):
    bs = MOBA_BLOCK
    k = k_ref[...]
    kb = k.astype(BF16)
    d = v_ref.shape[1]
    vtb = jnp.concatenate([v_ref[...].T, jnp.ones((BF16_SUBLANES, v_ref.shape[0]), F32)], axis=0).astype(BF16)
    qt = q_ref[...].T
    qtb = qt.astype(BF16)
    kmean = jnp.concatenate(
        [jnp.sum(k[n * bs:(n + 1) * bs, :], axis=0, keepdims=True) for n in range(nblk)], axis=0) * (1.0 / bs)
    gate_all = _dot_hi(kmean, qt)
    kpos = lax.broadcasted_iota(jnp.int32, (bs, bs), 0)
    qpos = lax.broadcasted_iota(jnp.int32, (bs, bs), 1)
    causal = kpos <= qpos

    def scores(qi):
        return jnp.dot(kb[0:(qi + 1) * bs, :], qtb[:, qi * bs:(qi + 1) * bs], preferred_element_type=F32)

    exp_scale = scale * math.log2(math.e)
    s_next = scores(0)
    for qi in range(nblk):
        qs = slice(qi * bs, (qi + 1) * bs)
        nk = (qi + 1) * bs
        s_all = s_next
        if qi + 1 < nblk:
            s_next = scores(qi + 1)
        blocks = []
        if qi > MOBA_TOPK:
            gate = gate_all[0:qi, qs]
            blk = lax.broadcasted_iota(jnp.int32, (qi, bs), 0)
        for n in range(qi):
            sn = s_all[n * bs:(n + 1) * bs, :]
            if qi > MOBA_TOPK:
                gn = gate[n:n + 1, :]
                beats = (gate > gn) | ((gate == gn) & (blk < n))
                rank = jnp.sum(jnp.where(beats, 1.0, 0.0), axis=0, keepdims=True)
                sn = jnp.where(rank < float(MOBA_TOPK), sn, NEG_BIG)
            blocks.append(sn)
        blocks.append(jnp.where(causal, s_all[qi * bs:nk, :], NEG_BIG))
        m = functools.reduce(jnp.maximum, [jnp.max(x, axis=0, keepdims=True) for x in blocks])
        pb = jnp.concatenate([jnp.exp2((x - m) * exp_scale).astype(BF16) for x in blocks], axis=0)
        acc = jnp.dot(vtb[:, 0:nk], pb, preferred_element_type=F32)
        o_ref[qs, :] = (acc[0:d, :] / acc[d:d + 1, :]).T


def _moba(proj, b, s, heads, col0):
    d = ATT_HEAD_DIM
    nblk = s // MOBA_BLOCK
    kern = functools.partial(_moba_kernel, nblk=nblk, scale=1.0 / math.sqrt(d))
    return pl.pallas_call(
        kern,
        out_shape=jax.ShapeDtypeStruct((b * s, heads * d), F32),
        grid=(b, heads),
        in_specs=[pl.BlockSpec((s, d), lambda bi, h: (bi, col0 + h)),
                  pl.BlockSpec((s, d), lambda bi, h: (bi, col0 + heads + h)),
                  pl.BlockSpec((s, d), lambda bi, h: (bi, col0 + 2 * heads + h))],
        out_specs=pl.BlockSpec((s, d), lambda bi, h: (bi, h)),
        compiler_params=pltpu.CompilerParams(
            dimension_semantics=("arbitrary", "arbitrary"), vmem_limit_bytes=VMEM_LIMIT),
        name="moba",
    )(proj, proj, proj)


def _head_sum(x, lo):
    s0 = jnp.sum(jnp.where(lo, x, 0.0), axis=-1, keepdims=True)
    s1 = jnp.sum(jnp.where(lo, 0.0, x), axis=-1, keepdims=True)
    return jnp.where(lo, s0, s1)


def _stack_heads(x, lo):
    return jnp.concatenate([jnp.where(lo, x, 0.0), jnp.where(lo, 0.0, x)], axis=0)


def _rwkv_kernel(p_ref, mu_ref, w0_ref, w2_ref, a0_ref, a2_ref, g2_ref, kk_ref, ka_ref, rk_ref,
                 lnw_ref, lnb_ref, o_ref, carry_ref, state_ref, *, n_sub, **dims):
    @pl.when(pl.program_id(1) == 0)
    def _():
        carry_ref[...] = jnp.zeros_like(carry_ref)
        state_ref[...] = jnp.zeros_like(state_ref)

    C = RWKV_CHUNK
    chunks = [_rwkv_chunk(p_ref.at[pl.ds(h * C, C)], mu_ref, w0_ref, w2_ref, a0_ref, a2_ref, g2_ref, kk_ref,
                          ka_ref, rk_ref, lnw_ref, lnb_ref, o_ref.at[pl.ds(h * C, C)], carry_ref, state_ref,
                          **dims) for h in range(n_sub)]
    _interleave(chunks, RWKV_CHUNK_LAG)


def _interleave(chunks, lag):
    n = len(chunks)
    steps, done, wrote, waiting = [0] * n, [False] * n, [False] * n, [False] * n

    def advance(j):
        if waiting[j] and j > 0:
            while not (wrote[j - 1] or done[j - 1]):
                advance(j - 1)
        waiting[j] = False
        try:
            tag = next(chunks[j])
        except StopIteration:
            done[j] = True
            return
        steps[j] += 1
        waiting[j] = tag == "need_state"
        wrote[j] = wrote[j] or tag == "state_done"

    while not all(done):
        for j in range(n):
            if not done[j] and (j == 0 or done[j - 1] or steps[j - 1] >= lag):
                advance(j)


def _rwkv_chunk(p_ref, mu_ref, w0_ref, w2_ref, a0_ref, a2_ref, g2_ref, kk_ref, ka_ref, rk_ref,
                lnw_ref, lnb_ref, o_ref, carry_ref, state_ref, *, cw, pw, pa, pg):
    C = RWKV_CHUNK
    C2 = 2 * C
    pr = p_ref[...]
    row = lax.broadcasted_iota(jnp.int32, pr.shape, 0)
    prev = jnp.where(row == 0, carry_ref[...], pltpu.roll(pr, 1, 0))
    carry_ref[...] = pr[C - 1:C, :]
    ps = pr + (prev - pr) * mu_ref[...]
    yield "step"

    r = ps[:, 0:cw]
    k = ps[:, cw:2 * cw]
    v = ps[:, 2 * cw:3 * cw]
    o = 3 * cw
    xw = ps[:, o:o + pw]
    xa = ps[:, o + pw:o + pw + pa]
    xg = ps[:, o + pw + pa:o + pw + pa + pg]

    z = -(w0_ref[...] + _dot(jnp.tanh(xw), w2_ref[...]))
    softplus = jnp.maximum(z, 0.0) + jnp.log(1.0 + jnp.exp(-jnp.abs(z)))
    logw = -jnp.exp(-softplus - 0.5)
    asig = _sigmoid(a0_ref[...] + _dot(xa, a2_ref[...]))
    gate = _dot(_sigmoid(xg), g2_ref[...])
    kkr = k * kk_ref[...]
    kmod = k * (1.0 + (asig - 1.0) * ka_ref[...])
    yield "step"

    ti = lax.broadcasted_iota(jnp.int32, (C, C), 0)
    si = lax.broadcasted_iota(jnp.int32, (C, C), 1)
    tril = jnp.where(si <= ti, 1.0, 0.0).astype(BF16)
    lh, lm, ll = _split3(logw)
    d32 = functools.partial(jnp.dot, preferred_element_type=F32)
    cum = d32(tril, lh) + (d32(tril, lm) + d32(tril, ll))
    cum_end = cum[C - 1:C, :]
    e_prev = jnp.exp(cum - logw)
    e_incl = jnp.exp(cum)
    e_inv = jnp.exp(-cum)
    g_end = jnp.exp(cum_end)
    e_end = g_end * e_inv
    yield "step"

    lane = lax.broadcasted_iota(jnp.int32, (C, LANES), 1)
    lo = lane < RWKV_HEAD_DIM
    tok = lax.broadcasted_iota(jnp.int32, (C, LANES), 0)
    src = jnp.bitwise_and(lane, RWKV_HEAD_DIM - 1)
    strict = src < tok
    incl = src <= tok
    eye = jnp.where(src == tok, 1.0, 0.0)
    rho = lax.broadcasted_iota(jnp.int32, (C2, C2), 0)
    sig = lax.broadcasted_iota(jnp.int32, (C2, C2), 1)
    same_head = (rho >= C) == (sig >= C)

    pairs = range(cw // LANES)
    sls = [slice(p * LANES, (p + 1) * LANES) for p in pairs]
    incl2 = jnp.concatenate([incl, incl], axis=1)
    strict2 = jnp.concatenate([strict, strict], axis=1)

    b2, r2, k2, v2, at, rt, v2s, gm = [], [], [], [], [], [], [], []
    for sl in sls:
        kk2 = kkr[:, sl]
        nrm = jnp.sqrt(_head_sum(kk2 * kk2, lo))
        kkn = kk2 / jnp.maximum(nrm, 1e-12)
        b2.append(kkn * asig[:, sl])
        r2.append(r[:, sl])
        k2.append(kmod[:, sl])
        v2.append(v[:, sl])
        at.append(-kkn * e_prev[:, sl])
        rt.append(r2[-1] * e_incl[:, sl])
        bt = b2[-1] * e_inv[:, sl]
        kt = k2[-1] * e_inv[:, sl]
        v2s.append(_stack_heads(v2[-1], lo))
        rhs = jnp.concatenate([_stack_heads(bt, lo), _stack_heads(kt, lo)], axis=0)
        gm.append(_dot_nt(jnp.concatenate([at[-1], rt[-1]], axis=0), rhs))
        yield "step"

    labk = [jnp.where(strict2, g[0:C, :], 0.0) for g in gm]
    mrbk = [jnp.where(incl2, g[C:C2, :], 0.0) for g in gm]
    lakv = [_dot(labk[p][:, C2:2 * C2], v2s[p]) for p in pairs]
    yield "step"

    lab = [m[:, 0:C2] for m in labk]
    tinv = [eye + m for m in lab]
    pw2 = [_dot(m, _stack_heads(m, lo)) for m in lab]
    yield "step"
    n_pow = int(math.log2(C))
    for k in range(1, n_pow):
        if k < n_pow - 1:
            prod = [_dot(jnp.concatenate([t, m], axis=0), _stack_heads(m, lo)) for t, m in zip(tinv, pw2)]
            tinv = [t + pr[0:C] for t, pr in zip(tinv, prod)]
            pw2 = [pr[C:C2] for pr in prod]
        else:
            tinv = [t + _dot(t, _stack_heads(m, lo)) for t, m in zip(tinv, pw2)]
        yield "step"

    yield "need_state"
    s0 = [state_ref[p] for p in pairs]
    ars = [_dot_nt(jnp.concatenate([at[p], rt[p]], axis=0), s0[p]) for p in pairs]
    yield "step"
    u2 = [_dot(t, _stack_heads(ars[p][0:C] + lakv[p], lo)) for p, t in zip(pairs, tinv)]
    yield "step"
    y2 = [ars[p][C:C2] + _dot(mrbk[p], jnp.concatenate([_stack_heads(u2[p], lo), v2s[p]], axis=0))
          for p in pairs]
    yield "step"

    for p, sl in zip(pairs, sls):
        uv = jnp.concatenate([u2[p], v2[p]], axis=0)
        bk = jnp.concatenate([b2[p] * e_end[:, sl], k2[p] * e_end[:, sl]], axis=0)
        upd = _dot(uv.T, bk)
        state_ref[p] = s0[p] * g_end[:, sl] + jnp.where(same_head, upd, 0.0)
    yield "state_done"

    for p, sl in zip(pairs, sls):
        mean = _head_sum(y2[p], lo) * (1.0 / RWKV_HEAD_DIM)
        yc = y2[p] - mean
        var = _head_sum(yc * yc, lo) * (1.0 / RWKV_HEAD_DIM)
        yn = yc * lax.rsqrt(var + GN_EPS) * lnw_ref[:, sl] + lnb_ref[:, sl]
        bonus = _head_sum(r2[p] * k2[p] * rk_ref[:, sl], lo) * v2[p]
        o_ref[:, sl] = (yn + bonus) * gate[:, sl]
        yield "step"


def _rwkv(proj, b, s, cw, wpad, params):
    (mu, w0, w2, a0, a2, g2, kk, ka, rk, lnw, lnb) = params
    n_sub = RWKV_CHUNKS_PER_STEP if s % (RWKV_CHUNKS_PER_STEP * RWKV_CHUNK) == 0 else 1
    C = n_sub * RWKV_CHUNK
    nch = s // C
    pw, pa, pg = w2.shape[0], a2.shape[0], g2.shape[0]
    kern = functools.partial(_rwkv_kernel, n_sub=n_sub, cw=cw, pw=pw, pa=pa, pg=pg)
    row = lambda n: pl.BlockSpec((1, n), lambda bi, c: (0, 0))
    full = lambda a: pl.BlockSpec(a.shape, lambda bi, c: (0, 0))
    return pl.pallas_call(
        kern,
        out_shape=jax.ShapeDtypeStruct((b * s, cw), F32),
        grid=(b, nch),
        in_specs=[pl.BlockSpec((C, wpad), lambda bi, c: (bi * nch + c, 0)),
                  row(wpad), row(cw), full(w2), row(cw), full(a2), full(g2),
                  row(cw), row(cw), row(cw), row(cw), row(cw)],
        out_specs=pl.BlockSpec((C, cw), lambda bi, c: (bi * nch + c, 0)),
        scratch_shapes=[pltpu.VMEM((1, wpad), F32),
                        pltpu.VMEM((cw // LANES, LANES, LANES), F32)],
        compiler_params=pltpu.CompilerParams(
            dimension_semantics=("arbitrary", "arbitrary"), vmem_limit_bytes=VMEM_LIMIT),
        name="rwkv",
    )(proj, mu, w0, w2, a0, a2, g2, kk, ka, rk, lnw, lnb)


def _out_proj_kernel(att_ref, rw_ref, x_ref, ga_ref, woa_ref, wor_ref, g2_ref, wrh_ref, wrm_ref, br_ref,
                     x1_ref, hn_ref, route_ref, counts_ref):
    att = _rms(att_ref[...], ga_ref[...])
    y = (jnp.dot(att.astype(BF16), woa_ref[...], preferred_element_type=F32)
         + jnp.dot(rw_ref[...].astype(BF16), wor_ref[...], preferred_element_type=F32))
    x1 = x_ref[...] + y
    x1_ref[...] = x1
    hn = _rms(x1, g2_ref[...])
    _store_slabs(hn_ref, _pack_bf16_pairs(hn))

    hh, hm, _ = _split3(hn)
    d32 = functools.partial(jnp.dot, preferred_element_type=F32)
    logits = d32(hh, wrh_ref[...]) + (d32(hm, wrh_ref[...]) + d32(hh, wrm_ref[...])) + br_ref[...]
    lane = lax.broadcasted_iota(jnp.int32, logits.shape, 1)
    lanef = lane.astype(F32)
    big = float(LANES)

    is_g = lane < N_GROUPS
    gmax = jnp.max(jnp.where(is_g, logits, -jnp.inf), axis=-1, keepdims=True)
    gidx = jnp.min(jnp.where(is_g & (logits == gmax), lanef, big), axis=-1, keepdims=True)
    gsum = jnp.sum(jnp.where(is_g, jnp.exp(logits - gmax), 0.0), axis=-1, keepdims=True)
    g_gate = 1.0 / gsum

    e_lo = N_GROUPS + EXPERTS_PER_GROUP * gidx
    in_grp = (lanef >= e_lo) & (lanef < e_lo + EXPERTS_PER_GROUP)
    e1 = jnp.max(jnp.where(in_grp, logits, -jnp.inf), axis=-1, keepdims=True)
    i1 = jnp.min(jnp.where(in_grp & (logits == e1), lanef, big), axis=-1, keepdims=True)
    rest = in_grp & (lanef != i1)
    e2 = jnp.max(jnp.where(rest, logits, -jnp.inf), axis=-1, keepdims=True)
    i2 = jnp.min(jnp.where(rest & (logits == e2), lanef, big), axis=-1, keepdims=True)
    t = jnp.exp(e2 - e1)
    w1 = g_gate * (1.0 / (1.0 + t))
    w2 = g_gate * (t / (1.0 + t))

    @pl.when(pl.program_id(0) == 0)
    def _():
        counts_ref[...] = jnp.zeros_like(counts_ref)

    tm = logits.shape[0]
    pick1 = lanef == i1
    pick2 = lanef == i2
    onehot = jnp.where(pick1 | pick2, 1.0, 0.0)
    ti = lax.broadcasted_iota(jnp.int32, (tm, tm), 0)
    si = lax.broadcasted_iota(jnp.int32, (tm, tm), 1)
    before = _dot(jnp.where(si < ti, 1.0, 0.0), onehot) + counts_ref[...]
    r1 = jnp.sum(jnp.where(pick1, before, 0.0), axis=-1, keepdims=True)
    r2 = jnp.sum(jnp.where(pick2, before, 0.0), axis=-1, keepdims=True)
    counts_ref[...] = counts_ref[...] + jnp.sum(onehot, axis=0, keepdims=True)

    route_ref[...] = jnp.where(lane == 0, i1 - N_GROUPS,
                     jnp.where(lane == 1, i2 - N_GROUPS,
                     jnp.where(lane == 2, w1,
                     jnp.where(lane == 3, w2,
                     jnp.where(lane == 4, r1,
                     jnp.where(lane == 5, r2, 0.0))))))


def _out_proj(att, rw, x2, ga, woa, wor, g2, wrh, wrm, br, tm):
    t, d = x2.shape
    wa = att.shape[1]
    wr = rw.shape[1]
    sr = d // 2 // SLAB_LANES
    c = lambda shape: pl.BlockSpec(shape, lambda i: (0, 0))
    return pl.pallas_call(
        _out_proj_kernel,
        out_shape=(jax.ShapeDtypeStruct((t, d), F32), jax.ShapeDtypeStruct((t * sr, SLAB_LANES), jnp.uint32),
                   jax.ShapeDtypeStruct((t, LANES), F32), jax.ShapeDtypeStruct((1, LANES), F32)),
        grid=(t // tm,),
        in_specs=[pl.BlockSpec((tm, wa), lambda i: (i, 0)),
                  pl.BlockSpec((tm, wr), lambda i: (i, 0)),
                  pl.BlockSpec((tm, d), lambda i: (i, 0)),
                  c((1, wa)), c((wa, d)), c((wr, d)), c((1, d)), c((d, LANES)), c((d, LANES)), c((1, LANES))],
        out_specs=(pl.BlockSpec((tm, d), lambda i: (i, 0)), pl.BlockSpec((tm * sr, SLAB_LANES), lambda i: (i, 0)),
                   pl.BlockSpec((tm, LANES), lambda i: (i, 0)), c((1, LANES))),
        compiler_params=pltpu.CompilerParams(
            dimension_semantics=("arbitrary",), vmem_limit_bytes=VMEM_LIMIT),
        name="out_proj",
    )(att, rw, x2, ga, woa, wor, g2, wrh, wrm, br)


def _pack_bf16_pairs(x):
    half = x.shape[1] // 2
    bits = pltpu.bitcast(x.astype(BF16).astype(F32), jnp.uint32)
    return (bits[:, half:] & jnp.uint32(0xFFFF0000)) | (bits[:, :half] >> jnp.uint32(16))


def _unpack_bf16_pairs(w):
    lo = pltpu.bitcast(w << jnp.uint32(16), F32)
    hi = pltpu.bitcast(w & jnp.uint32(0xFFFF0000), F32)
    return jnp.concatenate([lo, hi], axis=1).astype(BF16)


def _store_slabs(ref, x):
    rows, d = x.shape
    sr = d // SLAB_LANES
    for s in range(sr):
        ref[pl.ds(s, rows, stride=sr), :] = x[:, s * SLAB_LANES:(s + 1) * SLAB_LANES]


def _load_slabs(ref, slot, first, rows, sr):
    return jnp.concatenate([ref[slot, pl.ds(first * sr + s, rows, stride=sr), :] for s in range(sr)], axis=1)


def _gather_rows(i, nt, idx_hbm, src_hbm, idx_ref, buf_ref, isem, gsem, rows, sr):
    def idx_copy(tile, slot):
        return pltpu.make_async_copy(idx_hbm.at[pl.ds(tile, 1)], idx_ref.at[pl.ds(slot, 1)], isem.at[slot])

    def start_rows(slot):
        for r in range(rows):
            src = pl.multiple_of(idx_ref[slot, r], sr)
            pltpu.make_async_copy(src_hbm.at[pl.ds(src, sr)], buf_ref.at[slot, pl.ds(r * sr, sr)],
                                  gsem.at[slot]).start()

    slot = jnp.bitwise_and(i, 1)

    @pl.when(i == 0)
    def _():
        idx_copy(0, 0).start()
        idx_copy(0, 0).wait()
        start_rows(0)
        if nt > 1:
            idx_copy(1, 1).start()

    @pl.when(i + 1 < nt)
    def _():
        idx_copy(i + 1, 1 - slot).wait()
        start_rows(1 - slot)

    @pl.when(i + 2 < nt)
    def _():
        idx_copy(i + 2, slot).start()

    pltpu.make_async_copy(src_hbm.at[pl.ds(0, rows * sr)], buf_ref.at[slot], gsem.at[slot]).wait()
    return slot


def _dispatch_kernel(idx_hbm, x_hbm, o_hbm, idx_ref, tile_ref, zero_ref, isem, lsem, csem,
                     *, nt, n_tok, n_pad, sr):
    i = pl.program_id(0)
    slot = jnp.bitwise_and(i, 1)
    other = 1 - slot
    tg = n_tok // TOP_K
    n_all = n_tok + n_pad
    stage = lax.rem(i, 3)

    def idx_copy(tile, s):
        return pltpu.make_async_copy(idx_hbm.at[pl.ds(tile, 1)], idx_ref.at[pl.ds(s, 1)], isem.at[s])

    def load(tile, s):
        return pltpu.make_async_copy(x_hbm.at[pl.ds(tile * tg * sr, tg * sr)], tile_ref.at[s], lsem.at[s])

    def copies_done(s):
        return pltpu.make_async_copy(x_hbm.at[pl.ds(0, n_all * sr)], o_hbm.at[pl.ds(0, n_all * sr)], csem.at[s])

    @pl.when(i == 0)
    def _():
        zero_ref[...] = jnp.zeros_like(zero_ref)
        idx_copy(0, 0).start()
        load(0, 0).start()

    idx_copy(i, slot).wait()
    load(i, stage).wait()

    @pl.when(i + 1 < nt)
    def _():
        idx_copy(i + 1, other).start()
        load(i + 1, lax.rem(i + 1, 3)).start()

    for r in range(n_tok):
        dst = pl.multiple_of(idx_ref[slot, r], sr)
        pltpu.make_async_copy(tile_ref.at[stage, pl.ds((r % tg) * sr, sr)], o_hbm.at[pl.ds(dst, sr)],
                              csem.at[slot]).start()
    for r in range(n_tok, n_all):
        dst = pl.multiple_of(idx_ref[slot, r], sr)
        pltpu.make_async_copy(zero_ref, o_hbm.at[pl.ds(dst, sr)], csem.at[slot]).start()

    @pl.when(i > 0)
    def _():
        copies_done(other).wait()

    @pl.when(i == nt - 1)
    def _():
        copies_done(slot).wait()


def _dispatch(idx, hn_slabs, n_rows, sr):
    nt, n_all = idx.shape
    n_tok = TOP_K * COMBINE_ROWS
    hbm = pl.BlockSpec(memory_space=pl.ANY)
    return pl.pallas_call(
        functools.partial(_dispatch_kernel, nt=nt, n_tok=n_tok, n_pad=n_all - n_tok, sr=sr),
        out_shape=jax.ShapeDtypeStruct((n_rows * sr, SLAB_LANES), hn_slabs.dtype),
        grid=(nt,),
        in_specs=[hbm, hbm],
        out_specs=hbm,
        scratch_shapes=[pltpu.SMEM((2, n_all), jnp.int32),
                        pltpu.VMEM((3, COMBINE_ROWS * sr, SLAB_LANES), hn_slabs.dtype),
                        pltpu.VMEM((sr, SLAB_LANES), hn_slabs.dtype),
                        pltpu.SemaphoreType.DMA((2,)), pltpu.SemaphoreType.DMA((3,)),
                        pltpu.SemaphoreType.DMA((2,))],
        compiler_params=pltpu.CompilerParams(dimension_semantics=("arbitrary",)),
        name="dispatch",
    )(idx, hn_slabs)


def _expert_kernel(te_ref, first_ref, next_ref, nu_ref, x_ref, w1_hbm, w3_hbm, w2_hbm, o_ref,
                   stage_refs, wb_refs, wsem, *, sx):
    i = pl.program_id(0)
    tm = EXPERT_ROWS
    w_hbm = (w1_hbm, w3_hbm, w2_hbm)
    used = i < nu_ref[0]

    def weight_copies(e):
        out = []
        for j, (w, st) in enumerate(zip(w_hbm, stage_refs)):
            half = st.shape[0] // 2
            for q in range(2):
                rows = pl.ds(q * half, half)
                out.append((pltpu.make_async_copy(w.at[e, rows], st.at[rows], wsem.at[2 * j + q]), q))
        return out

    @pl.when(i == 0)
    def _():
        for cp, q in weight_copies(te_ref[0]):
            cp.start(priority=q)

    def cast_stage(j):
        st, wb = stage_refs[j], wb_refs[j]
        rows = st.shape[0] // WEIGHT_CAST_CHUNKS
        for c in range(WEIGHT_CAST_CHUNKS):
            wb[c * rows:(c + 1) * rows, :] = st[c * rows:(c + 1) * rows, :].astype(BF16)

    def tile_mlp(before_down_proj=None):
        xb = _unpack_bf16_pairs(jnp.concatenate([x_ref[pl.ds(s, tm, stride=sx), :] for s in range(sx)], axis=1))
        h1 = jnp.dot(xb, wb_refs[0][...], preferred_element_type=F32)
        h3 = jnp.dot(xb, wb_refs[1][...], preferred_element_type=F32)
        if before_down_proj is not None:
            before_down_proj()
        hid = (h1 * _sigmoid(h1)) * h3
        _store_slabs(o_ref, jnp.dot(hid.astype(BF16), wb_refs[2][...], preferred_element_type=F32))

    is_first = first_ref[i] == 1

    @pl.when(used & is_first)
    def _():
        for cp, _ in weight_copies(te_ref[i]):
            cp.wait()
        cast_stage(0)
        cast_stage(1)

        def finish_weights():
            cast_stage(2)

            @pl.when(next_ref[i] >= 0)
            def _():
                for cp, q in weight_copies(next_ref[i]):
                    cp.start(priority=q)

        tile_mlp(finish_weights)

    @pl.when(used & jnp.logical_not(is_first))
    def _():
        tile_mlp()

    @pl.when(jnp.logical_not(used))
    def _():
        o_ref[...] = jnp.zeros_like(o_ref)


def _experts(tile_e, tile_first, tile_next, n_used, x_slabs, w1, w3, w2):
    nt = tile_e.shape[0]
    d, de = w1.shape[1], w1.shape[2]
    sr = d // SLAB_LANES
    sx = d // 2 // SLAB_LANES
    tm = EXPERT_ROWS
    hbm = pl.BlockSpec(memory_space=pl.ANY)
    grid_spec = pltpu.PrefetchScalarGridSpec(
        num_scalar_prefetch=4,
        grid=(nt,),
        in_specs=[pl.BlockSpec((tm * sx, SLAB_LANES), lambda i, te, fi, nx, nu: (jnp.minimum(i, nu[0] - 1), 0)),
                  hbm, hbm, hbm],
        out_specs=pl.BlockSpec((tm * sr, SLAB_LANES), lambda i, *_: (i, 0)),
        scratch_shapes=[[pltpu.VMEM((d, de), F32), pltpu.VMEM((d, de), F32), pltpu.VMEM((de, d), F32)],
                        [pltpu.VMEM((d, de), BF16), pltpu.VMEM((d, de), BF16), pltpu.VMEM((de, d), BF16)],
                        pltpu.SemaphoreType.DMA((6,))],
    )
    return pl.pallas_call(
        functools.partial(_expert_kernel, sx=sx),
        out_shape=jax.ShapeDtypeStruct((nt * tm * sr, SLAB_LANES), F32),
        grid_spec=grid_spec,
        compiler_params=pltpu.CompilerParams(
            dimension_semantics=("arbitrary",), vmem_limit_bytes=EXPERT_VMEM_LIMIT),
        name="experts",
    )(tile_e, tile_first, tile_next, n_used, x_slabs, w1, w3, w2)


def _combine_kernel(dest_hbm, y_hbm, x1_ref, route_ref, fg_ref, o_ref, idx_ref, buf_ref, isem, gsem,
                    *, nt, sr):
    i = pl.program_id(0)
    tg = COMBINE_ROWS
    slot = _gather_rows(i, nt, dest_hbm, y_hbm, idx_ref, buf_ref, isem, gsem, TOP_K * tg, sr)
    route = route_ref[...]
    y = (_load_slabs(buf_ref, slot, 0, tg, sr) * route[:, 2:3]
         + _load_slabs(buf_ref, slot, tg, tg, sr) * route[:, 3:4])
    o_ref[...] = _rms(x1_ref[...] + y, fg_ref[...])


def _combine(dest, ybuf, x1, route, fg):
    t, d = x1.shape
    tg = COMBINE_ROWS
    sr = d // SLAB_LANES
    return pl.pallas_call(
        functools.partial(_combine_kernel, nt=t // tg, sr=sr),
        out_shape=jax.ShapeDtypeStruct((t, d), F32),
        grid=(t // tg,),
        in_specs=[pl.BlockSpec(memory_space=pl.ANY),
                  pl.BlockSpec(memory_space=pl.ANY),
                  pl.BlockSpec((tg, d), lambda i: (i, 0)),
                  pl.BlockSpec((tg, LANES), lambda i: (i, 0)),
                  pl.BlockSpec((1, d), lambda i: (0, 0))],
        out_specs=pl.BlockSpec((tg, d), lambda i: (i, 0)),
        scratch_shapes=[pltpu.SMEM((2, TOP_K * tg), jnp.int32),
                        pltpu.VMEM((2, TOP_K * tg * sr, SLAB_LANES), F32),
                        pltpu.SemaphoreType.DMA((2,)), pltpu.SemaphoreType.DMA((2,))],
        compiler_params=pltpu.CompilerParams(
            dimension_semantics=("arbitrary",), vmem_limit_bytes=VMEM_LIMIT),
        name="combine",
    )(dest, ybuf, x1, route, fg)


def _pad_cols(m, n):
    return jnp.pad(m, ((0, 0), (0, n - m.shape[1])))


def _pad_rows(m, n):
    return jnp.pad(m, ((0, n - m.shape[0]), (0, 0)))


def _pick_tile(n, cap):
    for unit in (MXU_WIDTH, LANES):
        fits = [c for c in range(unit, cap + 1, unit) if n % c == 0]
        if fits:
            return fits[-1]
    raise ValueError(f"no lane-aligned tile divides {n}")


def _dispatch_plan(experts, rank, counts, n_tok):
    tm = EXPERT_ROWS
    nt = n_tok * TOP_K // tm + N_EXPERTS
    tiles_e = (counts + tm - 1) // tm
    tile_end = jnp.cumsum(tiles_e)
    tile_start = tile_end - tiles_e
    dest = (tile_start[experts] * tm + rank).astype(jnp.int32)
    n_used = tile_end[-1:].astype(jnp.int32)
    tiles = jnp.arange(nt, dtype=jnp.int32)
    tile_e = jnp.minimum(jnp.sum((tile_end[None, :] <= tiles[:, None]).astype(jnp.int32), axis=1),
                         N_EXPERTS - 1).astype(jnp.int32)
    tile_first = ((tiles == 0) | (tile_e != jnp.roll(tile_e, 1))).astype(jnp.int32)
    next_start = tile_end[tile_e]
    tile_next = jnp.where(next_start < n_used[0], tile_e[jnp.minimum(next_start, nt - 1)], -1).astype(jnp.int32)

    seg_len = jnp.concatenate([tiles_e * tm - counts, (nt - n_used) * tm])
    seg_first = jnp.concatenate([tile_start * tm + counts, n_used * tm])
    seg_stop = jnp.cumsum(seg_len)
    j = jnp.arange(N_EXPERTS * tm, dtype=jnp.int32)
    seg = jnp.sum((seg_stop[None, :] <= j[:, None]).astype(jnp.int32), axis=1)
    pad_rows = (seg_first[seg] + j - (seg_stop - seg_len)[seg]).astype(jnp.int32)
    return tile_e, tile_first, tile_next, n_used, dest, pad_rows


def _layer(x2, b, s, norm1_g, w_in, attn_out_g, mu, w0, w2, a0, a2, g2, k_k, k_a, r_k, lnx_w, lnx_b,
           w_out, norm2_g, wg, bg, we, be, ew1, ew3, ew2, final_g):
    t, d = x2.shape
    aw = attn_out_g.shape[0]
    cw = w0.shape[0]
    heads = aw // ATT_HEAD_DIM
    dl, al, gl = w2.shape[0], a2.shape[0], g2.shape[0]
    pw, pa, pg = _round_up(dl, LANES), _round_up(al, LANES), _round_up(gl, LANES)
    wpad = 3 * cw + pw + pa + pg

    w_in = w_in.astype(BF16)
    sh = w_in[:, 3 * aw:]
    o = 3 * cw
    w_arr = jnp.concatenate(
        [sh[:, :o], _pad_cols(sh[:, o:o + dl], pw), _pad_cols(sh[:, o + dl:o + dl + al], pa),
         _pad_cols(sh[:, o + dl + al:o + dl + al + gl], pg), w_in[:, :3 * aw]], axis=1)
    mu_arr = jnp.concatenate(
        [mu[:o], jnp.pad(mu[o:o + dl], (0, pw - dl)), jnp.pad(mu[o + dl:o + dl + al], (0, pa - al)),
         jnp.pad(mu[o + dl + al:], (0, pg - gl))])[None, :]

    n_all = w_arr.shape[1]
    proj = _in_proj(x2, norm1_g[None, :], w_arr, tm=min(512, t), tn=_pick_tile(n_all, 4096))

    att = _moba(proj, b, s, heads, wpad // LANES)
    r1 = lambda a: a.reshape(1, -1)
    rw = _rwkv(proj, b, s, cw, wpad,
               (mu_arr, r1(w0), _pad_rows(w2, pw).astype(BF16), r1(a0), _pad_rows(a2, pa).astype(BF16),
                _pad_rows(g2, pg).astype(BF16), r1(k_k), r1(k_a), r1(r_k), r1(lnx_w), r1(lnx_b)))

    wr = _pad_cols(jnp.concatenate([wg, we], axis=1), LANES)
    wrh = wr.astype(BF16)
    wrm = (wr - wrh.astype(F32)).astype(BF16)
    br = jnp.pad(jnp.concatenate([bg, be]), (0, LANES - N_GROUPS - N_EXPERTS))[None, :]
    wo = w_out.astype(BF16)
    x1, hn, route, counts = _out_proj(att, rw, x2, r1(attn_out_g), wo[:aw], wo[aw:], r1(norm2_g), wrh, wrm, br,
                                      tm=min(512, t))

    plan_in = route[:, :6].astype(jnp.int32)
    counts = counts[0, N_GROUPS:N_GROUPS + N_EXPERTS].astype(jnp.int32)
    tg = COMBINE_ROWS
    n_tiles = t // tg
    pad_per_tile = N_EXPERTS * EXPERT_ROWS // n_tiles
    tile_e, tile_first, tile_next, n_used, dest, pad_rows = _dispatch_plan(
        plan_in[:, 0:TOP_K], plan_in[:, 4:4 + TOP_K], counts, t)

    sr = d // SLAB_LANES
    sx = sr // 2
    dest_tiles = dest.reshape(n_tiles, tg, TOP_K).transpose(0, 2, 1).reshape(n_tiles, TOP_K * tg)
    disp_idx = jnp.concatenate([dest_tiles, pad_rows.reshape(n_tiles, pad_per_tile)], axis=1) * sx
    xbuf = _dispatch(disp_idx, hn, tile_e.shape[0] * EXPERT_ROWS, sx)
    ybuf = _experts(tile_e, tile_first, tile_next, n_used, xbuf, ew1, ew3, ew2)
    return _combine(dest_tiles * sr, ybuf, x1, route, r1(final_g))


def kernel(x, norm1_g, w_in, attn_out_g, rwkv_mu, rwkv_w0, rwkv_w2, rwkv_a0, rwkv_a2, rwkv_g2, rwkv_k_k,
           rwkv_k_a, rwkv_r_k, rwkv_lnx_w, rwkv_lnx_b, w_out, norm2_g, router_group_w, router_group_b,
           router_expert_w, router_expert_b, expert_w1, expert_w3, expert_w2, final_g):
    b, s, d = x.shape
    assert norm1_g.shape[0] == 1, "single-layer block"
    assert s % MOBA_BLOCK == 0 and s % RWKV_CHUNK == 0
    assert (b * s) % max(EXPERT_ROWS, COMBINE_ROWS) == 0
    assert 2 * RWKV_CHUNK == LANES and RWKV_CHUNK == RWKV_HEAD_DIM, "rwkv packs two CxC head matrices per vreg row"
    assert (N_EXPERTS * EXPERT_ROWS) % ((b * s) // COMBINE_ROWS) == 0, "zero-fill rows split evenly over tiles"
    out = _layer(x.reshape(b * s, d), b, s, norm1_g[0], w_in[0], attn_out_g[0], rwkv_mu[0], rwkv_w0[0],
                 rwkv_w2[0], rwkv_a0[0], rwkv_a2[0], rwkv_g2[0], rwkv_k_k[0], rwkv_k_a[0], rwkv_r_k[0],
                 rwkv_lnx_w[0], rwkv_lnx_b[0], w_out[0], norm2_g[0], router_group_w[0], router_group_b[0],
                 router_expert_w[0], router_expert_b[0], expert_w1[0], expert_w3[0], expert_w2[0], final_g)
    return out.reshape(b, s, d)
```
